```python
import math
import jax, jax.numpy as jnp
from jax import lax
import numpy as np

D_MODEL = 1024
BATCH = 8
SEQ = 16384
DEPTH = 4

N_MIXERS = 4
N_MEM = 256
HEAD_DIM = 64
MEM_HEADS = 4
MEM_WIDTH = MEM_HEADS * HEAD_DIM
MLA_HEADS = 12
MLA_Q_LORA = 384
MLA_KV_LORA = 256
MLA_NOPE = 64
MLA_ROPE = 32
MLA_V = 64
ROPE_THETA = 10000.0
Q_BLOCK = 128
A_IN = MLA_Q_LORA + MLA_KV_LORA + MLA_ROPE + MEM_WIDTH
A_OUT = MLA_HEADS * MLA_V + MEM_WIDTH
DIL_GROUPS = ((128, 1), (512, 4), (2048, 16))
DIL_HEADS = 8
DIL_BLOCK = 64
DIL_QKV = len(DIL_GROUPS) * 3 * DIL_HEADS * HEAD_DIM
B_IN = DIL_QKV + MEM_WIDTH
B_OUT = DIL_HEADS * HEAD_DIM + MEM_WIDTH
ALIBI_MAX = 8.0
CONV_CH = 768
CONV_WIDTH = 31
C_IN = 2 * CONV_CH + MEM_WIDTH
C_OUT = CONV_CH + MEM_WIDTH
SC_CH = 768
SC_WIDTH = 3
D_IN = 3 * SC_CH + MEM_WIDTH
D_OUT = SC_CH + MEM_WIDTH
D_FF = 2816
EPS = 1e-6
NEG = -1e30

kernel_name = "hybrid_interleaved_mla_dilated_conformer_shortconv_encoder"


def _n_uses(m):
    return len(range(m, DEPTH, N_MIXERS))


def rms_norm(x, g):
    xf = x.astype(jnp.float32)
    y = xf * lax.rsqrt(jnp.mean(xf * xf, axis=-1, keepdims=True) + EPS)
    return (y * g.astype(jnp.float32)).astype(x.dtype)


def layer_norm(x, g, b):
    xf = x.astype(jnp.float32)
    mu = jnp.mean(xf, axis=-1, keepdims=True)
    var = jnp.mean(jnp.square(xf - mu), axis=-1, keepdims=True)
    y = (xf - mu) * lax.rsqrt(var + EPS)
    return (y * g.astype(jnp.float32) + b.astype(jnp.float32)).astype(x.dtype)


def swiglu(h, wg, wu, wd):
    return jnp.einsum('bsf,fd->bsd', jax.nn.silu(h @ wg) * (h @ wu), wd)


def depthwise_conv(u, w):
    width, ch = w.shape
    return lax.conv_general_dilated(
        u, w[:, None, :].astype(u.dtype), window_strides=(1,),
        padding=[(width // 2, width // 2)],
        dimension_numbers=('NWC', 'WIO', 'NWC'), feature_group_count=ch)


def alibi_slopes(n):
    return 2.0 ** (-ALIBI_MAX * (jnp.arange(n, dtype=jnp.float32) + 1.0) / n)


def apply_rope(t, cos, sin):
    if t.ndim == 4:
        cos, sin = cos[:, :, None, :], sin[:, :, None, :]
    half = t.shape[-1] // 2
    tf = t.astype(jnp.float32)
    t1, t2 = tf[..., :half], tf[..., half:]
    return jnp.concatenate([t1 * cos - t2 * sin, t1 * sin + t2 * cos], axis=-1).astype(t.dtype)


def memory_attention(q_mem, mem_n, w_kv):
    bsz, seq, _ = q_mem.shape
    kv = (mem_n @ w_kv).reshape(bsz, mem_n.shape[1], 2, MEM_HEADS, HEAD_DIM)
    k, v = kv[:, :, 0], kv[:, :, 1]
    q = q_mem.reshape(bsz, seq, MEM_HEADS, HEAD_DIM)
    s = jnp.einsum('bshd,bnhd->bhsn', q, k).astype(jnp.float32) * HEAD_DIM ** -0.5
    p = jax.nn.softmax(s, axis=-1).astype(v.dtype)
    return jnp.einsum('bhsn,bnhd->bshd', p, v).reshape(bsz, seq, MEM_WIDTH)


def mla_mixer(h, positions, w_in, q_g, kv_g, w_uq, w_ukv):
    bsz, seq, _ = h.shape
    z = h @ w_in
    o1 = MLA_Q_LORA
    o2 = o1 + MLA_KV_LORA
    o3 = o2 + MLA_ROPE
    c_q, c_kv, k_r, q_mem = z[..., :o1], z[..., o1:o2], z[..., o2:o3], z[..., o3:]
    half = MLA_ROPE // 2
    inv = ROPE_THETA ** (-jnp.arange(half, dtype=jnp.float32) / half)
    ang = positions.astype(jnp.float32)[..., None] * inv
    cos, sin = jnp.cos(ang), jnp.sin(ang)
    q = (rms_norm(c_q, q_g) @ w_uq).reshape(bsz, seq, MLA_HEADS, MLA_NOPE + MLA_ROPE)
    q_n, q_r = q[..., :MLA_NOPE], apply_rope(q[..., MLA_NOPE:], cos, sin)
    kv = (rms_norm(c_kv, kv_g) @ w_ukv).reshape(bsz, seq, MLA_HEADS, MLA_NOPE + MLA_V)
    k_n, v = kv[..., :MLA_NOPE], kv[..., MLA_NOPE:]
    k_r = apply_rope(k_r, cos, sin)
    scale = (MLA_NOPE + MLA_ROPE) ** -0.5
    nb = seq // Q_BLOCK
    qn_b = q_n.reshape(bsz, nb, Q_BLOCK, MLA_HEADS, MLA_NOPE).transpose(1, 0, 2, 3, 4)
    qr_b = q_r.reshape(bsz, nb, Q_BLOCK, MLA_HEADS, MLA_ROPE).transpose(1, 0, 2, 3, 4)

    def block(args):
        qn, qr = args
        s = (jnp.einsum('bqhd,bkhd->bhqk', qn, k_n)
             + jnp.einsum('bqhr,bkr->bhqk', qr, k_r)).astype(jnp.float32) * scale
        p = jax.nn.softmax(s, axis=-1).astype(v.dtype)
        return jnp.einsum('bhqk,bkhd->bqhd', p, v)

    o = lax.map(block, (qn_b, qr_b))
    o = o.transpose(1, 0, 2, 3, 4).reshape(bsz, seq, MLA_HEADS * MLA_V)
    return o, q_mem


def dilated_group(q, k, v, dilation, half, slopes):
    bsz, seq, nh, dh = q.shape
    sub_len = seq // dilation
    nb = -(-sub_len // DIL_BLOCK)
    lp = nb * DIL_BLOCK

    def sub(t):
        return t.reshape(bsz, sub_len, dilation, nh, dh).transpose(0, 2, 1, 3, 4)

    qs = jnp.pad(sub(q), ((0, 0), (0, 0), (0, lp - sub_len), (0, 0), (0, 0)))
    qs = qs.reshape(bsz, dilation, nb, DIL_BLOCK, nh, dh)

    def kv_blocks(t):
        t = jnp.pad(sub(t), ((0, 0), (0, 0), (DIL_BLOCK, lp - sub_len + DIL_BLOCK), (0, 0), (0, 0)))
        t = t.reshape(bsz, dilation, nb + 2, DIL_BLOCK, nh, dh)
        return jnp.concatenate([t[:, :, :-2], t[:, :, 1:-1], t[:, :, 2:]], axis=3)

    kb, vb = kv_blocks(k), kv_blocks(v)
    qi = jnp.arange(DIL_BLOCK)
    kj = jnp.arange(3 * DIL_BLOCK)
    rel = kj[None, :] - DIL_BLOCK - qi[:, None]
    key_idx = jnp.arange(nb)[:, None] * DIL_BLOCK - DIL_BLOCK + kj[None, :]
    mask = (jnp.abs(rel) <= half)[None] & ((key_idx >= 0) & (key_idx < sub_len))[:, None, :]
    dist = (jnp.abs(rel) * dilation).astype(jnp.float32)
    s = jnp.einsum('brnqhd,brnkhd->brnhqk', qs, kb).astype(jnp.float32) * dh ** -0.5
    s = s - slopes[:, None, None] * dist
    s = jnp.where(mask[:, None], s, NEG)
    lse = jax.nn.logsumexp(s, axis=-1)
    p = jnp.exp(s - lse[..., None]).astype(v.dtype)
    o = jnp.einsum('brnhqk,brnkhd->brnqhd', p, vb)
    o = o.reshape(bsz, dilation, lp, nh, dh)[:, :, :sub_len]
    o = o.transpose(0, 2, 1, 3, 4).reshape(bsz, seq, nh, dh)
    lse = lse.transpose(0, 1, 2, 4, 3).reshape(bsz, dilation, lp, nh)[:, :, :sub_len]
    lse = lse.transpose(0, 2, 1, 3).reshape(bsz, seq, nh)
    return o, lse


def dilated_mixer(h, w_in):
    bsz, seq, _ = h.shape
    z = h @ w_in
    n_g = len(DIL_GROUPS)
    zd = z[..., :DIL_QKV].reshape(bsz, seq, n_g, 3, DIL_HEADS, HEAD_DIM)
    q_mem = z[..., DIL_QKV:]
    slopes = alibi_slopes(n_g * DIL_HEADS).reshape(n_g, DIL_HEADS)
    outs, lses = [], []
    for g, (window, dilation) in enumerate(DIL_GROUPS):
        o, l = dilated_group(zd[:, :, g, 0], zd[:, :, g, 1], zd[:, :, g, 2],
                             dilation, window // (2 * dilation), slopes[g])
        outs.append(o)
        lses.append(l)
    wts = jax.nn.softmax(jnp.stack(lses, axis=0), axis=0)
    o = jnp.einsum('gbsh,gbshd->bshd', wts.astype(outs[0].dtype), jnp.stack(outs, axis=0))
    return o.reshape(bsz, seq, DIL_HEADS * HEAD_DIM), q_mem


def conformer_conv_mixer(h, w_in, conv_w, conv_b, ln_g, ln_b):
    z = h @ w_in
    a, gate, q_mem = z[..., :CONV_CH], z[..., CONV_CH:2 * CONV_CH], z[..., 2 * CONV_CH:]
    u = a * jax.nn.sigmoid(gate)
    u = depthwise_conv(u, conv_w) + conv_b
    u = jax.nn.silu(layer_norm(u, ln_g, ln_b))
    return u, q_mem


def short_conv_mixer(h, w_in, conv_w):
    z = h @ w_in
    bg, cg = z[..., :SC_CH], z[..., SC_CH:2 * SC_CH]
    hx, q_mem = z[..., 2 * SC_CH:3 * SC_CH], z[..., 3 * SC_CH:]
    return bg * depthwise_conv(cg * hx, conv_w), q_mem


def _fwd_setup_inputs(seed: int = 0) -> dict:
    key = jax.random.key(seed)
    ks = jax.random.split(key, 32)
    f32 = jnp.float32

    def w(k, shape, fan_in):
        return jax.random.normal(k, shape, f32) * fan_in ** -0.5

    def gain(k, shape):
        return 1.0 + 0.05 * jax.random.normal(k, shape, f32)

    na, nb_, nc, nd = _n_uses(0), _n_uses(1), _n_uses(2), _n_uses(3)
    offset = jax.random.randint(ks[2], (BATCH, 1), 0, 4096, dtype=jnp.int32)
    positions = offset + jnp.arange(SEQ, dtype=jnp.int32)[None, :]
    return {
        "x": jax.random.normal(ks[0], (BATCH, SEQ, D_MODEL), f32),
        "mem": jax.random.normal(ks[1], (BATCH, N_MEM, D_MODEL), f32),
        "positions": positions,
        "norm_g": gain(ks[3], (DEPTH, 7, D_MODEL)),
        "ffn_w_gate": w(ks[4], (DEPTH, 2, D_MODEL, D_FF), D_MODEL),
        "ffn_w_up": w(ks[5], (DEPTH, 2, D_MODEL, D_FF), D_MODEL),
        "ffn_w_down": w(ks[6], (DEPTH, 2, D_FF, D_MODEL), D_FF),
        "mem_w_kv": w(ks[7], (DEPTH, D_MODEL, 2 * MEM_WIDTH), D_MODEL),
        "a_w_in": w(ks[8], (na, D_MODEL, A_IN), D_MODEL),
        "a_q_norm": gain(ks[9], (na, MLA_Q_LORA)),
        "a_kv_norm": gain(ks[10], (na, MLA_KV_LORA)),
        "a_w_uq": w(ks[11], (na, MLA_Q_LORA, MLA_HEADS * (MLA_NOPE + MLA_ROPE)), MLA_Q_LORA),
        "a_w_ukv": w(ks[12], (na, MLA_KV_LORA, MLA_HEADS * (MLA_NOPE + MLA_V)), MLA_KV_LORA),
        "a_w_out": w(ks[13], (na, A_OUT, D_MODEL), A_OUT),
        "b_w_in": w(ks[14], (nb_, D_MODEL, B_IN), D_MODEL),
        "b_w_out": w(ks[15], (nb_, B_OUT, D_MODEL), B_OUT),
        "c_w_in": w(ks[16], (nc, D_MODEL, C_IN), D_MODEL),
        "c_conv_w": w(ks[17], (nc, CONV_WIDTH, CONV_CH), CONV_WIDTH),
        "c_conv_b": 0.02 * jax.random.normal(ks[18], (nc, CONV_CH), f32),
        "c_ln_g": gain(ks[19], (nc, CONV_CH)),
        "c_ln_b": 0.02 * jax.random.normal(ks[20], (nc, CONV_CH), f32),
        "c_w_out": w(ks[21], (nc, C_OUT, D_MODEL), C_OUT),
        "d_w_in": w(ks[22], (nd, D_MODEL, D_IN), D_MODEL),
        "d_conv_w": w(ks[23], (nd, SC_WIDTH, SC_CH), SC_WIDTH),
        "d_w_out": w(ks[24], (nd, D_OUT, D_MODEL), D_OUT),
    }


def _fwd_reference(x, mem, positions, norm_g, ffn_w_gate, ffn_w_up, ffn_w_down, mem_w_kv,
              a_w_in, a_q_norm, a_kv_norm, a_w_uq, a_w_ukv, a_w_out,
              b_w_in, b_w_out,
              c_w_in, c_conv_w, c_conv_b, c_ln_g, c_ln_b, c_w_out,
              d_w_in, d_conv_w, d_w_out):
    for i in range(DEPTH):
        g = norm_g[i]
        h = rms_norm(x, g[0])
        x = x + 0.5 * rms_norm(swiglu(h, ffn_w_gate[i, 0], ffn_w_up[i, 0], ffn_w_down[i, 0]), g[1])
        h = rms_norm(x, g[2])
        m, j = i % N_MIXERS, i // N_MIXERS
        if m == 0:
            o, q_mem = mla_mixer(h, positions, a_w_in[j], a_q_norm[j], a_kv_norm[j], a_w_uq[j], a_w_ukv[j])
            w_out = a_w_out[j]
        elif m == 1:
            o, q_mem = dilated_mixer(h, b_w_in[j])
            w_out = b_w_out[j]
        elif m == 2:
            o, q_mem = conformer_conv_mixer(h, c_w_in[j], c_conv_w[j], c_conv_b[j], c_ln_g[j], c_ln_b[j])
            w_out = c_w_out[j]
        else:
            o, q_mem = short_conv_mixer(h, d_w_in[j], d_conv_w[j])
            w_out = d_w_out[j]
        mo = memory_attention(q_mem, rms_norm(mem, g[6]), mem_w_kv[i])
        y = jnp.concatenate([o, mo], axis=-1) @ w_out
        x = x + rms_norm(y, g[3])
        h = rms_norm(x, g[4])
        x = x + 0.5 * rms_norm(swiglu(h, ffn_w_gate[i, 1], ffn_w_up[i, 1], ffn_w_down[i, 1]), g[5])
    return x


import jax as _jax
import jax.numpy as _jnp

TWIN_FORMAT = 'train_step'
FWD_PARAMS = ['x', 'mem', 'positions', 'norm_g', 'ffn_w_gate', 'ffn_w_up', 'ffn_w_down', 'mem_w_kv', 'a_w_in', 'a_q_norm', 'a_kv_norm', 'a_w_uq', 'a_w_ukv', 'a_w_out', 'b_w_in', 'b_w_out', 'c_w_in', 'c_conv_w', 'c_conv_b', 'c_ln_g', 'c_ln_b', 'c_w_out', 'd_w_in', 'd_conv_w', 'd_w_out']
TWIN_WEIGHTS = ['norm_g', 'ffn_w_gate', 'ffn_w_up', 'ffn_w_down', 'mem_w_kv', 'a_w_in', 'a_q_norm', 'a_kv_norm', 'a_w_uq', 'a_w_ukv', 'a_w_out', 'b_w_in', 'b_w_out', 'c_w_in', 'c_conv_w', 'c_conv_b', 'c_ln_g', 'c_ln_b', 'c_w_out', 'd_w_in', 'd_conv_w', 'd_w_out']
TWIN_DIFF_INPUT = 'x'
TWIN_INPUTS = ['x', 'mem', 'positions', 'norm_g', 'ffn_w_gate', 'ffn_w_up', 'ffn_w_down', 'mem_w_kv', 'a_w_in', 'a_q_norm', 'a_kv_norm', 'a_w_uq', 'a_w_ukv', 'a_w_out', 'b_w_in', 'b_w_out', 'c_w_in', 'c_conv_w', 'c_conv_b', 'c_ln_g', 'c_ln_b', 'c_w_out', 'd_w_in', 'd_conv_w', 'd_w_out', 'loss_target', 'm_norm_g', 'm_ffn_w_gate', 'm_ffn_w_up', 'm_ffn_w_down', 'm_mem_w_kv', 'm_a_w_in', 'm_a_q_norm', 'm_a_kv_norm', 'm_a_w_uq', 'm_a_w_ukv', 'm_a_w_out', 'm_b_w_in', 'm_b_w_out', 'm_c_w_in', 'm_c_conv_w', 'm_c_conv_b', 'm_c_ln_g', 'm_c_ln_b', 'm_c_w_out', 'm_d_w_in', 'm_d_conv_w', 'm_d_w_out', 'v_norm_g', 'v_ffn_w_gate', 'v_ffn_w_up', 'v_ffn_w_down', 'v_mem_w_kv', 'v_a_w_in', 'v_a_q_norm', 'v_a_kv_norm', 'v_a_w_uq', 'v_a_w_ukv', 'v_a_w_out', 'v_b_w_in', 'v_b_w_out', 'v_c_w_in', 'v_c_conv_w', 'v_c_conv_b', 'v_c_ln_g', 'v_c_ln_b', 'v_c_w_out', 'v_d_w_in', 'v_d_conv_w', 'v_d_w_out']
TWIN_OUTPUTS = ['loss', 'grad_x', 'grad_norm_g', 'grad_ffn_w_gate', 'grad_ffn_w_up', 'grad_ffn_w_down', 'grad_mem_w_kv', 'grad_a_w_in', 'grad_a_q_norm', 'grad_a_kv_norm', 'grad_a_w_uq', 'grad_a_w_ukv', 'grad_a_w_out', 'grad_b_w_in', 'grad_b_w_out', 'grad_c_w_in', 'grad_c_conv_w', 'grad_c_conv_b', 'grad_c_ln_g', 'grad_c_ln_b', 'grad_c_w_out', 'grad_d_w_in', 'grad_d_conv_w', 'grad_d_w_out', 'delta_norm_g', 'delta_ffn_w_gate', 'delta_ffn_w_up', 'delta_ffn_w_down', 'delta_mem_w_kv', 'delta_a_w_in', 'delta_a_q_norm', 'delta_a_kv_norm', 'delta_a_w_uq', 'delta_a_w_ukv', 'delta_a_w_out', 'delta_b_w_in', 'delta_b_w_out', 'delta_c_w_in', 'delta_c_conv_w', 'delta_c_conv_b', 'delta_c_ln_g', 'delta_c_ln_b', 'delta_c_w_out', 'delta_d_w_in', 'delta_d_conv_w', 'delta_d_w_out', 'new_m_norm_g', 'new_m_ffn_w_gate', 'new_m_ffn_w_up', 'new_m_ffn_w_down', 'new_m_mem_w_kv', 'new_m_a_w_in', 'new_m_a_q_norm', 'new_m_a_kv_norm', 'new_m_a_w_uq', 'new_m_a_w_ukv', 'new_m_a_w_out', 'new_m_b_w_in', 'new_m_b_w_out', 'new_m_c_w_in', 'new_m_c_conv_w', 'new_m_c_conv_b', 'new_m_c_ln_g', 'new_m_c_ln_b', 'new_m_c_w_out', 'new_m_d_w_in', 'new_m_d_conv_w', 'new_m_d_w_out', 'new_v_norm_g', 'new_v_ffn_w_gate', 'new_v_ffn_w_up', 'new_v_ffn_w_down', 'new_v_mem_w_kv', 'new_v_a_w_in', 'new_v_a_q_norm', 'new_v_a_kv_norm', 'new_v_a_w_uq', 'new_v_a_w_ukv', 'new_v_a_w_out', 'new_v_b_w_in', 'new_v_b_w_out', 'new_v_c_w_in', 'new_v_c_conv_w', 'new_v_c_conv_b', 'new_v_c_ln_g', 'new_v_c_ln_b', 'new_v_c_w_out', 'new_v_d_w_in', 'new_v_d_conv_w', 'new_v_d_w_out']
TWIN_LEAF_KINDS = {'loss': 'loss', 'grad_x': 'grad_x', 'grad_norm_g': 'grad_w', 'grad_ffn_w_gate': 'grad_w', 'grad_ffn_w_up': 'grad_w', 'grad_ffn_w_down': 'grad_w', 'grad_mem_w_kv': 'grad_w', 'grad_a_w_in': 'grad_w', 'grad_a_q_norm': 'grad_w', 'grad_a_kv_norm': 'grad_w', 'grad_a_w_uq': 'grad_w', 'grad_a_w_ukv': 'grad_w', 'grad_a_w_out': 'grad_w', 'grad_b_w_in': 'grad_w', 'grad_b_w_out': 'grad_w', 'grad_c_w_in': 'grad_w', 'grad_c_conv_w': 'grad_w', 'grad_c_conv_b': 'grad_w', 'grad_c_ln_g': 'grad_w', 'grad_c_ln_b': 'grad_w', 'grad_c_w_out': 'grad_w', 'grad_d_w_in': 'grad_w', 'grad_d_conv_w': 'grad_w', 'grad_d_w_out': 'grad_w', 'delta_norm_g': 'delta_w', 'delta_ffn_w_gate': 'delta_w', 'delta_ffn_w_up': 'delta_w', 'delta_ffn_w_down': 'delta_w', 'delta_mem_w_kv': 'delta_w', 'delta_a_w_in': 'delta_w', 'delta_a_q_norm': 'delta_w', 'delta_a_kv_norm': 'delta_w', 'delta_a_w_uq': 'delta_w', 'delta_a_w_ukv': 'delta_w', 'delta_a_w_out': 'delta_w', 'delta_b_w_in': 'delta_w', 'delta_b_w_out': 'delta_w', 'delta_c_w_in': 'delta_w', 'delta_c_conv_w': 'delta_w', 'delta_c_conv_b': 'delta_w', 'delta_c_ln_g': 'delta_w', 'delta_c_ln_b': 'delta_w', 'delta_c_w_out': 'delta_w', 'delta_d_w_in': 'delta_w', 'delta_d_conv_w': 'delta_w', 'delta_d_w_out': 'delta_w', 'new_m_norm_g': 'new_m', 'new_m_ffn_w_gate': 'new_m', 'new_m_ffn_w_up': 'new_m', 'new_m_ffn_w_down': 'new_m', 'new_m_mem_w_kv': 'new_m', 'new_m_a_w_in': 'new_m', 'new_m_a_q_norm': 'new_m', 'new_m_a_kv_norm': 'new_m', 'new_m_a_w_uq': 'new_m', 'new_m_a_w_ukv': 'new_m', 'new_m_a_w_out': 'new_m', 'new_m_b_w_in': 'new_m', 'new_m_b_w_out': 'new_m', 'new_m_c_w_in': 'new_m', 'new_m_c_conv_w': 'new_m', 'new_m_c_conv_b': 'new_m', 'new_m_c_ln_g': 'new_m', 'new_m_c_ln_b': 'new_m', 'new_m_c_w_out': 'new_m', 'new_m_d_w_in': 'new_m', 'new_m_d_conv_w': 'new_m', 'new_m_d_w_out': 'new_m', 'new_v_norm_g': 'new_v', 'new_v_ffn_w_gate': 'new_v', 'new_v_ffn_w_up': 'new_v', 'new_v_ffn_w_down': 'new_v', 'new_v_mem_w_kv': 'new_v', 'new_v_a_w_in': 'new_v', 'new_v_a_q_norm': 'new_v', 'new_v_a_kv_norm': 'new_v', 'new_v_a_w_uq': 'new_v', 'new_v_a_w_ukv': 'new_v', 'new_v_a_w_out': 'new_v', 'new_v_b_w_in': 'new_v', 'new_v_b_w_out': 'new_v', 'new_v_c_w_in': 'new_v', 'new_v_c_conv_w': 'new_v', 'new_v_c_conv_b': 'new_v', 'new_v_c_ln_g': 'new_v', 'new_v_c_ln_b': 'new_v', 'new_v_c_w_out': 'new_v', 'new_v_d_w_in': 'new_v', 'new_v_d_conv_w': 'new_v', 'new_v_d_w_out': 'new_v'}


def _forward(args):
    return _fwd_reference(*[args[k] for k in FWD_PARAMS])


def _output_shape():
    def fwd():
        inp = _fwd_setup_inputs(0)
        return _fwd_reference(*[inp[k] for k in FWD_PARAMS])
    out = _jax.eval_shape(fwd)
    return out.shape, out.dtype

N_MICROBATCH = 1
ADAM_LR = 0.001
ADAM_B1 = 0.9
ADAM_B2 = 0.999
ADAM_EPS = 1e-08
ADAM_WD = 0.01
ADAM_STEP = 10
PER_EXAMPLE_BATCH_AXIS = {'x': 0, 'mem': 0, 'positions': 0, 'loss_target': 0}
SHARED_INPUTS = []
_WEIGHT_DTYPES = {'norm_g': _jnp.float32, 'ffn_w_gate': _jnp.float32, 'ffn_w_up': _jnp.float32, 'ffn_w_down': _jnp.float32, 'mem_w_kv': _jnp.float32, 'a_w_in': _jnp.float32, 'a_q_norm': _jnp.float32, 'a_kv_norm': _jnp.float32, 'a_w_uq': _jnp.float32, 'a_w_ukv': _jnp.float32, 'a_w_out': _jnp.float32, 'b_w_in': _jnp.float32, 'b_w_out': _jnp.float32, 'c_w_in': _jnp.float32, 'c_conv_w': _jnp.float32, 'c_conv_b': _jnp.float32, 'c_ln_g': _jnp.float32, 'c_ln_b': _jnp.float32, 'c_w_out': _jnp.float32, 'd_w_in': _jnp.float32, 'd_conv_w': _jnp.float32, 'd_w_out': _jnp.float32}
MOMENT_SCALE = {'norm_g': 5.247105e+01, 'ffn_w_gate': 2.402333e+00, 'ffn_w_up': 2.955713e+00, 'ffn_w_down': 4.888508e+00, 'mem_w_kv': 1.846714e+01, 'a_w_in': 1.269334e+01, 'a_q_norm': 3.906493e+00, 'a_kv_norm': 3.810355e+01, 'a_w_uq': 2.234080e+00, 'a_w_ukv': 9.784123e+00, 'a_w_out': 2.684362e+01, 'b_w_in': 9.401470e+00, 'b_w_out': 2.324185e+01, 'c_w_in': 7.516352e+00, 'c_conv_w': 1.324097e+01, 'c_conv_b': 8.503196e+01, 'c_ln_g': 3.301554e+01, 'c_ln_b': 4.610222e+01, 'c_w_out': 1.994584e+01, 'd_w_in': 2.029061e+00, 'd_conv_w': 2.158951e+00, 'd_w_out': 2.191916e+00}


def _to_microbatches(a, axis):
    t = _jnp.moveaxis(a, axis, 0)
    t = t.reshape((N_MICROBATCH, t.shape[0] // N_MICROBATCH) + t.shape[1:])
    return _jnp.moveaxis(t, 1, axis + 1)


def setup_inputs(seed: int = 0) -> dict:
    inp = _fwd_setup_inputs(seed)
    key = _jax.random.fold_in(_jax.random.key(seed), 7919)
    shape, _ = _output_shape()
    out = dict(inp)
    out["loss_target"] = _jax.random.normal(_jax.random.fold_in(key, 0), shape, _jnp.float32)
    for i, name in enumerate(TWIN_WEIGHTS):
        w = inp[name].astype(_jnp.float32)
        if MOMENT_SCALE is None:
            s = _jnp.sqrt(_jnp.mean(_jnp.square(w)) + 1e-30)
        else:
            s = MOMENT_SCALE[name]
        km, kv = _jax.random.split(_jax.random.fold_in(key, i + 1))
        out[name] = w
        out["m_" + name] = s * _jax.random.normal(km, w.shape, _jnp.float32)
        out["v_" + name] = (s * s) * _jax.random.uniform(kv, w.shape, _jnp.float32, 0.5, 1.5)
    if N_MICROBATCH > 1:
        for name, axis in PER_EXAMPLE_BATCH_AXIS.items():
            out[name] = _to_microbatches(out[name], axis)
    return {'x': out['x'], 'mem': out['mem'], 'positions': out['positions'], 'norm_g': out['norm_g'], 'ffn_w_gate': out['ffn_w_gate'], 'ffn_w_up': out['ffn_w_up'], 'ffn_w_down': out['ffn_w_down'], 'mem_w_kv': out['mem_w_kv'], 'a_w_in': out['a_w_in'], 'a_q_norm': out['a_q_norm'], 'a_kv_norm': out['a_kv_norm'], 'a_w_uq': out['a_w_uq'], 'a_w_ukv': out['a_w_ukv'], 'a_w_out': out['a_w_out'], 'b_w_in': out['b_w_in'], 'b_w_out': out['b_w_out'], 'c_w_in': out['c_w_in'], 'c_conv_w': out['c_conv_w'], 'c_conv_b': out['c_conv_b'], 'c_ln_g': out['c_ln_g'], 'c_ln_b': out['c_ln_b'], 'c_w_out': out['c_w_out'], 'd_w_in': out['d_w_in'], 'd_conv_w': out['d_conv_w'], 'd_w_out': out['d_w_out'], 'loss_target': out['loss_target'], 'm_norm_g': out['m_norm_g'], 'm_ffn_w_gate': out['m_ffn_w_gate'], 'm_ffn_w_up': out['m_ffn_w_up'], 'm_ffn_w_down': out['m_ffn_w_down'], 'm_mem_w_kv': out['m_mem_w_kv'], 'm_a_w_in': out['m_a_w_in'], 'm_a_q_norm': out['m_a_q_norm'], 'm_a_kv_norm': out['m_a_kv_norm'], 'm_a_w_uq': out['m_a_w_uq'], 'm_a_w_ukv': out['m_a_w_ukv'], 'm_a_w_out': out['m_a_w_out'], 'm_b_w_in': out['m_b_w_in'], 'm_b_w_out': out['m_b_w_out'], 'm_c_w_in': out['m_c_w_in'], 'm_c_conv_w': out['m_c_conv_w'], 'm_c_conv_b': out['m_c_conv_b'], 'm_c_ln_g': out['m_c_ln_g'], 'm_c_ln_b': out['m_c_ln_b'], 'm_c_w_out': out['m_c_w_out'], 'm_d_w_in': out['m_d_w_in'], 'm_d_conv_w': out['m_d_conv_w'], 'm_d_w_out': out['m_d_w_out'], 'v_norm_g': out['v_norm_g'], 'v_ffn_w_gate': out['v_ffn_w_gate'], 'v_ffn_w_up': out['v_ffn_w_up'], 'v_ffn_w_down': out['v_ffn_w_down'], 'v_mem_w_kv': out['v_mem_w_kv'], 'v_a_w_in': out['v_a_w_in'], 'v_a_q_norm': out['v_a_q_norm'], 'v_a_kv_norm': out['v_a_kv_norm'], 'v_a_w_uq': out['v_a_w_uq'], 'v_a_w_ukv': out['v_a_w_ukv'], 'v_a_w_out': out['v_a_w_out'], 'v_b_w_in': out['v_b_w_in'], 'v_b_w_out': out['v_b_w_out'], 'v_c_w_in': out['v_c_w_in'], 'v_c_conv_w': out['v_c_conv_w'], 'v_c_conv_b': out['v_c_conv_b'], 'v_c_ln_g': out['v_c_ln_g'], 'v_c_ln_b': out['v_c_ln_b'], 'v_c_w_out': out['v_c_w_out'], 'v_d_w_in': out['v_d_w_in'], 'v_d_conv_w': out['v_d_conv_w'], 'v_d_w_out': out['v_d_w_out']}


def _loss(weights, diff, rest, loss_target):
    with _jax.named_scope("forward"):
        args = {**rest, TWIN_DIFF_INPUT: diff, **{k: w.astype(_WEIGHT_DTYPES[k]) for k, w in weights.items()}}
        y = _forward(args)
    with _jax.named_scope("loss_head"):
        err = _jnp.square(y.astype(_jnp.float32) - loss_target)
        return 0.5 * _jnp.sum(_jnp.mean(err, axis=-1)) if err.ndim else 0.5 * err


def _adamw(w, g, m, v):
    m = ADAM_B1 * m + (1.0 - ADAM_B1) * g
    v = ADAM_B2 * v + (1.0 - ADAM_B2) * _jnp.square(g)
    m_hat = m / (1.0 - ADAM_B1 ** ADAM_STEP)
    v_hat = v / (1.0 - ADAM_B2 ** ADAM_STEP)
    delta = -ADAM_LR * (m_hat / (_jnp.sqrt(v_hat) + ADAM_EPS) + ADAM_WD * w)
    return delta, m, v


def reference(x, mem, positions, norm_g, ffn_w_gate, ffn_w_up, ffn_w_down, mem_w_kv, a_w_in, a_q_norm, a_kv_norm, a_w_uq, a_w_ukv, a_w_out, b_w_in, b_w_out, c_w_in, c_conv_w, c_conv_b, c_ln_g, c_ln_b, c_w_out, d_w_in, d_conv_w, d_w_out, loss_target, m_norm_g, m_ffn_w_gate, m_ffn_w_up, m_ffn_w_down, m_mem_w_kv, m_a_w_in, m_a_q_norm, m_a_kv_norm, m_a_w_uq, m_a_w_ukv, m_a_w_out, m_b_w_in, m_b_w_out, m_c_w_in, m_c_conv_w, m_c_conv_b, m_c_ln_g, m_c_ln_b, m_c_w_out, m_d_w_in, m_d_conv_w, m_d_w_out, v_norm_g, v_ffn_w_gate, v_ffn_w_up, v_ffn_w_down, v_mem_w_kv, v_a_w_in, v_a_q_norm, v_a_kv_norm, v_a_w_uq, v_a_w_ukv, v_a_w_out, v_b_w_in, v_b_w_out, v_c_w_in, v_c_conv_w, v_c_conv_b, v_c_ln_g, v_c_ln_b, v_c_w_out, v_d_w_in, v_d_conv_w, v_d_w_out):
    given = dict(x=x, mem=mem, positions=positions, norm_g=norm_g, ffn_w_gate=ffn_w_gate, ffn_w_up=ffn_w_up, ffn_w_down=ffn_w_down, mem_w_kv=mem_w_kv, a_w_in=a_w_in, a_q_norm=a_q_norm, a_kv_norm=a_kv_norm, a_w_uq=a_w_uq, a_w_ukv=a_w_ukv, a_w_out=a_w_out, b_w_in=b_w_in, b_w_out=b_w_out, c_w_in=c_w_in, c_conv_w=c_conv_w, c_conv_b=c_conv_b, c_ln_g=c_ln_g, c_ln_b=c_ln_b, c_w_out=c_w_out, d_w_in=d_w_in, d_conv_w=d_conv_w, d_w_out=d_w_out, loss_target=loss_target, m_norm_g=m_norm_g, m_ffn_w_gate=m_ffn_w_gate, m_ffn_w_up=m_ffn_w_up, m_ffn_w_down=m_ffn_w_down, m_mem_w_kv=m_mem_w_kv, m_a_w_in=m_a_w_in, m_a_q_norm=m_a_q_norm, m_a_kv_norm=m_a_kv_norm, m_a_w_uq=m_a_w_uq, m_a_w_ukv=m_a_w_ukv, m_a_w_out=m_a_w_out, m_b_w_in=m_b_w_in, m_b_w_out=m_b_w_out, m_c_w_in=m_c_w_in, m_c_conv_w=m_c_conv_w, m_c_conv_b=m_c_conv_b, m_c_ln_g=m_c_ln_g, m_c_ln_b=m_c_ln_b, m_c_w_out=m_c_w_out, m_d_w_in=m_d_w_in, m_d_conv_w=m_d_conv_w, m_d_w_out=m_d_w_out, v_norm_g=v_norm_g, v_ffn_w_gate=v_ffn_w_gate, v_ffn_w_up=v_ffn_w_up, v_ffn_w_down=v_ffn_w_down, v_mem_w_kv=v_mem_w_kv, v_a_w_in=v_a_w_in, v_a_q_norm=v_a_q_norm, v_a_kv_norm=v_a_kv_norm, v_a_w_uq=v_a_w_uq, v_a_w_ukv=v_a_w_ukv, v_a_w_out=v_a_w_out, v_b_w_in=v_b_w_in, v_b_w_out=v_b_w_out, v_c_w_in=v_c_w_in, v_c_conv_w=v_c_conv_w, v_c_conv_b=v_c_conv_b, v_c_ln_g=v_c_ln_g, v_c_ln_b=v_c_ln_b, v_c_w_out=v_c_w_out, v_d_w_in=v_d_w_in, v_d_conv_w=v_d_conv_w, v_d_w_out=v_d_w_out)
    weights = {n: given[n] for n in TWIN_WEIGHTS}
    shared = {n: given[n] for n in SHARED_INPUTS}
    per_example = {n: given[n] for n in ['x', 'mem', 'positions']}
    grad_fn = _jax.value_and_grad(_loss, argnums=(0, 1))

    def one_microbatch(ex, loss_target):
        ex = dict(ex)
        diff = ex.pop(TWIN_DIFF_INPUT)
        return grad_fn(weights, diff, {**shared, **ex}, loss_target)

    if N_MICROBATCH == 1:
        loss, (grad_w, grad_x) = one_microbatch(per_example, given["loss_target"])
    else:
        def body(carry, xs):
            loss_sum, grad_sum = carry
            l_k, (gw_k, gx_k) = one_microbatch(xs[0], xs[1])
            with _jax.named_scope("update"):
                return (loss_sum + l_k, _jax.tree.map(_jnp.add, grad_sum, gw_k)), gx_k

        init = (_jnp.zeros((), _jnp.float32), _jax.tree.map(_jnp.zeros_like, weights))
        (loss, grad_w), grad_x = _jax.lax.scan(body, init, (per_example, given["loss_target"]))
    with _jax.named_scope("update"):
        delta_w, new_m, new_v = {}, {}, {}
        for n in TWIN_WEIGHTS:
            delta_w[n], new_m[n], new_v[n] = _adamw(weights[n], grad_w[n], given["m_" + n], given["v_" + n])
    return (loss, grad_x, *[grad_w[n] for n in TWIN_WEIGHTS], *[delta_w[n] for n in TWIN_WEIGHTS],
            *[new_m[n] for n in TWIN_WEIGHTS], *[new_v[n] for n in TWIN_WEIGHTS])
```

```python
import functools
import math

import jax
import jax.numpy as jnp
from jax import lax
from jax.experimental import pallas as pl
from jax.experimental.pallas import tpu as pltpu

F32 = jnp.float32
BF16 = jnp.bfloat16
MESH = pl.DeviceIdType.MESH

VMEM_LIMIT_BYTES = 56 * 1024 * 1024
LANES = 128
PACK_COLS = 1024
PACK_ROW_MULT = 512

HEAD_DIM = 64
MEM_HEADS = 4
MEM_WIDTH = MEM_HEADS * HEAD_DIM
MLA_HEADS = 12
MLA_Q_LORA = 384
MLA_KV_LORA = 256
MLA_NOPE = 64
MLA_ROPE = 32
MLA_V = 64
ROPE_THETA = 10000.0
DIL_GROUPS = ((128, 1), (512, 4), (2048, 16))
DIL_HEADS = 8
ALIBI_MAX = 8.0
CONV_CH = 768
CONV_WIDTH = 31
SC_CH = 768
SC_WIDTH = 3
CONV_PAD = 16
CONV_CHUNK = 256
EPS = 1e-6
NEG = -1e30
DEPTH = 4

ADAM_LR = 0.001
ADAM_B1 = 0.9
ADAM_B2 = 0.999
ADAM_EPS = 1e-08
ADAM_WD = 0.01
ADAM_STEP = 10

WEIGHT_NAMES = ['norm_g', 'ffn_w_gate', 'ffn_w_up', 'ffn_w_down', 'mem_w_kv', 'a_w_in', 'a_q_norm', 'a_kv_norm',
                'a_w_uq', 'a_w_ukv', 'a_w_out', 'b_w_in', 'b_w_out', 'c_w_in', 'c_conv_w', 'c_conv_b', 'c_ln_g',
                'c_ln_b', 'c_w_out', 'd_w_in', 'd_conv_w', 'd_w_out']
WEIGHT_LAYOUT = {
    'norm_g': (0, 2), 'ffn_w_gate': (2, 1), 'ffn_w_up': (2, 1), 'ffn_w_down': (2, 0), 'mem_w_kv': (1, 0),
    'a_w_in': (1, 0), 'a_q_norm': (1, None), 'a_kv_norm': (1, None), 'a_w_uq': (1, 1), 'a_w_ukv': (1, 1),
    'a_w_out': (1, 0), 'b_w_in': (1, 1), 'b_w_out': (1, 1), 'c_w_in': (1, 1), 'c_conv_w': (1, 1),
    'c_conv_b': (1, 0), 'c_ln_g': (1, 0), 'c_ln_b': (1, 0), 'c_w_out': (1, 0), 'd_w_in': (1, 1),
    'd_conv_w': (1, 1), 'd_w_out': (1, 0),
}
F32_GATHER = ('norm_g', 'c_conv_w', 'c_conv_b', 'c_ln_g', 'c_ln_b', 'd_conv_w')
N_CHIPS = 4


def _cparams(semantics):
    return pltpu.CompilerParams(dimension_semantics=semantics, vmem_limit_bytes=VMEM_LIMIT_BYTES)


def _div_tile(n, cap, mult):
    if n <= cap:
        return n
    for d in range(cap - cap % mult, 0, -mult):
        if n % d == 0:
            return d
    raise ValueError(f"no tile for {n} (cap {cap}, multiple of {mult})")


def _matmul(a, b, mode, out_dtype):
    if mode == 'nn':
        (M, K), N = a.shape, b.shape[1]
    elif mode == 'nt':
        (M, K), N = a.shape, b.shape[0]
    else:
        (K, M), N = a.shape, b.shape[1]
    if mode == 'tn':
        tm, tn, tk = _div_tile(M, 1024, LANES), _div_tile(N, 1536, LANES), _div_tile(K, 512, 16)
    else:
        tm, tn, tk = _div_tile(M, 512, 16), _div_tile(N, 2816, LANES), _div_tile(K, 2816, LANES)
    nk = K // tk
    if mode == 'nn':
        a_spec = pl.BlockSpec((tm, tk), lambda i, j, k: (i, k))
        b_spec = pl.BlockSpec((tk, tn), lambda i, j, k: (k, j))
        dims = (((1,), (0,)), ((), ()))
    elif mode == 'nt':
        a_spec = pl.BlockSpec((tm, tk), lambda i, j, k: (i, k))
        b_spec = pl.BlockSpec((tn, tk), lambda i, j, k: (j, k))
        dims = (((1,), (1,)), ((), ()))
    else:
        a_spec = pl.BlockSpec((tk, tm), lambda i, j, k: (k, i))
        b_spec = pl.BlockSpec((tk, tn), lambda i, j, k: (k, j))
        dims = (((0,), (0,)), ((), ()))

    def body(a_ref, b_ref, o_ref, *acc):
        r = lax.dot_general(a_ref[...].astype(BF16), b_ref[...].astype(BF16), dims, preferred_element_type=F32)
        if nk == 1:
            o_ref[...] = r.astype(o_ref.dtype)
        else:
            k = pl.program_id(2)

            @pl.when(k == 0)
            def _():
                acc[0][...] = r

            @pl.when(k > 0)
            def _():
                acc[0][...] += r

            @pl.when(k == nk - 1)
            def _():
                o_ref[...] = acc[0][...].astype(o_ref.dtype)

    return pl.pallas_call(
        body,
        out_shape=jax.ShapeDtypeStruct((M, N), out_dtype),
        grid=(M // tm, N // tn, nk),
        in_specs=[a_spec, b_spec],
        out_specs=pl.BlockSpec((tm, tn), lambda i, j, k: (i, j)),
        scratch_shapes=[pltpu.VMEM((tm, tn), F32)] if nk > 1 else [],
        compiler_params=_cparams(("parallel", "parallel", "arbitrary")),
        name=f"mm_{mode}_{M}x{K}x{N}",
    )(a, b)


@functools.partial(jax.custom_vjp, nondiff_argnums=(2,))
def mm(a, w, out_dtype):
    return _matmul(a, w.astype(BF16), 'nn', out_dtype)


def _mm_fwd(a, w, out_dtype):
    wb = w.astype(BF16)
    return _matmul(a, wb, 'nn', out_dtype), (a, wb)


def _mm_bwd(out_dtype, res, g):
    a, wb = res
    return _matmul(g, wb, 'nt', a.dtype), _matmul(a, g, 'tn', F32)


mm.defvjp(_mm_fwd, _mm_bwd)


def _row_tile(n_rows, widest):
    t = 1 << max(3, int(math.log2(max(8, 512 * 1024 // max(widest, 1)))))
    return min(n_rows, min(t, 2048))


def _rw_forward(fn, rows, params, name):
    n_rows = rows[0].shape[0]
    tile = _row_tile(n_rows, max(r.shape[1] for r in rows))
    outs = jax.eval_shape(fn, *[jax.ShapeDtypeStruct((tile, r.shape[1]), r.dtype) for r in rows],
                          *[jax.ShapeDtypeStruct(p.shape, p.dtype) for p in params])
    n_in = len(rows) + len(params)

    def body(*refs):
        res = fn(*[r[...] for r in refs[:n_in]])
        for o_ref, o in zip(refs[n_in:], res):
            o_ref[...] = o

    return pl.pallas_call(
        body,
        out_shape=[jax.ShapeDtypeStruct((n_rows, o.shape[1]), o.dtype) for o in outs],
        grid=(n_rows // tile,),
        in_specs=[pl.BlockSpec((tile, r.shape[1]), lambda i: (i, 0)) for r in rows]
        + [pl.BlockSpec(p.shape, lambda i: (0, 0)) for p in params],
        out_specs=[pl.BlockSpec((tile, o.shape[1]), lambda i: (i, 0)) for o in outs],
        compiler_params=_cparams(("parallel",)),
        name=name,
    )(*rows, *params)


def _rw_backward(fn, rows, params, cts, n_const, name):
    n_rows = rows[0].shape[0]
    tile = _row_tile(n_rows, max([r.shape[1] for r in rows] + [c.shape[1] for c in cts]))
    n_r, n_p, n_c = len(rows), len(params), len(cts)
    n_diff = n_r - n_const

    def body(*refs):
        row_vals = [r[...] for r in refs[:n_r]]
        par_vals = [r[...] for r in refs[n_r:n_r + n_p]]
        ct_vals = tuple(r[...] for r in refs[n_r + n_p:n_r + n_p + n_c])
        out_refs = refs[n_r + n_p + n_c:]

        def f(*diff):
            return fn(*diff[:n_diff], *row_vals[n_diff:], *diff[n_diff:])

        _, vjp = jax.vjp(f, *row_vals[:n_diff], *par_vals)
        grads = vjp(ct_vals)
        for o_ref, g in zip(out_refs[:n_diff], grads[:n_diff]):
            o_ref[...] = g
        i = pl.program_id(0)
        for o_ref, g in zip(out_refs[n_diff:], grads[n_diff:]):
            @pl.when(i == 0)
            def _(o_ref=o_ref, g=g):
                o_ref[...] = g

            @pl.when(i > 0)
            def _(o_ref=o_ref, g=g):
                o_ref[...] += g

    res = pl.pallas_call(
        body,
        out_shape=[jax.ShapeDtypeStruct(r.shape, r.dtype) for r in rows[:n_diff]]
        + [jax.ShapeDtypeStruct(p.shape, p.dtype) for p in params],
        grid=(n_rows // tile,),
        in_specs=[pl.BlockSpec((tile, r.shape[1]), lambda i: (i, 0)) for r in rows]
        + [pl.BlockSpec(p.shape, lambda i: (0, 0)) for p in params]
        + [pl.BlockSpec((tile, c.shape[1]), lambda i: (i, 0)) for c in cts],
        out_specs=[pl.BlockSpec((tile, r.shape[1]), lambda i: (i, 0)) for r in rows[:n_diff]]
        + [pl.BlockSpec(p.shape, lambda i: (0, 0)) for p in params],
        compiler_params=_cparams(("arbitrary",)),
        name=name + "_bwd",
    )(*rows, *params, *cts)
    return list(res[:n_diff]), list(res[n_diff:])


def rowwise(fn, rows, params, name, n_const=0):
    rows, params = list(rows), list(params)

    @jax.custom_vjp
    def op(rows, params):
        return tuple(_rw_forward(fn, rows, params, name))

    def op_fwd(rows, params):
        return tuple(_rw_forward(fn, rows, params, name)), (rows, params)

    def op_bwd(res, cts):
        rows, params = res
        d_rows, d_params = _rw_backward(fn, rows, params, list(cts), n_const, name)
        d_rows = d_rows + [jnp.zeros_like(r) for r in rows[len(rows) - n_const:]]
        return d_rows, d_params

    op.defvjp(op_fwd, op_bwd)
    return op(rows, params)


def _rms(x, g):
    xf = x.astype(F32)
    return xf * lax.rsqrt(jnp.mean(xf * xf, axis=-1, keepdims=True) + EPS) * g


ATTN_TILE = 1024


def _flash_fwd(q, k, v, scale):
    N, Lq, dk = q.shape
    Lk, dv = k.shape[1], v.shape[2]
    tq, tk = min(ATTN_TILE, Lq), min(ATTN_TILE, Lk)
    nk = Lk // tk

    def body(q_ref, k_ref, v_ref, o_ref, lse_ref, m_s, l_s, acc_s):
        j = pl.program_id(2)

        @pl.when(j == 0)
        def _():
            m_s[...] = jnp.full_like(m_s, NEG)
            l_s[...] = jnp.zeros_like(l_s)
            acc_s[...] = jnp.zeros_like(acc_s)

        s = lax.dot_general(q_ref[0], k_ref[0], (((1,), (1,)), ((), ())), preferred_element_type=F32) * scale
        m_prev = m_s[...]
        m_new = jnp.maximum(m_prev, jnp.max(s, axis=-1, keepdims=True))
        alpha = jnp.exp(m_prev - m_new)
        p = jnp.exp(s - m_new)
        l_s[...] = alpha * l_s[...] + jnp.sum(p, axis=-1, keepdims=True)
        acc_s[...] = alpha * acc_s[...] + jnp.dot(p.astype(BF16), v_ref[0], preferred_element_type=F32)
        m_s[...] = m_new

        @pl.when(j == nk - 1)
        def _():
            o_ref[0] = (acc_s[...] / l_s[...]).astype(o_ref.dtype)
            lse_ref[0] = m_s[...] + jnp.log(l_s[...])

    return pl.pallas_call(
        body,
        out_shape=[jax.ShapeDtypeStruct((N, Lq, dv), BF16), jax.ShapeDtypeStruct((N, Lq, 1), F32)],
        grid=(N, Lq // tq, nk),
        in_specs=[pl.BlockSpec((1, tq, dk), lambda n, i, j: (n, i, 0)),
                  pl.BlockSpec((1, tk, dk), lambda n, i, j: (n, j, 0)),
                  pl.BlockSpec((1, tk, dv), lambda n, i, j: (n, j, 0))],
        out_specs=[pl.BlockSpec((1, tq, dv), lambda n, i, j: (n, i, 0)),
                   pl.BlockSpec((1, tq, 1), lambda n, i, j: (n, i, 0))],
        scratch_shapes=[pltpu.VMEM((tq, 1), F32), pltpu.VMEM((tq, 1), F32), pltpu.VMEM((tq, dv), F32)],
        compiler_params=_cparams(("parallel", "parallel", "arbitrary")),
        name=f"flash_fwd_{N}x{Lq}x{Lk}x{dk}",
    )(q, k, v)


def _flash_dq(q, k, v, do, lse, dd, scale):
    N, Lq, dk = q.shape
    Lk, dv = k.shape[1], v.shape[2]
    tq, tk = min(ATTN_TILE, Lq), min(ATTN_TILE, Lk)
    nk = Lk // tk

    def body(q_ref, k_ref, v_ref, do_ref, lse_ref, dd_ref, dq_ref, acc_s):
        j = pl.program_id(2)

        @pl.when(j == 0)
        def _():
            acc_s[...] = jnp.zeros_like(acc_s)

        s = lax.dot_general(q_ref[0], k_ref[0], (((1,), (1,)), ((), ())), preferred_element_type=F32) * scale
        p = jnp.exp(s - lse_ref[0])
        dp = lax.dot_general(do_ref[0], v_ref[0], (((1,), (1,)), ((), ())), preferred_element_type=F32)
        ds = (p * (dp - dd_ref[0])).astype(BF16)
        acc_s[...] += jnp.dot(ds, k_ref[0], preferred_element_type=F32)

        @pl.when(j == nk - 1)
        def _():
            dq_ref[0] = (acc_s[...] * scale).astype(dq_ref.dtype)

    return pl.pallas_call(
        body,
        out_shape=jax.ShapeDtypeStruct(q.shape, q.dtype),
        grid=(N, Lq // tq, nk),
        in_specs=[pl.BlockSpec((1, tq, dk), lambda n, i, j: (n, i, 0)),
                  pl.BlockSpec((1, tk, dk), lambda n, i, j: (n, j, 0)),
                  pl.BlockSpec((1, tk, dv), lambda n, i, j: (n, j, 0)),
                  pl.BlockSpec((1, tq, dv), lambda n, i, j: (n, i, 0)),
                  pl.BlockSpec((1, tq, 1), lambda n, i, j: (n, i, 0)),
                  pl.BlockSpec((1, tq, 1), lambda n, i, j: (n, i, 0))],
        out_specs=pl.BlockSpec((1, tq, dk), lambda n, i, j: (n, i, 0)),
        scratch_shapes=[pltpu.VMEM((tq, dk), F32)],
        compiler_params=_cparams(("parallel", "parallel", "arbitrary")),
        name=f"flash_dq_{N}x{Lq}x{Lk}x{dk}",
    )(q, k, v, do, lse, dd)


def _flash_dkv(q, k, v, do, lse_row, dd_row, scale):
    N, Lq, dk = q.shape
    Lk, dv = k.shape[1], v.shape[2]
    tq, tk = min(ATTN_TILE, Lq), min(ATTN_TILE, Lk)
    nq = Lq // tq

    def body(q_ref, k_ref, v_ref, do_ref, lse_ref, dd_ref, dk_ref, dv_ref, dk_s, dv_s):
        j = pl.program_id(2)

        @pl.when(j == 0)
        def _():
            dk_s[...] = jnp.zeros_like(dk_s)
            dv_s[...] = jnp.zeros_like(dv_s)

        st = lax.dot_general(k_ref[0], q_ref[0], (((1,), (1,)), ((), ())), preferred_element_type=F32) * scale
        pt = jnp.exp(st - lse_ref[0])
        dv_s[...] += jnp.dot(pt.astype(BF16), do_ref[0], preferred_element_type=F32)
        dpt = lax.dot_general(v_ref[0], do_ref[0], (((1,), (1,)), ((), ())), preferred_element_type=F32)
        dst = (pt * (dpt - dd_ref[0])).astype(BF16)
        dk_s[...] += jnp.dot(dst, q_ref[0], preferred_element_type=F32)

        @pl.when(j == nq - 1)
        def _():
            dk_ref[0] = (dk_s[...] * scale).astype(dk_ref.dtype)
            dv_ref[0] = dv_s[...].astype(dv_ref.dtype)

    return pl.pallas_call(
        body,
        out_shape=[jax.ShapeDtypeStruct(k.shape, k.dtype), jax.ShapeDtypeStruct(v.shape, v.dtype)],
        grid=(N, Lk // tk, nq),
        in_specs=[pl.BlockSpec((1, tq, dk), lambda n, i, j: (n, j, 0)),
                  pl.BlockSpec((1, tk, dk), lambda n, i, j: (n, i, 0)),
                  pl.BlockSpec((1, tk, dv), lambda n, i, j: (n, i, 0)),
                  pl.BlockSpec((1, tq, dv), lambda n, i, j: (n, j, 0)),
                  pl.BlockSpec((1, 1, tq), lambda n, i, j: (n, 0, j)),
                  pl.BlockSpec((1, 1, tq), lambda n, i, j: (n, 0, j))],
        out_specs=[pl.BlockSpec((1, tk, dk), lambda n, i, j: (n, i, 0)),
                   pl.BlockSpec((1, tk, dv), lambda n, i, j: (n, i, 0))],
        scratch_shapes=[pltpu.VMEM((tk, dk), F32), pltpu.VMEM((tk, dv), F32)],
        compiler_params=_cparams(("parallel", "parallel", "arbitrary")),
        name=f"flash_dkv_{N}x{Lq}x{Lk}x{dk}",
    )(q, k, v, do, lse_row, dd_row)


def _attn_delta(do, o, dlse):
    N, L, dv = o.shape

    def fn(do_t, o_t, dl_t):
        return (jnp.sum(do_t.astype(F32) * o_t.astype(F32), axis=-1, keepdims=True) - dl_t,)

    (dd,) = _rw_forward(fn, [do.reshape(N * L, dv), o.reshape(N * L, dv), dlse.reshape(N * L, 1)], [], "attn_delta")
    return dd.reshape(N, L, 1)


def full_attention(q, k, v, scale):
    @jax.custom_vjp
    def op(q, k, v):
        return _flash_fwd(q, k, v, scale)[0]

    def op_fwd(q, k, v):
        o, lse = _flash_fwd(q, k, v, scale)
        return o, (q, k, v, o, lse)

    def op_bwd(res, do):
        q, k, v, o, lse = res
        N, Lq, _ = q.shape
        dd = _attn_delta(do, o, jnp.zeros_like(lse))
        dq = _flash_dq(q, k, v, do, lse, dd, scale)
        dk, dv = _flash_dkv(q, k, v, do, lse.reshape(N, 1, Lq), dd.reshape(N, 1, Lq), scale)
        return dq, dk, dv

    op.defvjp(op_fwd, op_bwd)
    return op(q, k, v)


BAND_TILE = 256
BAND_BATCH = 4


def _band_specs(L, d):
    t = min(BAND_TILE, L)
    nt = L // t
    cur = lambda n, i: (n, i, 0)
    prev = lambda n, i: (n, jnp.maximum(i - 1, 0), 0)
    nxt = lambda n, i: (n, jnp.minimum(i + 1, nt - 1), 0)
    return t, nt, cur, prev, nxt


def _band_mask(i, t, L, half):
    a = lax.broadcasted_iota(jnp.int32, (t, 3 * t), 0)
    b = lax.broadcasted_iota(jnp.int32, (t, 3 * t), 1)
    dist = jnp.abs(b - t - a)
    other = i * t - t + b
    mask = (dist <= half) & (other >= 0) & (other < L)
    return dist.astype(F32), mask


def _band_fwd(q, k, v, slope, half):
    N, L, d = q.shape
    t, nt, cur, prev, nxt = _band_specs(L, d)
    nb = BAND_BATCH
    scale = d ** -0.5

    def body(sl_ref, q_ref, kp, kc, kn, vp, vc, vn, o_ref, lse_ref):
        i = pl.program_id(1)
        kcat = jnp.concatenate([kp[...], kc[...], kn[...]], axis=1)
        vcat = jnp.concatenate([vp[...], vc[...], vn[...]], axis=1)
        s = lax.dot_general(q_ref[...], kcat, (((2,), (2,)), ((0,), (0,))), preferred_element_type=F32) * scale
        dist, mask = _band_mask(i, t, L, half)
        s = jnp.where(mask[None], s - sl_ref[...] * dist[None], NEG)
        m = jnp.max(s, axis=-1, keepdims=True)
        p = jnp.exp(s - m)
        l = jnp.sum(p, axis=-1, keepdims=True)
        o = lax.dot_general(p.astype(BF16), vcat, (((2,), (1,)), ((0,), (0,))), preferred_element_type=F32)
        o_ref[...] = o / l
        lse_ref[...] = m + jnp.log(l)

    blk = lambda f: pl.BlockSpec((nb, t, d), f)
    return pl.pallas_call(
        body,
        out_shape=[jax.ShapeDtypeStruct((N, L, d), F32), jax.ShapeDtypeStruct((N, L, 1), F32)],
        grid=(N // nb, nt),
        in_specs=[pl.BlockSpec((nb, 1, 1), lambda n, i: (n, 0, 0)), blk(cur),
                  blk(prev), blk(cur), blk(nxt), blk(prev), blk(cur), blk(nxt)],
        out_specs=[blk(cur), pl.BlockSpec((nb, t, 1), cur)],
        compiler_params=_cparams(("parallel", "parallel")),
        name=f"band_fwd_{N}x{L}",
    )(slope, q, k, k, k, v, v, v)


def _band_dq(q, k, v, slope, do, lse, dd, half):
    N, L, d = q.shape
    t, nt, cur, prev, nxt = _band_specs(L, d)
    nb = BAND_BATCH
    scale = d ** -0.5

    def body(sl_ref, q_ref, kp, kc, kn, vp, vc, vn, do_ref, lse_ref, dd_ref, dq_ref):
        i = pl.program_id(1)
        kcat = jnp.concatenate([kp[...], kc[...], kn[...]], axis=1)
        vcat = jnp.concatenate([vp[...], vc[...], vn[...]], axis=1)
        s = lax.dot_general(q_ref[...], kcat, (((2,), (2,)), ((0,), (0,))), preferred_element_type=F32) * scale
        dist, mask = _band_mask(i, t, L, half)
        p = jnp.where(mask[None], jnp.exp(s - sl_ref[...] * dist[None] - lse_ref[...]), 0.0)
        dp = lax.dot_general(do_ref[...].astype(BF16), vcat, (((2,), (2,)), ((0,), (0,))),
                             preferred_element_type=F32)
        ds = (p * (dp - dd_ref[...])).astype(BF16)
        dq = lax.dot_general(ds, kcat, (((2,), (1,)), ((0,), (0,))), preferred_element_type=F32)
        dq_ref[...] = (dq * scale).astype(dq_ref.dtype)

    blk = lambda f: pl.BlockSpec((nb, t, d), f)
    col = pl.BlockSpec((nb, t, 1), cur)
    return pl.pallas_call(
        body,
        out_shape=jax.ShapeDtypeStruct(q.shape, q.dtype),
        grid=(N // nb, nt),
        in_specs=[pl.BlockSpec((nb, 1, 1), lambda n, i: (n, 0, 0)), blk(cur),
                  blk(prev), blk(cur), blk(nxt), blk(prev), blk(cur), blk(nxt), blk(cur), col, col],
        out_specs=blk(cur),
        compiler_params=_cparams(("parallel", "parallel")),
        name=f"band_dq_{N}x{L}",
    )(slope, q, k, k, k, v, v, v, do, lse, dd)


def _band_dkv(q, k, v, slope, do, lse_row, dd_row, half):
    N, L, d = q.shape
    t, nt, cur, prev, nxt = _band_specs(L, d)
    nb = BAND_BATCH
    scale = d ** -0.5
    rcur = lambda n, i: (n, 0, i)
    rprev = lambda n, i: (n, 0, jnp.maximum(i - 1, 0))
    rnxt = lambda n, i: (n, 0, jnp.minimum(i + 1, nt - 1))

    def body(sl_ref, k_ref, v_ref, qp, qc, qn, dop, doc, don, lp, lc, ln, dp_, dc_, dn_, dk_ref, dv_ref):
        i = pl.program_id(1)
        qcat = jnp.concatenate([qp[...], qc[...], qn[...]], axis=1)
        docat = jnp.concatenate([dop[...], doc[...], don[...]], axis=1).astype(BF16)
        lrow = jnp.concatenate([lp[...], lc[...], ln[...]], axis=2)
        drow = jnp.concatenate([dp_[...], dc_[...], dn_[...]], axis=2)
        st = lax.dot_general(k_ref[...], qcat, (((2,), (2,)), ((0,), (0,))), preferred_element_type=F32) * scale
        dist, mask = _band_mask(i, t, L, half)
        pt = jnp.where(mask[None], jnp.exp(st - sl_ref[...] * dist[None] - lrow), 0.0)
        dv = lax.dot_general(pt.astype(BF16), docat, (((2,), (1,)), ((0,), (0,))), preferred_element_type=F32)
        dpt = lax.dot_general(v_ref[...], docat, (((2,), (2,)), ((0,), (0,))), preferred_element_type=F32)
        dst = (pt * (dpt - drow)).astype(BF16)
        dk = lax.dot_general(dst, qcat, (((2,), (1,)), ((0,), (0,))), preferred_element_type=F32)
        dk_ref[...] = (dk * scale).astype(dk_ref.dtype)
        dv_ref[...] = dv.astype(dv_ref.dtype)

    blk = lambda f: pl.BlockSpec((nb, t, d), f)
    row = lambda f: pl.BlockSpec((nb, 1, t), f)
    return pl.pallas_call(
        body,
        out_shape=[jax.ShapeDtypeStruct(k.shape, k.dtype), jax.ShapeDtypeStruct(v.shape, v.dtype)],
        grid=(N // nb, nt),
        in_specs=[pl.BlockSpec((nb, 1, 1), lambda n, i: (n, 0, 0)), blk(cur), blk(cur),
                  blk(prev), blk(cur), blk(nxt), blk(prev), blk(cur), blk(nxt),
                  row(rprev), row(rcur), row(rnxt), row(rprev), row(rcur), row(rnxt)],
        out_specs=[blk(cur), blk(cur)],
        compiler_params=_cparams(("parallel", "parallel")),
        name=f"band_dkv_{N}x{L}",
    )(slope, k, v, q, q, q, do, do, do, lse_row, lse_row, lse_row, dd_row, dd_row, dd_row)


def band_attention(q, k, v, slope, half):
    @jax.custom_vjp
    def op(q, k, v, slope):
        return tuple(_band_fwd(q, k, v, slope, half))

    def op_fwd(q, k, v, slope):
        o, lse = _band_fwd(q, k, v, slope, half)
        return (o, lse), (q, k, v, slope, o, lse)

    def op_bwd(res, cts):
        q, k, v, slope, o, lse = res
        do, dlse = cts
        N, L, _ = q.shape
        dd = _attn_delta(do, o, dlse)
        dq = _band_dq(q, k, v, slope, do, lse, dd, half)
        dk, dv = _band_dkv(q, k, v, slope, do, lse.reshape(N, 1, L), dd.reshape(N, 1, L), half)
        return dq, dk, dv, jnp.zeros_like(slope)

    op.defvjp(op_fwd, op_bwd)
    return op(q, k, v, slope)


def _conv_taps(width):
    return [CONV_PAD - width // 2 + j for j in range(width)]


def _dwconv_call(up, w, width):
    S, C = up.shape[0] - 2 * CONV_PAD, up.shape[1]
    ch = min(CONV_CHUNK, S)
    offs = _conv_taps(width)

    def body(u_ref, w_ref, o_ref):
        def chunk(ci, carry):
            base = pl.multiple_of(ci * ch, ch)
            win = u_ref[pl.ds(base, ch + 2 * CONV_PAD), :]
            acc = jnp.zeros((ch, LANES), F32)
            for j, off in enumerate(offs):
                acc = acc + win[off:off + ch, :] * w_ref[j:j + 1, :]
            o_ref[pl.ds(base, ch), :] = acc
            return carry

        lax.fori_loop(0, S // ch, chunk, 0)

    return pl.pallas_call(
        body,
        out_shape=jax.ShapeDtypeStruct((S, C), F32),
        grid=(C // LANES,),
        in_specs=[pl.BlockSpec((S + 2 * CONV_PAD, LANES), lambda c: (0, c)),
                  pl.BlockSpec((w.shape[0], LANES), lambda c: (0, c))],
        out_specs=pl.BlockSpec((S, LANES), lambda c: (0, c)),
        compiler_params=_cparams(("parallel",)),
        name=f"dwconv{width}_{S}x{C}",
    )(up, w)


def _dwconv_dw_call(up, g, width, w_rows):
    S, C = g.shape
    ch = min(CONV_CHUNK, S)
    offs = _conv_taps(width)

    def body(u_ref, g_ref, o_ref):
        def chunk(ci, accs):
            base = pl.multiple_of(ci * ch, ch)
            win = u_ref[pl.ds(base, ch + 2 * CONV_PAD), :]
            gt = g_ref[pl.ds(base, ch), :]
            return tuple(a + (gt * win[off:off + ch, :]).reshape(ch // 8, 8, LANES).sum(axis=0)
                         for a, off in zip(accs, offs))

        accs = lax.fori_loop(0, S // ch, chunk, tuple(jnp.zeros((8, LANES), F32) for _ in offs))
        o_ref[...] = jnp.zeros_like(o_ref)
        for j, a in enumerate(accs):
            o_ref[j:j + 1, :] = jnp.sum(a, axis=0, keepdims=True)

    return pl.pallas_call(
        body,
        out_shape=jax.ShapeDtypeStruct((w_rows, C), F32),
        grid=(C // LANES,),
        in_specs=[pl.BlockSpec((S + 2 * CONV_PAD, LANES), lambda c: (0, c)),
                  pl.BlockSpec((S, LANES), lambda c: (0, c))],
        out_specs=pl.BlockSpec((w_rows, LANES), lambda c: (0, c)),
        compiler_params=_cparams(("parallel",)),
        name=f"dwconv{width}_dw_{S}x{C}",
    )(up, g)


def _pad_rows(u):
    return jnp.pad(u, ((CONV_PAD, CONV_PAD), (0, 0)))


def _pad_taps(w):
    return jnp.pad(w, ((0, -w.shape[0] % 8), (0, 0)))


def depthwise_conv(u, w):
    width = w.shape[0]

    @jax.custom_vjp
    def op(u, w):
        return _dwconv_call(_pad_rows(u), _pad_taps(w), width)

    def op_fwd(u, w):
        up = _pad_rows(u)
        return _dwconv_call(up, _pad_taps(w), width), (up, w)

    def op_bwd(res, g):
        up, w = res
        du = _dwconv_call(_pad_rows(g), _pad_taps(w[::-1]), width)
        dw = _dwconv_dw_call(up, g, width, _pad_taps(w).shape[0])[:width]
        return du, dw

    op.defvjp(op_fwd, op_bwd)
    return op(u, w)


def _loss_call(y, target):
    S, D = y.shape
    tile = _row_tile(S, D)

    def body(y_ref, t_ref, o_ref):
        d = y_ref[...] - t_ref[...]
        part = jnp.sum(jnp.sum(d * d, axis=1, keepdims=True), axis=0, keepdims=True) * (0.5 / D)

        @pl.when(pl.program_id(0) == 0)
        def _():
            o_ref[...] = jnp.zeros_like(o_ref)

        o_ref[...] += jnp.broadcast_to(part, o_ref.shape)

    out = pl.pallas_call(
        body,
        out_shape=jax.ShapeDtypeStruct((8, LANES), F32),
        grid=(S // tile,),
        in_specs=[pl.BlockSpec((tile, D), lambda i: (i, 0)), pl.BlockSpec((tile, D), lambda i: (i, 0))],
        out_specs=pl.BlockSpec((8, LANES), lambda i: (0, 0)),
        compiler_params=_cparams(("arbitrary",)),
        name="loss_head",
    )(y, target)
    return out[0, 0]


@jax.custom_vjp
def loss_head(y, target):
    return _loss_call(y, target)


def _loss_fwd(y, target):
    return _loss_call(y, target), (y, target)


def _loss_bwd(res, g):
    y, target = res
    inv_d = 1.0 / y.shape[1]

    def fn(y_t, t_t, g_p):
        return ((y_t - t_t) * (g_p * inv_d),)

    (dy,) = _rw_forward(fn, [y, target], [g.reshape(1, 1).astype(F32)], "loss_head_bwd")
    return dy, jnp.zeros_like(target)


loss_head.defvjp(_loss_fwd, _loss_bwd)


ANY = pl.BlockSpec(memory_space=pl.ANY)


def all_gather_chips(mine, name):
    R, C = mine.shape

    def body(in_ref, out_ref, send_sems, recv_sems, local_sem):
        x, y, c = lax.axis_index("x"), lax.axis_index("y"), lax.axis_index("c")
        own = pltpu.make_async_copy(in_ref, out_ref.at[2 * x + y], local_sem)
        own.start()
        peers = [(1 - x, y), (x, 1 - y), (1 - x, 1 - y)]
        sends = []
        for k, (px, py) in enumerate(peers):
            cp = pltpu.make_async_remote_copy(
                src_ref=in_ref, dst_ref=out_ref.at[2 * x + y], send_sem=send_sems.at[k], recv_sem=recv_sems.at[k],
                device_id=(px, py, c), device_id_type=MESH)
            cp.start()
            sends.append(cp)
        for k, (px, py) in enumerate(peers):
            pltpu.make_async_remote_copy(
                src_ref=in_ref, dst_ref=out_ref.at[2 * px + py], send_sem=send_sems.at[k], recv_sem=recv_sems.at[k],
                device_id=(px, py, c), device_id_type=MESH).wait_recv()
        for cp in sends:
            cp.wait_send()
        own.wait()

    return pl.pallas_call(
        body,
        out_shape=jax.ShapeDtypeStruct((N_CHIPS, R, C), mine.dtype),
        in_specs=[ANY],
        out_specs=ANY,
        scratch_shapes=[pltpu.SemaphoreType.DMA((3,)), pltpu.SemaphoreType.DMA((3,)), pltpu.SemaphoreType.DMA],
        name=name,
    )(mine)


def scatter_to_owners(parts):
    _, _, H, C = parts.shape
    flips = [(dx, dy, dc) for dx in (0, 1) for dy in (0, 1) for dc in (0, 1)][1:]

    def body(in_ref, out_ref, send_sems, recv_sems, local_sem):
        x, y, c = lax.axis_index("x"), lax.axis_index("y"), lax.axis_index("c")
        me = 4 * x + 2 * y + c
        own = pltpu.make_async_copy(in_ref.at[2 * x + y, c], out_ref.at[me], local_sem)
        own.start()
        peers = [(1 - x if dx else x, 1 - y if dy else y, 1 - c if dc else c) for dx, dy, dc in flips]
        sends = []
        for k, (px, py, pc) in enumerate(peers):
            cp = pltpu.make_async_remote_copy(
                src_ref=in_ref.at[2 * px + py, pc], dst_ref=out_ref.at[me], send_sem=send_sems.at[k],
                recv_sem=recv_sems.at[k], device_id=(px, py, pc), device_id_type=MESH)
            cp.start()
            sends.append(cp)
        for k, (px, py, pc) in enumerate(peers):
            pltpu.make_async_remote_copy(
                src_ref=in_ref.at[2 * x + y, c], dst_ref=out_ref.at[4 * px + 2 * py + pc], send_sem=send_sems.at[k],
                recv_sem=recv_sems.at[k], device_id=(px, py, pc), device_id_type=MESH).wait_recv()
        for cp in sends:
            cp.wait_send()
        own.wait()

    return pl.pallas_call(
        body,
        out_shape=jax.ShapeDtypeStruct((8, H, C), parts.dtype),
        in_specs=[ANY],
        out_specs=ANY,
        scratch_shapes=[pltpu.SemaphoreType.DMA((7,)), pltpu.SemaphoreType.DMA((7,)), pltpu.SemaphoreType.DMA],
        name="grad_scatter",
    )(parts)


def exchange_with_sibling(mine):
    H, C = mine.shape

    def body(in_ref, out_ref, send_sem, recv_sem, local_sem):
        x, y, c = lax.axis_index("x"), lax.axis_index("y"), lax.axis_index("c")
        own = pltpu.make_async_copy(in_ref, out_ref.at[c], local_sem)
        own.start()
        cp = pltpu.make_async_remote_copy(
            src_ref=in_ref, dst_ref=out_ref.at[c], send_sem=send_sem, recv_sem=recv_sem,
            device_id=(x, y, 1 - c), device_id_type=MESH)
        cp.start()
        pltpu.make_async_remote_copy(
            src_ref=in_ref, dst_ref=out_ref.at[1 - c], send_sem=send_sem, recv_sem=recv_sem,
            device_id=(x, y, 1 - c), device_id_type=MESH).wait_recv()
        cp.wait_send()
        own.wait()

    return pl.pallas_call(
        body,
        out_shape=jax.ShapeDtypeStruct((2, H, C), mine.dtype),
        in_specs=[ANY],
        out_specs=ANY,
        scratch_shapes=[pltpu.SemaphoreType.DMA, pltpu.SemaphoreType.DMA, pltpu.SemaphoreType.DMA],
        name="grad_sibling_exchange",
    )(mine)


def sum_contributions(recv):
    n, H, C = recv.shape
    tile = _div_tile(H, 256, 16)

    def body(r_ref, o_ref):
        acc = r_ref[0].astype(F32)
        for j in range(1, n):
            acc = acc + r_ref[j].astype(F32)
        o_ref[...] = acc

    return pl.pallas_call(
        body,
        out_shape=jax.ShapeDtypeStruct((H, C), F32),
        grid=(H // tile,),
        in_specs=[pl.BlockSpec((n, tile, C), lambda i: (0, i, 0))],
        out_specs=pl.BlockSpec((tile, C), lambda i: (i, 0)),
        compiler_params=_cparams(("parallel",)),
        name="grad_sum",
    )(recv)


def adamw(w, g, m, v):
    R, C = w.shape
    tile = _div_tile(R, 256, 8)
    c1 = 1.0 - ADAM_B1 ** ADAM_STEP
    c2 = 1.0 - ADAM_B2 ** ADAM_STEP

    def body(w_ref, g_ref, m_ref, v_ref, d_ref, nm_ref, nv_ref):
        g_ = g_ref[...]
        nm = ADAM_B1 * m_ref[...] + (1.0 - ADAM_B1) * g_
        nv = ADAM_B2 * v_ref[...] + (1.0 - ADAM_B2) * (g_ * g_)
        d_ref[...] = -ADAM_LR * ((nm / c1) / (jnp.sqrt(nv / c2) + ADAM_EPS) + ADAM_WD * w_ref[...])
        nm_ref[...] = nm
        nv_ref[...] = nv

    spec = pl.BlockSpec((tile, C), lambda i: (i, 0))
    return pl.pallas_call(
        body,
        out_shape=[jax.ShapeDtypeStruct((R, C), F32)] * 3,
        grid=(R // tile,),
        in_specs=[spec] * 4,
        out_specs=[spec] * 3,
        compiler_params=_cparams(("parallel",)),
        name="adamw",
    )(w, g, m, v)


def _leaves(name, arr):
    n_lead = WEIGHT_LAYOUT[name][0]
    lead = arr.shape[:n_lead]
    flat = arr.reshape((-1,) + arr.shape[n_lead:])
    return [flat[i] for i in range(math.prod(lead))]


def _pack_flat(pieces, dtype):
    flat = jnp.concatenate([p.reshape(-1).astype(dtype) for p in pieces])
    unit = PACK_COLS * PACK_ROW_MULT
    flat = jnp.pad(flat, (0, -flat.shape[0] % unit))
    return flat.reshape(-1, PACK_COLS)


def _unpack_flat(buf, shapes):
    lead = buf.shape[:-2]
    flat = buf.reshape(lead + (-1,))
    out, off = [], 0
    for shp in shapes:
        n = math.prod(shp)
        out.append(lax.slice_in_dim(flat, off, off + n, axis=len(lead)).reshape(lead + tuple(shp)))
        off += n
    return out


def gather_weights(shards):
    full = {}
    for dtype, names, call in ((BF16, [n for n in WEIGHT_NAMES if n not in F32_GATHER], "weights_all_gather_bf16"),
                               (F32, list(F32_GATHER), "weights_all_gather_f32")):
        names = [n for n in names if WEIGHT_LAYOUT[n][1] is not None]
        pieces, owners = [], []
        for n in names:
            for leaf in _leaves(n, shards[n]):
                pieces.append(leaf)
                owners.append(n)
        gathered = all_gather_chips(_pack_flat(pieces, dtype), call)
        blocks = _unpack_flat(gathered, [p.shape for p in pieces])
        for n, blk in zip(owners, blocks):
            ax = WEIGHT_LAYOUT[n][1]
            leaf = jnp.concatenate([blk[s] for s in range(N_CHIPS)], axis=ax).astype(F32)
            full.setdefault(n, []).append(leaf)
    for n in WEIGHT_NAMES:
        if WEIGHT_LAYOUT[n][1] is None:
            full[n] = [leaf.astype(F32) for leaf in _leaves(n, shards[n])]
    return full


def reduce_gradients(grads):
    per_chip = []
    for s in range(N_CHIPS):
        pieces = []
        for n in WEIGHT_NAMES:
            ax = WEIGHT_LAYOUT[n][1]
            for leaf in grads[n]:
                if ax is None:
                    pieces.append(leaf)
                else:
                    w = leaf.shape[ax] // N_CHIPS
                    pieces.append(lax.slice_in_dim(leaf, s * w, (s + 1) * w, axis=ax))
        per_chip.append(_pack_flat(pieces, BF16))
    parts = jnp.stack(per_chip)
    R = parts.shape[1]
    recv = scatter_to_owners(parts.reshape(N_CHIPS, 2, R // 2, PACK_COLS))
    both = exchange_with_sibling(sum_contributions(recv))
    return both.reshape(R, PACK_COLS)


def _silu(x):
    return x * jax.nn.sigmoid(x)


def _heads_first(t, heads):
    S = t.shape[0]
    return t.reshape(S, heads, -1).transpose(1, 0, 2)


def _heads_last(t):
    return t.transpose(1, 0, 2).reshape(t.shape[1], -1)


def swiglu_half_step(x, h, wg, wu, wd, g_post, g_next):
    a = mm(h, wg, BF16)
    b = mm(h, wu, BF16)
    (u,) = rowwise(lambda a_, b_: ((_silu(a_.astype(F32)) * b_.astype(F32)).astype(BF16),), [a, b], [], "swiglu_act")
    y = mm(u, wd, F32)
    return residual_norm(x, y, 0.5, g_post, g_next)


def residual_norm(x, y, coef, g_post, g_next):
    if g_next is None:
        (xn,) = rowwise(lambda x_, y_, g1: (x_ + coef * _rms(y_, g1),), [x, y], [g_post], "residual")
        return xn, None

    def fn(x_, y_, g1, g2):
        xn = x_ + coef * _rms(y_, g1)
        return xn, _rms(xn, g2).astype(BF16)

    return rowwise(fn, [x, y], [g_post, g_next], "residual_norm")


def memory_attention(q_mem, memn, w_kv):
    kv = mm(memn, w_kv, BF16)
    n_mem = kv.shape[0]
    kv = kv.reshape(n_mem, 2, MEM_HEADS, HEAD_DIM).transpose(1, 2, 0, 3)
    o = full_attention(_heads_first(q_mem.astype(BF16), MEM_HEADS), kv[0], kv[1], HEAD_DIM ** -0.5)
    return _heads_last(o)


def mla_mixer(h, cos, sin, w_in, q_g, kv_g, w_uq, w_ukv):
    S = h.shape[0]
    a_in = MLA_Q_LORA + MLA_KV_LORA + MLA_ROPE + MEM_WIDTH
    z = mm(h, jnp.pad(w_in, ((0, 0), (0, -a_in % LANES))), F32)
    o1, o2, o3 = MLA_Q_LORA, MLA_Q_LORA + MLA_KV_LORA, MLA_Q_LORA + MLA_KV_LORA + MLA_ROPE
    c_q, c_kv, k_r, q_mem = z[:, :o1], z[:, o1:o2], z[:, o2:o3], z[:, o3:a_in]
    cqn, ckvn = rowwise(lambda a, b, ga, gb: (_rms(a, ga).astype(BF16), _rms(b, gb).astype(BF16)),
                        [c_q, c_kv], [q_g, kv_g], "mla_lora_norm")
    q = mm(cqn, w_uq, F32).reshape(S, MLA_HEADS, MLA_NOPE + MLA_ROPE)
    kv = mm(ckvn, w_ukv, BF16).reshape(S, MLA_HEADS, MLA_NOPE + MLA_V)
    half = MLA_ROPE // 2
    q1 = q[:, :, MLA_NOPE:MLA_NOPE + half].reshape(S, MLA_HEADS * half)
    q2 = q[:, :, MLA_NOPE + half:].reshape(S, MLA_HEADS * half)
    cos_h, sin_h = jnp.tile(cos, (1, MLA_HEADS)), jnp.tile(sin, (1, MLA_HEADS))

    def rope(q1_, q2_, k1_, k2_, ch, sh, c1, s1):
        return ((q1_ * ch - q2_ * sh).astype(BF16), (q1_ * sh + q2_ * ch).astype(BF16),
                (k1_ * c1 - k2_ * s1).astype(BF16), (k1_ * s1 + k2_ * c1).astype(BF16))

    qr1, qr2, kr1, kr2 = rowwise(rope, [q1, q2, k_r[:, :half], k_r[:, half:], cos_h, sin_h, cos, sin], [],
                                 "mla_rope", n_const=4)
    q_cat = jnp.concatenate([q[:, :, :MLA_NOPE].astype(BF16), qr1.reshape(S, MLA_HEADS, half),
                             qr2.reshape(S, MLA_HEADS, half)], axis=-1)
    k_rope = jnp.concatenate([kr1, kr2], axis=-1)
    k_cat = jnp.concatenate([kv[:, :, :MLA_NOPE], jnp.broadcast_to(k_rope[:, None, :], (S, MLA_HEADS, MLA_ROPE))],
                            axis=-1)
    o = full_attention(q_cat.transpose(1, 0, 2), k_cat.transpose(1, 0, 2), kv[:, :, MLA_NOPE:].transpose(1, 0, 2),
                       (MLA_NOPE + MLA_ROPE) ** -0.5)
    return _heads_last(o), q_mem


def dilated_mixer(h, w_in):
    S = h.shape[0]
    n_g = len(DIL_GROUPS)
    qkv_w = n_g * 3 * DIL_HEADS * HEAD_DIM
    z = mm(h, w_in, BF16)
    zd = z[:, :qkv_w].reshape(S, n_g, 3, DIL_HEADS, HEAD_DIM)
    q_mem = z[:, qkv_w:]
    slopes = (2.0 ** (-ALIBI_MAX * (jnp.arange(n_g * DIL_HEADS, dtype=F32) + 1.0) / (n_g * DIL_HEADS)))
    slopes = slopes.reshape(n_g, DIL_HEADS)
    outs, lses = [], []
    for g, (window, dil) in enumerate(DIL_GROUPS):
        L = S // dil

        def sub(t):
            return t.reshape(L, dil, DIL_HEADS, HEAD_DIM).transpose(1, 2, 0, 3).reshape(dil * DIL_HEADS, L, HEAD_DIM)

        slope = (jnp.tile(slopes[g], dil) * dil).reshape(dil * DIL_HEADS, 1, 1)
        o, lse = band_attention(sub(zd[:, g, 0]), sub(zd[:, g, 1]), sub(zd[:, g, 2]), slope, window // (2 * dil))
        outs.append(o.reshape(dil, DIL_HEADS, L, HEAD_DIM).transpose(2, 0, 1, 3).reshape(S * DIL_HEADS, HEAD_DIM))
        lses.append(lse.reshape(dil, DIL_HEADS, L).transpose(2, 0, 1).reshape(S * DIL_HEADS, 1))

    def merge(o0, o1, o2, l0, l1, l2):
        m = jnp.maximum(jnp.maximum(l0, l1), l2)
        e0, e1, e2 = jnp.exp(l0 - m), jnp.exp(l1 - m), jnp.exp(l2 - m)
        return (((e0 * o0 + e1 * o1 + e2 * o2) / (e0 + e1 + e2)).astype(BF16),)

    (o,) = rowwise(merge, outs + lses, [], "dilated_merge")
    return o.reshape(S, DIL_HEADS * HEAD_DIM), q_mem


def conformer_conv_mixer(h, w_in, conv_w, conv_b, ln_g, ln_b):
    z = mm(h, w_in, F32)
    a, gate, q_mem = z[:, :CONV_CH], z[:, CONV_CH:2 * CONV_CH], z[:, 2 * CONV_CH:]
    (u,) = rowwise(lambda a_, g_: (a_ * jax.nn.sigmoid(g_),), [a, gate], [], "conformer_glu")
    u = depthwise_conv(u, conv_w)

    def post(u_, b, g, beta):
        t = u_ + b
        mu = jnp.mean(t, axis=-1, keepdims=True)
        var = jnp.mean(jnp.square(t - mu), axis=-1, keepdims=True)
        return (_silu((t - mu) * lax.rsqrt(var + EPS) * g + beta).astype(BF16),)

    (o,) = rowwise(post, [u], [conv_b.reshape(1, -1), ln_g.reshape(1, -1), ln_b.reshape(1, -1)], "conformer_post")
    return o, q_mem


def short_conv_mixer(h, w_in, conv_w):
    z = mm(h, w_in, F32)
    bg, cg, hx, q_mem = z[:, :SC_CH], z[:, SC_CH:2 * SC_CH], z[:, 2 * SC_CH:3 * SC_CH], z[:, 3 * SC_CH:]
    (p,) = rowwise(lambda c_, h_: (c_ * h_,), [cg, hx], [], "shortconv_pre")
    cv = depthwise_conv(p, conv_w)
    (o,) = rowwise(lambda b_, c_: ((b_ * c_).astype(BF16),), [bg, cv], [], "shortconv_post")
    return o, q_mem


def local_loss(W, x, mem, cos, sin, target):
    norm_g = W['norm_g'][0]

    def gain(i, k):
        return norm_g[i, k].reshape(1, -1)

    (h,) = rowwise(lambda x_, g: (_rms(x_, g).astype(BF16),), [x], [gain(0, 0)], "input_norm")
    for i in range(DEPTH):
        x, h = swiglu_half_step(x, h, W['ffn_w_gate'][2 * i], W['ffn_w_up'][2 * i], W['ffn_w_down'][2 * i],
                                gain(i, 1), gain(i, 2))
        if i == 0:
            o, q_mem = mla_mixer(h, cos, sin, W['a_w_in'][0], W['a_q_norm'][0].reshape(1, -1),
                                 W['a_kv_norm'][0].reshape(1, -1), W['a_w_uq'][0], W['a_w_ukv'][0])
            w_out = W['a_w_out'][0]
        elif i == 1:
            o, q_mem = dilated_mixer(h, W['b_w_in'][0])
            w_out = W['b_w_out'][0]
        elif i == 2:
            o, q_mem = conformer_conv_mixer(h, W['c_w_in'][0], W['c_conv_w'][0], W['c_conv_b'][0], W['c_ln_g'][0],
                                            W['c_ln_b'][0])
            w_out = W['c_w_out'][0]
        else:
            o, q_mem = short_conv_mixer(h, W['d_w_in'][0], W['d_conv_w'][0])
            w_out = W['d_w_out'][0]
        (memn,) = rowwise(lambda m_, g: (_rms(m_, g).astype(BF16),), [mem], [gain(i, 6)], "memory_norm")
        mo = memory_attention(q_mem, memn, W['mem_w_kv'][i])
        y = mm(jnp.concatenate([o, mo], axis=-1), w_out, F32)
        x, h = residual_norm(x, y, 1.0, gain(i, 3), gain(i, 4))
        x, h = swiglu_half_step(x, h, W['ffn_w_gate'][2 * i + 1], W['ffn_w_up'][2 * i + 1],
                                W['ffn_w_down'][2 * i + 1], gain(i, 5), gain(i + 1, 0) if i + 1 < DEPTH else None)
    return loss_head(x, target)


def kernel(x, mem, positions, norm_g, ffn_w_gate, ffn_w_up, ffn_w_down, mem_w_kv, a_w_in, a_q_norm, a_kv_norm, a_w_uq, a_w_ukv, a_w_out, b_w_in, b_w_out, c_w_in, c_conv_w, c_conv_b, c_ln_g, c_ln_b, c_w_out, d_w_in, d_conv_w, d_w_out, loss_target, m_norm_g, m_ffn_w_gate, m_ffn_w_up, m_ffn_w_down, m_mem_w_kv, m_a_w_in, m_a_q_norm, m_a_kv_norm, m_a_w_uq, m_a_w_ukv, m_a_w_out, m_b_w_in, m_b_w_out, m_c_w_in, m_c_conv_w, m_c_conv_b, m_c_ln_g, m_c_ln_b, m_c_w_out, m_d_w_in, m_d_conv_w, m_d_w_out, v_norm_g, v_ffn_w_gate, v_ffn_w_up, v_ffn_w_down, v_mem_w_kv, v_a_w_in, v_a_q_norm, v_a_kv_norm, v_a_w_uq, v_a_w_ukv, v_a_w_out, v_b_w_in, v_b_w_out, v_c_w_in, v_c_conv_w, v_c_conv_b, v_c_ln_g, v_c_ln_b, v_c_w_out, v_d_w_in, v_d_conv_w, v_d_w_out):
    w_in = dict(zip(WEIGHT_NAMES, (norm_g, ffn_w_gate, ffn_w_up, ffn_w_down, mem_w_kv, a_w_in, a_q_norm, a_kv_norm, a_w_uq, a_w_ukv, a_w_out, b_w_in, b_w_out, c_w_in, c_conv_w, c_conv_b, c_ln_g, c_ln_b, c_w_out, d_w_in, d_conv_w, d_w_out)))
    m_in = dict(zip(WEIGHT_NAMES, (m_norm_g, m_ffn_w_gate, m_ffn_w_up, m_ffn_w_down, m_mem_w_kv, m_a_w_in, m_a_q_norm, m_a_kv_norm, m_a_w_uq, m_a_w_ukv, m_a_w_out, m_b_w_in, m_b_w_out, m_c_w_in, m_c_conv_w, m_c_conv_b, m_c_ln_g, m_c_ln_b, m_c_w_out, m_d_w_in, m_d_conv_w, m_d_w_out)))
    v_in = dict(zip(WEIGHT_NAMES, (v_norm_g, v_ffn_w_gate, v_ffn_w_up, v_ffn_w_down, v_mem_w_kv, v_a_w_in, v_a_q_norm, v_a_kv_norm, v_a_w_uq, v_a_w_ukv, v_a_w_out, v_b_w_in, v_b_w_out, v_c_w_in, v_c_conv_w, v_c_conv_b, v_c_ln_g, v_c_ln_b, v_c_w_out, v_d_w_in, v_d_conv_w, v_d_w_out)))

    W = gather_weights(w_in)
    half = MLA_ROPE // 2
    inv = ROPE_THETA ** (-jnp.arange(half, dtype=F32) / half)
    ang = positions[0].astype(F32)[:, None] * inv
    loss, (gW, gx) = jax.value_and_grad(local_loss, argnums=(0, 1))(
        W, x[0], mem[0], jnp.cos(ang), jnp.sin(ang), loss_target[0])
    loss = lax.psum(loss, ("x", "y", "c"))

    g_flat = reduce_gradients(gW)
    pack = lambda d: _pack_flat([d[n] for n in WEIGHT_NAMES], F32)
    delta, new_m, new_v = adamw(pack(w_in), g_flat, pack(m_in), pack(v_in))
    shapes = [w_in[n].shape for n in WEIGHT_NAMES]
    outs = [_unpack_flat(buf, shapes) for buf in (g_flat, delta, new_m, new_v)]
    return (loss, gx[None], *outs[0], *outs[1], *outs[2], *outs[3])
```

```python
import functools
import math

import jax
import jax.numpy as jnp
from jax import lax
from jax.experimental import pallas as pl
from jax.experimental.pallas import tpu as pltpu

F32 = jnp.float32
BF16 = jnp.bfloat16
MESH = pl.DeviceIdType.MESH

VMEM_LIMIT_BYTES = 56 * 1024 * 1024
LANES = 128
PACK_ROW_MULT = 512
ROW_TILE_ELEMS = 768 * 1024

HEAD_DIM = 64
MEM_HEADS = 4
MEM_WIDTH = MEM_HEADS * HEAD_DIM
MLA_HEADS = 12
MLA_Q_LORA = 384
MLA_KV_LORA = 256
MLA_NOPE = 64
MLA_ROPE = 32
MLA_V = 64
ROPE_THETA = 10000.0
DIL_GROUPS = ((128, 1), (512, 4), (2048, 16))
DIL_HEADS = 8
ALIBI_MAX = 8.0
CONV_CH = 768
CONV_WIDTH = 31
SC_CH = 768
SC_WIDTH = 3
CONV_PAD = 16
CONV_CHUNK = 256
EPS = 1e-6
NEG = -1e30
DEPTH = 4
LOG2E = 1.4426950408889634
LN2 = 0.6931471805599453

ADAM_LR = 0.001
ADAM_B1 = 0.9
ADAM_B2 = 0.999
ADAM_EPS = 1e-08
ADAM_WD = 0.01
ADAM_STEP = 10

WEIGHT_NAMES = ['norm_g', 'ffn_w_gate', 'ffn_w_up', 'ffn_w_down', 'mem_w_kv', 'a_w_in', 'a_q_norm', 'a_kv_norm',
                'a_w_uq', 'a_w_ukv', 'a_w_out', 'b_w_in', 'b_w_out', 'c_w_in', 'c_conv_w', 'c_conv_b', 'c_ln_g',
                'c_ln_b', 'c_w_out', 'd_w_in', 'd_conv_w', 'd_w_out']
WEIGHT_LAYOUT = {
    'norm_g': (0, 2), 'ffn_w_gate': (2, 1), 'ffn_w_up': (2, 1), 'ffn_w_down': (2, 0), 'mem_w_kv': (1, 0),
    'a_w_in': (1, 0), 'a_q_norm': (1, None), 'a_kv_norm': (1, None), 'a_w_uq': (1, 1), 'a_w_ukv': (1, 1),
    'a_w_out': (1, 0), 'b_w_in': (1, 1), 'b_w_out': (1, 1), 'c_w_in': (1, 1), 'c_conv_w': (1, 1),
    'c_conv_b': (1, 0), 'c_ln_g': (1, 0), 'c_ln_b': (1, 0), 'c_w_out': (1, 0), 'd_w_in': (1, 1),
    'd_conv_w': (1, 1), 'd_w_out': (1, 0),
}
SLAB_TRANSPOSED = ('ffn_w_gate', 'ffn_w_up', 'b_w_in', 'c_w_in', 'd_w_in')
SLAB_ROWS = ('ffn_w_down', 'a_w_out', 'c_w_out', 'd_w_out')
SLAB_NAMES = [n for n in WEIGHT_NAMES if n in SLAB_TRANSPOSED or n in SLAB_ROWS]
MISC_NAMES = [n for n in WEIGHT_NAMES if n not in SLAB_NAMES]
F32_GATHER = ('norm_g', 'c_conv_w', 'c_conv_b', 'c_ln_g', 'c_ln_b', 'd_conv_w')
N_CHIPS = 4


def _cparams(semantics):
    return pltpu.CompilerParams(dimension_semantics=semantics, vmem_limit_bytes=VMEM_LIMIT_BYTES)


def _div_tile(n, cap, mult):
    if n <= cap:
        return n
    for d in range(cap - cap % mult, 0, -mult):
        if n % d == 0:
            return d
    raise ValueError(f"no tile for {n} (cap {cap}, multiple of {mult})")


def _matmul(a, b, mode, out_dtype):
    if mode == 'nn':
        (M, K), N = a.shape, b.shape[1]
    elif mode == 'nt':
        (M, K), N = a.shape, b.shape[0]
    else:
        (K, M), N = a.shape, b.shape[1]
    if mode == 'tn':
        tm, tn, tk = _div_tile(M, 2816, LANES), _div_tile(N, 1536, LANES), _div_tile(K, 512, 16)
    else:
        tm, tn, tk = _div_tile(M, 512, 16), _div_tile(N, 2816, LANES), _div_tile(K, 2816, LANES)
    nk = K // tk
    if mode == 'nn':
        a_spec = pl.BlockSpec((tm, tk), lambda i, j, k: (i, k))
        b_spec = pl.BlockSpec((tk, tn), lambda i, j, k: (k, j))
        dims = (((1,), (0,)), ((), ()))
    elif mode == 'nt':
        a_spec = pl.BlockSpec((tm, tk), lambda i, j, k: (i, k))
        b_spec = pl.BlockSpec((tn, tk), lambda i, j, k: (j, k))
        dims = (((1,), (1,)), ((), ()))
    else:
        a_spec = pl.BlockSpec((tk, tm), lambda i, j, k: (k, i))
        b_spec = pl.BlockSpec((tk, tn), lambda i, j, k: (k, j))
        dims = (((0,), (0,)), ((), ()))

    def body(a_ref, b_ref, o_ref, *acc):
        r = lax.dot_general(a_ref[...].astype(BF16), b_ref[...].astype(BF16), dims, preferred_element_type=F32)
        if nk == 1:
            o_ref[...] = r.astype(o_ref.dtype)
        else:
            k = pl.program_id(2)

            @pl.when(k == 0)
            def _():
                acc[0][...] = r

            @pl.when(k > 0)
            def _():
                acc[0][...] += r

            @pl.when(k == nk - 1)
            def _():
                o_ref[...] = acc[0][...].astype(o_ref.dtype)

    return pl.pallas_call(
        body,
        out_shape=jax.ShapeDtypeStruct((M, N), out_dtype),
        grid=(M // tm, N // tn, nk),
        in_specs=[a_spec, b_spec],
        out_specs=pl.BlockSpec((tm, tn), lambda i, j, k: (i, j)),
        scratch_shapes=[pltpu.VMEM((tm, tn), F32)] if nk > 1 else [],
        compiler_params=_cparams(("parallel", "parallel", "arbitrary")),
        name=f"mm_{mode}_{M}x{K}x{N}",
    )(a, b)


@functools.partial(jax.custom_vjp, nondiff_argnums=(2,))
def mm(a, w, out_dtype):
    return _matmul(a, w.astype(BF16), 'nn', out_dtype)


def _mm_fwd(a, w, out_dtype):
    wb = w.astype(BF16)
    return _matmul(a, wb, 'nn', out_dtype), (a, wb)


def _mm_bwd(out_dtype, res, g):
    a, wb = res
    return _matmul(g, wb, 'nt', a.dtype), _matmul(a, g, 'tn', F32)


mm.defvjp(_mm_fwd, _mm_bwd)


@functools.partial(jax.custom_vjp, nondiff_argnums=(3, 4))
def mm_slab(a, wb, delta, transposed, out_dtype):
    return _matmul(a, wb, 'nt' if transposed else 'nn', out_dtype)


def _mm_slab_fwd(a, wb, delta, transposed, out_dtype):
    return _matmul(a, wb, 'nt' if transposed else 'nn', out_dtype), (a, wb)


def _mm_slab_bwd(transposed, out_dtype, res, g):
    a, wb = res
    if transposed:
        return _matmul(g, wb, 'nn', a.dtype), jnp.zeros_like(wb), _matmul(g, a, 'tn', BF16)
    return _matmul(g, wb, 'nt', a.dtype), jnp.zeros_like(wb), _matmul(a, g, 'tn', BF16)


mm_slab.defvjp(_mm_slab_fwd, _mm_slab_bwd)


def _row_tile(n_rows, widest):
    t = 1 << max(3, int(math.log2(max(8, ROW_TILE_ELEMS // max(widest, 1)))))
    return min(n_rows, min(t, 2048))


def _rw_forward(fn, rows, params, name):
    n_rows = rows[0].shape[0]
    tile = _row_tile(n_rows, max(r.shape[1] for r in rows))
    outs = jax.eval_shape(fn, *[jax.ShapeDtypeStruct((tile, r.shape[1]), r.dtype) for r in rows],
                          *[jax.ShapeDtypeStruct(p.shape, p.dtype) for p in params])
    n_in = len(rows) + len(params)

    def body(*refs):
        res = fn(*[r[...] for r in refs[:n_in]])
        for o_ref, o in zip(refs[n_in:], res):
            o_ref[...] = o

    return pl.pallas_call(
        body,
        out_shape=[jax.ShapeDtypeStruct((n_rows, o.shape[1]), o.dtype) for o in outs],
        grid=(n_rows // tile,),
        in_specs=[pl.BlockSpec((tile, r.shape[1]), lambda i: (i, 0)) for r in rows]
        + [pl.BlockSpec(p.shape, lambda i: (0, 0)) for p in params],
        out_specs=[pl.BlockSpec((tile, o.shape[1]), lambda i: (i, 0)) for o in outs],
        compiler_params=_cparams(("parallel",)),
        name=name,
    )(*rows, *params)


def _rw_backward(fn, rows, params, cts, n_const, name):
    n_rows = rows[0].shape[0]
    tile = _row_tile(n_rows, max([r.shape[1] for r in rows] + [c.shape[1] for c in cts]))
    n_r, n_p, n_c = len(rows), len(params), len(cts)
    n_diff = n_r - n_const

    def body(*refs):
        row_vals = [r[...] for r in refs[:n_r]]
        par_vals = [r[...] for r in refs[n_r:n_r + n_p]]
        ct_vals = tuple(r[...] for r in refs[n_r + n_p:n_r + n_p + n_c])
        out_refs = refs[n_r + n_p + n_c:]

        def f(*diff):
            return fn(*diff[:n_diff], *row_vals[n_diff:], *diff[n_diff:])

        _, vjp = jax.vjp(f, *row_vals[:n_diff], *par_vals)
        grads = vjp(ct_vals)
        for o_ref, g in zip(out_refs[:n_diff], grads[:n_diff]):
            o_ref[...] = g
        i = pl.program_id(0)
        for o_ref, g in zip(out_refs[n_diff:], grads[n_diff:]):
            @pl.when(i == 0)
            def _(o_ref=o_ref, g=g):
                o_ref[...] = g

            @pl.when(i > 0)
            def _(o_ref=o_ref, g=g):
                o_ref[...] += g

    res = pl.pallas_call(
        body,
        out_shape=[jax.ShapeDtypeStruct(r.shape, r.dtype) for r in rows[:n_diff]]
        + [jax.ShapeDtypeStruct(p.shape, p.dtype) for p in params],
        grid=(n_rows // tile,),
        in_specs=[pl.BlockSpec((tile, r.shape[1]), lambda i: (i, 0)) for r in rows]
        + [pl.BlockSpec(p.shape, lambda i: (0, 0)) for p in params]
        + [pl.BlockSpec((tile, c.shape[1]), lambda i: (i, 0)) for c in cts],
        out_specs=[pl.BlockSpec((tile, r.shape[1]), lambda i: (i, 0)) for r in rows[:n_diff]]
        + [pl.BlockSpec(p.shape, lambda i: (0, 0)) for p in params],
        compiler_params=_cparams(("arbitrary",)),
        name=name + "_bwd",
    )(*rows, *params, *cts)
    return list(res[:n_diff]), list(res[n_diff:])


def rowwise(fn, rows, params, name, n_const=0):
    rows, params = list(rows), list(params)

    @jax.custom_vjp
    def op(rows, params):
        return tuple(_rw_forward(fn, rows, params, name))

    def op_fwd(rows, params):
        return tuple(_rw_forward(fn, rows, params, name)), (rows, params)

    def op_bwd(res, cts):
        rows, params = res
        d_rows, d_params = _rw_backward(fn, rows, params, list(cts), n_const, name)
        d_rows = d_rows + [jnp.zeros_like(r) for r in rows[len(rows) - n_const:]]
        return d_rows, d_params

    op.defvjp(op_fwd, op_bwd)
    return op(rows, params)


def _rms(x, g):
    xf = x.astype(F32)
    return xf * lax.rsqrt(jnp.mean(xf * xf, axis=-1, keepdims=True) + EPS) * g


ATTN_TILE = 1024
ATTN_SPAN = 2048
ATTN_CHUNK = 512

NT_DIMS = (((1,), (1,)), ((), ()))


def _attn_tiles(stay, stream):
    span = min(ATTN_SPAN, stream)
    return min(ATTN_TILE, stay), span, min(ATTN_CHUNK, span)


def _flash_fwd(q, k, v, qk_mult):
    N, Lq, dk = q.shape
    Lk, dv = k.shape[1], v.shape[2]
    tq, span, ch = _attn_tiles(Lq, Lk)
    nk = Lk // span

    def body(q_ref, k_ref, v_ref, o_ref, lse_ref, m_s, l_s, acc_s):
        j = pl.program_id(2)

        @pl.when(j == 0)
        def _():
            m_s[...] = jnp.full_like(m_s, NEG)
            l_s[...] = jnp.zeros_like(l_s)
            acc_s[...] = jnp.zeros_like(acc_s)

        qv = q_ref[0]
        s = []
        for c in range(span // ch):
            sc = lax.dot_general(qv, k_ref[0, c * ch:(c + 1) * ch, :], NT_DIMS, preferred_element_type=F32)
            s.append(sc if qk_mult == 1.0 else sc * qk_mult)
        s_max = functools.reduce(jnp.maximum, s)
        m_new = jnp.maximum(m_s[...], jnp.max(s_max, axis=-1, keepdims=True))
        alpha = jnp.exp2(m_s[...] - m_new)
        p_sum, pv = None, None
        for c, sc in enumerate(s):
            p = jnp.exp2(sc - m_new)
            d = jnp.dot(p.astype(BF16), v_ref[0, c * ch:(c + 1) * ch, :], preferred_element_type=F32)
            p_sum, pv = (p, d) if c == 0 else (p_sum + p, pv + d)
        l_s[...] = alpha * l_s[...] + jnp.sum(p_sum, axis=-1, keepdims=True)
        acc_s[...] = alpha * acc_s[...] + pv
        m_s[...] = m_new

        @pl.when(j == nk - 1)
        def _():
            o_ref[0] = (acc_s[...] / l_s[...]).astype(o_ref.dtype)
            lse_ref[0] = m_s[...] + jnp.log(l_s[...]) * LOG2E

    return pl.pallas_call(
        body,
        out_shape=[jax.ShapeDtypeStruct((N, Lq, dv), BF16), jax.ShapeDtypeStruct((N, Lq, 1), F32)],
        grid=(N, Lq // tq, nk),
        in_specs=[pl.BlockSpec((1, tq, dk), lambda n, i, j: (n, i, 0)),
                  pl.BlockSpec((1, span, dk), lambda n, i, j: (n, j, 0)),
                  pl.BlockSpec((1, span, dv), lambda n, i, j: (n, j, 0))],
        out_specs=[pl.BlockSpec((1, tq, dv), lambda n, i, j: (n, i, 0)),
                   pl.BlockSpec((1, tq, 1), lambda n, i, j: (n, i, 0))],
        scratch_shapes=[pltpu.VMEM((tq, 1), F32), pltpu.VMEM((tq, 1), F32), pltpu.VMEM((tq, dv), F32)],
        compiler_params=_cparams(("parallel", "parallel", "arbitrary")),
        name=f"flash_fwd_{N}x{Lq}x{Lk}x{dk}",
    )(q, k, v)


def _flash_bwd(q, k, v, do, lse_row, dd_row, qk_mult, out_mult):
    N, Lq, dk = q.shape
    Lk, dv = k.shape[1], v.shape[2]
    tk, span, ch = _attn_tiles(Lk, Lq)
    nq, nkt = Lq // span, Lk // tk

    def body(q_ref, k_ref, v_ref, do_ref, lse_ref, dd_ref, dq_ref, dk_ref, dv_ref, dq_s, dk_s, dv_s):
        i, j = pl.program_id(1), pl.program_id(2)

        @pl.when((i == 0) & (j == 0))
        def _():
            dq_s[...] = jnp.zeros_like(dq_s)

        @pl.when(j == 0)
        def _():
            dk_s[...] = jnp.zeros_like(dk_s)
            dv_s[...] = jnp.zeros_like(dv_s)

        kv_, vv = k_ref[0], v_ref[0]
        for c in range(span // ch):
            qc = q_ref[0, c * ch:(c + 1) * ch, :]
            doc = do_ref[0, c * ch:(c + 1) * ch, :]
            st = lax.dot_general(kv_, qc, NT_DIMS, preferred_element_type=F32)
            if qk_mult != 1.0:
                st = st * qk_mult
            pt = jnp.exp2(st - lse_ref[0, :, c * ch:(c + 1) * ch])
            dv_s[...] += jnp.dot(pt.astype(BF16), doc, preferred_element_type=F32)
            dpt = lax.dot_general(vv, doc, NT_DIMS, preferred_element_type=F32)
            dst = (pt * (dpt - dd_ref[0, :, c * ch:(c + 1) * ch])).astype(BF16)
            dk_s[...] += jnp.dot(dst, qc, preferred_element_type=F32)
            rows = pl.ds(pl.multiple_of(j * span + c * ch, ch), ch)
            dq_s[rows, :] += lax.dot_general(dst, kv_, (((0,), (0,)), ((), ())), preferred_element_type=F32)

        @pl.when(j == nq - 1)
        def _():
            dk_ref[0] = (dk_s[...] * out_mult).astype(dk_ref.dtype)
            dv_ref[0] = dv_s[...].astype(dv_ref.dtype)

        @pl.when((i == nkt - 1) & (j == nq - 1))
        def _():
            dq_ref[0] = (dq_s[...] * out_mult).astype(dq_ref.dtype)

    return pl.pallas_call(
        body,
        out_shape=[jax.ShapeDtypeStruct(q.shape, q.dtype), jax.ShapeDtypeStruct(k.shape, k.dtype),
                   jax.ShapeDtypeStruct(v.shape, v.dtype)],
        grid=(N, nkt, nq),
        in_specs=[pl.BlockSpec((1, span, dk), lambda n, i, j: (n, j, 0)),
                  pl.BlockSpec((1, tk, dk), lambda n, i, j: (n, i, 0)),
                  pl.BlockSpec((1, tk, dv), lambda n, i, j: (n, i, 0)),
                  pl.BlockSpec((1, span, dv), lambda n, i, j: (n, j, 0)),
                  pl.BlockSpec((1, 1, span), lambda n, i, j: (n, 0, j)),
                  pl.BlockSpec((1, 1, span), lambda n, i, j: (n, 0, j))],
        out_specs=[pl.BlockSpec((1, Lq, dk), lambda n, i, j: (n, 0, 0)),
                   pl.BlockSpec((1, tk, dk), lambda n, i, j: (n, i, 0)),
                   pl.BlockSpec((1, tk, dv), lambda n, i, j: (n, i, 0))],
        scratch_shapes=[pltpu.VMEM((Lq, dk), F32), pltpu.VMEM((tk, dk), F32), pltpu.VMEM((tk, dv), F32)],
        compiler_params=_cparams(("parallel", "arbitrary", "arbitrary")),
        name=f"flash_bwd_{N}x{Lq}x{Lk}x{dk}",
    )(q, k, v, do, lse_row, dd_row)


def _attn_delta(do, o, dlse):
    N, L, dv = o.shape

    def fn(do_t, o_t, dl_t):
        return (jnp.sum(do_t.astype(F32) * o_t.astype(F32), axis=-1, keepdims=True) - dl_t,)

    (dd,) = _rw_forward(fn, [do.reshape(N * L, dv), o.reshape(N * L, dv), dlse.reshape(N * L, 1)], [], "attn_delta")
    return dd.reshape(N, L, 1)


def full_attention(q, k, v, scale, q_prescaled):
    qk_mult = 1.0 if q_prescaled else scale * LOG2E
    grad_mult = LN2 if q_prescaled else scale

    @jax.custom_vjp
    def op(q, k, v):
        return _flash_fwd(q, k, v, qk_mult)[0]

    def op_fwd(q, k, v):
        o, lse = _flash_fwd(q, k, v, qk_mult)
        return o, (q, k, v, o, lse)

    def op_bwd(res, do):
        q, k, v, o, lse = res
        N, Lq, _ = q.shape
        dd = _attn_delta(do, o, jnp.zeros_like(lse))
        return tuple(_flash_bwd(q, k, v, do, lse.reshape(N, 1, Lq), dd.reshape(N, 1, Lq), qk_mult, grad_mult))

    op.defvjp(op_fwd, op_bwd)
    return op(q, k, v)


BAND_TILE = 256
BAND_BATCH = 4


def _band_specs(L, d):
    t = min(BAND_TILE, L)
    nt = L // t
    cur = lambda n, i: (n, i, 0)
    prev = lambda n, i: (n, jnp.maximum(i - 1, 0), 0)
    nxt = lambda n, i: (n, jnp.minimum(i + 1, nt - 1), 0)
    return t, nt, cur, prev, nxt


def _band_mask(i, t, L, half):
    a = lax.broadcasted_iota(jnp.int32, (t, 3 * t), 0)
    b = lax.broadcasted_iota(jnp.int32, (t, 3 * t), 1)
    dist = jnp.abs(b - t - a)
    other = i * t - t + b
    mask = (dist <= half) & (other >= 0) & (other < L)
    return dist.astype(F32), mask


def _band_fwd(q, k, v, slope, half):
    N, L, d = q.shape
    t, nt, cur, prev, nxt = _band_specs(L, d)
    nb = BAND_BATCH
    scale = d ** -0.5

    def body(sl_ref, q_ref, kp, kc, kn, vp, vc, vn, o_ref, lse_ref):
        i = pl.program_id(1)
        kcat = jnp.concatenate([kp[...], kc[...], kn[...]], axis=1)
        vcat = jnp.concatenate([vp[...], vc[...], vn[...]], axis=1)
        s = lax.dot_general(q_ref[...], kcat, (((2,), (2,)), ((0,), (0,))), preferred_element_type=F32) * scale
        dist, mask = _band_mask(i, t, L, half)
        s = jnp.where(mask[None], s - sl_ref[...] * dist[None], NEG)
        m = jnp.max(s, axis=-1, keepdims=True)
        p = jnp.exp(s - m)
        l = jnp.sum(p, axis=-1, keepdims=True)
        o = lax.dot_general(p.astype(BF16), vcat, (((2,), (1,)), ((0,), (0,))), preferred_element_type=F32)
        o_ref[...] = o / l
        lse_ref[...] = m + jnp.log(l)

    blk = lambda f: pl.BlockSpec((nb, t, d), f)
    return pl.pallas_call(
        body,
        out_shape=[jax.ShapeDtypeStruct((N, L, d), F32), jax.ShapeDtypeStruct((N, L, 1), F32)],
        grid=(N // nb, nt),
        in_specs=[pl.BlockSpec((nb, 1, 1), lambda n, i: (n, 0, 0)), blk(cur),
                  blk(prev), blk(cur), blk(nxt), blk(prev), blk(cur), blk(nxt)],
        out_specs=[blk(cur), pl.BlockSpec((nb, t, 1), cur)],
        compiler_params=_cparams(("parallel", "parallel")),
        name=f"band_fwd_{N}x{L}",
    )(slope, q, k, k, k, v, v, v)


def _band_dq(q, k, v, slope, do, lse, dd, half):
    N, L, d = q.shape
    t, nt, cur, prev, nxt = _band_specs(L, d)
    nb = BAND_BATCH
    scale = d ** -0.5

    def body(sl_ref, q_ref, kp, kc, kn, vp, vc, vn, do_ref, lse_ref, dd_ref, dq_ref):
        i = pl.program_id(1)
        kcat = jnp.concatenate([kp[...], kc[...], kn[...]], axis=1)
        vcat = jnp.concatenate([vp[...], vc[...], vn[...]], axis=1)
        s = lax.dot_general(q_ref[...], kcat, (((2,), (2,)), ((0,), (0,))), preferred_element_type=F32) * scale
        dist, mask = _band_mask(i, t, L, half)
        p = jnp.where(mask[None], jnp.exp(s - sl_ref[...] * dist[None] - lse_ref[...]), 0.0)
        dp = lax.dot_general(do_ref[...].astype(BF16), vcat, (((2,), (2,)), ((0,), (0,))),
                             preferred_element_type=F32)
        ds = (p * (dp - dd_ref[...])).astype(BF16)
        dq = lax.dot_general(ds, kcat, (((2,), (1,)), ((0,), (0,))), preferred_element_type=F32)
        dq_ref[...] = (dq * scale).astype(dq_ref.dtype)

    blk = lambda f: pl.BlockSpec((nb, t, d), f)
    col = pl.BlockSpec((nb, t, 1), cur)
    return pl.pallas_call(
        body,
        out_shape=jax.ShapeDtypeStruct(q.shape, q.dtype),
        grid=(N // nb, nt),
        in_specs=[pl.BlockSpec((nb, 1, 1), lambda n, i: (n, 0, 0)), blk(cur),
                  blk(prev), blk(cur), blk(nxt), blk(prev), blk(cur), blk(nxt), blk(cur), col, col],
        out_specs=blk(cur),
        compiler_params=_cparams(("parallel", "parallel")),
        name=f"band_dq_{N}x{L}",
    )(slope, q, k, k, k, v, v, v, do, lse, dd)


def _band_dkv(q, k, v, slope, do, lse_row, dd_row, half):
    N, L, d = q.shape
    t, nt, cur, prev, nxt = _band_specs(L, d)
    nb = BAND_BATCH
    scale = d ** -0.5
    rcur = lambda n, i: (n, 0, i)
    rprev = lambda n, i: (n, 0, jnp.maximum(i - 1, 0))
    rnxt = lambda n, i: (n, 0, jnp.minimum(i + 1, nt - 1))

    def body(sl_ref, k_ref, v_ref, qp, qc, qn, dop, doc, don, lp, lc, ln, dp_, dc_, dn_, dk_ref, dv_ref):
        i = pl.program_id(1)
        qcat = jnp.concatenate([qp[...], qc[...], qn[...]], axis=1)
        docat = jnp.concatenate([dop[...], doc[...], don[...]], axis=1).astype(BF16)
        lrow = jnp.concatenate([lp[...], lc[...], ln[...]], axis=2)
        drow = jnp.concatenate([dp_[...], dc_[...], dn_[...]], axis=2)
        st = lax.dot_general(k_ref[...], qcat, (((2,), (2,)), ((0,), (0,))), preferred_element_type=F32) * scale
        dist, mask = _band_mask(i, t, L, half)
        pt = jnp.where(mask[None], jnp.exp(st - sl_ref[...] * dist[None] - lrow), 0.0)
        dv = lax.dot_general(pt.astype(BF16), docat, (((2,), (1,)), ((0,), (0,))), preferred_element_type=F32)
        dpt = lax.dot_general(v_ref[...], docat, (((2,), (2,)), ((0,), (0,))), preferred_element_type=F32)
        dst = (pt * (dpt - drow)).astype(BF16)
        dk = lax.dot_general(dst, qcat, (((2,), (1,)), ((0,), (0,))), preferred_element_type=F32)
        dk_ref[...] = (dk * scale).astype(dk_ref.dtype)
        dv_ref[...] = dv.astype(dv_ref.dtype)

    blk = lambda f: pl.BlockSpec((nb, t, d), f)
    row = lambda f: pl.BlockSpec((nb, 1, t), f)
    return pl.pallas_call(
        body,
        out_shape=[jax.ShapeDtypeStruct(k.shape, k.dtype), jax.ShapeDtypeStruct(v.shape, v.dtype)],
        grid=(N // nb, nt),
        in_specs=[pl.BlockSpec((nb, 1, 1), lambda n, i: (n, 0, 0)), blk(cur), blk(cur),
                  blk(prev), blk(cur), blk(nxt), blk(prev), blk(cur), blk(nxt),
                  row(rprev), row(rcur), row(rnxt), row(rprev), row(rcur), row(rnxt)],
        out_specs=[blk(cur), blk(cur)],
        compiler_params=_cparams(("parallel", "parallel")),
        name=f"band_dkv_{N}x{L}",
    )(slope, k, v, q, q, q, do, do, do, lse_row, lse_row, lse_row, dd_row, dd_row, dd_row)


def band_attention(q, k, v, slope, half):
    @jax.custom_vjp
    def op(q, k, v, slope):
        return tuple(_band_fwd(q, k, v, slope, half))

    def op_fwd(q, k, v, slope):
        o, lse = _band_fwd(q, k, v, slope, half)
        return (o, lse), (q, k, v, slope, o, lse)

    def op_bwd(res, cts):
        q, k, v, slope, o, lse = res
        do, dlse = cts
        N, L, _ = q.shape
        dd = _attn_delta(do, o, dlse)
        dq = _band_dq(q, k, v, slope, do, lse, dd, half)
        dk, dv = _band_dkv(q, k, v, slope, do, lse.reshape(N, 1, L), dd.reshape(N, 1, L), half)
        return dq, dk, dv, jnp.zeros_like(slope)

    op.defvjp(op_fwd, op_bwd)
    return op(q, k, v, slope)


def _conv_taps(width):
    return [CONV_PAD - width // 2 + j for j in range(width)]


def _dwconv_call(up, w, width):
    S, C = up.shape[0] - 2 * CONV_PAD, up.shape[1]
    ch = min(CONV_CHUNK, S)
    offs = _conv_taps(width)

    def body(u_ref, w_ref, o_ref):
        def chunk(ci, carry):
            base = pl.multiple_of(ci * ch, ch)
            win = u_ref[pl.ds(base, ch + 2 * CONV_PAD), :]
            acc = jnp.zeros((ch, LANES), F32)
            for j, off in enumerate(offs):
                acc = acc + win[off:off + ch, :] * w_ref[j:j + 1, :]
            o_ref[pl.ds(base, ch), :] = acc
            return carry

        lax.fori_loop(0, S // ch, chunk, 0)

    return pl.pallas_call(
        body,
        out_shape=jax.ShapeDtypeStruct((S, C), F32),
        grid=(C // LANES,),
        in_specs=[pl.BlockSpec((S + 2 * CONV_PAD, LANES), lambda c: (0, c)),
                  pl.BlockSpec((w.shape[0], LANES), lambda c: (0, c))],
        out_specs=pl.BlockSpec((S, LANES), lambda c: (0, c)),
        compiler_params=_cparams(("parallel",)),
        name=f"dwconv{width}_{S}x{C}",
    )(up, w)


def _dwconv_dw_call(up, g, width, w_rows):
    S, C = g.shape
    ch = min(CONV_CHUNK, S)
    offs = _conv_taps(width)

    def body(u_ref, g_ref, o_ref):
        def chunk(ci, accs):
            base = pl.multiple_of(ci * ch, ch)
            win = u_ref[pl.ds(base, ch + 2 * CONV_PAD), :]
            gt = g_ref[pl.ds(base, ch), :]
            return tuple(a + (gt * win[off:off + ch, :]).reshape(ch // 8, 8, LANES).sum(axis=0)
                         for a, off in zip(accs, offs))

        accs = lax.fori_loop(0, S // ch, chunk, tuple(jnp.zeros((8, LANES), F32) for _ in offs))
        o_ref[...] = jnp.zeros_like(o_ref)
        for j, a in enumerate(accs):
            o_ref[j:j + 1, :] = jnp.sum(a, axis=0, keepdims=True)

    return pl.pallas_call(
        body,
        out_shape=jax.ShapeDtypeStruct((w_rows, C), F32),
        grid=(C // LANES,),
        in_specs=[pl.BlockSpec((S + 2 * CONV_PAD, LANES), lambda c: (0, c)),
                  pl.BlockSpec((S, LANES), lambda c: (0, c))],
        out_specs=pl.BlockSpec((w_rows, LANES), lambda c: (0, c)),
        compiler_params=_cparams(("parallel",)),
        name=f"dwconv{width}_dw_{S}x{C}",
    )(up, g)


def _pad_rows(u):
    return jnp.pad(u, ((CONV_PAD, CONV_PAD), (0, 0)))


def _pad_taps(w):
    return jnp.pad(w, ((0, -w.shape[0] % 8), (0, 0)))


def depthwise_conv(u, w):
    width = w.shape[0]

    @jax.custom_vjp
    def op(u, w):
        return _dwconv_call(_pad_rows(u), _pad_taps(w), width)

    def op_fwd(u, w):
        up = _pad_rows(u)
        return _dwconv_call(up, _pad_taps(w), width), (up, w)

    def op_bwd(res, g):
        up, w = res
        du = _dwconv_call(_pad_rows(g), _pad_taps(w[::-1]), width)
        dw = _dwconv_dw_call(up, g, width, _pad_taps(w).shape[0])[:width]
        return du, dw

    op.defvjp(op_fwd, op_bwd)
    return op(u, w)


def _loss_call(y, target):
    S, D = y.shape
    tile = _row_tile(S, D)

    def body(y_ref, t_ref, o_ref):
        d = y_ref[...] - t_ref[...]
        part = jnp.sum(jnp.sum(d * d, axis=1, keepdims=True), axis=0, keepdims=True) * (0.5 / D)

        @pl.when(pl.program_id(0) == 0)
        def _():
            o_ref[...] = jnp.zeros_like(o_ref)

        o_ref[...] += jnp.broadcast_to(part, o_ref.shape)

    out = pl.pallas_call(
        body,
        out_shape=jax.ShapeDtypeStruct((8, LANES), F32),
        grid=(S // tile,),
        in_specs=[pl.BlockSpec((tile, D), lambda i: (i, 0)), pl.BlockSpec((tile, D), lambda i: (i, 0))],
        out_specs=pl.BlockSpec((8, LANES), lambda i: (0, 0)),
        compiler_params=_cparams(("arbitrary",)),
        name="loss_head",
    )(y, target)
    return out[0, 0]


@jax.custom_vjp
def loss_head(y, target):
    return _loss_call(y, target)


def _loss_fwd(y, target):
    return _loss_call(y, target), (y, target)


def _loss_bwd(res, g):
    y, target = res
    inv_d = 1.0 / y.shape[1]

    def fn(y_t, t_t, g_p):
        return ((y_t - t_t) * (g_p * inv_d),)

    (dy,) = _rw_forward(fn, [y, target], [g.reshape(1, 1).astype(F32)], "loss_head_bwd")
    return dy, jnp.zeros_like(target)


loss_head.defvjp(_loss_fwd, _loss_bwd)


ANY = pl.BlockSpec(memory_space=pl.ANY)


def all_gather_chips(mine, name):
    R, C = mine.shape

    def body(in_ref, out_ref, send_sems, recv_sems, local_sem):
        x, y, c = lax.axis_index("x"), lax.axis_index("y"), lax.axis_index("c")
        own = pltpu.make_async_copy(in_ref, out_ref.at[2 * x + y], local_sem)
        own.start()
        peers = [(1 - x, y), (x, 1 - y), (1 - x, 1 - y)]
        sends = []
        for k, (px, py) in enumerate(peers):
            cp = pltpu.make_async_remote_copy(
                src_ref=in_ref, dst_ref=out_ref.at[2 * x + y], send_sem=send_sems.at[k], recv_sem=recv_sems.at[k],
                device_id=(px, py, c), device_id_type=MESH)
            cp.start()
            sends.append(cp)
        for k, (px, py) in enumerate(peers):
            pltpu.make_async_remote_copy(
                src_ref=in_ref, dst_ref=out_ref.at[2 * px + py], send_sem=send_sems.at[k], recv_sem=recv_sems.at[k],
                device_id=(px, py, c), device_id_type=MESH).wait_recv()
        for cp in sends:
            cp.wait_send()
        own.wait()

    return pl.pallas_call(
        body,
        out_shape=jax.ShapeDtypeStruct((N_CHIPS, R, C), mine.dtype),
        in_specs=[ANY],
        out_specs=ANY,
        scratch_shapes=[pltpu.SemaphoreType.DMA((3,)), pltpu.SemaphoreType.DMA((3,)), pltpu.SemaphoreType.DMA],
        name=name,
    )(mine)


def scatter_to_owners(parts):
    _, _, H, C = parts.shape
    flips = [(dx, dy, dc) for dx in (0, 1) for dy in (0, 1) for dc in (0, 1)][1:]

    def body(in_ref, out_ref, send_sems, recv_sems, local_sem):
        x, y, c = lax.axis_index("x"), lax.axis_index("y"), lax.axis_index("c")
        me = 4 * x + 2 * y + c
        own = pltpu.make_async_copy(in_ref.at[2 * x + y, c], out_ref.at[me], local_sem)
        own.start()
        peers = [(1 - x if dx else x, 1 - y if dy else y, 1 - c if dc else c) for dx, dy, dc in flips]
        sends = []
        for k, (px, py, pc) in enumerate(peers):
            cp = pltpu.make_async_remote_copy(
                src_ref=in_ref.at[2 * px + py, pc], dst_ref=out_ref.at[me], send_sem=send_sems.at[k],
                recv_sem=recv_sems.at[k], device_id=(px, py, pc), device_id_type=MESH)
            cp.start()
            sends.append(cp)
        for k, (px, py, pc) in enumerate(peers):
            pltpu.make_async_remote_copy(
                src_ref=in_ref.at[2 * x + y, c], dst_ref=out_ref.at[4 * px + 2 * py + pc], send_sem=send_sems.at[k],
                recv_sem=recv_sems.at[k], device_id=(px, py, pc), device_id_type=MESH).wait_recv()
        for cp in sends:
            cp.wait_send()
        own.wait()

    return pl.pallas_call(
        body,
        out_shape=jax.ShapeDtypeStruct((8, H, C), parts.dtype),
        in_specs=[ANY],
        out_specs=ANY,
        scratch_shapes=[pltpu.SemaphoreType.DMA((7,)), pltpu.SemaphoreType.DMA((7,)), pltpu.SemaphoreType.DMA],
        name="grad_scatter",
    )(parts)


def exchange_with_sibling(mine):
    H, C = mine.shape

    def body(in_ref, out_ref, send_sem, recv_sem, local_sem):
        x, y, c = lax.axis_index("x"), lax.axis_index("y"), lax.axis_index("c")
        own = pltpu.make_async_copy(in_ref, out_ref.at[c], local_sem)
        own.start()
        cp = pltpu.make_async_remote_copy(
            src_ref=in_ref, dst_ref=out_ref.at[c], send_sem=send_sem, recv_sem=recv_sem,
            device_id=(x, y, 1 - c), device_id_type=MESH)
        cp.start()
        pltpu.make_async_remote_copy(
            src_ref=in_ref, dst_ref=out_ref.at[1 - c], send_sem=send_sem, recv_sem=recv_sem,
            device_id=(x, y, 1 - c), device_id_type=MESH).wait_recv()
        cp.wait_send()
        own.wait()

    return pl.pallas_call(
        body,
        out_shape=jax.ShapeDtypeStruct((2, H, C), mine.dtype),
        in_specs=[ANY],
        out_specs=ANY,
        scratch_shapes=[pltpu.SemaphoreType.DMA, pltpu.SemaphoreType.DMA, pltpu.SemaphoreType.DMA],
        name="grad_sibling_exchange",
    )(mine)


def sum_contributions(recv):
    n, H, C = recv.shape
    tile = _div_tile(H, 256, 16)

    def body(r_ref, o_ref):
        acc = r_ref[0].astype(F32)
        for j in range(1, n):
            acc = acc + r_ref[j].astype(F32)
        o_ref[...] = acc

    return pl.pallas_call(
        body,
        out_shape=jax.ShapeDtypeStruct((H, C), F32),
        grid=(H // tile,),
        in_specs=[pl.BlockSpec((n, tile, C), lambda i: (0, i, 0))],
        out_specs=pl.BlockSpec((tile, C), lambda i: (i, 0)),
        compiler_params=_cparams(("parallel",)),
        name="grad_sum",
    )(recv)


def adamw(w, g, m, v):
    shape = w.shape
    cols = shape[-1]
    rows = math.prod(shape[:-1])
    tile = _div_tile(rows, max(8, 256 * 1024 // cols // 8 * 8), 8)
    c1 = 1.0 - ADAM_B1 ** ADAM_STEP
    c2 = 1.0 - ADAM_B2 ** ADAM_STEP

    def body(w_ref, g_ref, m_ref, v_ref, d_ref, nm_ref, nv_ref):
        g_ = g_ref[...]
        nm = ADAM_B1 * m_ref[...] + (1.0 - ADAM_B1) * g_
        nv = ADAM_B2 * v_ref[...] + (1.0 - ADAM_B2) * (g_ * g_)
        d_ref[...] = -ADAM_LR * ((nm / c1) / (jnp.sqrt(nv / c2) + ADAM_EPS) + ADAM_WD * w_ref[...])
        nm_ref[...] = nm
        nv_ref[...] = nv

    spec = pl.BlockSpec((tile, cols), lambda i: (i, 0))
    outs = pl.pallas_call(
        body,
        out_shape=[jax.ShapeDtypeStruct((rows, cols), F32)] * 3,
        grid=(rows // tile,),
        in_specs=[spec] * 4,
        out_specs=[spec] * 3,
        compiler_params=_cparams(("parallel",)),
        name=f"adamw_{rows}x{cols}",
    )(*[t.reshape(rows, cols) for t in (w, g, m, v)])
    return [o.reshape(shape) for o in outs]


def _leaves(name, arr):
    n_lead = WEIGHT_LAYOUT[name][0]
    lead = arr.shape[:n_lead]
    flat = arr.reshape((-1,) + arr.shape[n_lead:])
    return [flat[i] for i in range(math.prod(lead))]


def _pad_pack_rows(buf):
    return jnp.pad(buf, ((0, -buf.shape[0] % PACK_ROW_MULT), (0, 0)))


def _pack_flat(pieces, dtype, cols):
    flat = jnp.concatenate([p.reshape(-1).astype(dtype) for p in pieces])
    flat = jnp.pad(flat, (0, -flat.shape[0] % (cols * PACK_ROW_MULT)))
    return flat.reshape(-1, cols)


def _unpack_flat(buf, shapes):
    lead = buf.shape[:-2]
    flat = buf.reshape(lead + (-1,))
    out, off = [], 0
    for shp in shapes:
        n = math.prod(shp)
        out.append(lax.slice_in_dim(flat, off, off + n, axis=len(lead)).reshape(lead + tuple(shp)))
        off += n
    return out


def _pack_slabs(slabs, dtype):
    return _pad_pack_rows(jnp.concatenate([s.astype(dtype) for s in slabs], axis=0))


def _local_slabs(name, shard):
    leaves = _leaves(name, shard)
    return [leaf.T for leaf in leaves] if name in SLAB_TRANSPOSED else leaves


def _chip_slice(leaf, ax, s):
    if ax is None:
        return leaf
    w = leaf.shape[ax] // N_CHIPS
    return lax.slice_in_dim(leaf, s * w, (s + 1) * w, axis=ax)


def gather_weights(shards, cols):
    slabs = [s for n in SLAB_NAMES for s in _local_slabs(n, shards[n])]
    slab_owner = [n for n in SLAB_NAMES for _ in _leaves(n, shards[n])]
    slab_buf = _pack_slabs(slabs, BF16)
    misc = {}
    for dtype in (BF16, F32):
        names = [n for n in MISC_NAMES if WEIGHT_LAYOUT[n][1] is not None and (n in F32_GATHER) == (dtype == F32)]
        pieces = [leaf for n in names for leaf in _leaves(n, shards[n])]
        owners = [n for n in names for _ in _leaves(n, shards[n])]
        buf = _pack_flat(pieces, dtype, cols)
        if dtype == BF16:
            gathered = all_gather_chips(jnp.concatenate([slab_buf, buf], axis=0), "weights_all_gather_bf16")
            slab_part, gathered = gathered[:, :slab_buf.shape[0]], gathered[:, slab_buf.shape[0]:]
        else:
            gathered = all_gather_chips(buf, "weights_all_gather_f32")
        for n, blk in zip(owners, _unpack_flat(gathered, [p.shape for p in pieces])):
            ax = WEIGHT_LAYOUT[n][1]
            misc.setdefault(n, []).append(jnp.concatenate([blk[s] for s in range(N_CHIPS)], axis=ax).astype(F32))
    for n in MISC_NAMES:
        if WEIGHT_LAYOUT[n][1] is None:
            misc[n] = [leaf.astype(F32) for leaf in _leaves(n, shards[n])]
    slab_full, off = {}, 0
    for n, s in zip(slab_owner, slabs):
        rows = s.shape[0]
        slab_full.setdefault(n, []).append(slab_part[:, off:off + rows].reshape(N_CHIPS * rows, cols))
        off += rows
    return slab_full, misc


def reduce_gradients(g_slab, g_misc, shards, cols):
    per_chip = []
    for s in range(N_CHIPS):
        slabs = []
        for n in SLAB_NAMES:
            for leaf in g_slab[n]:
                slabs.append(_chip_slice(leaf, 0, s))
        pieces = [_chip_slice(leaf, WEIGHT_LAYOUT[n][1], s) for n in MISC_NAMES for leaf in g_misc[n]]
        per_chip.append(jnp.concatenate([_pack_slabs(slabs, BF16), _pack_flat(pieces, BF16, cols)], axis=0))
    parts = jnp.stack(per_chip)
    R = parts.shape[1]
    recv = scatter_to_owners(parts.reshape(N_CHIPS, 2, R // 2, cols))
    red = exchange_with_sibling(sum_contributions(recv)).reshape(R, cols)

    out, off = {}, 0
    for n in SLAB_NAMES:
        leaves = []
        for slab in _local_slabs(n, shards[n]):
            rows = slab.shape[0]
            g = red[off:off + rows]
            leaves.append(g.T if n in SLAB_TRANSPOSED else g)
            off += rows
        out[n] = jnp.stack(leaves).reshape(shards[n].shape)
    off += -off % PACK_ROW_MULT
    misc_shapes = [shards[n].shape for n in MISC_NAMES]
    for n, g in zip(MISC_NAMES, _unpack_flat(red[off:], misc_shapes)):
        out[n] = g
    return out


def _silu(x):
    return x * jax.nn.sigmoid(x)


def _heads_first(t, heads):
    S = t.shape[0]
    return t.reshape(S, heads, -1).transpose(1, 0, 2)


def _heads_last(t):
    return t.transpose(1, 0, 2).reshape(t.shape[1], -1)


class SlabWeights:
    def __init__(self, values, deltas):
        self.values, self.deltas = values, deltas

    def matmul(self, a, name, leaf, out_dtype):
        return mm_slab(a, self.values[name][leaf], self.deltas[name][leaf], name in SLAB_TRANSPOSED, out_dtype)


def swiglu_half_step(x, h, slab, leaf, g_post, g_next):
    a = slab.matmul(h, 'ffn_w_gate', leaf, BF16)
    b = slab.matmul(h, 'ffn_w_up', leaf, BF16)
    (u,) = rowwise(lambda a_, b_: ((_silu(a_.astype(F32)) * b_.astype(F32)).astype(BF16),), [a, b], [], "swiglu_act")
    y = slab.matmul(u, 'ffn_w_down', leaf, F32)
    return residual_norm(x, y, 0.5, g_post, g_next)


def residual_norm(x, y, coef, g_post, g_next):
    if g_next is None:
        (xn,) = rowwise(lambda x_, y_, g1: (x_ + coef * _rms(y_, g1),), [x, y], [g_post], "residual")
        return xn, None

    def fn(x_, y_, g1, g2):
        xn = x_ + coef * _rms(y_, g1)
        return xn, _rms(xn, g2).astype(BF16)

    return rowwise(fn, [x, y], [g_post, g_next], "residual_norm")


def memory_attention(q_mem, memn, w_kv):
    kv = mm(memn, w_kv, BF16)
    n_mem = kv.shape[0]
    kv = kv.reshape(n_mem, 2, MEM_HEADS, HEAD_DIM).transpose(1, 2, 0, 3)
    o = full_attention(_heads_first(q_mem.astype(BF16), MEM_HEADS), kv[0], kv[1], HEAD_DIM ** -0.5, False)
    return _heads_last(o)


def mla_mixer(h, cos, sin, w_in, q_g, kv_g, w_uq, w_ukv):
    S = h.shape[0]
    a_in = MLA_Q_LORA + MLA_KV_LORA + MLA_ROPE + MEM_WIDTH
    z = mm(h, jnp.pad(w_in, ((0, 0), (0, -a_in % LANES))), F32)
    o1, o2, o3 = MLA_Q_LORA, MLA_Q_LORA + MLA_KV_LORA, MLA_Q_LORA + MLA_KV_LORA + MLA_ROPE
    c_q, c_kv, k_r, q_mem = z[:, :o1], z[:, o1:o2], z[:, o2:o3], z[:, o3:a_in]
    cqn, ckvn = rowwise(lambda a, b, ga, gb: (_rms(a, ga).astype(BF16), _rms(b, gb).astype(BF16)),
                        [c_q, c_kv], [q_g, kv_g], "mla_lora_norm")
    q = mm(cqn, w_uq, F32).reshape(S, MLA_HEADS, MLA_NOPE + MLA_ROPE)
    kv = mm(ckvn, w_ukv, BF16).reshape(S, MLA_HEADS, MLA_NOPE + MLA_V)
    half = MLA_ROPE // 2
    qn = q[:, :, :MLA_NOPE].reshape(S, MLA_HEADS * MLA_NOPE)
    q1 = q[:, :, MLA_NOPE:MLA_NOPE + half].reshape(S, MLA_HEADS * half)
    q2 = q[:, :, MLA_NOPE + half:].reshape(S, MLA_HEADS * half)
    cos_h, sin_h = jnp.tile(cos, (1, MLA_HEADS)), jnp.tile(sin, (1, MLA_HEADS))
    scale = (MLA_NOPE + MLA_ROPE) ** -0.5
    qm = scale * LOG2E

    def rope(qn_, q1_, q2_, k1_, k2_, ch, sh, c1, s1):
        return ((qn_ * qm).astype(BF16), ((q1_ * ch - q2_ * sh) * qm).astype(BF16),
                ((q1_ * sh + q2_ * ch) * qm).astype(BF16),
                (k1_ * c1 - k2_ * s1).astype(BF16), (k1_ * s1 + k2_ * c1).astype(BF16))

    qns, qr1, qr2, kr1, kr2 = rowwise(rope, [qn, q1, q2, k_r[:, :half], k_r[:, half:], cos_h, sin_h, cos, sin], [],
                                      "mla_rope", n_const=4)
    q_cat = jnp.concatenate([qns.reshape(S, MLA_HEADS, MLA_NOPE), qr1.reshape(S, MLA_HEADS, half),
                             qr2.reshape(S, MLA_HEADS, half)], axis=-1)
    k_rope = jnp.concatenate([kr1, kr2], axis=-1)
    k_cat = jnp.concatenate([kv[:, :, :MLA_NOPE], jnp.broadcast_to(k_rope[:, None, :], (S, MLA_HEADS, MLA_ROPE))],
                            axis=-1)
    o = full_attention(q_cat.transpose(1, 0, 2), k_cat.transpose(1, 0, 2), kv[:, :, MLA_NOPE:].transpose(1, 0, 2),
                       scale, True)
    return _heads_last(o), q_mem


def dilated_mixer(h, slab):
    S = h.shape[0]
    n_g = len(DIL_GROUPS)
    qkv_w = n_g * 3 * DIL_HEADS * HEAD_DIM
    z = slab.matmul(h, 'b_w_in', 0, BF16)
    zd = z[:, :qkv_w].reshape(S, n_g, 3, DIL_HEADS, HEAD_DIM)
    q_mem = z[:, qkv_w:]
    slopes = (2.0 ** (-ALIBI_MAX * (jnp.arange(n_g * DIL_HEADS, dtype=F32) + 1.0) / (n_g * DIL_HEADS)))
    slopes = slopes.reshape(n_g, DIL_HEADS)
    outs, lses = [], []
    for g, (window, dil) in enumerate(DIL_GROUPS):
        L = S // dil

        def sub(t):
            return t.reshape(L, dil, DIL_HEADS, HEAD_DIM).transpose(1, 2, 0, 3).reshape(dil * DIL_HEADS, L, HEAD_DIM)

        slope = (jnp.tile(slopes[g], dil) * dil).reshape(dil * DIL_HEADS, 1, 1)
        o, lse = band_attention(sub(zd[:, g, 0]), sub(zd[:, g, 1]), sub(zd[:, g, 2]), slope, window // (2 * dil))
        outs.append(o.reshape(dil, DIL_HEADS, L, HEAD_DIM).transpose(2, 0, 1, 3).reshape(S * DIL_HEADS, HEAD_DIM))
        lses.append(lse.reshape(dil, DIL_HEADS, L).transpose(2, 0, 1).reshape(S * DIL_HEADS, 1))

    def merge(o0, o1, o2, l0, l1, l2):
        m = jnp.maximum(jnp.maximum(l0, l1), l2)
        e0, e1, e2 = jnp.exp(l0 - m), jnp.exp(l1 - m), jnp.exp(l2 - m)
        return (((e0 * o0 + e1 * o1 + e2 * o2) / (e0 + e1 + e2)).astype(BF16),)

    (o,) = rowwise(merge, outs + lses, [], "dilated_merge")
    return o.reshape(S, DIL_HEADS * HEAD_DIM), q_mem


def conformer_conv_mixer(h, slab, conv_w, conv_b, ln_g, ln_b):
    z = slab.matmul(h, 'c_w_in', 0, F32)
    a, gate, q_mem = z[:, :CONV_CH], z[:, CONV_CH:2 * CONV_CH], z[:, 2 * CONV_CH:]
    (u,) = rowwise(lambda a_, g_: (a_ * jax.nn.sigmoid(g_),), [a, gate], [], "conformer_glu")
    u = depthwise_conv(u, conv_w)

    def post(u_, b, g, beta):
        t = u_ + b
        mu = jnp.mean(t, axis=-1, keepdims=True)
        var = jnp.mean(jnp.square(t - mu), axis=-1, keepdims=True)
        return (_silu((t - mu) * lax.rsqrt(var + EPS) * g + beta).astype(BF16),)

    (o,) = rowwise(post, [u], [conv_b.reshape(1, -1), ln_g.reshape(1, -1), ln_b.reshape(1, -1)], "conformer_post")
    return o, q_mem


def short_conv_mixer(h, slab, conv_w):
    z = slab.matmul(h, 'd_w_in', 0, F32)
    bg, cg, hx, q_mem = z[:, :SC_CH], z[:, SC_CH:2 * SC_CH], z[:, 2 * SC_CH:3 * SC_CH], z[:, 3 * SC_CH:]
    (p,) = rowwise(lambda c_, h_: (c_ * h_,), [cg, hx], [], "shortconv_pre")
    cv = depthwise_conv(p, conv_w)
    (o,) = rowwise(lambda b_, c_: ((b_ * c_).astype(BF16),), [bg, cv], [], "shortconv_post")
    return o, q_mem


def local_loss(deltas, W, x, slab_values, mem, cos, sin, target):
    slab = SlabWeights(slab_values, deltas)
    norm_g = W['norm_g'][0]

    def gain(i, k):
        return norm_g[i, k].reshape(1, -1)

    (h,) = rowwise(lambda x_, g: (_rms(x_, g).astype(BF16),), [x], [gain(0, 0)], "input_norm")
    for i in range(DEPTH):
        x, h = swiglu_half_step(x, h, slab, 2 * i, gain(i, 1), gain(i, 2))
        if i == 0:
            o, q_mem = mla_mixer(h, cos, sin, W['a_w_in'][0], W['a_q_norm'][0].reshape(1, -1),
                                 W['a_kv_norm'][0].reshape(1, -1), W['a_w_uq'][0], W['a_w_ukv'][0])
            out_name = 'a_w_out'
        elif i == 1:
            o, q_mem = dilated_mixer(h, slab)
            out_name = None
        elif i == 2:
            o, q_mem = conformer_conv_mixer(h, slab, W['c_conv_w'][0], W['c_conv_b'][0], W['c_ln_g'][0],
                                            W['c_ln_b'][0])
            out_name = 'c_w_out'
        else:
            o, q_mem = short_conv_mixer(h, slab, W['d_conv_w'][0])
            out_name = 'd_w_out'
        (memn,) = rowwise(lambda m_, g: (_rms(m_, g).astype(BF16),), [mem], [gain(i, 6)], "memory_norm")
        mo = memory_attention(q_mem, memn, W['mem_w_kv'][i])
        cat = jnp.concatenate([o, mo], axis=-1)
        y = mm(cat, W['b_w_out'][0], F32) if out_name is None else slab.matmul(cat, out_name, 0, F32)
        x, h = residual_norm(x, y, 1.0, gain(i, 3), gain(i, 4))
        x, h = swiglu_half_step(x, h, slab, 2 * i + 1, gain(i, 5), gain(i + 1, 0) if i + 1 < DEPTH else None)
    return loss_head(x, target)


def kernel(x, mem, positions, norm_g, ffn_w_gate, ffn_w_up, ffn_w_down, mem_w_kv, a_w_in, a_q_norm, a_kv_norm, a_w_uq, a_w_ukv, a_w_out, b_w_in, b_w_out, c_w_in, c_conv_w, c_conv_b, c_ln_g, c_ln_b, c_w_out, d_w_in, d_conv_w, d_w_out, loss_target, m_norm_g, m_ffn_w_gate, m_ffn_w_up, m_ffn_w_down, m_mem_w_kv, m_a_w_in, m_a_q_norm, m_a_kv_norm, m_a_w_uq, m_a_w_ukv, m_a_w_out, m_b_w_in, m_b_w_out, m_c_w_in, m_c_conv_w, m_c_conv_b, m_c_ln_g, m_c_ln_b, m_c_w_out, m_d_w_in, m_d_conv_w, m_d_w_out, v_norm_g, v_ffn_w_gate, v_ffn_w_up, v_ffn_w_down, v_mem_w_kv, v_a_w_in, v_a_q_norm, v_a_kv_norm, v_a_w_uq, v_a_w_ukv, v_a_w_out, v_b_w_in, v_b_w_out, v_c_w_in, v_c_conv_w, v_c_conv_b, v_c_ln_g, v_c_ln_b, v_c_w_out, v_d_w_in, v_d_conv_w, v_d_w_out):
    w_in = dict(zip(WEIGHT_NAMES, (norm_g, ffn_w_gate, ffn_w_up, ffn_w_down, mem_w_kv, a_w_in, a_q_norm, a_kv_norm, a_w_uq, a_w_ukv, a_w_out, b_w_in, b_w_out, c_w_in, c_conv_w, c_conv_b, c_ln_g, c_ln_b, c_w_out, d_w_in, d_conv_w, d_w_out)))
    m_in = dict(zip(WEIGHT_NAMES, (m_norm_g, m_ffn_w_gate, m_ffn_w_up, m_ffn_w_down, m_mem_w_kv, m_a_w_in, m_a_q_norm, m_a_kv_norm, m_a_w_uq, m_a_w_ukv, m_a_w_out, m_b_w_in, m_b_w_out, m_c_w_in, m_c_conv_w, m_c_conv_b, m_c_ln_g, m_c_ln_b, m_c_w_out, m_d_w_in, m_d_conv_w, m_d_w_out)))
    v_in = dict(zip(WEIGHT_NAMES, (v_norm_g, v_ffn_w_gate, v_ffn_w_up, v_ffn_w_down, v_mem_w_kv, v_a_w_in, v_a_q_norm, v_a_kv_norm, v_a_w_uq, v_a_w_ukv, v_a_w_out, v_b_w_in, v_b_w_out, v_c_w_in, v_c_conv_w, v_c_conv_b, v_c_ln_g, v_c_ln_b, v_c_w_out, v_d_w_in, v_d_conv_w, v_d_w_out)))

    d_model = x.shape[-1]
    slab_values, W = gather_weights(w_in, d_model)
    deltas = {n: [jnp.zeros(leaf.shape, BF16) for leaf in leaves] for n, leaves in slab_values.items()}
    half = MLA_ROPE // 2
    inv = ROPE_THETA ** (-jnp.arange(half, dtype=F32) / half)
    ang = positions[0].astype(F32)[:, None] * inv
    loss, (g_slab, g_misc, gx) = jax.value_and_grad(local_loss, argnums=(0, 1, 2))(
        deltas, W, x[0], slab_values, mem[0], jnp.cos(ang), jnp.sin(ang), loss_target[0])
    loss = lax.psum(loss, ("x", "y", "c"))

    grads = reduce_gradients(g_slab, g_misc, w_in, d_model)
    steps = {n: adamw(w_in[n], grads[n], m_in[n], v_in[n]) for n in WEIGHT_NAMES}
    return (loss, gx[None], *[grads[n] for n in WEIGHT_NAMES], *[steps[n][0] for n in WEIGHT_NAMES],
            *[steps[n][1] for n in WEIGHT_NAMES], *[steps[n][2] for n in WEIGHT_NAMES])
```

```python
import functools
import math

import jax
import jax.numpy as jnp
from jax import lax
from jax.experimental import pallas as pl
from jax.experimental.pallas import tpu as pltpu

F32 = jnp.float32
BF16 = jnp.bfloat16
MESH = pl.DeviceIdType.MESH

VMEM_LIMIT_BYTES = 56 * 1024 * 1024
LANES = 128
PACK_ROW_MULT = 512
ROW_TILE_ELEMS = 768 * 1024

HEAD_DIM = 64
MEM_HEADS = 4
MEM_WIDTH = MEM_HEADS * HEAD_DIM
MLA_HEADS = 12
MLA_Q_LORA = 384
MLA_KV_LORA = 256
MLA_NOPE = 64
MLA_ROPE = 32
MLA_V = 64
ROPE_THETA = 10000.0
DIL_GROUPS = ((128, 1), (512, 4), (2048, 16))
DIL_HEADS = 8
ALIBI_MAX = 8.0
CONV_CH = 768
CONV_WIDTH = 31
SC_CH = 768
SC_WIDTH = 3
CONV_PAD = 16
CONV_CHUNK = 256
EPS = 1e-6
NEG = -1e30
DEPTH = 4
LOG2E = 1.4426950408889634
LN2 = 0.6931471805599453

ADAM_LR = 0.001
ADAM_B1 = 0.9
ADAM_B2 = 0.999
ADAM_EPS = 1e-08
ADAM_WD = 0.01
ADAM_STEP = 10

WEIGHT_NAMES = ['norm_g', 'ffn_w_gate', 'ffn_w_up', 'ffn_w_down', 'mem_w_kv', 'a_w_in', 'a_q_norm', 'a_kv_norm',
                'a_w_uq', 'a_w_ukv', 'a_w_out', 'b_w_in', 'b_w_out', 'c_w_in', 'c_conv_w', 'c_conv_b', 'c_ln_g',
                'c_ln_b', 'c_w_out', 'd_w_in', 'd_conv_w', 'd_w_out']
WEIGHT_LAYOUT = {
    'norm_g': (0, 2), 'ffn_w_gate': (2, 1), 'ffn_w_up': (2, 1), 'ffn_w_down': (2, 0), 'mem_w_kv': (1, 0),
    'a_w_in': (1, 0), 'a_q_norm': (1, None), 'a_kv_norm': (1, None), 'a_w_uq': (1, 1), 'a_w_ukv': (1, 1),
    'a_w_out': (1, 0), 'b_w_in': (1, 1), 'b_w_out': (1, 1), 'c_w_in': (1, 1), 'c_conv_w': (1, 1),
    'c_conv_b': (1, 0), 'c_ln_g': (1, 0), 'c_ln_b': (1, 0), 'c_w_out': (1, 0), 'd_w_in': (1, 1),
    'd_conv_w': (1, 1), 'd_w_out': (1, 0),
}
SLAB_TRANSPOSED = ('ffn_w_gate', 'ffn_w_up', 'b_w_in', 'c_w_in', 'd_w_in')
SLAB_ROWS = ('ffn_w_down', 'a_w_out', 'c_w_out', 'd_w_out')
SLAB_NAMES = [n for n in WEIGHT_NAMES if n in SLAB_TRANSPOSED or n in SLAB_ROWS]
MISC_NAMES = [n for n in WEIGHT_NAMES if n not in SLAB_NAMES]
F32_GATHER = ('norm_g', 'c_conv_w', 'c_conv_b', 'c_ln_g', 'c_ln_b', 'd_conv_w')
N_CHIPS = 4


def _cparams(semantics):
    return pltpu.CompilerParams(dimension_semantics=semantics, vmem_limit_bytes=VMEM_LIMIT_BYTES)


def _div_tile(n, cap, mult):
    if n <= cap:
        return n
    for d in range(cap - cap % mult, 0, -mult):
        if n % d == 0:
            return d
    raise ValueError(f"no tile for {n} (cap {cap}, multiple of {mult})")


def _matmul(a, b, mode, out_dtype):
    if mode == 'nn':
        (M, K), N = a.shape, b.shape[1]
    elif mode == 'nt':
        (M, K), N = a.shape, b.shape[0]
    else:
        (K, M), N = a.shape, b.shape[1]
    if mode == 'tn':
        tm, tn, tk = _div_tile(M, 2816, LANES), _div_tile(N, 1536, LANES), _div_tile(K, 1024, 16)
    else:
        tm, tn, tk = _div_tile(M, 512, 16), _div_tile(N, 2816, LANES), _div_tile(K, 2816, LANES)
    nk = K // tk
    if mode == 'nn':
        a_spec = pl.BlockSpec((tm, tk), lambda i, j, k: (i, k))
        b_spec = pl.BlockSpec((tk, tn), lambda i, j, k: (k, j))
        dims = (((1,), (0,)), ((), ()))
    elif mode == 'nt':
        a_spec = pl.BlockSpec((tm, tk), lambda i, j, k: (i, k))
        b_spec = pl.BlockSpec((tn, tk), lambda i, j, k: (j, k))
        dims = (((1,), (1,)), ((), ()))
    else:
        a_spec = pl.BlockSpec((tk, tm), lambda i, j, k: (k, i))
        b_spec = pl.BlockSpec((tk, tn), lambda i, j, k: (k, j))
        dims = (((0,), (0,)), ((), ()))

    def body(a_ref, b_ref, o_ref, *acc):
        r = lax.dot_general(a_ref[...].astype(BF16), b_ref[...].astype(BF16), dims, preferred_element_type=F32)
        if nk == 1:
            o_ref[...] = r.astype(o_ref.dtype)
        else:
            k = pl.program_id(2)

            @pl.when(k == 0)
            def _():
                acc[0][...] = r

            @pl.when(k > 0)
            def _():
                acc[0][...] += r

            @pl.when(k == nk - 1)
            def _():
                o_ref[...] = acc[0][...].astype(o_ref.dtype)

    return pl.pallas_call(
        body,
        out_shape=jax.ShapeDtypeStruct((M, N), out_dtype),
        grid=(M // tm, N // tn, nk),
        in_specs=[a_spec, b_spec],
        out_specs=pl.BlockSpec((tm, tn), lambda i, j, k: (i, j)),
        scratch_shapes=[pltpu.VMEM((tm, tn), F32)] if nk > 1 else [],
        compiler_params=_cparams(("parallel", "parallel", "arbitrary")),
        name=f"mm_{mode}_{M}x{K}x{N}",
    )(a, b)


@functools.partial(jax.custom_vjp, nondiff_argnums=(2,))
def mm(a, w, out_dtype):
    return _matmul(a, w.astype(BF16), 'nn', out_dtype)


def _mm_fwd(a, w, out_dtype):
    wb = w.astype(BF16)
    return _matmul(a, wb, 'nn', out_dtype), (a, wb)


def _mm_bwd(out_dtype, res, g):
    a, wb = res
    return _matmul(g, wb, 'nt', a.dtype), _matmul(a, g, 'tn', F32)


mm.defvjp(_mm_fwd, _mm_bwd)


@functools.partial(jax.custom_vjp, nondiff_argnums=(3, 4))
def mm_slab(a, wb, delta, transposed, out_dtype):
    return _matmul(a, wb, 'nt' if transposed else 'nn', out_dtype)


def _mm_slab_fwd(a, wb, delta, transposed, out_dtype):
    return _matmul(a, wb, 'nt' if transposed else 'nn', out_dtype), (a, wb)


def _mm_slab_bwd(transposed, out_dtype, res, g):
    a, wb = res
    if transposed:
        return _matmul(g, wb, 'nn', a.dtype), jnp.zeros_like(wb), _matmul(g, a, 'tn', BF16)
    return _matmul(g, wb, 'nt', a.dtype), jnp.zeros_like(wb), _matmul(a, g, 'tn', BF16)


mm_slab.defvjp(_mm_slab_fwd, _mm_slab_bwd)


def _row_tile(n_rows, widest):
    t = 1 << max(3, int(math.log2(max(8, ROW_TILE_ELEMS // max(widest, 1)))))
    return min(n_rows, min(t, 2048))


def _rw_forward(fn, rows, params, name):
    n_rows = rows[0].shape[0]
    tile = _row_tile(n_rows, max(r.shape[1] for r in rows))
    outs = jax.eval_shape(fn, *[jax.ShapeDtypeStruct((tile, r.shape[1]), r.dtype) for r in rows],
                          *[jax.ShapeDtypeStruct(p.shape, p.dtype) for p in params])
    n_in = len(rows) + len(params)

    def body(*refs):
        res = fn(*[r[...] for r in refs[:n_in]])
        for o_ref, o in zip(refs[n_in:], res):
            o_ref[...] = o

    return pl.pallas_call(
        body,
        out_shape=[jax.ShapeDtypeStruct((n_rows, o.shape[1]), o.dtype) for o in outs],
        grid=(n_rows // tile,),
        in_specs=[pl.BlockSpec((tile, r.shape[1]), lambda i: (i, 0)) for r in rows]
        + [pl.BlockSpec(p.shape, lambda i: (0, 0)) for p in params],
        out_specs=[pl.BlockSpec((tile, o.shape[1]), lambda i: (i, 0)) for o in outs],
        compiler_params=_cparams(("parallel",)),
        name=name,
    )(*rows, *params)


def _rw_backward(fn, rows, params, cts, n_const, name):
    n_rows = rows[0].shape[0]
    tile = _row_tile(n_rows, max([r.shape[1] for r in rows] + [c.shape[1] for c in cts]))
    n_r, n_p, n_c = len(rows), len(params), len(cts)
    n_diff = n_r - n_const

    def body(*refs):
        row_vals = [r[...] for r in refs[:n_r]]
        par_vals = [r[...] for r in refs[n_r:n_r + n_p]]
        ct_vals = tuple(r[...] for r in refs[n_r + n_p:n_r + n_p + n_c])
        out_refs = refs[n_r + n_p + n_c:]

        def f(*diff):
            return fn(*diff[:n_diff], *row_vals[n_diff:], *diff[n_diff:])

        _, vjp = jax.vjp(f, *row_vals[:n_diff], *par_vals)
        grads = vjp(ct_vals)
        for o_ref, g in zip(out_refs[:n_diff], grads[:n_diff]):
            o_ref[...] = g
        i = pl.program_id(0)
        for o_ref, g in zip(out_refs[n_diff:], grads[n_diff:]):
            @pl.when(i == 0)
            def _(o_ref=o_ref, g=g):
                o_ref[...] = g

            @pl.when(i > 0)
            def _(o_ref=o_ref, g=g):
                o_ref[...] += g

    res = pl.pallas_call(
        body,
        out_shape=[jax.ShapeDtypeStruct(r.shape, r.dtype) for r in rows[:n_diff]]
        + [jax.ShapeDtypeStruct(p.shape, p.dtype) for p in params],
        grid=(n_rows // tile,),
        in_specs=[pl.BlockSpec((tile, r.shape[1]), lambda i: (i, 0)) for r in rows]
        + [pl.BlockSpec(p.shape, lambda i: (0, 0)) for p in params]
        + [pl.BlockSpec((tile, c.shape[1]), lambda i: (i, 0)) for c in cts],
        out_specs=[pl.BlockSpec((tile, r.shape[1]), lambda i: (i, 0)) for r in rows[:n_diff]]
        + [pl.BlockSpec(p.shape, lambda i: (0, 0)) for p in params],
        compiler_params=_cparams(("arbitrary",)),
        name=name + "_bwd",
    )(*rows, *params, *cts)
    return list(res[:n_diff]), list(res[n_diff:])


def rowwise(fn, rows, params, name, n_const=0):
    rows, params = list(rows), list(params)

    @jax.custom_vjp
    def op(rows, params):
        return tuple(_rw_forward(fn, rows, params, name))

    def op_fwd(rows, params):
        return tuple(_rw_forward(fn, rows, params, name)), (rows, params)

    def op_bwd(res, cts):
        rows, params = res
        d_rows, d_params = _rw_backward(fn, rows, params, list(cts), n_const, name)
        d_rows = d_rows + [jnp.zeros_like(r) for r in rows[len(rows) - n_const:]]
        return d_rows, d_params

    op.defvjp(op_fwd, op_bwd)
    return op(rows, params)


def _rms(x, g):
    xf = x.astype(F32)
    return xf * lax.rsqrt(jnp.mean(xf * xf, axis=-1, keepdims=True) + EPS) * g


ATTN_TILE = 1024
ATTN_SPAN = 2048
ATTN_CHUNK = 512

NT_DIMS = (((1,), (1,)), ((), ()))


def _attn_tiles(stay, stream):
    span = min(ATTN_SPAN, stream)
    return min(ATTN_TILE, stay), span, min(ATTN_CHUNK, span)


def _flash_fwd(q, k, v, qk_mult):
    N, Lq, dk = q.shape
    Lk, dv = k.shape[1], v.shape[2]
    tq, span, ch = _attn_tiles(Lq, Lk)
    tq = min(Lq, tq * max(1, ATTN_TILE // span))
    nk = Lk // span
    v_ones = jnp.concatenate([v, jnp.ones((N, Lk, 1), v.dtype), jnp.zeros((N, Lk, LANES - dv - 1), v.dtype)], axis=2)

    def body(q_ref, k_ref, v_ref, o_ref, lse_ref, m_s, acc_s):
        j = pl.program_id(2)

        @pl.when(j == 0)
        def _():
            m_s[...] = jnp.full_like(m_s, NEG)
            acc_s[...] = jnp.zeros_like(acc_s)

        qv = q_ref[0]
        s = []
        for c in range(span // ch):
            sc = lax.dot_general(qv, k_ref[0, c * ch:(c + 1) * ch, :], NT_DIMS, preferred_element_type=F32)
            s.append(sc if qk_mult == 1.0 else sc * qk_mult)
        s_max = functools.reduce(jnp.maximum, s)
        m_new = jnp.maximum(m_s[...], jnp.max(s_max, axis=-1, keepdims=True))
        pv = None
        for c, sc in enumerate(s):
            p = jnp.exp2(sc - m_new).astype(BF16)
            d = jnp.dot(p, v_ref[0, c * ch:(c + 1) * ch, :], preferred_element_type=F32)
            pv = d if c == 0 else pv + d
        acc_s[...] = jnp.exp2(m_s[...] - m_new) * acc_s[...] + pv
        m_s[...] = m_new

        @pl.when(j == nk - 1)
        def _():
            acc = acc_s[...]
            l = acc[:, dv:dv + 1]
            o_ref[0] = (acc[:, :dv] / l).astype(o_ref.dtype)
            lse_ref[0] = m_s[...] + jnp.log(l) * LOG2E

    return pl.pallas_call(
        body,
        out_shape=[jax.ShapeDtypeStruct((N, Lq, dv), BF16), jax.ShapeDtypeStruct((N, Lq, 1), F32)],
        grid=(N, Lq // tq, nk),
        in_specs=[pl.BlockSpec((1, tq, dk), lambda n, i, j: (n, i, 0)),
                  pl.BlockSpec((1, span, dk), lambda n, i, j: (n, j, 0)),
                  pl.BlockSpec((1, span, LANES), lambda n, i, j: (n, j, 0))],
        out_specs=[pl.BlockSpec((1, tq, dv), lambda n, i, j: (n, i, 0)),
                   pl.BlockSpec((1, tq, 1), lambda n, i, j: (n, i, 0))],
        scratch_shapes=[pltpu.VMEM((tq, 1), F32), pltpu.VMEM((tq, LANES), F32)],
        compiler_params=_cparams(("parallel", "parallel", "arbitrary")),
        name=f"flash_fwd_{N}x{Lq}x{Lk}x{dk}",
    )(q, k, v_ones)


def _flash_bwd(q, k, v, do, lse_row, dd_row, qk_mult, out_mult):
    N, Lq, dk = q.shape
    Lk, dv = k.shape[1], v.shape[2]
    tk, span, ch = _attn_tiles(Lk, Lq)
    nq, nkt = Lq // span, Lk // tk

    def body(q_ref, k_ref, v_ref, do_ref, lse_ref, dd_ref, dq_ref, dk_ref, dv_ref, dq_s, dk_s, dv_s):
        i, j = pl.program_id(1), pl.program_id(2)

        @pl.when((i == 0) & (j == 0))
        def _():
            dq_s[...] = jnp.zeros_like(dq_s)

        @pl.when(j == 0)
        def _():
            dk_s[...] = jnp.zeros_like(dk_s)
            dv_s[...] = jnp.zeros_like(dv_s)

        kv_, vv = k_ref[0], v_ref[0]
        for c in range(span // ch):
            qc = q_ref[0, c * ch:(c + 1) * ch, :]
            doc = do_ref[0, c * ch:(c + 1) * ch, :]
            st = lax.dot_general(kv_, qc, NT_DIMS, preferred_element_type=F32)
            if qk_mult != 1.0:
                st = st * qk_mult
            pt = jnp.exp2(st - lse_ref[0, :, c * ch:(c + 1) * ch])
            dv_s[...] += jnp.dot(pt.astype(BF16), doc, preferred_element_type=F32)
            dpt = lax.dot_general(vv, doc, NT_DIMS, preferred_element_type=F32)
            dst = (pt * (dpt - dd_ref[0, :, c * ch:(c + 1) * ch])).astype(BF16)
            dk_s[...] += jnp.dot(dst, qc, preferred_element_type=F32)
            rows = pl.ds(pl.multiple_of(j * span + c * ch, ch), ch)
            dq_s[rows, :] += lax.dot_general(dst, kv_, (((0,), (0,)), ((), ())), preferred_element_type=F32)

        @pl.when(j == nq - 1)
        def _():
            dk_ref[0] = (dk_s[...] * out_mult).astype(dk_ref.dtype)
            dv_ref[0] = dv_s[...].astype(dv_ref.dtype)

        @pl.when((i == nkt - 1) & (j == nq - 1))
        def _():
            dq_ref[0] = (dq_s[...] * out_mult).astype(dq_ref.dtype)

    return pl.pallas_call(
        body,
        out_shape=[jax.ShapeDtypeStruct(q.shape, q.dtype), jax.ShapeDtypeStruct(k.shape, k.dtype),
                   jax.ShapeDtypeStruct(v.shape, v.dtype)],
        grid=(N, nkt, nq),
        in_specs=[pl.BlockSpec((1, span, dk), lambda n, i, j: (n, j, 0)),
                  pl.BlockSpec((1, tk, dk), lambda n, i, j: (n, i, 0)),
                  pl.BlockSpec((1, tk, dv), lambda n, i, j: (n, i, 0)),
                  pl.BlockSpec((1, span, dv), lambda n, i, j: (n, j, 0)),
                  pl.BlockSpec((1, 1, span), lambda n, i, j: (n, 0, j)),
                  pl.BlockSpec((1, 1, span), lambda n, i, j: (n, 0, j))],
        out_specs=[pl.BlockSpec((1, Lq, dk), lambda n, i, j: (n, 0, 0)),
                   pl.BlockSpec((1, tk, dk), lambda n, i, j: (n, i, 0)),
                   pl.BlockSpec((1, tk, dv), lambda n, i, j: (n, i, 0))],
        scratch_shapes=[pltpu.VMEM((Lq, dk), F32), pltpu.VMEM((tk, dk), F32), pltpu.VMEM((tk, dv), F32)],
        compiler_params=_cparams(("parallel", "arbitrary", "arbitrary")),
        name=f"flash_bwd_{N}x{Lq}x{Lk}x{dk}",
    )(q, k, v, do, lse_row, dd_row)


def _attn_delta(do, o, dlse):
    N, L, dv = o.shape

    def fn(do_t, o_t, dl_t):
        return (jnp.sum(do_t.astype(F32) * o_t.astype(F32), axis=-1, keepdims=True) - dl_t,)

    (dd,) = _rw_forward(fn, [do.reshape(N * L, dv), o.reshape(N * L, dv), dlse.reshape(N * L, 1)], [], "attn_delta")
    return dd.reshape(N, L, 1)


def full_attention(q, k, v, scale, q_prescaled):
    qk_mult = 1.0 if q_prescaled else scale * LOG2E
    grad_mult = LN2 if q_prescaled else scale

    @jax.custom_vjp
    def op(q, k, v):
        return _flash_fwd(q, k, v, qk_mult)[0]

    def op_fwd(q, k, v):
        o, lse = _flash_fwd(q, k, v, qk_mult)
        return o, (q, k, v, o, lse)

    def op_bwd(res, do):
        q, k, v, o, lse = res
        N, Lq, _ = q.shape
        dd = _attn_delta(do, o, jnp.zeros_like(lse))
        return tuple(_flash_bwd(q, k, v, do, lse.reshape(N, 1, Lq), dd.reshape(N, 1, Lq), qk_mult, grad_mult))

    op.defvjp(op_fwd, op_bwd)
    return op(q, k, v)


BAND_TILE = 128
BAND_BATCH = 8


def _band_specs(L, d):
    t = min(BAND_TILE, L)
    nt = L // t
    cur = lambda n, i: (n, i, 0)
    prev = lambda n, i: (n, jnp.maximum(i - 1, 0), 0)
    nxt = lambda n, i: (n, jnp.minimum(i + 1, nt - 1), 0)
    return t, nt, cur, prev, nxt


def _band_mask(i, t, L, half):
    a = lax.broadcasted_iota(jnp.int32, (t, 3 * t), 0)
    b = lax.broadcasted_iota(jnp.int32, (t, 3 * t), 1)
    dist = jnp.abs(b - t - a)
    other = i * t - t + b
    mask = (dist <= half) & (other >= 0) & (other < L)
    return dist.astype(F32), mask


def _band_fwd(q, k, v, slope, half):
    N, L, d = q.shape
    t, nt, cur, prev, nxt = _band_specs(L, d)
    nb = BAND_BATCH
    scale = d ** -0.5

    def body(sl_ref, q_ref, kp, kc, kn, vp, vc, vn, o_ref, lse_ref):
        i = pl.program_id(1)
        kcat = jnp.concatenate([kp[...], kc[...], kn[...]], axis=1)
        vcat = jnp.concatenate([vp[...], vc[...], vn[...]], axis=1)
        s = lax.dot_general(q_ref[...], kcat, (((2,), (2,)), ((0,), (0,))), preferred_element_type=F32) * scale
        dist, mask = _band_mask(i, t, L, half)
        s = jnp.where(mask[None], s - sl_ref[...] * dist[None], NEG)
        m = jnp.max(s, axis=-1, keepdims=True)
        p = jnp.exp(s - m)
        l = jnp.sum(p, axis=-1, keepdims=True)
        o = lax.dot_general(p.astype(BF16), vcat, (((2,), (1,)), ((0,), (0,))), preferred_element_type=F32)
        o_ref[...] = o / l
        lse_ref[...] = m + jnp.log(l)

    blk = lambda f: pl.BlockSpec((nb, t, d), f)
    return pl.pallas_call(
        body,
        out_shape=[jax.ShapeDtypeStruct((N, L, d), F32), jax.ShapeDtypeStruct((N, L, 1), F32)],
        grid=(N // nb, nt),
        in_specs=[pl.BlockSpec((nb, 1, 1), lambda n, i: (n, 0, 0)), blk(cur),
                  blk(prev), blk(cur), blk(nxt), blk(prev), blk(cur), blk(nxt)],
        out_specs=[blk(cur), pl.BlockSpec((nb, t, 1), cur)],
        compiler_params=_cparams(("parallel", "parallel")),
        name=f"band_fwd_{N}x{L}",
    )(slope, q, k, k, k, v, v, v)


def _band_dq(q, k, v, slope, do, lse, dd, half):
    N, L, d = q.shape
    t, nt, cur, prev, nxt = _band_specs(L, d)
    nb = BAND_BATCH
    scale = d ** -0.5

    def body(sl_ref, q_ref, kp, kc, kn, vp, vc, vn, do_ref, lse_ref, dd_ref, dq_ref):
        i = pl.program_id(1)
        kcat = jnp.concatenate([kp[...], kc[...], kn[...]], axis=1)
        vcat = jnp.concatenate([vp[...], vc[...], vn[...]], axis=1)
        s = lax.dot_general(q_ref[...], kcat, (((2,), (2,)), ((0,), (0,))), preferred_element_type=F32) * scale
        dist, mask = _band_mask(i, t, L, half)
        p = jnp.where(mask[None], jnp.exp(s - sl_ref[...] * dist[None] - lse_ref[...]), 0.0)
        dp = lax.dot_general(do_ref[...].astype(BF16), vcat, (((2,), (2,)), ((0,), (0,))),
                             preferred_element_type=F32)
        ds = (p * (dp - dd_ref[...])).astype(BF16)
        dq = lax.dot_general(ds, kcat, (((2,), (1,)), ((0,), (0,))), preferred_element_type=F32)
        dq_ref[...] = (dq * scale).astype(dq_ref.dtype)

    blk = lambda f: pl.BlockSpec((nb, t, d), f)
    col = pl.BlockSpec((nb, t, 1), cur)
    return pl.pallas_call(
        body,
        out_shape=jax.ShapeDtypeStruct(q.shape, q.dtype),
        grid=(N // nb, nt),
        in_specs=[pl.BlockSpec((nb, 1, 1), lambda n, i: (n, 0, 0)), blk(cur),
                  blk(prev), blk(cur), blk(nxt), blk(prev), blk(cur), blk(nxt), blk(cur), col, col],
        out_specs=blk(cur),
        compiler_params=_cparams(("parallel", "parallel")),
        name=f"band_dq_{N}x{L}",
    )(slope, q, k, k, k, v, v, v, do, lse, dd)


def _band_dkv(q, k, v, slope, do, lse_row, dd_row, half):
    N, L, d = q.shape
    t, nt, cur, prev, nxt = _band_specs(L, d)
    nb = BAND_BATCH
    scale = d ** -0.5
    rcur = lambda n, i: (n, 0, i)
    rprev = lambda n, i: (n, 0, jnp.maximum(i - 1, 0))
    rnxt = lambda n, i: (n, 0, jnp.minimum(i + 1, nt - 1))

    def body(sl_ref, k_ref, v_ref, qp, qc, qn, dop, doc, don, lp, lc, ln, dp_, dc_, dn_, dk_ref, dv_ref):
        i = pl.program_id(1)
        qcat = jnp.concatenate([qp[...], qc[...], qn[...]], axis=1)
        docat = jnp.concatenate([dop[...], doc[...], don[...]], axis=1).astype(BF16)
        lrow = jnp.concatenate([lp[...], lc[...], ln[...]], axis=2)
        drow = jnp.concatenate([dp_[...], dc_[...], dn_[...]], axis=2)
        st = lax.dot_general(k_ref[...], qcat, (((2,), (2,)), ((0,), (0,))), preferred_element_type=F32) * scale
        dist, mask = _band_mask(i, t, L, half)
        pt = jnp.where(mask[None], jnp.exp(st - sl_ref[...] * dist[None] - lrow), 0.0)
        dv = lax.dot_general(pt.astype(BF16), docat, (((2,), (1,)), ((0,), (0,))), preferred_element_type=F32)
        dpt = lax.dot_general(v_ref[...], docat, (((2,), (2,)), ((0,), (0,))), preferred_element_type=F32)
        dst = (pt * (dpt - drow)).astype(BF16)
        dk = lax.dot_general(dst, qcat, (((2,), (1,)), ((0,), (0,))), preferred_element_type=F32)
        dk_ref[...] = (dk * scale).astype(dk_ref.dtype)
        dv_ref[...] = dv.astype(dv_ref.dtype)

    blk = lambda f: pl.BlockSpec((nb, t, d), f)
    row = lambda f: pl.BlockSpec((nb, 1, t), f)
    return pl.pallas_call(
        body,
        out_shape=[jax.ShapeDtypeStruct(k.shape, k.dtype), jax.ShapeDtypeStruct(v.shape, v.dtype)],
        grid=(N // nb, nt),
        in_specs=[pl.BlockSpec((nb, 1, 1), lambda n, i: (n, 0, 0)), blk(cur), blk(cur),
                  blk(prev), blk(cur), blk(nxt), blk(prev), blk(cur), blk(nxt),
                  row(rprev), row(rcur), row(rnxt), row(rprev), row(rcur), row(rnxt)],
        out_specs=[blk(cur), blk(cur)],
        compiler_params=_cparams(("parallel", "parallel")),
        name=f"band_dkv_{N}x{L}",
    )(slope, k, v, q, q, q, do, do, do, lse_row, lse_row, lse_row, dd_row, dd_row, dd_row)


def band_attention(q, k, v, slope, half):
    @jax.custom_vjp
    def op(q, k, v, slope):
        return tuple(_band_fwd(q, k, v, slope, half))

    def op_fwd(q, k, v, slope):
        o, lse = _band_fwd(q, k, v, slope, half)
        return (o, lse), (q, k, v, slope, o, lse)

    def op_bwd(res, cts):
        q, k, v, slope, o, lse = res
        do, dlse = cts
        N, L, _ = q.shape
        dd = _attn_delta(do, o, dlse)
        dq = _band_dq(q, k, v, slope, do, lse, dd, half)
        dk, dv = _band_dkv(q, k, v, slope, do, lse.reshape(N, 1, L), dd.reshape(N, 1, L), half)
        return dq, dk, dv, jnp.zeros_like(slope)

    op.defvjp(op_fwd, op_bwd)
    return op(q, k, v, slope)


def _conv_taps(width):
    return [CONV_PAD - width // 2 + j for j in range(width)]


def _dwconv_call(up, w, width):
    S, C = up.shape[0] - 2 * CONV_PAD, up.shape[1]
    ch = min(CONV_CHUNK, S)
    offs = _conv_taps(width)

    def body(u_ref, w_ref, o_ref):
        def chunk(ci, carry):
            base = pl.multiple_of(ci * ch, ch)
            win = u_ref[pl.ds(base, ch + 2 * CONV_PAD), :]
            acc = jnp.zeros((ch, LANES), F32)
            for j, off in enumerate(offs):
                acc = acc + win[off:off + ch, :] * w_ref[j:j + 1, :]
            o_ref[pl.ds(base, ch), :] = acc
            return carry

        lax.fori_loop(0, S // ch, chunk, 0)

    return pl.pallas_call(
        body,
        out_shape=jax.ShapeDtypeStruct((S, C), F32),
        grid=(C // LANES,),
        in_specs=[pl.BlockSpec((S + 2 * CONV_PAD, LANES), lambda c: (0, c)),
                  pl.BlockSpec((w.shape[0], LANES), lambda c: (0, c))],
        out_specs=pl.BlockSpec((S, LANES), lambda c: (0, c)),
        compiler_params=_cparams(("parallel",)),
        name=f"dwconv{width}_{S}x{C}",
    )(up, w)


def _dwconv_dw_call(up, g, width, w_rows):
    S, C = g.shape
    ch = min(CONV_CHUNK, S)
    offs = _conv_taps(width)

    def body(u_ref, g_ref, o_ref):
        def chunk(ci, accs):
            base = pl.multiple_of(ci * ch, ch)
            win = u_ref[pl.ds(base, ch + 2 * CONV_PAD), :]
            gt = g_ref[pl.ds(base, ch), :]
            return tuple(a + (gt * win[off:off + ch, :]).reshape(ch // 8, 8, LANES).sum(axis=0)
                         for a, off in zip(accs, offs))

        accs = lax.fori_loop(0, S // ch, chunk, tuple(jnp.zeros((8, LANES), F32) for _ in offs))
        o_ref[...] = jnp.zeros_like(o_ref)
        for j, a in enumerate(accs):
            o_ref[j:j + 1, :] = jnp.sum(a, axis=0, keepdims=True)

    return pl.pallas_call(
        body,
        out_shape=jax.ShapeDtypeStruct((w_rows, C), F32),
        grid=(C // LANES,),
        in_specs=[pl.BlockSpec((S + 2 * CONV_PAD, LANES), lambda c: (0, c)),
                  pl.BlockSpec((S, LANES), lambda c: (0, c))],
        out_specs=pl.BlockSpec((w_rows, LANES), lambda c: (0, c)),
        compiler_params=_cparams(("parallel",)),
        name=f"dwconv{width}_dw_{S}x{C}",
    )(up, g)


def _pad_rows(u):
    return jnp.pad(u, ((CONV_PAD, CONV_PAD), (0, 0)))


def _pad_taps(w):
    return jnp.pad(w, ((0, -w.shape[0] % 8), (0, 0)))


def depthwise_conv(u, w):
    width = w.shape[0]

    @jax.custom_vjp
    def op(u, w):
        return _dwconv_call(_pad_rows(u), _pad_taps(w), width)

    def op_fwd(u, w):
        up = _pad_rows(u)
        return _dwconv_call(up, _pad_taps(w), width), (up, w)

    def op_bwd(res, g):
        up, w = res
        du = _dwconv_call(_pad_rows(g), _pad_taps(w[::-1]), width)
        dw = _dwconv_dw_call(up, g, width, _pad_taps(w).shape[0])[:width]
        return du, dw

    op.defvjp(op_fwd, op_bwd)
    return op(u, w)


def _loss_call(y, target):
    S, D = y.shape
    tile = _row_tile(S, D)

    def body(y_ref, t_ref, o_ref):
        d = y_ref[...] - t_ref[...]
        part = jnp.sum(jnp.sum(d * d, axis=1, keepdims=True), axis=0, keepdims=True) * (0.5 / D)

        @pl.when(pl.program_id(0) == 0)
        def _():
            o_ref[...] = jnp.zeros_like(o_ref)

        o_ref[...] += jnp.broadcast_to(part, o_ref.shape)

    out = pl.pallas_call(
        body,
        out_shape=jax.ShapeDtypeStruct((8, LANES), F32),
        grid=(S // tile,),
        in_specs=[pl.BlockSpec((tile, D), lambda i: (i, 0)), pl.BlockSpec((tile, D), lambda i: (i, 0))],
        out_specs=pl.BlockSpec((8, LANES), lambda i: (0, 0)),
        compiler_params=_cparams(("arbitrary",)),
        name="loss_head",
    )(y, target)
    return out[0, 0]


@jax.custom_vjp
def loss_head(y, target):
    return _loss_call(y, target)


def _loss_fwd(y, target):
    return _loss_call(y, target), (y, target)


def _loss_bwd(res, g):
    y, target = res
    inv_d = 1.0 / y.shape[1]

    def fn(y_t, t_t, g_p):
        return ((y_t - t_t) * (g_p * inv_d),)

    (dy,) = _rw_forward(fn, [y, target], [g.reshape(1, 1).astype(F32)], "loss_head_bwd")
    return dy, jnp.zeros_like(target)


loss_head.defvjp(_loss_fwd, _loss_bwd)


ANY = pl.BlockSpec(memory_space=pl.ANY)


AG_ICI_PIECES = 2
AG_D2D_PIECES = 2
D2D_PIECES = 8


def all_gather_chips(mine, name):
    R, C = mine.shape
    ni, nf = AG_ICI_PIECES, AG_D2D_PIECES
    rows = R // (2 * ni * nf)
    n_ici, n_all = 3 * ni, 3 * ni + 3 * ni * nf

    def body(in_ref, out_ref, send_sems, recv_sems, local_sem):
        x, y, c = lax.axis_index("x"), lax.axis_index("y"), lax.axis_index("c")
        own = pltpu.make_async_copy(in_ref, out_ref.at[2 * x + y], local_sem)
        own.start()
        chips = [(1 - x, y), (x, 1 - y), (1 - x, 1 - y)]

        def copy(sem, src, dst, to):
            return pltpu.make_async_remote_copy(src_ref=src, dst_ref=dst, send_sem=send_sems.at[sem],
                                                recv_sem=recv_sems.at[sem], device_id=to, device_id_type=MESH)

        sends = []
        for k, (px, py) in enumerate(chips):
            for a in range(ni):
                sends.append(copy(k * ni + a, in_ref.at[c, a], out_ref.at[2 * x + y, c, a], (px, py, c)))
                sends[-1].start()
        for k, (px, py) in enumerate(chips):
            for a in range(ni):
                landed = out_ref.at[2 * px + py, c, a]
                copy(k * ni + a, in_ref.at[c, a], landed, (px, py, c)).wait_recv()
                for b in range(nf):
                    sends.append(copy(n_ici + (k * ni + a) * nf + b, landed.at[b], landed.at[b], (x, y, 1 - c)))
                    sends[-1].start()
        for k, (px, py) in enumerate(chips):
            for a in range(ni):
                for b in range(nf):
                    theirs = out_ref.at[2 * px + py, 1 - c, a, b]
                    copy(n_ici + (k * ni + a) * nf + b, theirs, theirs, (x, y, 1 - c)).wait_recv()
        for cp in sends:
            cp.wait_send()
        own.wait()

    out = pl.pallas_call(
        body,
        out_shape=jax.ShapeDtypeStruct((N_CHIPS, 2, ni, nf, rows, C), mine.dtype),
        in_specs=[ANY],
        out_specs=ANY,
        scratch_shapes=[pltpu.SemaphoreType.DMA((n_all,)), pltpu.SemaphoreType.DMA((n_all,)),
                        pltpu.SemaphoreType.DMA],
        name=name,
    )(mine.reshape(2, ni, nf, rows, C))
    return out.reshape(N_CHIPS, R, C)


def scatter_to_owners(parts):
    _, _, H, C = parts.shape
    flips = [(dx, dy, dc) for dx in (0, 1) for dy in (0, 1) for dc in (0, 1)][1:]

    def body(in_ref, out_ref, send_sems, recv_sems, local_sem):
        x, y, c = lax.axis_index("x"), lax.axis_index("y"), lax.axis_index("c")
        me = 4 * x + 2 * y + c
        own = pltpu.make_async_copy(in_ref.at[2 * x + y, c], out_ref.at[me], local_sem)
        own.start()
        peers = [(1 - x if dx else x, 1 - y if dy else y, 1 - c if dc else c) for dx, dy, dc in flips]
        sends = []
        for k, (px, py, pc) in enumerate(peers):
            cp = pltpu.make_async_remote_copy(
                src_ref=in_ref.at[2 * px + py, pc], dst_ref=out_ref.at[me], send_sem=send_sems.at[k],
                recv_sem=recv_sems.at[k], device_id=(px, py, pc), device_id_type=MESH)
            cp.start()
            sends.append(cp)
        for k, (px, py, pc) in enumerate(peers):
            pltpu.make_async_remote_copy(
                src_ref=in_ref.at[2 * x + y, c], dst_ref=out_ref.at[4 * px + 2 * py + pc], send_sem=send_sems.at[k],
                recv_sem=recv_sems.at[k], device_id=(px, py, pc), device_id_type=MESH).wait_recv()
        for cp in sends:
            cp.wait_send()
        own.wait()

    return pl.pallas_call(
        body,
        out_shape=jax.ShapeDtypeStruct((8, H, C), parts.dtype),
        in_specs=[ANY],
        out_specs=ANY,
        scratch_shapes=[pltpu.SemaphoreType.DMA((7,)), pltpu.SemaphoreType.DMA((7,)), pltpu.SemaphoreType.DMA],
        name="grad_scatter",
    )(parts)


def exchange_with_sibling(mine):
    H, C = mine.shape
    n = D2D_PIECES

    def body(in_ref, out_ref, send_sems, recv_sems, local_sem):
        x, y, c = lax.axis_index("x"), lax.axis_index("y"), lax.axis_index("c")
        own = pltpu.make_async_copy(in_ref, out_ref.at[c], local_sem)
        own.start()

        def copy(b, half):
            return pltpu.make_async_remote_copy(
                src_ref=in_ref.at[b], dst_ref=out_ref.at[half, b], send_sem=send_sems.at[b], recv_sem=recv_sems.at[b],
                device_id=(x, y, 1 - c), device_id_type=MESH)

        sends = [copy(b, c) for b in range(n)]
        for cp in sends:
            cp.start()
        for b in range(n):
            copy(b, 1 - c).wait_recv()
        for cp in sends:
            cp.wait_send()
        own.wait()

    out = pl.pallas_call(
        body,
        out_shape=jax.ShapeDtypeStruct((2, n, H // n, C), mine.dtype),
        in_specs=[ANY],
        out_specs=ANY,
        scratch_shapes=[pltpu.SemaphoreType.DMA((n,)), pltpu.SemaphoreType.DMA((n,)), pltpu.SemaphoreType.DMA],
        name="grad_sibling_exchange",
    )(mine.reshape(n, H // n, C))
    return out.reshape(2, H, C)


def sum_contributions(recv):
    n, H, C = recv.shape
    tile = _div_tile(H, 256, 16)

    def body(r_ref, o_ref):
        acc = r_ref[0].astype(F32)
        for j in range(1, n):
            acc = acc + r_ref[j].astype(F32)
        o_ref[...] = acc

    return pl.pallas_call(
        body,
        out_shape=jax.ShapeDtypeStruct((H, C), F32),
        grid=(H // tile,),
        in_specs=[pl.BlockSpec((n, tile, C), lambda i: (0, i, 0))],
        out_specs=pl.BlockSpec((tile, C), lambda i: (i, 0)),
        compiler_params=_cparams(("parallel",)),
        name="grad_sum",
    )(recv)


def adamw(w, g, m, v):
    shape = w.shape
    cols = shape[-1]
    rows = math.prod(shape[:-1])
    tile = _div_tile(rows, max(8, 256 * 1024 // cols // 8 * 8), 8)
    c1 = 1.0 - ADAM_B1 ** ADAM_STEP
    c2 = 1.0 - ADAM_B2 ** ADAM_STEP

    def body(w_ref, g_ref, m_ref, v_ref, d_ref, nm_ref, nv_ref):
        g_ = g_ref[...]
        nm = ADAM_B1 * m_ref[...] + (1.0 - ADAM_B1) * g_
        nv = ADAM_B2 * v_ref[...] + (1.0 - ADAM_B2) * (g_ * g_)
        d_ref[...] = -ADAM_LR * ((nm / c1) / (jnp.sqrt(nv / c2) + ADAM_EPS) + ADAM_WD * w_ref[...])
        nm_ref[...] = nm
        nv_ref[...] = nv

    spec = pl.BlockSpec((tile, cols), lambda i: (i, 0))
    outs = pl.pallas_call(
        body,
        out_shape=[jax.ShapeDtypeStruct((rows, cols), F32)] * 3,
        grid=(rows // tile,),
        in_specs=[spec] * 4,
        out_specs=[spec] * 3,
        compiler_params=_cparams(("parallel",)),
        name=f"adamw_{rows}x{cols}",
    )(*[t.reshape(rows, cols) for t in (w, g, m, v)])
    return [o.reshape(shape) for o in outs]


def _leaves(name, arr):
    n_lead = WEIGHT_LAYOUT[name][0]
    lead = arr.shape[:n_lead]
    flat = arr.reshape((-1,) + arr.shape[n_lead:])
    return [flat[i] for i in range(math.prod(lead))]


def _pad_pack_rows(buf):
    return jnp.pad(buf, ((0, -buf.shape[0] % PACK_ROW_MULT), (0, 0)))


def _pack_flat(pieces, dtype, cols):
    flat = jnp.concatenate([p.reshape(-1).astype(dtype) for p in pieces])
    flat = jnp.pad(flat, (0, -flat.shape[0] % (cols * PACK_ROW_MULT)))
    return flat.reshape(-1, cols)


def _unpack_flat(buf, shapes):
    lead = buf.shape[:-2]
    flat = buf.reshape(lead + (-1,))
    out, off = [], 0
    for shp in shapes:
        n = math.prod(shp)
        out.append(lax.slice_in_dim(flat, off, off + n, axis=len(lead)).reshape(lead + tuple(shp)))
        off += n
    return out


def _pack_slabs(slabs, dtype):
    return _pad_pack_rows(jnp.concatenate([s.astype(dtype) for s in slabs], axis=0))


def _local_slabs(name, shard):
    leaves = _leaves(name, shard)
    return [leaf.T for leaf in leaves] if name in SLAB_TRANSPOSED else leaves


def _chip_slice(leaf, ax, s):
    if ax is None:
        return leaf
    w = leaf.shape[ax] // N_CHIPS
    return lax.slice_in_dim(leaf, s * w, (s + 1) * w, axis=ax)


def gather_weights(shards, cols):
    slabs = [s for n in SLAB_NAMES for s in _local_slabs(n, shards[n])]
    slab_owner = [n for n in SLAB_NAMES for _ in _leaves(n, shards[n])]
    slab_buf = _pack_slabs(slabs, BF16)
    misc = {}
    for dtype in (BF16, F32):
        names = [n for n in MISC_NAMES if WEIGHT_LAYOUT[n][1] is not None and (n in F32_GATHER) == (dtype == F32)]
        pieces = [leaf for n in names for leaf in _leaves(n, shards[n])]
        owners = [n for n in names for _ in _leaves(n, shards[n])]
        buf = _pack_flat(pieces, dtype, cols)
        if dtype == BF16:
            gathered = all_gather_chips(jnp.concatenate([slab_buf, buf], axis=0), "weights_all_gather_bf16")
            slab_part, gathered = gathered[:, :slab_buf.shape[0]], gathered[:, slab_buf.shape[0]:]
        else:
            gathered = all_gather_chips(buf, "weights_all_gather_f32")
        for n, blk in zip(owners, _unpack_flat(gathered, [p.shape for p in pieces])):
            ax = WEIGHT_LAYOUT[n][1]
            misc.setdefault(n, []).append(jnp.concatenate([blk[s] for s in range(N_CHIPS)], axis=ax).astype(F32))
    for n in MISC_NAMES:
        if WEIGHT_LAYOUT[n][1] is None:
            misc[n] = [leaf.astype(F32) for leaf in _leaves(n, shards[n])]
    slab_full, off = {}, 0
    for n, s in zip(slab_owner, slabs):
        rows = s.shape[0]
        slab_full.setdefault(n, []).append(slab_part[:, off:off + rows].reshape(N_CHIPS * rows, cols))
        off += rows
    return slab_full, misc


def reduce_gradients(g_slab, g_misc, shards, cols):
    per_chip = []
    for s in range(N_CHIPS):
        slabs = []
        for n in SLAB_NAMES:
            for leaf in g_slab[n]:
                slabs.append(_chip_slice(leaf, 0, s))
        pieces = [_chip_slice(leaf, WEIGHT_LAYOUT[n][1], s) for n in MISC_NAMES for leaf in g_misc[n]]
        per_chip.append(jnp.concatenate([_pack_slabs(slabs, BF16), _pack_flat(pieces, BF16, cols)], axis=0))
    parts = jnp.stack(per_chip)
    R = parts.shape[1]
    recv = scatter_to_owners(parts.reshape(N_CHIPS, 2, R // 2, cols))
    red = exchange_with_sibling(sum_contributions(recv)).reshape(R, cols)

    out, off = {}, 0
    for n in SLAB_NAMES:
        leaves = []
        for slab in _local_slabs(n, shards[n]):
            rows = slab.shape[0]
            g = red[off:off + rows]
            leaves.append(g.T if n in SLAB_TRANSPOSED else g)
            off += rows
        out[n] = jnp.stack(leaves).reshape(shards[n].shape)
    off += -off % PACK_ROW_MULT
    misc_shapes = [shards[n].shape for n in MISC_NAMES]
    for n, g in zip(MISC_NAMES, _unpack_flat(red[off:], misc_shapes)):
        out[n] = g
    return out


def _silu(x):
    return x * jax.nn.sigmoid(x)


def _heads_first(t, heads):
    S = t.shape[0]
    return t.reshape(S, heads, -1).transpose(1, 0, 2)


def _heads_last(t):
    return t.transpose(1, 0, 2).reshape(t.shape[1], -1)


class SlabWeights:
    def __init__(self, values, deltas):
        self.values, self.deltas = values, deltas

    def matmul(self, a, name, leaf, out_dtype):
        return mm_slab(a, self.values[name][leaf], self.deltas[name][leaf], name in SLAB_TRANSPOSED, out_dtype)


def swiglu_half_step(x, h, slab, leaf, g_post, g_next):
    a = slab.matmul(h, 'ffn_w_gate', leaf, BF16)
    b = slab.matmul(h, 'ffn_w_up', leaf, BF16)
    (u,) = rowwise(lambda a_, b_: ((_silu(a_.astype(F32)) * b_.astype(F32)).astype(BF16),), [a, b], [], "swiglu_act")
    y = slab.matmul(u, 'ffn_w_down', leaf, F32)
    return residual_norm(x, y, 0.5, g_post, g_next)


def residual_norm(x, y, coef, g_post, g_next):
    if g_next is None:
        (xn,) = rowwise(lambda x_, y_, g1: (x_ + coef * _rms(y_, g1),), [x, y], [g_post], "residual")
        return xn, None

    def fn(x_, y_, g1, g2):
        xn = x_ + coef * _rms(y_, g1)
        return xn, _rms(xn, g2).astype(BF16)

    return rowwise(fn, [x, y], [g_post, g_next], "residual_norm")


def memory_attention(q_mem, memn, w_kv):
    kv = mm(memn, w_kv, BF16)
    n_mem = kv.shape[0]
    kv = kv.reshape(n_mem, 2, MEM_HEADS, HEAD_DIM).transpose(1, 2, 0, 3)
    o = full_attention(_heads_first(q_mem.astype(BF16), MEM_HEADS), kv[0], kv[1], HEAD_DIM ** -0.5, False)
    return _heads_last(o)


def mla_mixer(h, cos, sin, w_in, q_g, kv_g, w_uq, w_ukv):
    S = h.shape[0]
    a_in = MLA_Q_LORA + MLA_KV_LORA + MLA_ROPE + MEM_WIDTH
    z = mm(h, jnp.pad(w_in, ((0, 0), (0, -a_in % LANES))), F32)
    o1, o2, o3 = MLA_Q_LORA, MLA_Q_LORA + MLA_KV_LORA, MLA_Q_LORA + MLA_KV_LORA + MLA_ROPE
    c_q, c_kv, k_r, q_mem = z[:, :o1], z[:, o1:o2], z[:, o2:o3], z[:, o3:a_in]
    cqn, ckvn = rowwise(lambda a, b, ga, gb: (_rms(a, ga).astype(BF16), _rms(b, gb).astype(BF16)),
                        [c_q, c_kv], [q_g, kv_g], "mla_lora_norm")
    q = mm(cqn, w_uq, F32).reshape(S, MLA_HEADS, MLA_NOPE + MLA_ROPE)
    kv = mm(ckvn, w_ukv, BF16).reshape(S, MLA_HEADS, MLA_NOPE + MLA_V)
    half = MLA_ROPE // 2
    qn = q[:, :, :MLA_NOPE].reshape(S, MLA_HEADS * MLA_NOPE)
    q1 = q[:, :, MLA_NOPE:MLA_NOPE + half].reshape(S, MLA_HEADS * half)
    q2 = q[:, :, MLA_NOPE + half:].reshape(S, MLA_HEADS * half)
    cos_h, sin_h = jnp.tile(cos, (1, MLA_HEADS)), jnp.tile(sin, (1, MLA_HEADS))
    scale = (MLA_NOPE + MLA_ROPE) ** -0.5
    qm = scale * LOG2E

    def rope(qn_, q1_, q2_, k1_, k2_, ch, sh, c1, s1):
        return ((qn_ * qm).astype(BF16), ((q1_ * ch - q2_ * sh) * qm).astype(BF16),
                ((q1_ * sh + q2_ * ch) * qm).astype(BF16),
                (k1_ * c1 - k2_ * s1).astype(BF16), (k1_ * s1 + k2_ * c1).astype(BF16))

    qns, qr1, qr2, kr1, kr2 = rowwise(rope, [qn, q1, q2, k_r[:, :half], k_r[:, half:], cos_h, sin_h, cos, sin], [],
                                      "mla_rope", n_const=4)
    q_cat = jnp.concatenate([qns.reshape(S, MLA_HEADS, MLA_NOPE), qr1.reshape(S, MLA_HEADS, half),
                             qr2.reshape(S, MLA_HEADS, half)], axis=-1)
    k_rope = jnp.concatenate([kr1, kr2], axis=-1)
    k_cat = jnp.concatenate([kv[:, :, :MLA_NOPE], jnp.broadcast_to(k_rope[:, None, :], (S, MLA_HEADS, MLA_ROPE))],
                            axis=-1)
    o = full_attention(q_cat.transpose(1, 0, 2), k_cat.transpose(1, 0, 2), kv[:, :, MLA_NOPE:].transpose(1, 0, 2),
                       scale, True)
    return _heads_last(o), q_mem


def dilated_mixer(h, slab):
    S = h.shape[0]
    n_g = len(DIL_GROUPS)
    qkv_w = n_g * 3 * DIL_HEADS * HEAD_DIM
    z = slab.matmul(h, 'b_w_in', 0, BF16)
    zd = z[:, :qkv_w].reshape(S, n_g, 3, DIL_HEADS, HEAD_DIM)
    q_mem = z[:, qkv_w:]
    slopes = (2.0 ** (-ALIBI_MAX * (jnp.arange(n_g * DIL_HEADS, dtype=F32) + 1.0) / (n_g * DIL_HEADS)))
    slopes = slopes.reshape(n_g, DIL_HEADS)
    outs, lses = [], []
    for g, (window, dil) in enumerate(DIL_GROUPS):
        L = S // dil

        def sub(t):
            return t.reshape(L, dil, DIL_HEADS, HEAD_DIM).transpose(1, 2, 0, 3).reshape(dil * DIL_HEADS, L, HEAD_DIM)

        slope = (jnp.tile(slopes[g], dil) * dil).reshape(dil * DIL_HEADS, 1, 1)
        o, lse = band_attention(sub(zd[:, g, 0]), sub(zd[:, g, 1]), sub(zd[:, g, 2]), slope, window // (2 * dil))
        outs.append(o.reshape(dil, DIL_HEADS, L, HEAD_DIM).transpose(2, 0, 1, 3).reshape(S * DIL_HEADS, HEAD_DIM))
        lses.append(lse.reshape(dil, DIL_HEADS, L).transpose(2, 0, 1).reshape(S * DIL_HEADS, 1))

    def merge(o0, o1, o2, l0, l1, l2):
        m = jnp.maximum(jnp.maximum(l0, l1), l2)
        e0, e1, e2 = jnp.exp(l0 - m), jnp.exp(l1 - m), jnp.exp(l2 - m)
        return (((e0 * o0 + e1 * o1 + e2 * o2) / (e0 + e1 + e2)).astype(BF16),)

    (o,) = rowwise(merge, outs + lses, [], "dilated_merge")
    return o.reshape(S, DIL_HEADS * HEAD_DIM), q_mem


def conformer_conv_mixer(h, slab, conv_w, conv_b, ln_g, ln_b):
    z = slab.matmul(h, 'c_w_in', 0, F32)
    a, gate, q_mem = z[:, :CONV_CH], z[:, CONV_CH:2 * CONV_CH], z[:, 2 * CONV_CH:]
    (u,) = rowwise(lambda a_, g_: (a_ * jax.nn.sigmoid(g_),), [a, gate], [], "conformer_glu")
    u = depthwise_conv(u, conv_w)

    def post(u_, b, g, beta):
        t = u_ + b
        mu = jnp.mean(t, axis=-1, keepdims=True)
        var = jnp.mean(jnp.square(t - mu), axis=-1, keepdims=True)
        return (_silu((t - mu) * lax.rsqrt(var + EPS) * g + beta).astype(BF16),)

    (o,) = rowwise(post, [u], [conv_b.reshape(1, -1), ln_g.reshape(1, -1), ln_b.reshape(1, -1)], "conformer_post")
    return o, q_mem


def short_conv_mixer(h, slab, conv_w):
    z = slab.matmul(h, 'd_w_in', 0, F32)
    bg, cg, hx, q_mem = z[:, :SC_CH], z[:, SC_CH:2 * SC_CH], z[:, 2 * SC_CH:3 * SC_CH], z[:, 3 * SC_CH:]
    (p,) = rowwise(lambda c_, h_: (c_ * h_,), [cg, hx], [], "shortconv_pre")
    cv = depthwise_conv(p, conv_w)
    (o,) = rowwise(lambda b_, c_: ((b_ * c_).astype(BF16),), [bg, cv], [], "shortconv_post")
    return o, q_mem


def local_loss(deltas, W, x, slab_values, mem, cos, sin, target):
    slab = SlabWeights(slab_values, deltas)
    norm_g = W['norm_g'][0]

    def gain(i, k):
        return norm_g[i, k].reshape(1, -1)

    (h,) = rowwise(lambda x_, g: (_rms(x_, g).astype(BF16),), [x], [gain(0, 0)], "input_norm")
    for i in range(DEPTH):
        x, h = swiglu_half_step(x, h, slab, 2 * i, gain(i, 1), gain(i, 2))
        if i == 0:
            o, q_mem = mla_mixer(h, cos, sin, W['a_w_in'][0], W['a_q_norm'][0].reshape(1, -1),
                                 W['a_kv_norm'][0].reshape(1, -1), W['a_w_uq'][0], W['a_w_ukv'][0])
            out_name = 'a_w_out'
        elif i == 1:
            o, q_mem = dilated_mixer(h, slab)
            out_name = None
        elif i == 2:
            o, q_mem = conformer_conv_mixer(h, slab, W['c_conv_w'][0], W['c_conv_b'][0], W['c_ln_g'][0],
                                            W['c_ln_b'][0])
            out_name = 'c_w_out'
        else:
            o, q_mem = short_conv_mixer(h, slab, W['d_conv_w'][0])
            out_name = 'd_w_out'
        (memn,) = rowwise(lambda m_, g: (_rms(m_, g).astype(BF16),), [mem], [gain(i, 6)], "memory_norm")
        mo = memory_attention(q_mem, memn, W['mem_w_kv'][i])
        cat = jnp.concatenate([o, mo], axis=-1)
        y = mm(cat, W['b_w_out'][0], F32) if out_name is None else slab.matmul(cat, out_name, 0, F32)
        x, h = residual_norm(x, y, 1.0, gain(i, 3), gain(i, 4))
        x, h = swiglu_half_step(x, h, slab, 2 * i + 1, gain(i, 5), gain(i + 1, 0) if i + 1 < DEPTH else None)
    return loss_head(x, target)


def kernel(x, mem, positions, norm_g, ffn_w_gate, ffn_w_up, ffn_w_down, mem_w_kv, a_w_in, a_q_norm, a_kv_norm, a_w_uq, a_w_ukv, a_w_out, b_w_in, b_w_out, c_w_in, c_conv_w, c_conv_b, c_ln_g, c_ln_b, c_w_out, d_w_in, d_conv_w, d_w_out, loss_target, m_norm_g, m_ffn_w_gate, m_ffn_w_up, m_ffn_w_down, m_mem_w_kv, m_a_w_in, m_a_q_norm, m_a_kv_norm, m_a_w_uq, m_a_w_ukv, m_a_w_out, m_b_w_in, m_b_w_out, m_c_w_in, m_c_conv_w, m_c_conv_b, m_c_ln_g, m_c_ln_b, m_c_w_out, m_d_w_in, m_d_conv_w, m_d_w_out, v_norm_g, v_ffn_w_gate, v_ffn_w_up, v_ffn_w_down, v_mem_w_kv, v_a_w_in, v_a_q_norm, v_a_kv_norm, v_a_w_uq, v_a_w_ukv, v_a_w_out, v_b_w_in, v_b_w_out, v_c_w_in, v_c_conv_w, v_c_conv_b, v_c_ln_g, v_c_ln_b, v_c_w_out, v_d_w_in, v_d_conv_w, v_d_w_out):
    w_in = dict(zip(WEIGHT_NAMES, (norm_g, ffn_w_gate, ffn_w_up, ffn_w_down, mem_w_kv, a_w_in, a_q_norm, a_kv_norm, a_w_uq, a_w_ukv, a_w_out, b_w_in, b_w_out, c_w_in, c_conv_w, c_conv_b, c_ln_g, c_ln_b, c_w_out, d_w_in, d_conv_w, d_w_out)))
    m_in = dict(zip(WEIGHT_NAMES, (m_norm_g, m_ffn_w_gate, m_ffn_w_up, m_ffn_w_down, m_mem_w_kv, m_a_w_in, m_a_q_norm, m_a_kv_norm, m_a_w_uq, m_a_w_ukv, m_a_w_out, m_b_w_in, m_b_w_out, m_c_w_in, m_c_conv_w, m_c_conv_b, m_c_ln_g, m_c_ln_b, m_c_w_out, m_d_w_in, m_d_conv_w, m_d_w_out)))
    v_in = dict(zip(WEIGHT_NAMES, (v_norm_g, v_ffn_w_gate, v_ffn_w_up, v_ffn_w_down, v_mem_w_kv, v_a_w_in, v_a_q_norm, v_a_kv_norm, v_a_w_uq, v_a_w_ukv, v_a_w_out, v_b_w_in, v_b_w_out, v_c_w_in, v_c_conv_w, v_c_conv_b, v_c_ln_g, v_c_ln_b, v_c_w_out, v_d_w_in, v_d_conv_w, v_d_w_out)))

    d_model = x.shape[-1]
    slab_values, W = gather_weights(w_in, d_model)
    deltas = {n: [jnp.zeros(leaf.shape, BF16) for leaf in leaves] for n, leaves in slab_values.items()}
    half = MLA_ROPE // 2
    inv = ROPE_THETA ** (-jnp.arange(half, dtype=F32) / half)
    ang = positions[0].astype(F32)[:, None] * inv
    loss, (g_slab, g_misc, gx) = jax.value_and_grad(local_loss, argnums=(0, 1, 2))(
        deltas, W, x[0], slab_values, mem[0], jnp.cos(ang), jnp.sin(ang), loss_target[0])
    loss = lax.psum(loss, ("x", "y", "c"))

    grads = reduce_gradients(g_slab, g_misc, w_in, d_model)
    steps = {n: adamw(w_in[n], grads[n], m_in[n], v_in[n]) for n in WEIGHT_NAMES}
    return (loss, gx[None], *[grads[n] for n in WEIGHT_NAMES], *[steps[n][0] for n in WEIGHT_NAMES],
            *[steps[n][1] for n in WEIGHT_NAMES], *[steps[n][2] for n in WEIGHT_NAMES])
```

```python
import functools
import math

import jax
import jax.numpy as jnp
from jax import lax
from jax.experimental import pallas as pl
from jax.experimental.pallas import tpu as pltpu

F32 = jnp.float32
BF16 = jnp.bfloat16
MESH = pl.DeviceIdType.MESH

VMEM_LIMIT_BYTES = 56 * 1024 * 1024
LANES = 128
PACK_ROW_MULT = 512
ROW_TILE_ELEMS = 768 * 1024

HEAD_DIM = 64
MEM_HEADS = 4
MEM_WIDTH = MEM_HEADS * HEAD_DIM
MLA_HEADS = 12
MLA_Q_LORA = 384
MLA_KV_LORA = 256
MLA_NOPE = 64
MLA_ROPE = 32
MLA_V = 64
ROPE_THETA = 10000.0
DIL_GROUPS = ((128, 1), (512, 4), (2048, 16))
DIL_HEADS = 8
ALIBI_MAX = 8.0
CONV_CH = 768
CONV_WIDTH = 31
SC_CH = 768
SC_WIDTH = 3
CONV_PAD = 16
CONV_CHUNK = 256
EPS = 1e-6
NEG = -1e30
DEPTH = 4
LOG2E = 1.4426950408889634
LN2 = 0.6931471805599453

ADAM_LR = 0.001
ADAM_B1 = 0.9
ADAM_B2 = 0.999
ADAM_EPS = 1e-08
ADAM_WD = 0.01
ADAM_STEP = 10

WEIGHT_NAMES = ['norm_g', 'ffn_w_gate', 'ffn_w_up', 'ffn_w_down', 'mem_w_kv', 'a_w_in', 'a_q_norm', 'a_kv_norm',
                'a_w_uq', 'a_w_ukv', 'a_w_out', 'b_w_in', 'b_w_out', 'c_w_in', 'c_conv_w', 'c_conv_b', 'c_ln_g',
                'c_ln_b', 'c_w_out', 'd_w_in', 'd_conv_w', 'd_w_out']
WEIGHT_LAYOUT = {
    'norm_g': (0, 2), 'ffn_w_gate': (2, 1), 'ffn_w_up': (2, 1), 'ffn_w_down': (2, 0), 'mem_w_kv': (1, 0),
    'a_w_in': (1, 0), 'a_q_norm': (1, None), 'a_kv_norm': (1, None), 'a_w_uq': (1, 1), 'a_w_ukv': (1, 1),
    'a_w_out': (1, 0), 'b_w_in': (1, 1), 'b_w_out': (1, 1), 'c_w_in': (1, 1), 'c_conv_w': (1, 1),
    'c_conv_b': (1, 0), 'c_ln_g': (1, 0), 'c_ln_b': (1, 0), 'c_w_out': (1, 0), 'd_w_in': (1, 1),
    'd_conv_w': (1, 1), 'd_w_out': (1, 0),
}
SLAB_TRANSPOSED = ('ffn_w_gate', 'ffn_w_up', 'b_w_in', 'c_w_in', 'd_w_in')
SLAB_ROWS = ('ffn_w_down', 'a_w_out', 'c_w_out', 'd_w_out')
SLAB_NAMES = [n for n in WEIGHT_NAMES if n in SLAB_TRANSPOSED or n in SLAB_ROWS]
MISC_NAMES = [n for n in WEIGHT_NAMES if n not in SLAB_NAMES]
F32_GATHER = ('norm_g', 'c_conv_w', 'c_conv_b', 'c_ln_g', 'c_ln_b', 'd_conv_w')
N_CHIPS = 4


def _cparams(semantics):
    return pltpu.CompilerParams(dimension_semantics=semantics, vmem_limit_bytes=VMEM_LIMIT_BYTES)


def _div_tile(n, cap, mult):
    if n <= cap:
        return n
    for d in range(cap - cap % mult, 0, -mult):
        if n % d == 0:
            return d
    raise ValueError(f"no tile for {n} (cap {cap}, multiple of {mult})")


def _matmul(a, b, mode, out_dtype):
    if mode == 'nn':
        (M, K), N = a.shape, b.shape[1]
    elif mode == 'nt':
        (M, K), N = a.shape, b.shape[0]
    else:
        (K, M), N = a.shape, b.shape[1]
    if mode == 'tn':
        tm, tn, tk = _div_tile(M, 2816, LANES), _div_tile(N, 1536, LANES), _div_tile(K, 1024, 16)
    else:
        tm, tn, tk = _div_tile(M, 512, 16), _div_tile(N, 2816, LANES), _div_tile(K, 2816, LANES)
    nk = K // tk
    if mode == 'nn':
        a_spec = pl.BlockSpec((tm, tk), lambda i, j, k: (i, k))
        b_spec = pl.BlockSpec((tk, tn), lambda i, j, k: (k, j))
        dims = (((1,), (0,)), ((), ()))
    elif mode == 'nt':
        a_spec = pl.BlockSpec((tm, tk), lambda i, j, k: (i, k))
        b_spec = pl.BlockSpec((tn, tk), lambda i, j, k: (j, k))
        dims = (((1,), (1,)), ((), ()))
    else:
        a_spec = pl.BlockSpec((tk, tm), lambda i, j, k: (k, i))
        b_spec = pl.BlockSpec((tk, tn), lambda i, j, k: (k, j))
        dims = (((0,), (0,)), ((), ()))

    def body(a_ref, b_ref, o_ref, *acc):
        r = lax.dot_general(a_ref[...].astype(BF16), b_ref[...].astype(BF16), dims, preferred_element_type=F32)
        if nk == 1:
            o_ref[...] = r.astype(o_ref.dtype)
        else:
            k = pl.program_id(2)

            @pl.when(k == 0)
            def _():
                acc[0][...] = r

            @pl.when(k > 0)
            def _():
                acc[0][...] += r

            @pl.when(k == nk - 1)
            def _():
                o_ref[...] = acc[0][...].astype(o_ref.dtype)

    return pl.pallas_call(
        body,
        out_shape=jax.ShapeDtypeStruct((M, N), out_dtype),
        grid=(M // tm, N // tn, nk),
        in_specs=[a_spec, b_spec],
        out_specs=pl.BlockSpec((tm, tn), lambda i, j, k: (i, j)),
        scratch_shapes=[pltpu.VMEM((tm, tn), F32)] if nk > 1 else [],
        compiler_params=_cparams(("parallel", "parallel", "arbitrary")),
        name=f"mm_{mode}_{M}x{K}x{N}",
    )(a, b)


@functools.partial(jax.custom_vjp, nondiff_argnums=(2,))
def mm(a, w, out_dtype):
    return _matmul(a, w.astype(BF16), 'nn', out_dtype)


def _mm_fwd(a, w, out_dtype):
    wb = w.astype(BF16)
    return _matmul(a, wb, 'nn', out_dtype), (a, wb)


def _mm_bwd(out_dtype, res, g):
    a, wb = res
    return _matmul(g, wb, 'nt', a.dtype), _matmul(a, g, 'tn', F32)


mm.defvjp(_mm_fwd, _mm_bwd)


@functools.partial(jax.custom_vjp, nondiff_argnums=(3, 4))
def mm_slab(a, wb, delta, transposed, out_dtype):
    return _matmul(a, wb, 'nt' if transposed else 'nn', out_dtype)


def _mm_slab_fwd(a, wb, delta, transposed, out_dtype):
    return _matmul(a, wb, 'nt' if transposed else 'nn', out_dtype), (a, wb)


def _mm_slab_bwd(transposed, out_dtype, res, g):
    a, wb = res
    if transposed:
        return _matmul(g, wb, 'nn', a.dtype), jnp.zeros_like(wb), _matmul(g, a, 'tn', BF16)
    return _matmul(g, wb, 'nt', a.dtype), jnp.zeros_like(wb), _matmul(a, g, 'tn', BF16)


mm_slab.defvjp(_mm_slab_fwd, _mm_slab_bwd)


def _row_tile(n_rows, widest):
    t = 1 << max(3, int(math.log2(max(8, ROW_TILE_ELEMS // max(widest, 1)))))
    return min(n_rows, min(t, 2048))


def _rw_forward(fn, rows, params, name):
    n_rows = rows[0].shape[0]
    tile = _row_tile(n_rows, max(r.shape[1] for r in rows))
    outs = jax.eval_shape(fn, *[jax.ShapeDtypeStruct((tile, r.shape[1]), r.dtype) for r in rows],
                          *[jax.ShapeDtypeStruct(p.shape, p.dtype) for p in params])
    n_in = len(rows) + len(params)

    def body(*refs):
        res = fn(*[r[...] for r in refs[:n_in]])
        for o_ref, o in zip(refs[n_in:], res):
            o_ref[...] = o

    return pl.pallas_call(
        body,
        out_shape=[jax.ShapeDtypeStruct((n_rows, o.shape[1]), o.dtype) for o in outs],
        grid=(n_rows // tile,),
        in_specs=[pl.BlockSpec((tile, r.shape[1]), lambda i: (i, 0)) for r in rows]
        + [pl.BlockSpec(p.shape, lambda i: (0, 0)) for p in params],
        out_specs=[pl.BlockSpec((tile, o.shape[1]), lambda i: (i, 0)) for o in outs],
        compiler_params=_cparams(("parallel",)),
        name=name,
    )(*rows, *params)


def _rw_backward(fn, rows, params, cts, n_const, name):
    n_rows = rows[0].shape[0]
    tile = _row_tile(n_rows, max([r.shape[1] for r in rows] + [c.shape[1] for c in cts]))
    n_r, n_p, n_c = len(rows), len(params), len(cts)
    n_diff = n_r - n_const

    def body(*refs):
        row_vals = [r[...] for r in refs[:n_r]]
        par_vals = [r[...] for r in refs[n_r:n_r + n_p]]
        ct_vals = tuple(r[...] for r in refs[n_r + n_p:n_r + n_p + n_c])
        out_refs = refs[n_r + n_p + n_c:]

        def f(*diff):
            return fn(*diff[:n_diff], *row_vals[n_diff:], *diff[n_diff:])

        _, vjp = jax.vjp(f, *row_vals[:n_diff], *par_vals)
        grads = vjp(ct_vals)
        for o_ref, g in zip(out_refs[:n_diff], grads[:n_diff]):
            o_ref[...] = g
        i = pl.program_id(0)
        for o_ref, g in zip(out_refs[n_diff:], grads[n_diff:]):
            @pl.when(i == 0)
            def _(o_ref=o_ref, g=g):
                o_ref[...] = g

            @pl.when(i > 0)
            def _(o_ref=o_ref, g=g):
                o_ref[...] += g

    res = pl.pallas_call(
        body,
        out_shape=[jax.ShapeDtypeStruct(r.shape, r.dtype) for r in rows[:n_diff]]
        + [jax.ShapeDtypeStruct(p.shape, p.dtype) for p in params],
        grid=(n_rows // tile,),
        in_specs=[pl.BlockSpec((tile, r.shape[1]), lambda i: (i, 0)) for r in rows]
        + [pl.BlockSpec(p.shape, lambda i: (0, 0)) for p in params]
        + [pl.BlockSpec((tile, c.shape[1]), lambda i: (i, 0)) for c in cts],
        out_specs=[pl.BlockSpec((tile, r.shape[1]), lambda i: (i, 0)) for r in rows[:n_diff]]
        + [pl.BlockSpec(p.shape, lambda i: (0, 0)) for p in params],
        compiler_params=_cparams(("arbitrary",)),
        name=name + "_bwd",
    )(*rows, *params, *cts)
    return list(res[:n_diff]), list(res[n_diff:])


def rowwise(fn, rows, params, name, n_const=0):
    rows, params = list(rows), list(params)

    @jax.custom_vjp
    def op(rows, params):
        return tuple(_rw_forward(fn, rows, params, name))

    def op_fwd(rows, params):
        return tuple(_rw_forward(fn, rows, params, name)), (rows, params)

    def op_bwd(res, cts):
        rows, params = res
        d_rows, d_params = _rw_backward(fn, rows, params, list(cts), n_const, name)
        d_rows = d_rows + [jnp.zeros_like(r) for r in rows[len(rows) - n_const:]]
        return d_rows, d_params

    op.defvjp(op_fwd, op_bwd)
    return op(rows, params)


def _rms(x, g):
    xf = x.astype(F32)
    return xf * lax.rsqrt(jnp.mean(xf * xf, axis=-1, keepdims=True) + EPS) * g


ATTN_TILE = 1024
ATTN_SPAN = 2048
ATTN_CHUNK = 512

NT_DIMS = (((1,), (1,)), ((), ()))


def _attn_tiles(stay, stream):
    span = min(ATTN_SPAN, stream)
    return min(ATTN_TILE, stay), span, min(ATTN_CHUNK, span)


def _flash_fwd(q, k, v, qk_mult):
    N, Lq, dk = q.shape
    Lk, dv = k.shape[1], v.shape[2]
    tq, span, ch = _attn_tiles(Lq, Lk)
    tq = min(Lq, tq * max(1, ATTN_TILE // span))
    nk = Lk // span
    v_ones = jnp.concatenate([v, jnp.ones((N, Lk, 1), v.dtype), jnp.zeros((N, Lk, LANES - dv - 1), v.dtype)], axis=2)

    def body(q_ref, k_ref, v_ref, o_ref, lse_ref, m_s, acc_s):
        j = pl.program_id(2)

        @pl.when(j == 0)
        def _():
            m_s[...] = jnp.full_like(m_s, NEG)
            acc_s[...] = jnp.zeros_like(acc_s)

        qv = q_ref[0]
        s = []
        for c in range(span // ch):
            sc = lax.dot_general(qv, k_ref[0, c * ch:(c + 1) * ch, :], NT_DIMS, preferred_element_type=F32)
            s.append(sc if qk_mult == 1.0 else sc * qk_mult)
        s_max = functools.reduce(jnp.maximum, s)
        m_new = jnp.maximum(m_s[...], jnp.max(s_max, axis=-1, keepdims=True))
        pv = None
        for c, sc in enumerate(s):
            p = jnp.exp2(sc - m_new).astype(BF16)
            d = jnp.dot(p, v_ref[0, c * ch:(c + 1) * ch, :], preferred_element_type=F32)
            pv = d if c == 0 else pv + d
        acc_s[...] = jnp.exp2(m_s[...] - m_new) * acc_s[...] + pv
        m_s[...] = m_new

        @pl.when(j == nk - 1)
        def _():
            acc = acc_s[...]
            l = acc[:, dv:dv + 1]
            o_ref[0] = (acc[:, :dv] / l).astype(o_ref.dtype)
            lse_ref[0] = m_s[...] + jnp.log(l) * LOG2E

    return pl.pallas_call(
        body,
        out_shape=[jax.ShapeDtypeStruct((N, Lq, dv), BF16), jax.ShapeDtypeStruct((N, Lq, 1), F32)],
        grid=(N, Lq // tq, nk),
        in_specs=[pl.BlockSpec((1, tq, dk), lambda n, i, j: (n, i, 0)),
                  pl.BlockSpec((1, span, dk), lambda n, i, j: (n, j, 0)),
                  pl.BlockSpec((1, span, LANES), lambda n, i, j: (n, j, 0))],
        out_specs=[pl.BlockSpec((1, tq, dv), lambda n, i, j: (n, i, 0)),
                   pl.BlockSpec((1, tq, 1), lambda n, i, j: (n, i, 0))],
        scratch_shapes=[pltpu.VMEM((tq, 1), F32), pltpu.VMEM((tq, LANES), F32)],
        compiler_params=_cparams(("parallel", "parallel", "arbitrary")),
        name=f"flash_fwd_{N}x{Lq}x{Lk}x{dk}",
    )(q, k, v_ones)


def _flash_bwd(q, k, v, do, lse_row, dd_row, qk_mult, out_mult):
    N, Lq, dk = q.shape
    Lk, dv = k.shape[1], v.shape[2]
    tk, span, ch = _attn_tiles(Lk, Lq)
    nq, nkt = Lq // span, Lk // tk

    def body(q_ref, k_ref, v_ref, do_ref, lse_ref, dd_ref, dq_ref, dk_ref, dv_ref, dq_s, dk_s, dv_s):
        i, j = pl.program_id(1), pl.program_id(2)

        @pl.when((i == 0) & (j == 0))
        def _():
            dq_s[...] = jnp.zeros_like(dq_s)

        @pl.when(j == 0)
        def _():
            dk_s[...] = jnp.zeros_like(dk_s)
            dv_s[...] = jnp.zeros_like(dv_s)

        kv_, vv = k_ref[0], v_ref[0]
        for c in range(span // ch):
            qc = q_ref[0, c * ch:(c + 1) * ch, :]
            doc = do_ref[0, c * ch:(c + 1) * ch, :]
            st = lax.dot_general(kv_, qc, NT_DIMS, preferred_element_type=F32)
            if qk_mult != 1.0:
                st = st * qk_mult
            pt = jnp.exp2(st - lse_ref[0, :, c * ch:(c + 1) * ch])
            dv_s[...] += jnp.dot(pt.astype(BF16), doc, preferred_element_type=F32)
            dpt = lax.dot_general(vv, doc, NT_DIMS, preferred_element_type=F32)
            dst = (pt * (dpt - dd_ref[0, :, c * ch:(c + 1) * ch])).astype(BF16)
            dk_s[...] += jnp.dot(dst, qc, preferred_element_type=F32)
            rows = pl.ds(pl.multiple_of(j * span + c * ch, ch), ch)
            dq_s[rows, :] += lax.dot_general(dst, kv_, (((0,), (0,)), ((), ())), preferred_element_type=F32)

        @pl.when(j == nq - 1)
        def _():
            dk_ref[0] = (dk_s[...] * out_mult).astype(dk_ref.dtype)
            dv_ref[0] = dv_s[...].astype(dv_ref.dtype)

        @pl.when((i == nkt - 1) & (j == nq - 1))
        def _():
            dq_ref[0] = (dq_s[...] * out_mult).astype(dq_ref.dtype)

    return pl.pallas_call(
        body,
        out_shape=[jax.ShapeDtypeStruct(q.shape, q.dtype), jax.ShapeDtypeStruct(k.shape, k.dtype),
                   jax.ShapeDtypeStruct(v.shape, v.dtype)],
        grid=(N, nkt, nq),
        in_specs=[pl.BlockSpec((1, span, dk), lambda n, i, j: (n, j, 0)),
                  pl.BlockSpec((1, tk, dk), lambda n, i, j: (n, i, 0)),
                  pl.BlockSpec((1, tk, dv), lambda n, i, j: (n, i, 0)),
                  pl.BlockSpec((1, span, dv), lambda n, i, j: (n, j, 0)),
                  pl.BlockSpec((1, 1, span), lambda n, i, j: (n, 0, j)),
                  pl.BlockSpec((1, 1, span), lambda n, i, j: (n, 0, j))],
        out_specs=[pl.BlockSpec((1, Lq, dk), lambda n, i, j: (n, 0, 0)),
                   pl.BlockSpec((1, tk, dk), lambda n, i, j: (n, i, 0)),
                   pl.BlockSpec((1, tk, dv), lambda n, i, j: (n, i, 0))],
        scratch_shapes=[pltpu.VMEM((Lq, dk), F32), pltpu.VMEM((tk, dk), F32), pltpu.VMEM((tk, dv), F32)],
        compiler_params=_cparams(("parallel", "arbitrary", "arbitrary")),
        name=f"flash_bwd_{N}x{Lq}x{Lk}x{dk}",
    )(q, k, v, do, lse_row, dd_row)


def _attn_delta(do, o, dlse):
    N, L, dv = o.shape

    def fn(do_t, o_t, dl_t):
        return (jnp.sum(do_t.astype(F32) * o_t.astype(F32), axis=-1, keepdims=True) - dl_t,)

    (dd,) = _rw_forward(fn, [do.reshape(N * L, dv), o.reshape(N * L, dv), dlse.reshape(N * L, 1)], [], "attn_delta")
    return dd.reshape(N, L, 1)


def full_attention(q, k, v, scale, q_prescaled):
    qk_mult = 1.0 if q_prescaled else scale * LOG2E
    grad_mult = LN2 if q_prescaled else scale

    @jax.custom_vjp
    def op(q, k, v):
        return _flash_fwd(q, k, v, qk_mult)[0]

    def op_fwd(q, k, v):
        o, lse = _flash_fwd(q, k, v, qk_mult)
        return o, (q, k, v, o, lse)

    def op_bwd(res, do):
        q, k, v, o, lse = res
        N, Lq, _ = q.shape
        dd = _attn_delta(do, o, jnp.zeros_like(lse))
        return tuple(_flash_bwd(q, k, v, do, lse.reshape(N, 1, Lq), dd.reshape(N, 1, Lq), qk_mult, grad_mult))

    op.defvjp(op_fwd, op_bwd)
    return op(q, k, v)


BAND_TILE = 128
BAND_BATCH = 8


def _band_specs(L, d):
    t = min(BAND_TILE, L)
    nt = L // t
    cur = lambda n, i: (n, i, 0)
    prev = lambda n, i: (n, jnp.maximum(i - 1, 0), 0)
    nxt = lambda n, i: (n, jnp.minimum(i + 1, nt - 1), 0)
    return t, nt, cur, prev, nxt


def _band_mask(i, t, L, half):
    a = lax.broadcasted_iota(jnp.int32, (t, 3 * t), 0)
    b = lax.broadcasted_iota(jnp.int32, (t, 3 * t), 1)
    dist = jnp.abs(b - t - a)
    other = i * t - t + b
    mask = (dist <= half) & (other >= 0) & (other < L)
    return dist.astype(F32), mask


def _band_fwd(q, k, v, slope, half):
    N, L, d = q.shape
    t, nt, cur, prev, nxt = _band_specs(L, d)
    nb = BAND_BATCH
    scale = d ** -0.5

    def body(sl_ref, q_ref, kp, kc, kn, vp, vc, vn, o_ref, lse_ref):
        i = pl.program_id(1)
        kcat = jnp.concatenate([kp[...], kc[...], kn[...]], axis=1)
        vcat = jnp.concatenate([vp[...], vc[...], vn[...]], axis=1)
        s = lax.dot_general(q_ref[...], kcat, (((2,), (2,)), ((0,), (0,))), preferred_element_type=F32) * scale
        dist, mask = _band_mask(i, t, L, half)
        s = jnp.where(mask[None], s - sl_ref[...] * dist[None], NEG)
        m = jnp.max(s, axis=-1, keepdims=True)
        p = jnp.exp(s - m)
        l = jnp.sum(p, axis=-1, keepdims=True)
        o = lax.dot_general(p.astype(BF16), vcat, (((2,), (1,)), ((0,), (0,))), preferred_element_type=F32)
        o_ref[...] = o / l
        lse_ref[...] = m + jnp.log(l)

    blk = lambda f: pl.BlockSpec((nb, t, d), f)
    return pl.pallas_call(
        body,
        out_shape=[jax.ShapeDtypeStruct((N, L, d), F32), jax.ShapeDtypeStruct((N, L, 1), F32)],
        grid=(N // nb, nt),
        in_specs=[pl.BlockSpec((nb, 1, 1), lambda n, i: (n, 0, 0)), blk(cur),
                  blk(prev), blk(cur), blk(nxt), blk(prev), blk(cur), blk(nxt)],
        out_specs=[blk(cur), pl.BlockSpec((nb, t, 1), cur)],
        compiler_params=_cparams(("parallel", "parallel")),
        name=f"band_fwd_{N}x{L}",
    )(slope, q, k, k, k, v, v, v)


def _band_dq(q, k, v, slope, do, lse, dd, half):
    N, L, d = q.shape
    t, nt, cur, prev, nxt = _band_specs(L, d)
    nb = BAND_BATCH
    scale = d ** -0.5

    def body(sl_ref, q_ref, kp, kc, kn, vp, vc, vn, do_ref, lse_ref, dd_ref, dq_ref):
        i = pl.program_id(1)
        kcat = jnp.concatenate([kp[...], kc[...], kn[...]], axis=1)
        vcat = jnp.concatenate([vp[...], vc[...], vn[...]], axis=1)
        s = lax.dot_general(q_ref[...], kcat, (((2,), (2,)), ((0,), (0,))), preferred_element_type=F32) * scale
        dist, mask = _band_mask(i, t, L, half)
        p = jnp.where(mask[None], jnp.exp(s - sl_ref[...] * dist[None] - lse_ref[...]), 0.0)
        dp = lax.dot_general(do_ref[...].astype(BF16), vcat, (((2,), (2,)), ((0,), (0,))),
                             preferred_element_type=F32)
        ds = (p * (dp - dd_ref[...])).astype(BF16)
        dq = lax.dot_general(ds, kcat, (((2,), (1,)), ((0,), (0,))), preferred_element_type=F32)
        dq_ref[...] = (dq * scale).astype(dq_ref.dtype)

    blk = lambda f: pl.BlockSpec((nb, t, d), f)
    col = pl.BlockSpec((nb, t, 1), cur)
    return pl.pallas_call(
        body,
        out_shape=jax.ShapeDtypeStruct(q.shape, q.dtype),
        grid=(N // nb, nt),
        in_specs=[pl.BlockSpec((nb, 1, 1), lambda n, i: (n, 0, 0)), blk(cur),
                  blk(prev), blk(cur), blk(nxt), blk(prev), blk(cur), blk(nxt), blk(cur), col, col],
        out_specs=blk(cur),
        compiler_params=_cparams(("parallel", "parallel")),
        name=f"band_dq_{N}x{L}",
    )(slope, q, k, k, k, v, v, v, do, lse, dd)


def _band_dkv(q, k, v, slope, do, lse_row, dd_row, half):
    N, L, d = q.shape
    t, nt, cur, prev, nxt = _band_specs(L, d)
    nb = BAND_BATCH
    scale = d ** -0.5
    rcur = lambda n, i: (n, 0, i)
    rprev = lambda n, i: (n, 0, jnp.maximum(i - 1, 0))
    rnxt = lambda n, i: (n, 0, jnp.minimum(i + 1, nt - 1))

    def body(sl_ref, k_ref, v_ref, qp, qc, qn, dop, doc, don, lp, lc, ln, dp_, dc_, dn_, dk_ref, dv_ref):
        i = pl.program_id(1)
        qcat = jnp.concatenate([qp[...], qc[...], qn[...]], axis=1)
        docat = jnp.concatenate([dop[...], doc[...], don[...]], axis=1).astype(BF16)
        lrow = jnp.concatenate([lp[...], lc[...], ln[...]], axis=2)
        drow = jnp.concatenate([dp_[...], dc_[...], dn_[...]], axis=2)
        st = lax.dot_general(k_ref[...], qcat, (((2,), (2,)), ((0,), (0,))), preferred_element_type=F32) * scale
        dist, mask = _band_mask(i, t, L, half)
        pt = jnp.where(mask[None], jnp.exp(st - sl_ref[...] * dist[None] - lrow), 0.0)
        dv = lax.dot_general(pt.astype(BF16), docat, (((2,), (1,)), ((0,), (0,))), preferred_element_type=F32)
        dpt = lax.dot_general(v_ref[...], docat, (((2,), (2,)), ((0,), (0,))), preferred_element_type=F32)
        dst = (pt * (dpt - drow)).astype(BF16)
        dk = lax.dot_general(dst, qcat, (((2,), (1,)), ((0,), (0,))), preferred_element_type=F32)
        dk_ref[...] = (dk * scale).astype(dk_ref.dtype)
        dv_ref[...] = dv.astype(dv_ref.dtype)

    blk = lambda f: pl.BlockSpec((nb, t, d), f)
    row = lambda f: pl.BlockSpec((nb, 1, t), f)
    return pl.pallas_call(
        body,
        out_shape=[jax.ShapeDtypeStruct(k.shape, k.dtype), jax.ShapeDtypeStruct(v.shape, v.dtype)],
        grid=(N // nb, nt),
        in_specs=[pl.BlockSpec((nb, 1, 1), lambda n, i: (n, 0, 0)), blk(cur), blk(cur),
                  blk(prev), blk(cur), blk(nxt), blk(prev), blk(cur), blk(nxt),
                  row(rprev), row(rcur), row(rnxt), row(rprev), row(rcur), row(rnxt)],
        out_specs=[blk(cur), blk(cur)],
        compiler_params=_cparams(("parallel", "parallel")),
        name=f"band_dkv_{N}x{L}",
    )(slope, k, v, q, q, q, do, do, do, lse_row, lse_row, lse_row, dd_row, dd_row, dd_row)


def band_attention(q, k, v, slope, half):
    @jax.custom_vjp
    def op(q, k, v, slope):
        return tuple(_band_fwd(q, k, v, slope, half))

    def op_fwd(q, k, v, slope):
        o, lse = _band_fwd(q, k, v, slope, half)
        return (o, lse), (q, k, v, slope, o, lse)

    def op_bwd(res, cts):
        q, k, v, slope, o, lse = res
        do, dlse = cts
        N, L, _ = q.shape
        dd = _attn_delta(do, o, dlse)
        dq = _band_dq(q, k, v, slope, do, lse, dd, half)
        dk, dv = _band_dkv(q, k, v, slope, do, lse.reshape(N, 1, L), dd.reshape(N, 1, L), half)
        return dq, dk, dv, jnp.zeros_like(slope)

    op.defvjp(op_fwd, op_bwd)
    return op(q, k, v, slope)


def _conv_taps(width):
    return [CONV_PAD - width // 2 + j for j in range(width)]


def _dwconv_call(up, w, width):
    S, C = up.shape[0] - 2 * CONV_PAD, up.shape[1]
    ch = min(CONV_CHUNK, S)
    offs = _conv_taps(width)

    def body(u_ref, w_ref, o_ref):
        def chunk(ci, carry):
            base = pl.multiple_of(ci * ch, ch)
            win = u_ref[pl.ds(base, ch + 2 * CONV_PAD), :]
            acc = jnp.zeros((ch, LANES), F32)
            for j, off in enumerate(offs):
                acc = acc + win[off:off + ch, :] * w_ref[j:j + 1, :]
            o_ref[pl.ds(base, ch), :] = acc
            return carry

        lax.fori_loop(0, S // ch, chunk, 0)

    return pl.pallas_call(
        body,
        out_shape=jax.ShapeDtypeStruct((S, C), F32),
        grid=(C // LANES,),
        in_specs=[pl.BlockSpec((S + 2 * CONV_PAD, LANES), lambda c: (0, c)),
                  pl.BlockSpec((w.shape[0], LANES), lambda c: (0, c))],
        out_specs=pl.BlockSpec((S, LANES), lambda c: (0, c)),
        compiler_params=_cparams(("parallel",)),
        name=f"dwconv{width}_{S}x{C}",
    )(up, w)


def _dwconv_dw_call(up, g, width, w_rows):
    S, C = g.shape
    ch = min(CONV_CHUNK, S)
    offs = _conv_taps(width)

    def body(u_ref, g_ref, o_ref):
        def chunk(ci, accs):
            base = pl.multiple_of(ci * ch, ch)
            win = u_ref[pl.ds(base, ch + 2 * CONV_PAD), :]
            gt = g_ref[pl.ds(base, ch), :]
            return tuple(a + (gt * win[off:off + ch, :]).reshape(ch // 8, 8, LANES).sum(axis=0)
                         for a, off in zip(accs, offs))

        accs = lax.fori_loop(0, S // ch, chunk, tuple(jnp.zeros((8, LANES), F32) for _ in offs))
        o_ref[...] = jnp.zeros_like(o_ref)
        for j, a in enumerate(accs):
            o_ref[j:j + 1, :] = jnp.sum(a, axis=0, keepdims=True)

    return pl.pallas_call(
        body,
        out_shape=jax.ShapeDtypeStruct((w_rows, C), F32),
        grid=(C // LANES,),
        in_specs=[pl.BlockSpec((S + 2 * CONV_PAD, LANES), lambda c: (0, c)),
                  pl.BlockSpec((S, LANES), lambda c: (0, c))],
        out_specs=pl.BlockSpec((w_rows, LANES), lambda c: (0, c)),
        compiler_params=_cparams(("parallel",)),
        name=f"dwconv{width}_dw_{S}x{C}",
    )(up, g)


def _pad_rows(u):
    return jnp.pad(u, ((CONV_PAD, CONV_PAD), (0, 0)))


def _pad_taps(w):
    return jnp.pad(w, ((0, -w.shape[0] % 8), (0, 0)))


def depthwise_conv(u, w):
    width = w.shape[0]

    @jax.custom_vjp
    def op(u, w):
        return _dwconv_call(_pad_rows(u), _pad_taps(w), width)

    def op_fwd(u, w):
        up = _pad_rows(u)
        return _dwconv_call(up, _pad_taps(w), width), (up, w)

    def op_bwd(res, g):
        up, w = res
        du = _dwconv_call(_pad_rows(g), _pad_taps(w[::-1]), width)
        dw = _dwconv_dw_call(up, g, width, _pad_taps(w).shape[0])[:width]
        return du, dw

    op.defvjp(op_fwd, op_bwd)
    return op(u, w)


def _loss_call(y, target):
    S, D = y.shape
    tile = _row_tile(S, D)

    def body(y_ref, t_ref, o_ref):
        d = y_ref[...] - t_ref[...]
        part = jnp.sum(jnp.sum(d * d, axis=1, keepdims=True), axis=0, keepdims=True) * (0.5 / D)

        @pl.when(pl.program_id(0) == 0)
        def _():
            o_ref[...] = jnp.zeros_like(o_ref)

        o_ref[...] += jnp.broadcast_to(part, o_ref.shape)

    out = pl.pallas_call(
        body,
        out_shape=jax.ShapeDtypeStruct((8, LANES), F32),
        grid=(S // tile,),
        in_specs=[pl.BlockSpec((tile, D), lambda i: (i, 0)), pl.BlockSpec((tile, D), lambda i: (i, 0))],
        out_specs=pl.BlockSpec((8, LANES), lambda i: (0, 0)),
        compiler_params=_cparams(("arbitrary",)),
        name="loss_head",
    )(y, target)
    return out[0, 0]


@jax.custom_vjp
def loss_head(y, target):
    return _loss_call(y, target)


def _loss_fwd(y, target):
    return _loss_call(y, target), (y, target)


def _loss_bwd(res, g):
    y, target = res
    inv_d = 1.0 / y.shape[1]

    def fn(y_t, t_t, g_p):
        return ((y_t - t_t) * (g_p * inv_d),)

    (dy,) = _rw_forward(fn, [y, target], [g.reshape(1, 1).astype(F32)], "loss_head_bwd")
    return dy, jnp.zeros_like(target)


loss_head.defvjp(_loss_fwd, _loss_bwd)


ANY = pl.BlockSpec(memory_space=pl.ANY)


AG_ICI_PIECES = 4
AG_D2D_PIECES = 2
D2D_PIECES = 16
ICI_PIECES = 4
LOCAL_PIECES = 8


def _local_copies(src, dst, sems, n):
    return [pltpu.make_async_copy(src.at[b], dst.at[b], sems.at[b]) for b in range(n)]


def all_gather_chips(mine, name):
    R, C = mine.shape
    ni, nf = AG_ICI_PIECES, AG_D2D_PIECES
    rows = R // (2 * ni * nf)
    n_ici, n_all = 3 * ni, 3 * ni + 3 * ni * nf

    def body(in_ref, out_ref, send_sems, recv_sems, local_sems):
        x, y, c = lax.axis_index("x"), lax.axis_index("y"), lax.axis_index("c")
        own = [pltpu.make_async_copy(in_ref.at[h, a], out_ref.at[2 * x + y, h, a], local_sems.at[h * ni + a])
               for h in range(2) for a in range(ni)]
        for cp in own:
            cp.start()
        chips = [(1 - x, y), (x, 1 - y), (1 - x, 1 - y)]

        def copy(sem, src, dst, to):
            return pltpu.make_async_remote_copy(src_ref=src, dst_ref=dst, send_sem=send_sems.at[sem],
                                                recv_sem=recv_sems.at[sem], device_id=to, device_id_type=MESH)

        sends = []
        for k, (px, py) in enumerate(chips):
            for a in range(ni):
                sends.append(copy(k * ni + a, in_ref.at[c, a], out_ref.at[2 * x + y, c, a], (px, py, c)))
                sends[-1].start()
        for k, (px, py) in enumerate(chips):
            for a in range(ni):
                landed = out_ref.at[2 * px + py, c, a]
                copy(k * ni + a, in_ref.at[c, a], landed, (px, py, c)).wait_recv()
                for b in range(nf):
                    sends.append(copy(n_ici + (k * ni + a) * nf + b, landed.at[b], landed.at[b], (x, y, 1 - c)))
                    sends[-1].start()
        for k, (px, py) in enumerate(chips):
            for a in range(ni):
                for b in range(nf):
                    theirs = out_ref.at[2 * px + py, 1 - c, a, b]
                    copy(n_ici + (k * ni + a) * nf + b, theirs, theirs, (x, y, 1 - c)).wait_recv()
        for cp in sends:
            cp.wait_send()
        for cp in own:
            cp.wait()

    out = pl.pallas_call(
        body,
        out_shape=jax.ShapeDtypeStruct((N_CHIPS, 2, ni, nf, rows, C), mine.dtype),
        in_specs=[ANY],
        out_specs=ANY,
        scratch_shapes=[pltpu.SemaphoreType.DMA((n_all,)), pltpu.SemaphoreType.DMA((n_all,)),
                        pltpu.SemaphoreType.DMA((2 * ni,))],
        name=name,
    )(mine.reshape(2, ni, nf, rows, C))
    return out.reshape(N_CHIPS, R, C)


def swap_halves_with_sibling(parts):
    n_s, _, H, C = parts.shape
    n = ICI_PIECES

    def body(in_ref, out_ref, send_sems, recv_sems):
        x, y, c = lax.axis_index("x"), lax.axis_index("y"), lax.axis_index("c")

        def copy(s, b, half):
            return pltpu.make_async_remote_copy(
                src_ref=in_ref.at[s, half, b], dst_ref=out_ref.at[s, b], send_sem=send_sems.at[s * n + b],
                recv_sem=recv_sems.at[s * n + b], device_id=(x, y, 1 - c), device_id_type=MESH)

        sends = [copy(s, b, 1 - c) for s in range(n_s) for b in range(n)]
        for cp in sends:
            cp.start()
        for s in range(n_s):
            for b in range(n):
                copy(s, b, 1 - c).wait_recv()
        for cp in sends:
            cp.wait_send()

    out = pl.pallas_call(
        body,
        out_shape=jax.ShapeDtypeStruct((n_s, n, H // n, C), parts.dtype),
        in_specs=[ANY],
        out_specs=ANY,
        scratch_shapes=[pltpu.SemaphoreType.DMA((n_s * n,)), pltpu.SemaphoreType.DMA((n_s * n,))],
        name="grad_swap_halves",
    )(parts.reshape(n_s, 2, n, H // n, C))
    return out.reshape(n_s, H, C)


def add_pairs(parts, theirs, core):
    n_s, _, H, C = parts.shape
    tile = _div_tile(H, 512, 16)

    def body(c_ref, a_ref, b_ref, o_ref):
        o_ref[...] = (a_ref[...].astype(F32) + b_ref[...].astype(F32)).astype(o_ref.dtype)

    return pl.pallas_call(
        body,
        out_shape=jax.ShapeDtypeStruct((n_s, 1, H, C), parts.dtype),
        grid_spec=pltpu.PrefetchScalarGridSpec(
            num_scalar_prefetch=1,
            grid=(n_s, H // tile),
            in_specs=[pl.BlockSpec((1, 1, tile, C), lambda s, i, c_ref: (s, c_ref[0], i, 0)),
                      pl.BlockSpec((1, 1, tile, C), lambda s, i, c_ref: (s, 0, i, 0))],
            out_specs=pl.BlockSpec((1, 1, tile, C), lambda s, i, c_ref: (s, 0, i, 0)),
        ),
        compiler_params=_cparams(("parallel", "parallel")),
        name="grad_add_pairs",
    )(core.reshape(1).astype(jnp.int32), parts, theirs.reshape(n_s, 1, H, C)).reshape(n_s, H, C)


def scatter_to_chips(parts):
    n_s, H, C = parts.shape
    n = ICI_PIECES

    def body(in_ref, out_ref, send_sems, recv_sems, local_sems):
        x, y, c = lax.axis_index("x"), lax.axis_index("y"), lax.axis_index("c")
        own = _local_copies(in_ref.at[2 * x + y], out_ref.at[2 * x + y], local_sems, n)
        for cp in own:
            cp.start()
        chips = [(1 - x, y), (x, 1 - y), (1 - x, 1 - y)]

        def copy(k, b, src_slot, dst_slot, to):
            return pltpu.make_async_remote_copy(
                src_ref=in_ref.at[src_slot, b], dst_ref=out_ref.at[dst_slot, b], send_sem=send_sems.at[k * n + b],
                recv_sem=recv_sems.at[k * n + b], device_id=to, device_id_type=MESH)

        sends = [copy(k, b, 2 * px + py, 2 * x + y, (px, py, c)) for k, (px, py) in enumerate(chips) for b in range(n)]
        for cp in sends:
            cp.start()
        for k, (px, py) in enumerate(chips):
            for b in range(n):
                copy(k, b, 2 * x + y, 2 * px + py, (px, py, c)).wait_recv()
        for cp in sends:
            cp.wait_send()
        for cp in own:
            cp.wait()

    out = pl.pallas_call(
        body,
        out_shape=jax.ShapeDtypeStruct((n_s, n, H // n, C), parts.dtype),
        in_specs=[ANY],
        out_specs=ANY,
        scratch_shapes=[pltpu.SemaphoreType.DMA((3 * n,)), pltpu.SemaphoreType.DMA((3 * n,)),
                        pltpu.SemaphoreType.DMA((n,))],
        name="grad_scatter",
    )(parts.reshape(n_s, n, H // n, C))
    return out.reshape(n_s, H, C)


def exchange_with_sibling(mine):
    H, C = mine.shape
    n = D2D_PIECES

    def body(in_ref, out_ref, send_sems, recv_sems, local_sems):
        x, y, c = lax.axis_index("x"), lax.axis_index("y"), lax.axis_index("c")
        own = _local_copies(in_ref, out_ref.at[c], local_sems, n)
        for cp in own:
            cp.start()

        def copy(b, half):
            return pltpu.make_async_remote_copy(
                src_ref=in_ref.at[b], dst_ref=out_ref.at[half, b], send_sem=send_sems.at[b], recv_sem=recv_sems.at[b],
                device_id=(x, y, 1 - c), device_id_type=MESH)

        sends = [copy(b, c) for b in range(n)]
        for cp in sends:
            cp.start()
        for b in range(n):
            copy(b, 1 - c).wait_recv()
        for cp in sends:
            cp.wait_send()
        for cp in own:
            cp.wait()

    out = pl.pallas_call(
        body,
        out_shape=jax.ShapeDtypeStruct((2, n, H // n, C), mine.dtype),
        in_specs=[ANY],
        out_specs=ANY,
        scratch_shapes=[pltpu.SemaphoreType.DMA((n,)), pltpu.SemaphoreType.DMA((n,)), pltpu.SemaphoreType.DMA((n,))],
        name="grad_sibling_exchange",
    )(mine.reshape(n, H // n, C))
    return out.reshape(2, H, C)


def sum_contributions(recv):
    n, H, C = recv.shape
    tile = _div_tile(H, 256, 16)

    def body(r_ref, o_ref):
        acc = r_ref[0].astype(F32)
        for j in range(1, n):
            acc = acc + r_ref[j].astype(F32)
        o_ref[...] = acc

    return pl.pallas_call(
        body,
        out_shape=jax.ShapeDtypeStruct((H, C), F32),
        grid=(H // tile,),
        in_specs=[pl.BlockSpec((n, tile, C), lambda i: (0, i, 0))],
        out_specs=pl.BlockSpec((tile, C), lambda i: (i, 0)),
        compiler_params=_cparams(("parallel",)),
        name="grad_sum",
    )(recv)


def adamw(w, g, m, v):
    shape = w.shape
    cols = shape[-1]
    rows = math.prod(shape[:-1])
    tile = _div_tile(rows, max(8, 256 * 1024 // cols // 8 * 8), 8)
    c1 = 1.0 - ADAM_B1 ** ADAM_STEP
    c2 = 1.0 - ADAM_B2 ** ADAM_STEP

    def body(w_ref, g_ref, m_ref, v_ref, d_ref, nm_ref, nv_ref):
        g_ = g_ref[...]
        nm = ADAM_B1 * m_ref[...] + (1.0 - ADAM_B1) * g_
        nv = ADAM_B2 * v_ref[...] + (1.0 - ADAM_B2) * (g_ * g_)
        d_ref[...] = -ADAM_LR * ((nm / c1) / (jnp.sqrt(nv / c2) + ADAM_EPS) + ADAM_WD * w_ref[...])
        nm_ref[...] = nm
        nv_ref[...] = nv

    spec = pl.BlockSpec((tile, cols), lambda i: (i, 0))
    outs = pl.pallas_call(
        body,
        out_shape=[jax.ShapeDtypeStruct((rows, cols), F32)] * 3,
        grid=(rows // tile,),
        in_specs=[spec] * 4,
        out_specs=[spec] * 3,
        compiler_params=_cparams(("parallel",)),
        name=f"adamw_{rows}x{cols}",
    )(*[t.reshape(rows, cols) for t in (w, g, m, v)])
    return [o.reshape(shape) for o in outs]


def _leaves(name, arr):
    n_lead = WEIGHT_LAYOUT[name][0]
    lead = arr.shape[:n_lead]
    flat = arr.reshape((-1,) + arr.shape[n_lead:])
    return [flat[i] for i in range(math.prod(lead))]


def _pad_pack_rows(buf):
    return jnp.pad(buf, ((0, -buf.shape[0] % PACK_ROW_MULT), (0, 0)))


def _pack_flat(pieces, dtype, cols):
    flat = jnp.concatenate([p.reshape(-1).astype(dtype) for p in pieces])
    flat = jnp.pad(flat, (0, -flat.shape[0] % (cols * PACK_ROW_MULT)))
    return flat.reshape(-1, cols)


def _unpack_flat(buf, shapes):
    lead = buf.shape[:-2]
    flat = buf.reshape(lead + (-1,))
    out, off = [], 0
    for shp in shapes:
        n = math.prod(shp)
        out.append(lax.slice_in_dim(flat, off, off + n, axis=len(lead)).reshape(lead + tuple(shp)))
        off += n
    return out


def _pack_slabs(slabs, dtype):
    return _pad_pack_rows(jnp.concatenate([s.astype(dtype) for s in slabs], axis=0))


def _local_slabs(name, shard):
    leaves = _leaves(name, shard)
    return [leaf.T for leaf in leaves] if name in SLAB_TRANSPOSED else leaves


def _chip_slice(leaf, ax, s):
    if ax is None:
        return leaf
    w = leaf.shape[ax] // N_CHIPS
    return lax.slice_in_dim(leaf, s * w, (s + 1) * w, axis=ax)


def gather_weights(shards, cols):
    slabs = [s for n in SLAB_NAMES for s in _local_slabs(n, shards[n])]
    slab_owner = [n for n in SLAB_NAMES for _ in _leaves(n, shards[n])]
    slab_buf = _pack_slabs(slabs, BF16)
    misc = {}
    for dtype in (BF16, F32):
        names = [n for n in MISC_NAMES if WEIGHT_LAYOUT[n][1] is not None and (n in F32_GATHER) == (dtype == F32)]
        pieces = [leaf for n in names for leaf in _leaves(n, shards[n])]
        owners = [n for n in names for _ in _leaves(n, shards[n])]
        buf = _pack_flat(pieces, dtype, cols)
        if dtype == BF16:
            gathered = all_gather_chips(jnp.concatenate([slab_buf, buf], axis=0), "weights_all_gather_bf16")
            slab_part, gathered = gathered[:, :slab_buf.shape[0]], gathered[:, slab_buf.shape[0]:]
        else:
            gathered = all_gather_chips(buf, "weights_all_gather_f32")
        for n, blk in zip(owners, _unpack_flat(gathered, [p.shape for p in pieces])):
            ax = WEIGHT_LAYOUT[n][1]
            misc.setdefault(n, []).append(jnp.concatenate([blk[s] for s in range(N_CHIPS)], axis=ax).astype(F32))
    for n in MISC_NAMES:
        if WEIGHT_LAYOUT[n][1] is None:
            misc[n] = [leaf.astype(F32) for leaf in _leaves(n, shards[n])]
    slab_full, off = {}, 0
    for n, s in zip(slab_owner, slabs):
        rows = s.shape[0]
        slab_full.setdefault(n, []).append(slab_part[:, off:off + rows].reshape(N_CHIPS * rows, cols))
        off += rows
    return slab_full, misc


def reduce_gradients(g_slab, g_misc, shards, cols):
    per_chip = []
    for s in range(N_CHIPS):
        slabs = []
        for n in SLAB_NAMES:
            for leaf in g_slab[n]:
                slabs.append(_chip_slice(leaf, 0, s))
        pieces = [_chip_slice(leaf, WEIGHT_LAYOUT[n][1], s) for n in MISC_NAMES for leaf in g_misc[n]]
        per_chip.append(jnp.concatenate([_pack_slabs(slabs, BF16), _pack_flat(pieces, BF16, cols)], axis=0))
    parts = jnp.stack(per_chip)
    R = parts.shape[1]
    parts = parts.reshape(N_CHIPS, 2, R // 2, cols)
    chip_parts = add_pairs(parts, swap_halves_with_sibling(parts), lax.axis_index("c"))
    recv = scatter_to_chips(chip_parts)
    red = exchange_with_sibling(sum_contributions(recv)).reshape(R, cols)

    out, off = {}, 0
    for n in SLAB_NAMES:
        leaves = []
        for slab in _local_slabs(n, shards[n]):
            rows = slab.shape[0]
            g = red[off:off + rows]
            leaves.append(g.T if n in SLAB_TRANSPOSED else g)
            off += rows
        out[n] = jnp.stack(leaves).reshape(shards[n].shape)
    off += -off % PACK_ROW_MULT
    misc_shapes = [shards[n].shape for n in MISC_NAMES]
    for n, g in zip(MISC_NAMES, _unpack_flat(red[off:], misc_shapes)):
        out[n] = g
    return out


def _silu(x):
    return x * jax.nn.sigmoid(x)


def _heads_first(t, heads):
    S = t.shape[0]
    return t.reshape(S, heads, -1).transpose(1, 0, 2)


def _heads_last(t):
    return t.transpose(1, 0, 2).reshape(t.shape[1], -1)


class SlabWeights:
    def __init__(self, values, deltas):
        self.values, self.deltas = values, deltas

    def matmul(self, a, name, leaf, out_dtype):
        return mm_slab(a, self.values[name][leaf], self.deltas[name][leaf], name in SLAB_TRANSPOSED, out_dtype)


def swiglu_half_step(x, h, slab, leaf, g_post, g_next):
    a = slab.matmul(h, 'ffn_w_gate', leaf, BF16)
    b = slab.matmul(h, 'ffn_w_up', leaf, BF16)
    (u,) = rowwise(lambda a_, b_: ((_silu(a_.astype(F32)) * b_.astype(F32)).astype(BF16),), [a, b], [], "swiglu_act")
    y = slab.matmul(u, 'ffn_w_down', leaf, F32)
    return residual_norm(x, y, 0.5, g_post, g_next)


def residual_norm(x, y, coef, g_post, g_next):
    if g_next is None:
        (xn,) = rowwise(lambda x_, y_, g1: (x_ + coef * _rms(y_, g1),), [x, y], [g_post], "residual")
        return xn, None

    def fn(x_, y_, g1, g2):
        xn = x_ + coef * _rms(y_, g1)
        return xn, _rms(xn, g2).astype(BF16)

    return rowwise(fn, [x, y], [g_post, g_next], "residual_norm")


def memory_attention(q_mem, memn, w_kv):
    kv = mm(memn, w_kv, BF16)
    n_mem = kv.shape[0]
    kv = kv.reshape(n_mem, 2, MEM_HEADS, HEAD_DIM).transpose(1, 2, 0, 3)
    o = full_attention(_heads_first(q_mem.astype(BF16), MEM_HEADS), kv[0], kv[1], HEAD_DIM ** -0.5, False)
    return _heads_last(o)


def mla_mixer(h, cos, sin, w_in, q_g, kv_g, w_uq, w_ukv):
    S = h.shape[0]
    a_in = MLA_Q_LORA + MLA_KV_LORA + MLA_ROPE + MEM_WIDTH
    z = mm(h, jnp.pad(w_in, ((0, 0), (0, -a_in % LANES))), F32)
    o1, o2, o3 = MLA_Q_LORA, MLA_Q_LORA + MLA_KV_LORA, MLA_Q_LORA + MLA_KV_LORA + MLA_ROPE
    c_q, c_kv, k_r, q_mem = z[:, :o1], z[:, o1:o2], z[:, o2:o3], z[:, o3:a_in]
    cqn, ckvn = rowwise(lambda a, b, ga, gb: (_rms(a, ga).astype(BF16), _rms(b, gb).astype(BF16)),
                        [c_q, c_kv], [q_g, kv_g], "mla_lora_norm")
    q = mm(cqn, w_uq, F32).reshape(S, MLA_HEADS, MLA_NOPE + MLA_ROPE)
    kv = mm(ckvn, w_ukv, BF16).reshape(S, MLA_HEADS, MLA_NOPE + MLA_V)
    half = MLA_ROPE // 2
    qn = q[:, :, :MLA_NOPE].reshape(S, MLA_HEADS * MLA_NOPE)
    q1 = q[:, :, MLA_NOPE:MLA_NOPE + half].reshape(S, MLA_HEADS * half)
    q2 = q[:, :, MLA_NOPE + half:].reshape(S, MLA_HEADS * half)
    cos_h, sin_h = jnp.tile(cos, (1, MLA_HEADS)), jnp.tile(sin, (1, MLA_HEADS))
    scale = (MLA_NOPE + MLA_ROPE) ** -0.5
    qm = scale * LOG2E

    def rope(qn_, q1_, q2_, k1_, k2_, ch, sh, c1, s1):
        return ((qn_ * qm).astype(BF16), ((q1_ * ch - q2_ * sh) * qm).astype(BF16),
                ((q1_ * sh + q2_ * ch) * qm).astype(BF16),
                (k1_ * c1 - k2_ * s1).astype(BF16), (k1_ * s1 + k2_ * c1).astype(BF16))

    qns, qr1, qr2, kr1, kr2 = rowwise(rope, [qn, q1, q2, k_r[:, :half], k_r[:, half:], cos_h, sin_h, cos, sin], [],
                                      "mla_rope", n_const=4)
    q_cat = jnp.concatenate([qns.reshape(S, MLA_HEADS, MLA_NOPE), qr1.reshape(S, MLA_HEADS, half),
                             qr2.reshape(S, MLA_HEADS, half)], axis=-1)
    k_rope = jnp.concatenate([kr1, kr2], axis=-1)
    k_cat = jnp.concatenate([kv[:, :, :MLA_NOPE], jnp.broadcast_to(k_rope[:, None, :], (S, MLA_HEADS, MLA_ROPE))],
                            axis=-1)
    o = full_attention(q_cat.transpose(1, 0, 2), k_cat.transpose(1, 0, 2), kv[:, :, MLA_NOPE:].transpose(1, 0, 2),
                       scale, True)
    return _heads_last(o), q_mem


def dilated_mixer(h, slab):
    S = h.shape[0]
    n_g = len(DIL_GROUPS)
    qkv_w = n_g * 3 * DIL_HEADS * HEAD_DIM
    z = slab.matmul(h, 'b_w_in', 0, BF16)
    zd = z[:, :qkv_w].reshape(S, n_g, 3, DIL_HEADS, HEAD_DIM)
    q_mem = z[:, qkv_w:]
    slopes = (2.0 ** (-ALIBI_MAX * (jnp.arange(n_g * DIL_HEADS, dtype=F32) + 1.0) / (n_g * DIL_HEADS)))
    slopes = slopes.reshape(n_g, DIL_HEADS)
    outs, lses = [], []
    for g, (window, dil) in enumerate(DIL_GROUPS):
        L = S // dil

        def sub(t):
            return t.reshape(L, dil, DIL_HEADS, HEAD_DIM).transpose(1, 2, 0, 3).reshape(dil * DIL_HEADS, L, HEAD_DIM)

        slope = (jnp.tile(slopes[g], dil) * dil).reshape(dil * DIL_HEADS, 1, 1)
        o, lse = band_attention(sub(zd[:, g, 0]), sub(zd[:, g, 1]), sub(zd[:, g, 2]), slope, window // (2 * dil))
        outs.append(o.reshape(dil, DIL_HEADS, L, HEAD_DIM).transpose(2, 0, 1, 3).reshape(S * DIL_HEADS, HEAD_DIM))
        lses.append(lse.reshape(dil, DIL_HEADS, L).transpose(2, 0, 1).reshape(S * DIL_HEADS, 1))

    def merge(o0, o1, o2, l0, l1, l2):
        m = jnp.maximum(jnp.maximum(l0, l1), l2)
        e0, e1, e2 = jnp.exp(l0 - m), jnp.exp(l1 - m), jnp.exp(l2 - m)
        return (((e0 * o0 + e1 * o1 + e2 * o2) / (e0 + e1 + e2)).astype(BF16),)

    (o,) = rowwise(merge, outs + lses, [], "dilated_merge")
    return o.reshape(S, DIL_HEADS * HEAD_DIM), q_mem


def conformer_conv_mixer(h, slab, conv_w, conv_b, ln_g, ln_b):
    z = slab.matmul(h, 'c_w_in', 0, F32)
    a, gate, q_mem = z[:, :CONV_CH], z[:, CONV_CH:2 * CONV_CH], z[:, 2 * CONV_CH:]
    (u,) = rowwise(lambda a_, g_: (a_ * jax.nn.sigmoid(g_),), [a, gate], [], "conformer_glu")
    u = depthwise_conv(u, conv_w)

    def post(u_, b, g, beta):
        t = u_ + b
        mu = jnp.mean(t, axis=-1, keepdims=True)
        var = jnp.mean(jnp.square(t - mu), axis=-1, keepdims=True)
        return (_silu((t - mu) * lax.rsqrt(var + EPS) * g + beta).astype(BF16),)

    (o,) = rowwise(post, [u], [conv_b.reshape(1, -1), ln_g.reshape(1, -1), ln_b.reshape(1, -1)], "conformer_post")
    return o, q_mem


def short_conv_mixer(h, slab, conv_w):
    z = slab.matmul(h, 'd_w_in', 0, F32)
    bg, cg, hx, q_mem = z[:, :SC_CH], z[:, SC_CH:2 * SC_CH], z[:, 2 * SC_CH:3 * SC_CH], z[:, 3 * SC_CH:]
    (p,) = rowwise(lambda c_, h_: (c_ * h_,), [cg, hx], [], "shortconv_pre")
    cv = depthwise_conv(p, conv_w)
    (o,) = rowwise(lambda b_, c_: ((b_ * c_).astype(BF16),), [bg, cv], [], "shortconv_post")
    return o, q_mem


def local_loss(deltas, W, x, slab_values, mem, cos, sin, target):
    slab = SlabWeights(slab_values, deltas)
    norm_g = W['norm_g'][0]

    def gain(i, k):
        return norm_g[i, k].reshape(1, -1)

    (h,) = rowwise(lambda x_, g: (_rms(x_, g).astype(BF16),), [x], [gain(0, 0)], "input_norm")
    for i in range(DEPTH):
        x, h = swiglu_half_step(x, h, slab, 2 * i, gain(i, 1), gain(i, 2))
        if i == 0:
            o, q_mem = mla_mixer(h, cos, sin, W['a_w_in'][0], W['a_q_norm'][0].reshape(1, -1),
                                 W['a_kv_norm'][0].reshape(1, -1), W['a_w_uq'][0], W['a_w_ukv'][0])
            out_name = 'a_w_out'
        elif i == 1:
            o, q_mem = dilated_mixer(h, slab)
            out_name = None
        elif i == 2:
            o, q_mem = conformer_conv_mixer(h, slab, W['c_conv_w'][0], W['c_conv_b'][0], W['c_ln_g'][0],
                                            W['c_ln_b'][0])
            out_name = 'c_w_out'
        else:
            o, q_mem = short_conv_mixer(h, slab, W['d_conv_w'][0])
            out_name = 'd_w_out'
        (memn,) = rowwise(lambda m_, g: (_rms(m_, g).astype(BF16),), [mem], [gain(i, 6)], "memory_norm")
        mo = memory_attention(q_mem, memn, W['mem_w_kv'][i])
        cat = jnp.concatenate([o, mo], axis=-1)
        y = mm(cat, W['b_w_out'][0], F32) if out_name is None else slab.matmul(cat, out_name, 0, F32)
        x, h = residual_norm(x, y, 1.0, gain(i, 3), gain(i, 4))
        x, h = swiglu_half_step(x, h, slab, 2 * i + 1, gain(i, 5), gain(i + 1, 0) if i + 1 < DEPTH else None)
    return loss_head(x, target)


def kernel(x, mem, positions, norm_g, ffn_w_gate, ffn_w_up, ffn_w_down, mem_w_kv, a_w_in, a_q_norm, a_kv_norm, a_w_uq, a_w_ukv, a_w_out, b_w_in, b_w_out, c_w_in, c_conv_w, c_conv_b, c_ln_g, c_ln_b, c_w_out, d_w_in, d_conv_w, d_w_out, loss_target, m_norm_g, m_ffn_w_gate, m_ffn_w_up, m_ffn_w_down, m_mem_w_kv, m_a_w_in, m_a_q_norm, m_a_kv_norm, m_a_w_uq, m_a_w_ukv, m_a_w_out, m_b_w_in, m_b_w_out, m_c_w_in, m_c_conv_w, m_c_conv_b, m_c_ln_g, m_c_ln_b, m_c_w_out, m_d_w_in, m_d_conv_w, m_d_w_out, v_norm_g, v_ffn_w_gate, v_ffn_w_up, v_ffn_w_down, v_mem_w_kv, v_a_w_in, v_a_q_norm, v_a_kv_norm, v_a_w_uq, v_a_w_ukv, v_a_w_out, v_b_w_in, v_b_w_out, v_c_w_in, v_c_conv_w, v_c_conv_b, v_c_ln_g, v_c_ln_b, v_c_w_out, v_d_w_in, v_d_conv_w, v_d_w_out):
    w_in = dict(zip(WEIGHT_NAMES, (norm_g, ffn_w_gate, ffn_w_up, ffn_w_down, mem_w_kv, a_w_in, a_q_norm, a_kv_norm, a_w_uq, a_w_ukv, a_w_out, b_w_in, b_w_out, c_w_in, c_conv_w, c_conv_b, c_ln_g, c_ln_b, c_w_out, d_w_in, d_conv_w, d_w_out)))
    m_in = dict(zip(WEIGHT_NAMES, (m_norm_g, m_ffn_w_gate, m_ffn_w_up, m_ffn_w_down, m_mem_w_kv, m_a_w_in, m_a_q_norm, m_a_kv_norm, m_a_w_uq, m_a_w_ukv, m_a_w_out, m_b_w_in, m_b_w_out, m_c_w_in, m_c_conv_w, m_c_conv_b, m_c_ln_g, m_c_ln_b, m_c_w_out, m_d_w_in, m_d_conv_w, m_d_w_out)))
    v_in = dict(zip(WEIGHT_NAMES, (v_norm_g, v_ffn_w_gate, v_ffn_w_up, v_ffn_w_down, v_mem_w_kv, v_a_w_in, v_a_q_norm, v_a_kv_norm, v_a_w_uq, v_a_w_ukv, v_a_w_out, v_b_w_in, v_b_w_out, v_c_w_in, v_c_conv_w, v_c_conv_b, v_c_ln_g, v_c_ln_b, v_c_w_out, v_d_w_in, v_d_conv_w, v_d_w_out)))

    d_model = x.shape[-1]
    slab_values, W = gather_weights(w_in, d_model)
    deltas = {n: [jnp.zeros(leaf.shape, BF16) for leaf in leaves] for n, leaves in slab_values.items()}
    half = MLA_ROPE // 2
    inv = ROPE_THETA ** (-jnp.arange(half, dtype=F32) / half)
    ang = positions[0].astype(F32)[:, None] * inv
    loss, (g_slab, g_misc, gx) = jax.value_and_grad(local_loss, argnums=(0, 1, 2))(
        deltas, W, x[0], slab_values, mem[0], jnp.cos(ang), jnp.sin(ang), loss_target[0])
    loss = lax.psum(loss, ("x", "y", "c"))

    grads = reduce_gradients(g_slab, g_misc, w_in, d_model)
    steps = {n: adamw(w_in[n], grads[n], m_in[n], v_in[n]) for n in WEIGHT_NAMES}
    return (loss, gx[None], *[grads[n] for n in WEIGHT_NAMES], *[steps[n][0] for n in WEIGHT_NAMES],
            *[steps[n][1] for n in WEIGHT_NAMES], *[steps[n][2] for n in WEIGHT_NAMES])
```

```python
import functools
import math

import jax
import jax.numpy as jnp
from jax import lax
from jax.experimental import pallas as pl
from jax.experimental.pallas import tpu as pltpu

F32 = jnp.float32
BF16 = jnp.bfloat16
MESH = pl.DeviceIdType.MESH

VMEM_LIMIT_BYTES = 56 * 1024 * 1024
LANES = 128
PACK_ROW_MULT = 512
ROW_TILE_ELEMS = 768 * 1024

HEAD_DIM = 64
MEM_HEADS = 4
MEM_WIDTH = MEM_HEADS * HEAD_DIM
MLA_HEADS = 12
MLA_Q_LORA = 384
MLA_KV_LORA = 256
MLA_NOPE = 64
MLA_ROPE = 32
MLA_V = 64
ROPE_THETA = 10000.0
DIL_GROUPS = ((128, 1), (512, 4), (2048, 16))
DIL_HEADS = 8
ALIBI_MAX = 8.0
CONV_CH = 768
CONV_WIDTH = 31
SC_CH = 768
SC_WIDTH = 3
CONV_PAD = 16
CONV_CHUNK = 256
EPS = 1e-6
NEG = -1e30
DEPTH = 4
LOG2E = 1.4426950408889634
LN2 = 0.6931471805599453

ADAM_LR = 0.001
ADAM_B1 = 0.9
ADAM_B2 = 0.999
ADAM_EPS = 1e-08
ADAM_WD = 0.01
ADAM_STEP = 10

WEIGHT_NAMES = ['norm_g', 'ffn_w_gate', 'ffn_w_up', 'ffn_w_down', 'mem_w_kv', 'a_w_in', 'a_q_norm', 'a_kv_norm',
                'a_w_uq', 'a_w_ukv', 'a_w_out', 'b_w_in', 'b_w_out', 'c_w_in', 'c_conv_w', 'c_conv_b', 'c_ln_g',
                'c_ln_b', 'c_w_out', 'd_w_in', 'd_conv_w', 'd_w_out']
WEIGHT_LAYOUT = {
    'norm_g': (0, 2), 'ffn_w_gate': (2, 1), 'ffn_w_up': (2, 1), 'ffn_w_down': (2, 0), 'mem_w_kv': (1, 0),
    'a_w_in': (1, 0), 'a_q_norm': (1, None), 'a_kv_norm': (1, None), 'a_w_uq': (1, 1), 'a_w_ukv': (1, 1),
    'a_w_out': (1, 0), 'b_w_in': (1, 1), 'b_w_out': (1, 1), 'c_w_in': (1, 1), 'c_conv_w': (1, 1),
    'c_conv_b': (1, 0), 'c_ln_g': (1, 0), 'c_ln_b': (1, 0), 'c_w_out': (1, 0), 'd_w_in': (1, 1),
    'd_conv_w': (1, 1), 'd_w_out': (1, 0),
}
SLAB_TRANSPOSED = ('ffn_w_gate', 'ffn_w_up', 'b_w_in', 'c_w_in', 'd_w_in')
SLAB_ROWS = ('ffn_w_down', 'a_w_out', 'c_w_out', 'd_w_out')
SLAB_NAMES = [n for n in WEIGHT_NAMES if n in SLAB_TRANSPOSED or n in SLAB_ROWS]
MISC_NAMES = [n for n in WEIGHT_NAMES if n not in SLAB_NAMES]
F32_GATHER = ('norm_g', 'c_conv_w', 'c_conv_b', 'c_ln_g', 'c_ln_b', 'd_conv_w')
N_CHIPS = 4


def _cparams(semantics):
    return pltpu.CompilerParams(dimension_semantics=semantics, vmem_limit_bytes=VMEM_LIMIT_BYTES)


def _div_tile(n, cap, mult):
    if n <= cap:
        return n
    for d in range(cap - cap % mult, 0, -mult):
        if n % d == 0:
            return d
    raise ValueError(f"no tile for {n} (cap {cap}, multiple of {mult})")


def _matmul(a, b, mode, out_dtype):
    if mode == 'nn':
        (M, K), N = a.shape, b.shape[1]
    elif mode == 'nt':
        (M, K), N = a.shape, b.shape[0]
    else:
        (K, M), N = a.shape, b.shape[1]
    if mode == 'tn':
        tm, tn, tk = _div_tile(M, 2816, LANES), _div_tile(N, 1536, LANES), _div_tile(K, 1024, 16)
    else:
        tm, tn, tk = _div_tile(M, 512, 16), _div_tile(N, 2816, LANES), _div_tile(K, 2816, LANES)
    nk = K // tk
    if mode == 'nn':
        a_spec = pl.BlockSpec((tm, tk), lambda i, j, k: (i, k))
        b_spec = pl.BlockSpec((tk, tn), lambda i, j, k: (k, j))
        dims = (((1,), (0,)), ((), ()))
    elif mode == 'nt':
        a_spec = pl.BlockSpec((tm, tk), lambda i, j, k: (i, k))
        b_spec = pl.BlockSpec((tn, tk), lambda i, j, k: (j, k))
        dims = (((1,), (1,)), ((), ()))
    else:
        a_spec = pl.BlockSpec((tk, tm), lambda i, j, k: (k, i))
        b_spec = pl.BlockSpec((tk, tn), lambda i, j, k: (k, j))
        dims = (((0,), (0,)), ((), ()))

    def body(a_ref, b_ref, o_ref, *acc):
        r = lax.dot_general(a_ref[...].astype(BF16), b_ref[...].astype(BF16), dims, preferred_element_type=F32)
        if nk == 1:
            o_ref[...] = r.astype(o_ref.dtype)
        else:
            k = pl.program_id(2)

            @pl.when(k == 0)
            def _():
                acc[0][...] = r

            @pl.when(k > 0)
            def _():
                acc[0][...] += r

            @pl.when(k == nk - 1)
            def _():
                o_ref[...] = acc[0][...].astype(o_ref.dtype)

    return pl.pallas_call(
        body,
        out_shape=jax.ShapeDtypeStruct((M, N), out_dtype),
        grid=(M // tm, N // tn, nk),
        in_specs=[a_spec, b_spec],
        out_specs=pl.BlockSpec((tm, tn), lambda i, j, k: (i, j)),
        scratch_shapes=[pltpu.VMEM((tm, tn), F32)] if nk > 1 else [],
        compiler_params=_cparams(("parallel", "parallel", "arbitrary")),
        name=f"mm_{mode}_{M}x{K}x{N}",
    )(a, b)


@functools.partial(jax.custom_vjp, nondiff_argnums=(2,))
def mm(a, w, out_dtype):
    return _matmul(a, w.astype(BF16), 'nn', out_dtype)


def _mm_fwd(a, w, out_dtype):
    wb = w.astype(BF16)
    return _matmul(a, wb, 'nn', out_dtype), (a, wb)


def _mm_bwd(out_dtype, res, g):
    a, wb = res
    return _matmul(g, wb, 'nt', a.dtype), _matmul(a, g, 'tn', F32)


mm.defvjp(_mm_fwd, _mm_bwd)


@functools.partial(jax.custom_vjp, nondiff_argnums=(3, 4))
def mm_slab(a, wb, delta, transposed, out_dtype):
    return _matmul(a, wb, 'nt' if transposed else 'nn', out_dtype)


def _mm_slab_fwd(a, wb, delta, transposed, out_dtype):
    return _matmul(a, wb, 'nt' if transposed else 'nn', out_dtype), (a, wb)


def _mm_slab_bwd(transposed, out_dtype, res, g):
    a, wb = res
    if transposed:
        return _matmul(g, wb, 'nn', a.dtype), jnp.zeros_like(wb), _matmul(g, a, 'tn', BF16)
    return _matmul(g, wb, 'nt', a.dtype), jnp.zeros_like(wb), _matmul(a, g, 'tn', BF16)


mm_slab.defvjp(_mm_slab_fwd, _mm_slab_bwd)


def _row_tile(n_rows, widest):
    t = 1 << max(3, int(math.log2(max(8, ROW_TILE_ELEMS // max(widest, 1)))))
    return min(n_rows, min(t, 2048))


def _rw_forward(fn, rows, params, name):
    n_rows = rows[0].shape[0]
    tile = _row_tile(n_rows, max(r.shape[1] for r in rows))
    outs = jax.eval_shape(fn, *[jax.ShapeDtypeStruct((tile, r.shape[1]), r.dtype) for r in rows],
                          *[jax.ShapeDtypeStruct(p.shape, p.dtype) for p in params])
    n_in = len(rows) + len(params)

    def body(*refs):
        res = fn(*[r[...] for r in refs[:n_in]])
        for o_ref, o in zip(refs[n_in:], res):
            o_ref[...] = o

    return pl.pallas_call(
        body,
        out_shape=[jax.ShapeDtypeStruct((n_rows, o.shape[1]), o.dtype) for o in outs],
        grid=(n_rows // tile,),
        in_specs=[pl.BlockSpec((tile, r.shape[1]), lambda i: (i, 0)) for r in rows]
        + [pl.BlockSpec(p.shape, lambda i: (0, 0)) for p in params],
        out_specs=[pl.BlockSpec((tile, o.shape[1]), lambda i: (i, 0)) for o in outs],
        compiler_params=_cparams(("parallel",)),
        name=name,
    )(*rows, *params)


def _rw_backward(fn, rows, params, cts, n_const, name):
    n_rows = rows[0].shape[0]
    tile = _row_tile(n_rows, max([r.shape[1] for r in rows] + [c.shape[1] for c in cts]))
    n_r, n_p, n_c = len(rows), len(params), len(cts)
    n_diff = n_r - n_const

    def body(*refs):
        row_vals = [r[...] for r in refs[:n_r]]
        par_vals = [r[...] for r in refs[n_r:n_r + n_p]]
        ct_vals = tuple(r[...] for r in refs[n_r + n_p:n_r + n_p + n_c])
        out_refs = refs[n_r + n_p + n_c:]

        def f(*diff):
            return fn(*diff[:n_diff], *row_vals[n_diff:], *diff[n_diff:])

        _, vjp = jax.vjp(f, *row_vals[:n_diff], *par_vals)
        grads = vjp(ct_vals)
        for o_ref, g in zip(out_refs[:n_diff], grads[:n_diff]):
            o_ref[...] = g
        i = pl.program_id(0)
        for o_ref, g in zip(out_refs[n_diff:], grads[n_diff:]):
            @pl.when(i == 0)
            def _(o_ref=o_ref, g=g):
                o_ref[...] = g

            @pl.when(i > 0)
            def _(o_ref=o_ref, g=g):
                o_ref[...] += g

    res = pl.pallas_call(
        body,
        out_shape=[jax.ShapeDtypeStruct(r.shape, r.dtype) for r in rows[:n_diff]]
        + [jax.ShapeDtypeStruct(p.shape, p.dtype) for p in params],
        grid=(n_rows // tile,),
        in_specs=[pl.BlockSpec((tile, r.shape[1]), lambda i: (i, 0)) for r in rows]
        + [pl.BlockSpec(p.shape, lambda i: (0, 0)) for p in params]
        + [pl.BlockSpec((tile, c.shape[1]), lambda i: (i, 0)) for c in cts],
        out_specs=[pl.BlockSpec((tile, r.shape[1]), lambda i: (i, 0)) for r in rows[:n_diff]]
        + [pl.BlockSpec(p.shape, lambda i: (0, 0)) for p in params],
        compiler_params=_cparams(("arbitrary",)),
        name=name + "_bwd",
    )(*rows, *params, *cts)
    return list(res[:n_diff]), list(res[n_diff:])


def rowwise(fn, rows, params, name, n_const=0):
    rows, params = list(rows), list(params)

    @jax.custom_vjp
    def op(rows, params):
        return tuple(_rw_forward(fn, rows, params, name))

    def op_fwd(rows, params):
        return tuple(_rw_forward(fn, rows, params, name)), (rows, params)

    def op_bwd(res, cts):
        rows, params = res
        d_rows, d_params = _rw_backward(fn, rows, params, list(cts), n_const, name)
        d_rows = d_rows + [jnp.zeros_like(r) for r in rows[len(rows) - n_const:]]
        return d_rows, d_params

    op.defvjp(op_fwd, op_bwd)
    return op(rows, params)


def _rms(x, g):
    xf = x.astype(F32)
    return xf * lax.rsqrt(jnp.mean(xf * xf, axis=-1, keepdims=True) + EPS) * g


ATTN_TILE = 1024
ATTN_SPAN = 2048
ATTN_CHUNK = 512

NT_DIMS = (((1,), (1,)), ((), ()))


def _attn_tiles(stay, stream):
    span = min(ATTN_SPAN, stream)
    return min(ATTN_TILE, stay), span, min(ATTN_CHUNK, span)


def _flash_fwd(q, k, v, qk_mult):
    N, Lq, dk = q.shape
    Lk, dv = k.shape[1], v.shape[2]
    tq, span, ch = _attn_tiles(Lq, Lk)
    tq = min(Lq, tq * max(1, ATTN_TILE // span))
    nk = Lk // span
    v_ones = jnp.concatenate([v, jnp.ones((N, Lk, 1), v.dtype), jnp.zeros((N, Lk, LANES - dv - 1), v.dtype)], axis=2)

    def body(q_ref, k_ref, v_ref, o_ref, lse_ref, m_s, acc_s):
        j = pl.program_id(2)

        @pl.when(j == 0)
        def _():
            m_s[...] = jnp.full_like(m_s, NEG)
            acc_s[...] = jnp.zeros_like(acc_s)

        qv = q_ref[0]
        s = []
        for c in range(span // ch):
            sc = lax.dot_general(qv, k_ref[0, c * ch:(c + 1) * ch, :], NT_DIMS, preferred_element_type=F32)
            s.append(sc if qk_mult == 1.0 else sc * qk_mult)
        s_max = functools.reduce(jnp.maximum, s)
        m_new = jnp.maximum(m_s[...], jnp.max(s_max, axis=-1, keepdims=True))
        pv = None
        for c, sc in enumerate(s):
            p = jnp.exp2(sc - m_new).astype(BF16)
            d = jnp.dot(p, v_ref[0, c * ch:(c + 1) * ch, :], preferred_element_type=F32)
            pv = d if c == 0 else pv + d
        acc_s[...] = jnp.exp2(m_s[...] - m_new) * acc_s[...] + pv
        m_s[...] = m_new

        @pl.when(j == nk - 1)
        def _():
            acc = acc_s[...]
            l = acc[:, dv:dv + 1]
            o_ref[0] = (acc[:, :dv] / l).astype(o_ref.dtype)
            lse_ref[0] = m_s[...] + jnp.log(l) * LOG2E

    return pl.pallas_call(
        body,
        out_shape=[jax.ShapeDtypeStruct((N, Lq, dv), BF16), jax.ShapeDtypeStruct((N, Lq, 1), F32)],
        grid=(N, Lq // tq, nk),
        in_specs=[pl.BlockSpec((1, tq, dk), lambda n, i, j: (n, i, 0)),
                  pl.BlockSpec((1, span, dk), lambda n, i, j: (n, j, 0)),
                  pl.BlockSpec((1, span, LANES), lambda n, i, j: (n, j, 0))],
        out_specs=[pl.BlockSpec((1, tq, dv), lambda n, i, j: (n, i, 0)),
                   pl.BlockSpec((1, tq, 1), lambda n, i, j: (n, i, 0))],
        scratch_shapes=[pltpu.VMEM((tq, 1), F32), pltpu.VMEM((tq, LANES), F32)],
        compiler_params=_cparams(("parallel", "parallel", "arbitrary")),
        name=f"flash_fwd_{N}x{Lq}x{Lk}x{dk}",
    )(q, k, v_ones)


def _flash_bwd(q, k, v, do, lse_row, dd_row, qk_mult, out_mult):
    N, Lq, dk = q.shape
    Lk, dv = k.shape[1], v.shape[2]
    tk, span, ch = _attn_tiles(Lk, Lq)
    nq, nkt = Lq // span, Lk // tk

    def body(q_ref, k_ref, v_ref, do_ref, lse_ref, dd_ref, dq_ref, dk_ref, dv_ref, dq_s, dk_s, dv_s):
        i, j = pl.program_id(1), pl.program_id(2)

        @pl.when((i == 0) & (j == 0))
        def _():
            dq_s[...] = jnp.zeros_like(dq_s)

        @pl.when(j == 0)
        def _():
            dk_s[...] = jnp.zeros_like(dk_s)
            dv_s[...] = jnp.zeros_like(dv_s)

        kv_, vv = k_ref[0], v_ref[0]
        for c in range(span // ch):
            qc = q_ref[0, c * ch:(c + 1) * ch, :]
            doc = do_ref[0, c * ch:(c + 1) * ch, :]
            st = lax.dot_general(kv_, qc, NT_DIMS, preferred_element_type=F32)
            if qk_mult != 1.0:
                st = st * qk_mult
            pt = jnp.exp2(st - lse_ref[0, :, c * ch:(c + 1) * ch])
            dv_s[...] += jnp.dot(pt.astype(BF16), doc, preferred_element_type=F32)
            dpt = lax.dot_general(vv, doc, NT_DIMS, preferred_element_type=F32)
            dst = (pt * (dpt - dd_ref[0, :, c * ch:(c + 1) * ch])).astype(BF16)
            dk_s[...] += jnp.dot(dst, qc, preferred_element_type=F32)
            rows = pl.ds(pl.multiple_of(j * span + c * ch, ch), ch)
            dq_s[rows, :] += lax.dot_general(dst, kv_, (((0,), (0,)), ((), ())), preferred_element_type=F32)

        @pl.when(j == nq - 1)
        def _():
            dk_ref[0] = (dk_s[...] * out_mult).astype(dk_ref.dtype)
            dv_ref[0] = dv_s[...].astype(dv_ref.dtype)

        @pl.when((i == nkt - 1) & (j == nq - 1))
        def _():
            dq_ref[0] = (dq_s[...] * out_mult).astype(dq_ref.dtype)

    return pl.pallas_call(
        body,
        out_shape=[jax.ShapeDtypeStruct(q.shape, q.dtype), jax.ShapeDtypeStruct(k.shape, k.dtype),
                   jax.ShapeDtypeStruct(v.shape, v.dtype)],
        grid=(N, nkt, nq),
        in_specs=[pl.BlockSpec((1, span, dk), lambda n, i, j: (n, j, 0)),
                  pl.BlockSpec((1, tk, dk), lambda n, i, j: (n, i, 0)),
                  pl.BlockSpec((1, tk, dv), lambda n, i, j: (n, i, 0)),
                  pl.BlockSpec((1, span, dv), lambda n, i, j: (n, j, 0)),
                  pl.BlockSpec((1, 1, span), lambda n, i, j: (n, 0, j)),
                  pl.BlockSpec((1, 1, span), lambda n, i, j: (n, 0, j))],
        out_specs=[pl.BlockSpec((1, Lq, dk), lambda n, i, j: (n, 0, 0)),
                   pl.BlockSpec((1, tk, dk), lambda n, i, j: (n, i, 0)),
                   pl.BlockSpec((1, tk, dv), lambda n, i, j: (n, i, 0))],
        scratch_shapes=[pltpu.VMEM((Lq, dk), F32), pltpu.VMEM((tk, dk), F32), pltpu.VMEM((tk, dv), F32)],
        compiler_params=_cparams(("parallel", "arbitrary", "arbitrary")),
        name=f"flash_bwd_{N}x{Lq}x{Lk}x{dk}",
    )(q, k, v, do, lse_row, dd_row)


def _attn_delta(do, o, dlse):
    N, L, dv = o.shape

    def fn(do_t, o_t, dl_t):
        return (jnp.sum(do_t.astype(F32) * o_t.astype(F32), axis=-1, keepdims=True) - dl_t,)

    (dd,) = _rw_forward(fn, [do.reshape(N * L, dv), o.reshape(N * L, dv), dlse.reshape(N * L, 1)], [], "attn_delta")
    return dd.reshape(N, L, 1)


def full_attention(q, k, v, scale, q_prescaled):
    qk_mult = 1.0 if q_prescaled else scale * LOG2E
    grad_mult = LN2 if q_prescaled else scale

    @jax.custom_vjp
    def op(q, k, v):
        return _flash_fwd(q, k, v, qk_mult)[0]

    def op_fwd(q, k, v):
        o, lse = _flash_fwd(q, k, v, qk_mult)
        return o, (q, k, v, o, lse)

    def op_bwd(res, do):
        q, k, v, o, lse = res
        N, Lq, _ = q.shape
        dd = _attn_delta(do, o, jnp.zeros_like(lse))
        return tuple(_flash_bwd(q, k, v, do, lse.reshape(N, 1, Lq), dd.reshape(N, 1, Lq), qk_mult, grad_mult))

    op.defvjp(op_fwd, op_bwd)
    return op(q, k, v)


BAND_TILE = 128
BAND_BATCH = 8


def _band_specs(L, d):
    t = min(BAND_TILE, L)
    nt = L // t
    cur = lambda n, i: (n, i, 0)
    prev = lambda n, i: (n, jnp.maximum(i - 1, 0), 0)
    nxt = lambda n, i: (n, jnp.minimum(i + 1, nt - 1), 0)
    return t, nt, cur, prev, nxt


def _band_mask(i, t, L, half):
    a = lax.broadcasted_iota(jnp.int32, (t, 3 * t), 0)
    b = lax.broadcasted_iota(jnp.int32, (t, 3 * t), 1)
    dist = jnp.abs(b - t - a)
    other = i * t - t + b
    mask = (dist <= half) & (other >= 0) & (other < L)
    return dist.astype(F32), mask


def _band_fwd(q, k, v, slope, half):
    N, L, d = q.shape
    t, nt, cur, prev, nxt = _band_specs(L, d)
    nb = BAND_BATCH
    scale = d ** -0.5

    def body(sl_ref, q_ref, kp, kc, kn, vp, vc, vn, o_ref, lse_ref):
        i = pl.program_id(1)
        kcat = jnp.concatenate([kp[...], kc[...], kn[...]], axis=1)
        vcat = jnp.concatenate([vp[...], vc[...], vn[...]], axis=1)
        s = lax.dot_general(q_ref[...], kcat, (((2,), (2,)), ((0,), (0,))), preferred_element_type=F32) * scale
        dist, mask = _band_mask(i, t, L, half)
        s = jnp.where(mask[None], s - sl_ref[...] * dist[None], NEG)
        m = jnp.max(s, axis=-1, keepdims=True)
        p = jnp.exp(s - m)
        l = jnp.sum(p, axis=-1, keepdims=True)
        o = lax.dot_general(p.astype(BF16), vcat, (((2,), (1,)), ((0,), (0,))), preferred_element_type=F32)
        o_ref[...] = o / l
        lse_ref[...] = m + jnp.log(l)

    blk = lambda f: pl.BlockSpec((nb, t, d), f)
    return pl.pallas_call(
        body,
        out_shape=[jax.ShapeDtypeStruct((N, L, d), F32), jax.ShapeDtypeStruct((N, L, 1), F32)],
        grid=(N // nb, nt),
        in_specs=[pl.BlockSpec((nb, 1, 1), lambda n, i: (n, 0, 0)), blk(cur),
                  blk(prev), blk(cur), blk(nxt), blk(prev), blk(cur), blk(nxt)],
        out_specs=[blk(cur), pl.BlockSpec((nb, t, 1), cur)],
        compiler_params=_cparams(("parallel", "parallel")),
        name=f"band_fwd_{N}x{L}",
    )(slope, q, k, k, k, v, v, v)


def _band_dq(q, k, v, slope, do, lse, dd, half):
    N, L, d = q.shape
    t, nt, cur, prev, nxt = _band_specs(L, d)
    nb = BAND_BATCH
    scale = d ** -0.5

    def body(sl_ref, q_ref, kp, kc, kn, vp, vc, vn, do_ref, lse_ref, dd_ref, dq_ref):
        i = pl.program_id(1)
        kcat = jnp.concatenate([kp[...], kc[...], kn[...]], axis=1)
        vcat = jnp.concatenate([vp[...], vc[...], vn[...]], axis=1)
        s = lax.dot_general(q_ref[...], kcat, (((2,), (2,)), ((0,), (0,))), preferred_element_type=F32) * scale
        dist, mask = _band_mask(i, t, L, half)
        p = jnp.where(mask[None], jnp.exp(s - sl_ref[...] * dist[None] - lse_ref[...]), 0.0)
        dp = lax.dot_general(do_ref[...].astype(BF16), vcat, (((2,), (2,)), ((0,), (0,))),
                             preferred_element_type=F32)
        ds = (p * (dp - dd_ref[...])).astype(BF16)
        dq = lax.dot_general(ds, kcat, (((2,), (1,)), ((0,), (0,))), preferred_element_type=F32)
        dq_ref[...] = (dq * scale).astype(dq_ref.dtype)

    blk = lambda f: pl.BlockSpec((nb, t, d), f)
    col = pl.BlockSpec((nb, t, 1), cur)
    return pl.pallas_call(
        body,
        out_shape=jax.ShapeDtypeStruct(q.shape, q.dtype),
        grid=(N // nb, nt),
        in_specs=[pl.BlockSpec((nb, 1, 1), lambda n, i: (n, 0, 0)), blk(cur),
                  blk(prev), blk(cur), blk(nxt), blk(prev), blk(cur), blk(nxt), blk(cur), col, col],
        out_specs=blk(cur),
        compiler_params=_cparams(("parallel", "parallel")),
        name=f"band_dq_{N}x{L}",
    )(slope, q, k, k, k, v, v, v, do, lse, dd)


def _band_dkv(q, k, v, slope, do, lse_row, dd_row, half):
    N, L, d = q.shape
    t, nt, cur, prev, nxt = _band_specs(L, d)
    nb = BAND_BATCH
    scale = d ** -0.5
    rcur = lambda n, i: (n, 0, i)
    rprev = lambda n, i: (n, 0, jnp.maximum(i - 1, 0))
    rnxt = lambda n, i: (n, 0, jnp.minimum(i + 1, nt - 1))

    def body(sl_ref, k_ref, v_ref, qp, qc, qn, dop, doc, don, lp, lc, ln, dp_, dc_, dn_, dk_ref, dv_ref):
        i = pl.program_id(1)
        qcat = jnp.concatenate([qp[...], qc[...], qn[...]], axis=1)
        docat = jnp.concatenate([dop[...], doc[...], don[...]], axis=1).astype(BF16)
        lrow = jnp.concatenate([lp[...], lc[...], ln[...]], axis=2)
        drow = jnp.concatenate([dp_[...], dc_[...], dn_[...]], axis=2)
        st = lax.dot_general(k_ref[...], qcat, (((2,), (2,)), ((0,), (0,))), preferred_element_type=F32) * scale
        dist, mask = _band_mask(i, t, L, half)
        pt = jnp.where(mask[None], jnp.exp(st - sl_ref[...] * dist[None] - lrow), 0.0)
        dv = lax.dot_general(pt.astype(BF16), docat, (((2,), (1,)), ((0,), (0,))), preferred_element_type=F32)
        dpt = lax.dot_general(v_ref[...], docat, (((2,), (2,)), ((0,), (0,))), preferred_element_type=F32)
        dst = (pt * (dpt - drow)).astype(BF16)
        dk = lax.dot_general(dst, qcat, (((2,), (1,)), ((0,), (0,))), preferred_element_type=F32)
        dk_ref[...] = (dk * scale).astype(dk_ref.dtype)
        dv_ref[...] = dv.astype(dv_ref.dtype)

    blk = lambda f: pl.BlockSpec((nb, t, d), f)
    row = lambda f: pl.BlockSpec((nb, 1, t), f)
    return pl.pallas_call(
        body,
        out_shape=[jax.ShapeDtypeStruct(k.shape, k.dtype), jax.ShapeDtypeStruct(v.shape, v.dtype)],
        grid=(N // nb, nt),
        in_specs=[pl.BlockSpec((nb, 1, 1), lambda n, i: (n, 0, 0)), blk(cur), blk(cur),
                  blk(prev), blk(cur), blk(nxt), blk(prev), blk(cur), blk(nxt),
                  row(rprev), row(rcur), row(rnxt), row(rprev), row(rcur), row(rnxt)],
        out_specs=[blk(cur), blk(cur)],
        compiler_params=_cparams(("parallel", "parallel")),
        name=f"band_dkv_{N}x{L}",
    )(slope, k, v, q, q, q, do, do, do, lse_row, lse_row, lse_row, dd_row, dd_row, dd_row)


def band_attention(q, k, v, slope, half):
    @jax.custom_vjp
    def op(q, k, v, slope):
        return tuple(_band_fwd(q, k, v, slope, half))

    def op_fwd(q, k, v, slope):
        o, lse = _band_fwd(q, k, v, slope, half)
        return (o, lse), (q, k, v, slope, o, lse)

    def op_bwd(res, cts):
        q, k, v, slope, o, lse = res
        do, dlse = cts
        N, L, _ = q.shape
        dd = _attn_delta(do, o, dlse)
        dq = _band_dq(q, k, v, slope, do, lse, dd, half)
        dk, dv = _band_dkv(q, k, v, slope, do, lse.reshape(N, 1, L), dd.reshape(N, 1, L), half)
        return dq, dk, dv, jnp.zeros_like(slope)

    op.defvjp(op_fwd, op_bwd)
    return op(q, k, v, slope)


def _conv_taps(width):
    return [CONV_PAD - width // 2 + j for j in range(width)]


def _dwconv_call(up, w, width):
    S, C = up.shape[0] - 2 * CONV_PAD, up.shape[1]
    ch = min(CONV_CHUNK, S)
    offs = _conv_taps(width)

    def body(u_ref, w_ref, o_ref):
        def chunk(ci, carry):
            base = pl.multiple_of(ci * ch, ch)
            win = u_ref[pl.ds(base, ch + 2 * CONV_PAD), :]
            acc = jnp.zeros((ch, LANES), F32)
            for j, off in enumerate(offs):
                acc = acc + win[off:off + ch, :] * w_ref[j:j + 1, :]
            o_ref[pl.ds(base, ch), :] = acc
            return carry

        lax.fori_loop(0, S // ch, chunk, 0)

    return pl.pallas_call(
        body,
        out_shape=jax.ShapeDtypeStruct((S, C), F32),
        grid=(C // LANES,),
        in_specs=[pl.BlockSpec((S + 2 * CONV_PAD, LANES), lambda c: (0, c)),
                  pl.BlockSpec((w.shape[0], LANES), lambda c: (0, c))],
        out_specs=pl.BlockSpec((S, LANES), lambda c: (0, c)),
        compiler_params=_cparams(("parallel",)),
        name=f"dwconv{width}_{S}x{C}",
    )(up, w)


def _dwconv_dw_call(up, g, width, w_rows):
    S, C = g.shape
    ch = min(CONV_CHUNK, S)
    offs = _conv_taps(width)

    def body(u_ref, g_ref, o_ref):
        def chunk(ci, accs):
            base = pl.multiple_of(ci * ch, ch)
            win = u_ref[pl.ds(base, ch + 2 * CONV_PAD), :]
            gt = g_ref[pl.ds(base, ch), :]
            return tuple(a + (gt * win[off:off + ch, :]).reshape(ch // 8, 8, LANES).sum(axis=0)
                         for a, off in zip(accs, offs))

        accs = lax.fori_loop(0, S // ch, chunk, tuple(jnp.zeros((8, LANES), F32) for _ in offs))
        o_ref[...] = jnp.zeros_like(o_ref)
        for j, a in enumerate(accs):
            o_ref[j:j + 1, :] = jnp.sum(a, axis=0, keepdims=True)

    return pl.pallas_call(
        body,
        out_shape=jax.ShapeDtypeStruct((w_rows, C), F32),
        grid=(C // LANES,),
        in_specs=[pl.BlockSpec((S + 2 * CONV_PAD, LANES), lambda c: (0, c)),
                  pl.BlockSpec((S, LANES), lambda c: (0, c))],
        out_specs=pl.BlockSpec((w_rows, LANES), lambda c: (0, c)),
        compiler_params=_cparams(("parallel",)),
        name=f"dwconv{width}_dw_{S}x{C}",
    )(up, g)


def _pad_rows(u):
    return jnp.pad(u, ((CONV_PAD, CONV_PAD), (0, 0)))


def _pad_taps(w):
    return jnp.pad(w, ((0, -w.shape[0] % 8), (0, 0)))


def depthwise_conv(u, w):
    width = w.shape[0]

    @jax.custom_vjp
    def op(u, w):
        return _dwconv_call(_pad_rows(u), _pad_taps(w), width)

    def op_fwd(u, w):
        up = _pad_rows(u)
        return _dwconv_call(up, _pad_taps(w), width), (up, w)

    def op_bwd(res, g):
        up, w = res
        du = _dwconv_call(_pad_rows(g), _pad_taps(w[::-1]), width)
        dw = _dwconv_dw_call(up, g, width, _pad_taps(w).shape[0])[:width]
        return du, dw

    op.defvjp(op_fwd, op_bwd)
    return op(u, w)


def _loss_call(y, target):
    S, D = y.shape
    tile = _row_tile(S, D)

    def body(y_ref, t_ref, o_ref):
        d = y_ref[...] - t_ref[...]
        part = jnp.sum(jnp.sum(d * d, axis=1, keepdims=True), axis=0, keepdims=True) * (0.5 / D)

        @pl.when(pl.program_id(0) == 0)
        def _():
            o_ref[...] = jnp.zeros_like(o_ref)

        o_ref[...] += jnp.broadcast_to(part, o_ref.shape)

    out = pl.pallas_call(
        body,
        out_shape=jax.ShapeDtypeStruct((8, LANES), F32),
        grid=(S // tile,),
        in_specs=[pl.BlockSpec((tile, D), lambda i: (i, 0)), pl.BlockSpec((tile, D), lambda i: (i, 0))],
        out_specs=pl.BlockSpec((8, LANES), lambda i: (0, 0)),
        compiler_params=_cparams(("arbitrary",)),
        name="loss_head",
    )(y, target)
    return out[0, 0]


@jax.custom_vjp
def loss_head(y, target):
    return _loss_call(y, target)


def _loss_fwd(y, target):
    return _loss_call(y, target), (y, target)


def _loss_bwd(res, g):
    y, target = res
    inv_d = 1.0 / y.shape[1]

    def fn(y_t, t_t, g_p):
        return ((y_t - t_t) * (g_p * inv_d),)

    (dy,) = _rw_forward(fn, [y, target], [g.reshape(1, 1).astype(F32)], "loss_head_bwd")
    return dy, jnp.zeros_like(target)


loss_head.defvjp(_loss_fwd, _loss_bwd)


ANY = pl.BlockSpec(memory_space=pl.ANY)


AG_ICI_PIECES = 4
AG_D2D_PIECES = 2
D2D_PIECES = 16
ICI_PIECES = 4


def _with_own_block(received, own, slot):
    return lax.dynamic_update_slice_in_dim(received, own[None], slot, axis=0)


def all_gather_chips(mine, name):
    R, C = mine.shape
    ni, nf = AG_ICI_PIECES, AG_D2D_PIECES
    rows = R // (2 * ni * nf)
    n_ici, n_all = 3 * ni, 3 * ni + 3 * ni * nf

    def body(in_ref, out_ref, send_sems, recv_sems):
        x, y, c = lax.axis_index("x"), lax.axis_index("y"), lax.axis_index("c")
        chips = [(1 - x, y), (x, 1 - y), (1 - x, 1 - y)]

        def copy(sem, src, dst, to):
            return pltpu.make_async_remote_copy(src_ref=src, dst_ref=dst, send_sem=send_sems.at[sem],
                                                recv_sem=recv_sems.at[sem], device_id=to, device_id_type=MESH)

        sends = []
        for k, (px, py) in enumerate(chips):
            for a in range(ni):
                sends.append(copy(k * ni + a, in_ref.at[c, a], out_ref.at[2 * x + y, c, a], (px, py, c)))
                sends[-1].start()
        for k, (px, py) in enumerate(chips):
            for a in range(ni):
                landed = out_ref.at[2 * px + py, c, a]
                copy(k * ni + a, in_ref.at[c, a], landed, (px, py, c)).wait_recv()
                for b in range(nf):
                    sends.append(copy(n_ici + (k * ni + a) * nf + b, landed.at[b], landed.at[b], (x, y, 1 - c)))
                    sends[-1].start()
        for k, (px, py) in enumerate(chips):
            for a in range(ni):
                for b in range(nf):
                    theirs = out_ref.at[2 * px + py, 1 - c, a, b]
                    copy(n_ici + (k * ni + a) * nf + b, theirs, theirs, (x, y, 1 - c)).wait_recv()
        for cp in sends:
            cp.wait_send()

    out = pl.pallas_call(
        body,
        out_shape=jax.ShapeDtypeStruct((N_CHIPS, 2, ni, nf, rows, C), mine.dtype),
        in_specs=[ANY],
        out_specs=ANY,
        scratch_shapes=[pltpu.SemaphoreType.DMA((n_all,)), pltpu.SemaphoreType.DMA((n_all,))],
        name=name,
    )(mine.reshape(2, ni, nf, rows, C))
    return _with_own_block(out.reshape(N_CHIPS, R, C), mine, 2 * lax.axis_index("x") + lax.axis_index("y"))


def swap_halves_with_sibling(parts):
    n_s, _, H, C = parts.shape
    n = ICI_PIECES

    def body(in_ref, out_ref, send_sems, recv_sems):
        x, y, c = lax.axis_index("x"), lax.axis_index("y"), lax.axis_index("c")

        def copy(s, b, half):
            return pltpu.make_async_remote_copy(
                src_ref=in_ref.at[s, half, b], dst_ref=out_ref.at[s, b], send_sem=send_sems.at[s * n + b],
                recv_sem=recv_sems.at[s * n + b], device_id=(x, y, 1 - c), device_id_type=MESH)

        sends = [copy(s, b, 1 - c) for s in range(n_s) for b in range(n)]
        for cp in sends:
            cp.start()
        for s in range(n_s):
            for b in range(n):
                copy(s, b, 1 - c).wait_recv()
        for cp in sends:
            cp.wait_send()

    out = pl.pallas_call(
        body,
        out_shape=jax.ShapeDtypeStruct((n_s, n, H // n, C), parts.dtype),
        in_specs=[ANY],
        out_specs=ANY,
        scratch_shapes=[pltpu.SemaphoreType.DMA((n_s * n,)), pltpu.SemaphoreType.DMA((n_s * n,))],
        name="grad_swap_halves",
    )(parts.reshape(n_s, 2, n, H // n, C))
    return out.reshape(n_s, H, C)


def add_pairs(parts, theirs, core):
    n_s, _, H, C = parts.shape
    tile = _div_tile(H, 512, 16)

    def body(c_ref, a_ref, b_ref, o_ref):
        o_ref[...] = (a_ref[...].astype(F32) + b_ref[...].astype(F32)).astype(o_ref.dtype)

    return pl.pallas_call(
        body,
        out_shape=jax.ShapeDtypeStruct((n_s, 1, H, C), parts.dtype),
        grid_spec=pltpu.PrefetchScalarGridSpec(
            num_scalar_prefetch=1,
            grid=(n_s, H // tile),
            in_specs=[pl.BlockSpec((1, 1, tile, C), lambda s, i, c_ref: (s, c_ref[0], i, 0)),
                      pl.BlockSpec((1, 1, tile, C), lambda s, i, c_ref: (s, 0, i, 0))],
            out_specs=pl.BlockSpec((1, 1, tile, C), lambda s, i, c_ref: (s, 0, i, 0)),
        ),
        compiler_params=_cparams(("parallel", "parallel")),
        name="grad_add_pairs",
    )(core.reshape(1).astype(jnp.int32), parts, theirs.reshape(n_s, 1, H, C)).reshape(n_s, H, C)


def scatter_to_chips(parts):
    n_s, H, C = parts.shape
    n = ICI_PIECES

    def body(in_ref, out_ref, send_sems, recv_sems):
        x, y, c = lax.axis_index("x"), lax.axis_index("y"), lax.axis_index("c")
        chips = [(1 - x, y), (x, 1 - y), (1 - x, 1 - y)]

        def copy(k, b, src_slot, dst_slot, to):
            return pltpu.make_async_remote_copy(
                src_ref=in_ref.at[src_slot, b], dst_ref=out_ref.at[dst_slot, b], send_sem=send_sems.at[k * n + b],
                recv_sem=recv_sems.at[k * n + b], device_id=to, device_id_type=MESH)

        sends = [copy(k, b, 2 * px + py, 2 * x + y, (px, py, c)) for k, (px, py) in enumerate(chips) for b in range(n)]
        for cp in sends:
            cp.start()
        for k, (px, py) in enumerate(chips):
            for b in range(n):
                copy(k, b, 2 * x + y, 2 * px + py, (px, py, c)).wait_recv()
        for cp in sends:
            cp.wait_send()

    out = pl.pallas_call(
        body,
        out_shape=jax.ShapeDtypeStruct((n_s, n, H // n, C), parts.dtype),
        in_specs=[ANY],
        out_specs=ANY,
        scratch_shapes=[pltpu.SemaphoreType.DMA((3 * n,)), pltpu.SemaphoreType.DMA((3 * n,))],
        name="grad_scatter",
    )(parts.reshape(n_s, n, H // n, C))
    slot = 2 * lax.axis_index("x") + lax.axis_index("y")
    return _with_own_block(out.reshape(n_s, H, C), lax.dynamic_index_in_dim(parts, slot, axis=0, keepdims=False), slot)


def exchange_with_sibling(mine):
    H, C = mine.shape
    n = D2D_PIECES

    def body(in_ref, out_ref, send_sems, recv_sems):
        x, y, c = lax.axis_index("x"), lax.axis_index("y"), lax.axis_index("c")

        def copy(b, half):
            return pltpu.make_async_remote_copy(
                src_ref=in_ref.at[b], dst_ref=out_ref.at[half, b], send_sem=send_sems.at[b], recv_sem=recv_sems.at[b],
                device_id=(x, y, 1 - c), device_id_type=MESH)

        sends = [copy(b, c) for b in range(n)]
        for cp in sends:
            cp.start()
        for b in range(n):
            copy(b, 1 - c).wait_recv()
        for cp in sends:
            cp.wait_send()

    out = pl.pallas_call(
        body,
        out_shape=jax.ShapeDtypeStruct((2, n, H // n, C), mine.dtype),
        in_specs=[ANY],
        out_specs=ANY,
        scratch_shapes=[pltpu.SemaphoreType.DMA((n,)), pltpu.SemaphoreType.DMA((n,))],
        name="grad_sibling_exchange",
    )(mine.reshape(n, H // n, C))
    return _with_own_block(out.reshape(2, H, C), mine, lax.axis_index("c"))


def sum_contributions(recv):
    n, H, C = recv.shape
    tile = _div_tile(H, 256, 16)

    def body(r_ref, o_ref):
        acc = r_ref[0].astype(F32)
        for j in range(1, n):
            acc = acc + r_ref[j].astype(F32)
        o_ref[...] = acc

    return pl.pallas_call(
        body,
        out_shape=jax.ShapeDtypeStruct((H, C), F32),
        grid=(H // tile,),
        in_specs=[pl.BlockSpec((n, tile, C), lambda i: (0, i, 0))],
        out_specs=pl.BlockSpec((tile, C), lambda i: (i, 0)),
        compiler_params=_cparams(("parallel",)),
        name="grad_sum",
    )(recv)


def adamw(w, g, m, v):
    shape = w.shape
    cols = shape[-1]
    rows = math.prod(shape[:-1])
    tile = _div_tile(rows, max(8, 256 * 1024 // cols // 8 * 8), 8)
    c1 = 1.0 - ADAM_B1 ** ADAM_STEP
    c2 = 1.0 - ADAM_B2 ** ADAM_STEP

    def body(w_ref, g_ref, m_ref, v_ref, d_ref, nm_ref, nv_ref):
        g_ = g_ref[...]
        nm = ADAM_B1 * m_ref[...] + (1.0 - ADAM_B1) * g_
        nv = ADAM_B2 * v_ref[...] + (1.0 - ADAM_B2) * (g_ * g_)
        d_ref[...] = -ADAM_LR * ((nm / c1) / (jnp.sqrt(nv / c2) + ADAM_EPS) + ADAM_WD * w_ref[...])
        nm_ref[...] = nm
        nv_ref[...] = nv

    spec = pl.BlockSpec((tile, cols), lambda i: (i, 0))
    outs = pl.pallas_call(
        body,
        out_shape=[jax.ShapeDtypeStruct((rows, cols), F32)] * 3,
        grid=(rows // tile,),
        in_specs=[spec] * 4,
        out_specs=[spec] * 3,
        compiler_params=_cparams(("parallel",)),
        name=f"adamw_{rows}x{cols}",
    )(*[t.reshape(rows, cols) for t in (w, g, m, v)])
    return [o.reshape(shape) for o in outs]


def _leaves(name, arr):
    n_lead = WEIGHT_LAYOUT[name][0]
    lead = arr.shape[:n_lead]
    flat = arr.reshape((-1,) + arr.shape[n_lead:])
    return [flat[i] for i in range(math.prod(lead))]


def _pad_pack_rows(buf):
    return jnp.pad(buf, ((0, -buf.shape[0] % PACK_ROW_MULT), (0, 0)))


def _pack_flat(pieces, dtype, cols):
    flat = jnp.concatenate([p.reshape(-1).astype(dtype) for p in pieces])
    flat = jnp.pad(flat, (0, -flat.shape[0] % (cols * PACK_ROW_MULT)))
    return flat.reshape(-1, cols)


def _unpack_flat(buf, shapes):
    lead = buf.shape[:-2]
    flat = buf.reshape(lead + (-1,))
    out, off = [], 0
    for shp in shapes:
        n = math.prod(shp)
        out.append(lax.slice_in_dim(flat, off, off + n, axis=len(lead)).reshape(lead + tuple(shp)))
        off += n
    return out


def _pack_slabs(slabs, dtype):
    return _pad_pack_rows(jnp.concatenate([s.astype(dtype) for s in slabs], axis=0))


def _local_slabs(name, shard):
    leaves = _leaves(name, shard)
    return [leaf.T for leaf in leaves] if name in SLAB_TRANSPOSED else leaves


def _chip_slice(leaf, ax, s):
    if ax is None:
        return leaf
    w = leaf.shape[ax] // N_CHIPS
    return lax.slice_in_dim(leaf, s * w, (s + 1) * w, axis=ax)


def gather_weights(shards, cols):
    slabs = [s for n in SLAB_NAMES for s in _local_slabs(n, shards[n])]
    slab_owner = [n for n in SLAB_NAMES for _ in _leaves(n, shards[n])]
    slab_buf = _pack_slabs(slabs, BF16)
    misc = {}
    for dtype in (BF16, F32):
        names = [n for n in MISC_NAMES if WEIGHT_LAYOUT[n][1] is not None and (n in F32_GATHER) == (dtype == F32)]
        pieces = [leaf for n in names for leaf in _leaves(n, shards[n])]
        owners = [n for n in names for _ in _leaves(n, shards[n])]
        buf = _pack_flat(pieces, dtype, cols)
        if dtype == BF16:
            gathered = all_gather_chips(jnp.concatenate([slab_buf, buf], axis=0), "weights_all_gather_bf16")
            slab_part, gathered = gathered[:, :slab_buf.shape[0]], gathered[:, slab_buf.shape[0]:]
        else:
            gathered = all_gather_chips(buf, "weights_all_gather_f32")
        for n, blk in zip(owners, _unpack_flat(gathered, [p.shape for p in pieces])):
            ax = WEIGHT_LAYOUT[n][1]
            misc.setdefault(n, []).append(jnp.concatenate([blk[s] for s in range(N_CHIPS)], axis=ax).astype(F32))
    for n in MISC_NAMES:
        if WEIGHT_LAYOUT[n][1] is None:
            misc[n] = [leaf.astype(F32) for leaf in _leaves(n, shards[n])]
    slab_full, off = {}, 0
    for n, s in zip(slab_owner, slabs):
        rows = s.shape[0]
        slab_full.setdefault(n, []).append(slab_part[:, off:off + rows].reshape(N_CHIPS * rows, cols))
        off += rows
    return slab_full, misc


def reduce_gradients(g_slab, g_misc, shards, cols):
    per_chip = []
    for s in range(N_CHIPS):
        slabs = []
        for n in SLAB_NAMES:
            for leaf in g_slab[n]:
                slabs.append(_chip_slice(leaf, 0, s))
        pieces = [_chip_slice(leaf, WEIGHT_LAYOUT[n][1], s) for n in MISC_NAMES for leaf in g_misc[n]]
        per_chip.append(jnp.concatenate([_pack_slabs(slabs, BF16), _pack_flat(pieces, BF16, cols)], axis=0))
    parts = jnp.stack(per_chip)
    R = parts.shape[1]
    parts = parts.reshape(N_CHIPS, 2, R // 2, cols)
    chip_parts = add_pairs(parts, swap_halves_with_sibling(parts), lax.axis_index("c"))
    recv = scatter_to_chips(chip_parts)
    red = exchange_with_sibling(sum_contributions(recv)).reshape(R, cols)

    out, off = {}, 0
    for n in SLAB_NAMES:
        leaves = []
        for slab in _local_slabs(n, shards[n]):
            rows = slab.shape[0]
            g = red[off:off + rows]
            leaves.append(g.T if n in SLAB_TRANSPOSED else g)
            off += rows
        out[n] = jnp.stack(leaves).reshape(shards[n].shape)
    off += -off % PACK_ROW_MULT
    misc_shapes = [shards[n].shape for n in MISC_NAMES]
    for n, g in zip(MISC_NAMES, _unpack_flat(red[off:], misc_shapes)):
        out[n] = g
    return out


def _silu(x):
    return x * jax.nn.sigmoid(x)


def _heads_first(t, heads):
    S = t.shape[0]
    return t.reshape(S, heads, -1).transpose(1, 0, 2)


def _heads_last(t):
    return t.transpose(1, 0, 2).reshape(t.shape[1], -1)


class SlabWeights:
    def __init__(self, values, deltas):
        self.values, self.deltas = values, deltas

    def matmul(self, a, name, leaf, out_dtype):
        return mm_slab(a, self.values[name][leaf], self.deltas[name][leaf], name in SLAB_TRANSPOSED, out_dtype)


def swiglu_half_step(x, h, slab, leaf, g_post, g_next):
    a = slab.matmul(h, 'ffn_w_gate', leaf, BF16)
    b = slab.matmul(h, 'ffn_w_up', leaf, BF16)
    (u,) = rowwise(lambda a_, b_: ((_silu(a_.astype(F32)) * b_.astype(F32)).astype(BF16),), [a, b], [], "swiglu_act")
    y = slab.matmul(u, 'ffn_w_down', leaf, F32)
    return residual_norm(x, y, 0.5, g_post, g_next)


def residual_norm(x, y, coef, g_post, g_next):
    if g_next is None:
        (xn,) = rowwise(lambda x_, y_, g1: (x_ + coef * _rms(y_, g1),), [x, y], [g_post], "residual")
        return xn, None

    def fn(x_, y_, g1, g2):
        xn = x_ + coef * _rms(y_, g1)
        return xn, _rms(xn, g2).astype(BF16)

    return rowwise(fn, [x, y], [g_post, g_next], "residual_norm")


def memory_attention(q_mem, memn, w_kv):
    kv = mm(memn, w_kv, BF16)
    n_mem = kv.shape[0]
    kv = kv.reshape(n_mem, 2, MEM_HEADS, HEAD_DIM).transpose(1, 2, 0, 3)
    o = full_attention(_heads_first(q_mem.astype(BF16), MEM_HEADS), kv[0], kv[1], HEAD_DIM ** -0.5, False)
    return _heads_last(o)


def mla_mixer(h, cos, sin, w_in, q_g, kv_g, w_uq, w_ukv):
    S = h.shape[0]
    a_in = MLA_Q_LORA + MLA_KV_LORA + MLA_ROPE + MEM_WIDTH
    z = mm(h, jnp.pad(w_in, ((0, 0), (0, -a_in % LANES))), F32)
    o1, o2, o3 = MLA_Q_LORA, MLA_Q_LORA + MLA_KV_LORA, MLA_Q_LORA + MLA_KV_LORA + MLA_ROPE
    c_q, c_kv, k_r, q_mem = z[:, :o1], z[:, o1:o2], z[:, o2:o3], z[:, o3:a_in]
    cqn, ckvn = rowwise(lambda a, b, ga, gb: (_rms(a, ga).astype(BF16), _rms(b, gb).astype(BF16)),
                        [c_q, c_kv], [q_g, kv_g], "mla_lora_norm")
    q = mm(cqn, w_uq, F32).reshape(S, MLA_HEADS, MLA_NOPE + MLA_ROPE)
    kv = mm(ckvn, w_ukv, BF16).reshape(S, MLA_HEADS, MLA_NOPE + MLA_V)
    half = MLA_ROPE // 2
    qn = q[:, :, :MLA_NOPE].reshape(S, MLA_HEADS * MLA_NOPE)
    q1 = q[:, :, MLA_NOPE:MLA_NOPE + half].reshape(S, MLA_HEADS * half)
    q2 = q[:, :, MLA_NOPE + half:].reshape(S, MLA_HEADS * half)
    cos_h, sin_h = jnp.tile(cos, (1, MLA_HEADS)), jnp.tile(sin, (1, MLA_HEADS))
    scale = (MLA_NOPE + MLA_ROPE) ** -0.5
    qm = scale * LOG2E

    def rope(qn_, q1_, q2_, k1_, k2_, ch, sh, c1, s1):
        return ((qn_ * qm).astype(BF16), ((q1_ * ch - q2_ * sh) * qm).astype(BF16),
                ((q1_ * sh + q2_ * ch) * qm).astype(BF16),
                (k1_ * c1 - k2_ * s1).astype(BF16), (k1_ * s1 + k2_ * c1).astype(BF16))

    qns, qr1, qr2, kr1, kr2 = rowwise(rope, [qn, q1, q2, k_r[:, :half], k_r[:, half:], cos_h, sin_h, cos, sin], [],
                                      "mla_rope", n_const=4)
    q_cat = jnp.concatenate([qns.reshape(S, MLA_HEADS, MLA_NOPE), qr1.reshape(S, MLA_HEADS, half),
                             qr2.reshape(S, MLA_HEADS, half)], axis=-1)
    k_rope = jnp.concatenate([kr1, kr2], axis=-1)
    k_cat = jnp.concatenate([kv[:, :, :MLA_NOPE], jnp.broadcast_to(k_rope[:, None, :], (S, MLA_HEADS, MLA_ROPE))],
                            axis=-1)
    o = full_attention(q_cat.transpose(1, 0, 2), k_cat.transpose(1, 0, 2), kv[:, :, MLA_NOPE:].transpose(1, 0, 2),
                       scale, True)
    return _heads_last(o), q_mem


def dilated_mixer(h, slab):
    S = h.shape[0]
    n_g = len(DIL_GROUPS)
    qkv_w = n_g * 3 * DIL_HEADS * HEAD_DIM
    z = slab.matmul(h, 'b_w_in', 0, BF16)
    zd = z[:, :qkv_w].reshape(S, n_g, 3, DIL_HEADS, HEAD_DIM)
    q_mem = z[:, qkv_w:]
    slopes = (2.0 ** (-ALIBI_MAX * (jnp.arange(n_g * DIL_HEADS, dtype=F32) + 1.0) / (n_g * DIL_HEADS)))
    slopes = slopes.reshape(n_g, DIL_HEADS)
    outs, lses = [], []
    for g, (window, dil) in enumerate(DIL_GROUPS):
        L = S // dil

        def sub(t):
            return t.reshape(L, dil, DIL_HEADS, HEAD_DIM).transpose(1, 2, 0, 3).reshape(dil * DIL_HEADS, L, HEAD_DIM)

        slope = (jnp.tile(slopes[g], dil) * dil).reshape(dil * DIL_HEADS, 1, 1)
        o, lse = band_attention(sub(zd[:, g, 0]), sub(zd[:, g, 1]), sub(zd[:, g, 2]), slope, window // (2 * dil))
        outs.append(o.reshape(dil, DIL_HEADS, L, HEAD_DIM).transpose(2, 0, 1, 3).reshape(S * DIL_HEADS, HEAD_DIM))
        lses.append(lse.reshape(dil, DIL_HEADS, L).transpose(2, 0, 1).reshape(S * DIL_HEADS, 1))

    def merge(o0, o1, o2, l0, l1, l2):
        m = jnp.maximum(jnp.maximum(l0, l1), l2)
        e0, e1, e2 = jnp.exp(l0 - m), jnp.exp(l1 - m), jnp.exp(l2 - m)
        return (((e0 * o0 + e1 * o1 + e2 * o2) / (e0 + e1 + e2)).astype(BF16),)

    (o,) = rowwise(merge, outs + lses, [], "dilated_merge")
    return o.reshape(S, DIL_HEADS * HEAD_DIM), q_mem


def conformer_conv_mixer(h, slab, conv_w, conv_b, ln_g, ln_b):
    z = slab.matmul(h, 'c_w_in', 0, F32)
    a, gate, q_mem = z[:, :CONV_CH], z[:, CONV_CH:2 * CONV_CH], z[:, 2 * CONV_CH:]
    (u,) = rowwise(lambda a_, g_: (a_ * jax.nn.sigmoid(g_),), [a, gate], [], "conformer_glu")
    u = depthwise_conv(u, conv_w)

    def post(u_, b, g, beta):
        t = u_ + b
        mu = jnp.mean(t, axis=-1, keepdims=True)
        var = jnp.mean(jnp.square(t - mu), axis=-1, keepdims=True)
        return (_silu((t - mu) * lax.rsqrt(var + EPS) * g + beta).astype(BF16),)

    (o,) = rowwise(post, [u], [conv_b.reshape(1, -1), ln_g.reshape(1, -1), ln_b.reshape(1, -1)], "conformer_post")
    return o, q_mem


def short_conv_mixer(h, slab, conv_w):
    z = slab.matmul(h, 'd_w_in', 0, F32)
    bg, cg, hx, q_mem = z[:, :SC_CH], z[:, SC_CH:2 * SC_CH], z[:, 2 * SC_CH:3 * SC_CH], z[:, 3 * SC_CH:]
    (p,) = rowwise(lambda c_, h_: (c_ * h_,), [cg, hx], [], "shortconv_pre")
    cv = depthwise_conv(p, conv_w)
    (o,) = rowwise(lambda b_, c_: ((b_ * c_).astype(BF16),), [bg, cv], [], "shortconv_post")
    return o, q_mem


def local_loss(deltas, W, x, slab_values, mem, cos, sin, target):
    slab = SlabWeights(slab_values, deltas)
    norm_g = W['norm_g'][0]

    def gain(i, k):
        return norm_g[i, k].reshape(1, -1)

    (h,) = rowwise(lambda x_, g: (_rms(x_, g).astype(BF16),), [x], [gain(0, 0)], "input_norm")
    for i in range(DEPTH):
        x, h = swiglu_half_step(x, h, slab, 2 * i, gain(i, 1), gain(i, 2))
        if i == 0:
            o, q_mem = mla_mixer(h, cos, sin, W['a_w_in'][0], W['a_q_norm'][0].reshape(1, -1),
                                 W['a_kv_norm'][0].reshape(1, -1), W['a_w_uq'][0], W['a_w_ukv'][0])
            out_name = 'a_w_out'
        elif i == 1:
            o, q_mem = dilated_mixer(h, slab)
            out_name = None
        elif i == 2:
            o, q_mem = conformer_conv_mixer(h, slab, W['c_conv_w'][0], W['c_conv_b'][0], W['c_ln_g'][0],
                                            W['c_ln_b'][0])
            out_name = 'c_w_out'
        else:
            o, q_mem = short_conv_mixer(h, slab, W['d_conv_w'][0])
            out_name = 'd_w_out'
        (memn,) = rowwise(lambda m_, g: (_rms(m_, g).astype(BF16),), [mem], [gain(i, 6)], "memory_norm")
        mo = memory_attention(q_mem, memn, W['mem_w_kv'][i])
        cat = jnp.concatenate([o, mo], axis=-1)
        y = mm(cat, W['b_w_out'][0], F32) if out_name is None else slab.matmul(cat, out_name, 0, F32)
        x, h = residual_norm(x, y, 1.0, gain(i, 3), gain(i, 4))
        x, h = swiglu_half_step(x, h, slab, 2 * i + 1, gain(i, 5), gain(i + 1, 0) if i + 1 < DEPTH else None)
    return loss_head(x, target)


def kernel(x, mem, positions, norm_g, ffn_w_gate, ffn_w_up, ffn_w_down, mem_w_kv, a_w_in, a_q_norm, a_kv_norm, a_w_uq, a_w_ukv, a_w_out, b_w_in, b_w_out, c_w_in, c_conv_w, c_conv_b, c_ln_g, c_ln_b, c_w_out, d_w_in, d_conv_w, d_w_out, loss_target, m_norm_g, m_ffn_w_gate, m_ffn_w_up, m_ffn_w_down, m_mem_w_kv, m_a_w_in, m_a_q_norm, m_a_kv_norm, m_a_w_uq, m_a_w_ukv, m_a_w_out, m_b_w_in, m_b_w_out, m_c_w_in, m_c_conv_w, m_c_conv_b, m_c_ln_g, m_c_ln_b, m_c_w_out, m_d_w_in, m_d_conv_w, m_d_w_out, v_norm_g, v_ffn_w_gate, v_ffn_w_up, v_ffn_w_down, v_mem_w_kv, v_a_w_in, v_a_q_norm, v_a_kv_norm, v_a_w_uq, v_a_w_ukv, v_a_w_out, v_b_w_in, v_b_w_out, v_c_w_in, v_c_conv_w, v_c_conv_b, v_c_ln_g, v_c_ln_b, v_c_w_out, v_d_w_in, v_d_conv_w, v_d_w_out):
    w_in = dict(zip(WEIGHT_NAMES, (norm_g, ffn_w_gate, ffn_w_up, ffn_w_down, mem_w_kv, a_w_in, a_q_norm, a_kv_norm, a_w_uq, a_w_ukv, a_w_out, b_w_in, b_w_out, c_w_in, c_conv_w, c_conv_b, c_ln_g, c_ln_b, c_w_out, d_w_in, d_conv_w, d_w_out)))
    m_in = dict(zip(WEIGHT_NAMES, (m_norm_g, m_ffn_w_gate, m_ffn_w_up, m_ffn_w_down, m_mem_w_kv, m_a_w_in, m_a_q_norm, m_a_kv_norm, m_a_w_uq, m_a_w_ukv, m_a_w_out, m_b_w_in, m_b_w_out, m_c_w_in, m_c_conv_w, m_c_conv_b, m_c_ln_g, m_c_ln_b, m_c_w_out, m_d_w_in, m_d_conv_w, m_d_w_out)))
    v_in = dict(zip(WEIGHT_NAMES, (v_norm_g, v_ffn_w_gate, v_ffn_w_up, v_ffn_w_down, v_mem_w_kv, v_a_w_in, v_a_q_norm, v_a_kv_norm, v_a_w_uq, v_a_w_ukv, v_a_w_out, v_b_w_in, v_b_w_out, v_c_w_in, v_c_conv_w, v_c_conv_b, v_c_ln_g, v_c_ln_b, v_c_w_out, v_d_w_in, v_d_conv_w, v_d_w_out)))

    d_model = x.shape[-1]
    slab_values, W = gather_weights(w_in, d_model)
    deltas = {n: [jnp.zeros(leaf.shape, BF16) for leaf in leaves] for n, leaves in slab_values.items()}
    half = MLA_ROPE // 2
    inv = ROPE_THETA ** (-jnp.arange(half, dtype=F32) / half)
    ang = positions[0].astype(F32)[:, None] * inv
    loss, (g_slab, g_misc, gx) = jax.value_and_grad(local_loss, argnums=(0, 1, 2))(
        deltas, W, x[0], slab_values, mem[0], jnp.cos(ang), jnp.sin(ang), loss_target[0])
    loss = lax.psum(loss, ("x", "y", "c"))

    grads = reduce_gradients(g_slab, g_misc, w_in, d_model)
    steps = {n: adamw(w_in[n], grads[n], m_in[n], v_in[n]) for n in WEIGHT_NAMES}
    return (loss, gx[None], *[grads[n] for n in WEIGHT_NAMES], *[steps[n][0] for n in WEIGHT_NAMES],
            *[steps[n][1] for n in WEIGHT_NAMES], *[steps[n][2] for n in WEIGHT_NAMES])
```

```python
import functools
import math

import jax
import jax.numpy as jnp
from jax import lax
from jax.experimental import pallas as pl
from jax.experimental.pallas import tpu as pltpu

F32 = jnp.float32
BF16 = jnp.bfloat16
MESH = pl.DeviceIdType.MESH

VMEM_LIMIT_BYTES = 56 * 1024 * 1024
LANES = 128
PACK_ROW_MULT = 512
ROW_TILE_ELEMS = 768 * 1024

HEAD_DIM = 64
MEM_HEADS = 4
MEM_WIDTH = MEM_HEADS * HEAD_DIM
MLA_HEADS = 12
MLA_Q_LORA = 384
MLA_KV_LORA = 256
MLA_NOPE = 64
MLA_ROPE = 32
MLA_V = 64
ROPE_THETA = 10000.0
DIL_GROUPS = ((128, 1), (512, 4), (2048, 16))
DIL_HEADS = 8
ALIBI_MAX = 8.0
CONV_CH = 768
CONV_WIDTH = 31
SC_CH = 768
SC_WIDTH = 3
CONV_PAD = 16
CONV_CHUNK = 256
EPS = 1e-6
NEG = -1e30
DEPTH = 4
LOG2E = 1.4426950408889634
LN2 = 0.6931471805599453

ADAM_LR = 0.001
ADAM_B1 = 0.9
ADAM_B2 = 0.999
ADAM_EPS = 1e-08
ADAM_WD = 0.01
ADAM_STEP = 10

WEIGHT_NAMES = ['norm_g', 'ffn_w_gate', 'ffn_w_up', 'ffn_w_down', 'mem_w_kv', 'a_w_in', 'a_q_norm', 'a_kv_norm',
                'a_w_uq', 'a_w_ukv', 'a_w_out', 'b_w_in', 'b_w_out', 'c_w_in', 'c_conv_w', 'c_conv_b', 'c_ln_g',
                'c_ln_b', 'c_w_out', 'd_w_in', 'd_conv_w', 'd_w_out']
WEIGHT_LAYOUT = {
    'norm_g': (0, 2), 'ffn_w_gate': (2, 1), 'ffn_w_up': (2, 1), 'ffn_w_down': (2, 0), 'mem_w_kv': (1, 0),
    'a_w_in': (1, 0), 'a_q_norm': (1, None), 'a_kv_norm': (1, None), 'a_w_uq': (1, 1), 'a_w_ukv': (1, 1),
    'a_w_out': (1, 0), 'b_w_in': (1, 1), 'b_w_out': (1, 1), 'c_w_in': (1, 1), 'c_conv_w': (1, 1),
    'c_conv_b': (1, 0), 'c_ln_g': (1, 0), 'c_ln_b': (1, 0), 'c_w_out': (1, 0), 'd_w_in': (1, 1),
    'd_conv_w': (1, 1), 'd_w_out': (1, 0),
}
SLAB_TRANSPOSED = ('ffn_w_gate', 'ffn_w_up', 'b_w_in', 'c_w_in', 'd_w_in')
SLAB_ROWS = ('ffn_w_down', 'a_w_out', 'c_w_out', 'd_w_out')
SLAB_NAMES = [n for n in WEIGHT_NAMES if n in SLAB_TRANSPOSED or n in SLAB_ROWS]
MISC_NAMES = [n for n in WEIGHT_NAMES if n not in SLAB_NAMES]
F32_GATHER = ('norm_g', 'c_conv_w', 'c_conv_b', 'c_ln_g', 'c_ln_b', 'd_conv_w')
N_CHIPS = 4


def _cparams(semantics):
    return pltpu.CompilerParams(dimension_semantics=semantics, vmem_limit_bytes=VMEM_LIMIT_BYTES)


def _div_tile(n, cap, mult):
    if n <= cap:
        return n
    for d in range(cap - cap % mult, 0, -mult):
        if n % d == 0:
            return d
    raise ValueError(f"no tile for {n} (cap {cap}, multiple of {mult})")


def _matmul(a, b, mode, out_dtype):
    if mode == 'nn':
        (M, K), N = a.shape, b.shape[1]
    elif mode == 'nt':
        (M, K), N = a.shape, b.shape[0]
    else:
        (K, M), N = a.shape, b.shape[1]
    if mode == 'tn':
        tm, tn, tk = _div_tile(M, 2816, LANES), _div_tile(N, 1536, LANES), _div_tile(K, 1024, 16)
    else:
        tm, tn, tk = _div_tile(M, 512, 16), _div_tile(N, 2816, LANES), _div_tile(K, 2816, LANES)
    nk = K // tk
    if mode == 'nn':
        a_spec = pl.BlockSpec((tm, tk), lambda i, j, k: (i, k))
        b_spec = pl.BlockSpec((tk, tn), lambda i, j, k: (k, j))
        dims = (((1,), (0,)), ((), ()))
    elif mode == 'nt':
        a_spec = pl.BlockSpec((tm, tk), lambda i, j, k: (i, k))
        b_spec = pl.BlockSpec((tn, tk), lambda i, j, k: (j, k))
        dims = (((1,), (1,)), ((), ()))
    else:
        a_spec = pl.BlockSpec((tk, tm), lambda i, j, k: (k, i))
        b_spec = pl.BlockSpec((tk, tn), lambda i, j, k: (k, j))
        dims = (((0,), (0,)), ((), ()))

    def body(a_ref, b_ref, o_ref, *acc):
        r = lax.dot_general(a_ref[...].astype(BF16), b_ref[...].astype(BF16), dims, preferred_element_type=F32)
        if nk == 1:
            o_ref[...] = r.astype(o_ref.dtype)
        else:
            k = pl.program_id(2)

            @pl.when(k == 0)
            def _():
                acc[0][...] = r

            @pl.when(k > 0)
            def _():
                acc[0][...] += r

            @pl.when(k == nk - 1)
            def _():
                o_ref[...] = acc[0][...].astype(o_ref.dtype)

    return pl.pallas_call(
        body,
        out_shape=jax.ShapeDtypeStruct((M, N), out_dtype),
        grid=(M // tm, N // tn, nk),
        in_specs=[a_spec, b_spec],
        out_specs=pl.BlockSpec((tm, tn), lambda i, j, k: (i, j)),
        scratch_shapes=[pltpu.VMEM((tm, tn), F32)] if nk > 1 else [],
        compiler_params=_cparams(("parallel", "parallel", "arbitrary")),
        name=f"mm_{mode}_{M}x{K}x{N}",
    )(a, b)


@functools.partial(jax.custom_vjp, nondiff_argnums=(2,))
def mm(a, w, out_dtype):
    return _matmul(a, w.astype(BF16), 'nn', out_dtype)


def _mm_fwd(a, w, out_dtype):
    wb = w.astype(BF16)
    return _matmul(a, wb, 'nn', out_dtype), (a, wb)


def _mm_bwd(out_dtype, res, g):
    a, wb = res
    return _matmul(g, wb, 'nt', a.dtype), _matmul(a, g, 'tn', F32)


mm.defvjp(_mm_fwd, _mm_bwd)


@functools.partial(jax.custom_vjp, nondiff_argnums=(3, 4))
def mm_slab(a, wb, delta, transposed, out_dtype):
    return _matmul(a, wb, 'nt' if transposed else 'nn', out_dtype)


def _mm_slab_fwd(a, wb, delta, transposed, out_dtype):
    return _matmul(a, wb, 'nt' if transposed else 'nn', out_dtype), (a, wb)


def _mm_slab_bwd(transposed, out_dtype, res, g):
    a, wb = res
    if transposed:
        return _matmul(g, wb, 'nn', a.dtype), jnp.zeros_like(wb), _matmul(g, a, 'tn', BF16)
    return _matmul(g, wb, 'nt', a.dtype), jnp.zeros_like(wb), _matmul(a, g, 'tn', BF16)


mm_slab.defvjp(_mm_slab_fwd, _mm_slab_bwd)


def _row_tile(n_rows, widest):
    t = 1 << max(3, int(math.log2(max(8, ROW_TILE_ELEMS // max(widest, 1)))))
    return min(n_rows, min(t, 2048))


def _rw_forward(fn, rows, params, name):
    n_rows = rows[0].shape[0]
    tile = _row_tile(n_rows, max(r.shape[1] for r in rows))
    outs = jax.eval_shape(fn, *[jax.ShapeDtypeStruct((tile, r.shape[1]), r.dtype) for r in rows],
                          *[jax.ShapeDtypeStruct(p.shape, p.dtype) for p in params])
    n_in = len(rows) + len(params)

    def body(*refs):
        res = fn(*[r[...] for r in refs[:n_in]])
        for o_ref, o in zip(refs[n_in:], res):
            o_ref[...] = o

    return pl.pallas_call(
        body,
        out_shape=[jax.ShapeDtypeStruct((n_rows, o.shape[1]), o.dtype) for o in outs],
        grid=(n_rows // tile,),
        in_specs=[pl.BlockSpec((tile, r.shape[1]), lambda i: (i, 0)) for r in rows]
        + [pl.BlockSpec(p.shape, lambda i: (0, 0)) for p in params],
        out_specs=[pl.BlockSpec((tile, o.shape[1]), lambda i: (i, 0)) for o in outs],
        compiler_params=_cparams(("parallel",)),
        name=name,
    )(*rows, *params)


def _rw_backward(fn, rows, params, cts, n_const, name):
    n_rows = rows[0].shape[0]
    tile = _row_tile(n_rows, max([r.shape[1] for r in rows] + [c.shape[1] for c in cts]))
    n_r, n_p, n_c = len(rows), len(params), len(cts)
    n_diff = n_r - n_const

    def body(*refs):
        row_vals = [r[...] for r in refs[:n_r]]
        par_vals = [r[...] for r in refs[n_r:n_r + n_p]]
        ct_vals = tuple(r[...] for r in refs[n_r + n_p:n_r + n_p + n_c])
        out_refs = refs[n_r + n_p + n_c:]

        def f(*diff):
            return fn(*diff[:n_diff], *row_vals[n_diff:], *diff[n_diff:])

        _, vjp = jax.vjp(f, *row_vals[:n_diff], *par_vals)
        grads = vjp(ct_vals)
        for o_ref, g in zip(out_refs[:n_diff], grads[:n_diff]):
            o_ref[...] = g
        i = pl.program_id(0)
        for o_ref, g in zip(out_refs[n_diff:], grads[n_diff:]):
            @pl.when(i == 0)
            def _(o_ref=o_ref, g=g):
                o_ref[...] = g

            @pl.when(i > 0)
            def _(o_ref=o_ref, g=g):
                o_ref[...] += g

    res = pl.pallas_call(
        body,
        out_shape=[jax.ShapeDtypeStruct(r.shape, r.dtype) for r in rows[:n_diff]]
        + [jax.ShapeDtypeStruct(p.shape, p.dtype) for p in params],
        grid=(n_rows // tile,),
        in_specs=[pl.BlockSpec((tile, r.shape[1]), lambda i: (i, 0)) for r in rows]
        + [pl.BlockSpec(p.shape, lambda i: (0, 0)) for p in params]
        + [pl.BlockSpec((tile, c.shape[1]), lambda i: (i, 0)) for c in cts],
        out_specs=[pl.BlockSpec((tile, r.shape[1]), lambda i: (i, 0)) for r in rows[:n_diff]]
        + [pl.BlockSpec(p.shape, lambda i: (0, 0)) for p in params],
        compiler_params=_cparams(("arbitrary",)),
        name=name + "_bwd",
    )(*rows, *params, *cts)
    return list(res[:n_diff]), list(res[n_diff:])


def rowwise(fn, rows, params, name, n_const=0):
    rows, params = list(rows), list(params)

    @jax.custom_vjp
    def op(rows, params):
        return tuple(_rw_forward(fn, rows, params, name))

    def op_fwd(rows, params):
        return tuple(_rw_forward(fn, rows, params, name)), (rows, params)

    def op_bwd(res, cts):
        rows, params = res
        d_rows, d_params = _rw_backward(fn, rows, params, list(cts), n_const, name)
        d_rows = d_rows + [jnp.zeros_like(r) for r in rows[len(rows) - n_const:]]
        return d_rows, d_params

    op.defvjp(op_fwd, op_bwd)
    return op(rows, params)


def _rms(x, g):
    xf = x.astype(F32)
    return xf * lax.rsqrt(jnp.mean(xf * xf, axis=-1, keepdims=True) + EPS) * g


ATTN_TILE = 1024
ATTN_SPAN = 2048
ATTN_CHUNK = 512

NT_DIMS = (((1,), (1,)), ((), ()))


def _attn_tiles(stay, stream):
    span = min(ATTN_SPAN, stream)
    return min(ATTN_TILE, stay), span, min(ATTN_CHUNK, span)


def _flash_fwd(q, k, v, qk_mult):
    N, Lq, dk = q.shape
    Lk, dv = k.shape[1], v.shape[2]
    tq, span, ch = _attn_tiles(Lq, Lk)
    tq = min(Lq, tq * max(2, ATTN_TILE // span))
    nk = Lk // span
    v_ones = jnp.concatenate([v, jnp.ones((N, Lk, 1), v.dtype), jnp.zeros((N, Lk, LANES - dv - 1), v.dtype)], axis=2)

    def body(q_ref, k_ref, v_ref, o_ref, lse_ref, m_s, acc_s):
        j = pl.program_id(2)

        @pl.when(j == 0)
        def _():
            m_s[...] = jnp.full_like(m_s, NEG)
            acc_s[...] = jnp.zeros_like(acc_s)

        qv = q_ref[0]
        s = []
        for c in range(span // ch):
            sc = lax.dot_general(qv, k_ref[0, c * ch:(c + 1) * ch, :], NT_DIMS, preferred_element_type=F32)
            s.append(sc if qk_mult == 1.0 else sc * qk_mult)
        s_max = functools.reduce(jnp.maximum, s)
        m_new = jnp.maximum(m_s[...], jnp.max(s_max, axis=-1, keepdims=True))
        pv = None
        for c, sc in enumerate(s):
            p = jnp.exp2(sc - m_new).astype(BF16)
            d = jnp.dot(p, v_ref[0, c * ch:(c + 1) * ch, :], preferred_element_type=F32)
            pv = d if c == 0 else pv + d
        acc_s[...] = jnp.exp2(m_s[...] - m_new) * acc_s[...] + pv
        m_s[...] = m_new

        @pl.when(j == nk - 1)
        def _():
            acc = acc_s[...]
            l = acc[:, dv:dv + 1]
            o_ref[0] = (acc[:, :dv] / l).astype(o_ref.dtype)
            lse_ref[0] = m_s[...] + jnp.log(l) * LOG2E

    return pl.pallas_call(
        body,
        out_shape=[jax.ShapeDtypeStruct((N, Lq, dv), BF16), jax.ShapeDtypeStruct((N, Lq, 1), F32)],
        grid=(N, Lq // tq, nk),
        in_specs=[pl.BlockSpec((1, tq, dk), lambda n, i, j: (n, i, 0)),
                  pl.BlockSpec((1, span, dk), lambda n, i, j: (n, j, 0)),
                  pl.BlockSpec((1, span, LANES), lambda n, i, j: (n, j, 0))],
        out_specs=[pl.BlockSpec((1, tq, dv), lambda n, i, j: (n, i, 0)),
                   pl.BlockSpec((1, tq, 1), lambda n, i, j: (n, i, 0))],
        scratch_shapes=[pltpu.VMEM((tq, 1), F32), pltpu.VMEM((tq, LANES), F32)],
        compiler_params=_cparams(("parallel", "parallel", "arbitrary")),
        name=f"flash_fwd_{N}x{Lq}x{Lk}x{dk}",
    )(q, k, v_ones)


def _flash_bwd(q, k, v, do, lse_row, dd_row, qk_mult, out_mult):
    N, Lq, dk = q.shape
    Lk, dv = k.shape[1], v.shape[2]
    tk, span, ch = _attn_tiles(Lk, Lq)
    nq, nkt = Lq // span, Lk // tk

    def body(q_ref, k_ref, v_ref, do_ref, lse_ref, dd_ref, dq_ref, dk_ref, dv_ref, dq_s, dk_s, dv_s):
        i, j = pl.program_id(1), pl.program_id(2)

        @pl.when((i == 0) & (j == 0))
        def _():
            dq_s[...] = jnp.zeros_like(dq_s)

        @pl.when(j == 0)
        def _():
            dk_s[...] = jnp.zeros_like(dk_s)
            dv_s[...] = jnp.zeros_like(dv_s)

        kv_, vv = k_ref[0], v_ref[0]
        for c in range(span // ch):
            qc = q_ref[0, c * ch:(c + 1) * ch, :]
            doc = do_ref[0, c * ch:(c + 1) * ch, :]
            st = lax.dot_general(kv_, qc, NT_DIMS, preferred_element_type=F32)
            if qk_mult != 1.0:
                st = st * qk_mult
            pt = jnp.exp2(st - lse_ref[0, :, c * ch:(c + 1) * ch])
            dv_s[...] += jnp.dot(pt.astype(BF16), doc, preferred_element_type=F32)
            dpt = lax.dot_general(vv, doc, NT_DIMS, preferred_element_type=F32)
            dst = (pt * (dpt - dd_ref[0, :, c * ch:(c + 1) * ch])).astype(BF16)
            dk_s[...] += jnp.dot(dst, qc, preferred_element_type=F32)
            rows = pl.ds(pl.multiple_of(j * span + c * ch, ch), ch)
            dq_s[rows, :] += lax.dot_general(dst, kv_, (((0,), (0,)), ((), ())), preferred_element_type=F32)

        @pl.when(j == nq - 1)
        def _():
            dk_ref[0] = (dk_s[...] * out_mult).astype(dk_ref.dtype)
            dv_ref[0] = dv_s[...].astype(dv_ref.dtype)

        @pl.when((i == nkt - 1) & (j == nq - 1))
        def _():
            dq_ref[0] = (dq_s[...] * out_mult).astype(dq_ref.dtype)

    return pl.pallas_call(
        body,
        out_shape=[jax.ShapeDtypeStruct(q.shape, q.dtype), jax.ShapeDtypeStruct(k.shape, k.dtype),
                   jax.ShapeDtypeStruct(v.shape, v.dtype)],
        grid=(N, nkt, nq),
        in_specs=[pl.BlockSpec((1, span, dk), lambda n, i, j: (n, j, 0)),
                  pl.BlockSpec((1, tk, dk), lambda n, i, j: (n, i, 0)),
                  pl.BlockSpec((1, tk, dv), lambda n, i, j: (n, i, 0)),
                  pl.BlockSpec((1, span, dv), lambda n, i, j: (n, j, 0)),
                  pl.BlockSpec((1, 1, span), lambda n, i, j: (n, 0, j)),
                  pl.BlockSpec((1, 1, span), lambda n, i, j: (n, 0, j))],
        out_specs=[pl.BlockSpec((1, Lq, dk), lambda n, i, j: (n, 0, 0)),
                   pl.BlockSpec((1, tk, dk), lambda n, i, j: (n, i, 0)),
                   pl.BlockSpec((1, tk, dv), lambda n, i, j: (n, i, 0))],
        scratch_shapes=[pltpu.VMEM((Lq, dk), F32), pltpu.VMEM((tk, dk), F32), pltpu.VMEM((tk, dv), F32)],
        compiler_params=_cparams(("parallel", "arbitrary", "arbitrary")),
        name=f"flash_bwd_{N}x{Lq}x{Lk}x{dk}",
    )(q, k, v, do, lse_row, dd_row)


def _attn_delta(do, o, dlse):
    N, L, dv = o.shape

    def fn(do_t, o_t, dl_t):
        return (jnp.sum(do_t.astype(F32) * o_t.astype(F32), axis=-1, keepdims=True) - dl_t,)

    (dd,) = _rw_forward(fn, [do.reshape(N * L, dv), o.reshape(N * L, dv), dlse.reshape(N * L, 1)], [], "attn_delta")
    return dd.reshape(N, L, 1)


def full_attention(q, k, v, scale, q_prescaled):
    qk_mult = 1.0 if q_prescaled else scale * LOG2E
    grad_mult = LN2 if q_prescaled else scale

    @jax.custom_vjp
    def op(q, k, v):
        return _flash_fwd(q, k, v, qk_mult)[0]

    def op_fwd(q, k, v):
        o, lse = _flash_fwd(q, k, v, qk_mult)
        return o, (q, k, v, o, lse)

    def op_bwd(res, do):
        q, k, v, o, lse = res
        N, Lq, _ = q.shape
        dd = _attn_delta(do, o, jnp.zeros_like(lse))
        return tuple(_flash_bwd(q, k, v, do, lse.reshape(N, 1, Lq), dd.reshape(N, 1, Lq), qk_mult, grad_mult))

    op.defvjp(op_fwd, op_bwd)
    return op(q, k, v)


BAND_TILE = 128
BAND_BATCH = 8


def _band_specs(L, d):
    t = min(BAND_TILE, L)
    nt = L // t
    cur = lambda n, i: (n, i, 0)
    prev = lambda n, i: (n, jnp.maximum(i - 1, 0), 0)
    nxt = lambda n, i: (n, jnp.minimum(i + 1, nt - 1), 0)
    return t, nt, cur, prev, nxt


def _band_mask(i, t, L, half):
    a = lax.broadcasted_iota(jnp.int32, (t, 3 * t), 0)
    b = lax.broadcasted_iota(jnp.int32, (t, 3 * t), 1)
    dist = jnp.abs(b - t - a)
    other = i * t - t + b
    mask = (dist <= half) & (other >= 0) & (other < L)
    return dist.astype(F32), mask


def _band_fwd(q, k, v, slope, half):
    N, L, d = q.shape
    t, nt, cur, prev, nxt = _band_specs(L, d)
    nb = BAND_BATCH
    scale = d ** -0.5

    def body(sl_ref, q_ref, kp, kc, kn, vp, vc, vn, o_ref, lse_ref):
        i = pl.program_id(1)
        kcat = jnp.concatenate([kp[...], kc[...], kn[...]], axis=1)
        vcat = jnp.concatenate([vp[...], vc[...], vn[...]], axis=1)
        s = lax.dot_general(q_ref[...], kcat, (((2,), (2,)), ((0,), (0,))), preferred_element_type=F32) * scale
        dist, mask = _band_mask(i, t, L, half)
        s = jnp.where(mask[None], s - sl_ref[...] * dist[None], NEG)
        m = jnp.max(s, axis=-1, keepdims=True)
        p = jnp.exp(s - m)
        l = jnp.sum(p, axis=-1, keepdims=True)
        o = lax.dot_general(p.astype(BF16), vcat, (((2,), (1,)), ((0,), (0,))), preferred_element_type=F32)
        o_ref[...] = o / l
        lse_ref[...] = m + jnp.log(l)

    blk = lambda f: pl.BlockSpec((nb, t, d), f)
    return pl.pallas_call(
        body,
        out_shape=[jax.ShapeDtypeStruct((N, L, d), F32), jax.ShapeDtypeStruct((N, L, 1), F32)],
        grid=(N // nb, nt),
        in_specs=[pl.BlockSpec((nb, 1, 1), lambda n, i: (n, 0, 0)), blk(cur),
                  blk(prev), blk(cur), blk(nxt), blk(prev), blk(cur), blk(nxt)],
        out_specs=[blk(cur), pl.BlockSpec((nb, t, 1), cur)],
        compiler_params=_cparams(("parallel", "parallel")),
        name=f"band_fwd_{N}x{L}",
    )(slope, q, k, k, k, v, v, v)


def _band_dq(q, k, v, slope, do, lse, dd, half):
    N, L, d = q.shape
    t, nt, cur, prev, nxt = _band_specs(L, d)
    nb = BAND_BATCH
    scale = d ** -0.5

    def body(sl_ref, q_ref, kp, kc, kn, vp, vc, vn, do_ref, lse_ref, dd_ref, dq_ref):
        i = pl.program_id(1)
        kcat = jnp.concatenate([kp[...], kc[...], kn[...]], axis=1)
        vcat = jnp.concatenate([vp[...], vc[...], vn[...]], axis=1)
        s = lax.dot_general(q_ref[...], kcat, (((2,), (2,)), ((0,), (0,))), preferred_element_type=F32) * scale
        dist, mask = _band_mask(i, t, L, half)
        p = jnp.where(mask[None], jnp.exp(s - sl_ref[...] * dist[None] - lse_ref[...]), 0.0)
        dp = lax.dot_general(do_ref[...].astype(BF16), vcat, (((2,), (2,)), ((0,), (0,))),
                             preferred_element_type=F32)
        ds = (p * (dp - dd_ref[...])).astype(BF16)
        dq = lax.dot_general(ds, kcat, (((2,), (1,)), ((0,), (0,))), preferred_element_type=F32)
        dq_ref[...] = (dq * scale).astype(dq_ref.dtype)

    blk = lambda f: pl.BlockSpec((nb, t, d), f)
    col = pl.BlockSpec((nb, t, 1), cur)
    return pl.pallas_call(
        body,
        out_shape=jax.ShapeDtypeStruct(q.shape, q.dtype),
        grid=(N // nb, nt),
        in_specs=[pl.BlockSpec((nb, 1, 1), lambda n, i: (n, 0, 0)), blk(cur),
                  blk(prev), blk(cur), blk(nxt), blk(prev), blk(cur), blk(nxt), blk(cur), col, col],
        out_specs=blk(cur),
        compiler_params=_cparams(("parallel", "parallel")),
        name=f"band_dq_{N}x{L}",
    )(slope, q, k, k, k, v, v, v, do, lse, dd)


def _band_dkv(q, k, v, slope, do, lse_row, dd_row, half):
    N, L, d = q.shape
    t, nt, cur, prev, nxt = _band_specs(L, d)
    nb = BAND_BATCH
    scale = d ** -0.5
    rcur = lambda n, i: (n, 0, i)
    rprev = lambda n, i: (n, 0, jnp.maximum(i - 1, 0))
    rnxt = lambda n, i: (n, 0, jnp.minimum(i + 1, nt - 1))

    def body(sl_ref, k_ref, v_ref, qp, qc, qn, dop, doc, don, lp, lc, ln, dp_, dc_, dn_, dk_ref, dv_ref):
        i = pl.program_id(1)
        qcat = jnp.concatenate([qp[...], qc[...], qn[...]], axis=1)
        docat = jnp.concatenate([dop[...], doc[...], don[...]], axis=1).astype(BF16)
        lrow = jnp.concatenate([lp[...], lc[...], ln[...]], axis=2)
        drow = jnp.concatenate([dp_[...], dc_[...], dn_[...]], axis=2)
        st = lax.dot_general(k_ref[...], qcat, (((2,), (2,)), ((0,), (0,))), preferred_element_type=F32) * scale
        dist, mask = _band_mask(i, t, L, half)
        pt = jnp.where(mask[None], jnp.exp(st - sl_ref[...] * dist[None] - lrow), 0.0)
        dv = lax.dot_general(pt.astype(BF16), docat, (((2,), (1,)), ((0,), (0,))), preferred_element_type=F32)
        dpt = lax.dot_general(v_ref[...], docat, (((2,), (2,)), ((0,), (0,))), preferred_element_type=F32)
        dst = (pt * (dpt - drow)).astype(BF16)
        dk = lax.dot_general(dst, qcat, (((2,), (1,)), ((0,), (0,))), preferred_element_type=F32)
        dk_ref[...] = (dk * scale).astype(dk_ref.dtype)
        dv_ref[...] = dv.astype(dv_ref.dtype)

    blk = lambda f: pl.BlockSpec((nb, t, d), f)
    row = lambda f: pl.BlockSpec((nb, 1, t), f)
    return pl.pallas_call(
        body,
        out_shape=[jax.ShapeDtypeStruct(k.shape, k.dtype), jax.ShapeDtypeStruct(v.shape, v.dtype)],
        grid=(N // nb, nt),
        in_specs=[pl.BlockSpec((nb, 1, 1), lambda n, i: (n, 0, 0)), blk(cur), blk(cur),
                  blk(prev), blk(cur), blk(nxt), blk(prev), blk(cur), blk(nxt),
                  row(rprev), row(rcur), row(rnxt), row(rprev), row(rcur), row(rnxt)],
        out_specs=[blk(cur), blk(cur)],
        compiler_params=_cparams(("parallel", "parallel")),
        name=f"band_dkv_{N}x{L}",
    )(slope, k, v, q, q, q, do, do, do, lse_row, lse_row, lse_row, dd_row, dd_row, dd_row)


def band_attention(q, k, v, slope, half):
    @jax.custom_vjp
    def op(q, k, v, slope):
        return tuple(_band_fwd(q, k, v, slope, half))

    def op_fwd(q, k, v, slope):
        o, lse = _band_fwd(q, k, v, slope, half)
        return (o, lse), (q, k, v, slope, o, lse)

    def op_bwd(res, cts):
        q, k, v, slope, o, lse = res
        do, dlse = cts
        N, L, _ = q.shape
        dd = _attn_delta(do, o, dlse)
        dq = _band_dq(q, k, v, slope, do, lse, dd, half)
        dk, dv = _band_dkv(q, k, v, slope, do, lse.reshape(N, 1, L), dd.reshape(N, 1, L), half)
        return dq, dk, dv, jnp.zeros_like(slope)

    op.defvjp(op_fwd, op_bwd)
    return op(q, k, v, slope)


def _conv_taps(width):
    return [CONV_PAD - width // 2 + j for j in range(width)]


def _dwconv_call(up, w, width):
    S, C = up.shape[0] - 2 * CONV_PAD, up.shape[1]
    ch = min(CONV_CHUNK, S)
    offs = _conv_taps(width)

    def body(u_ref, w_ref, o_ref):
        def chunk(ci, carry):
            base = pl.multiple_of(ci * ch, ch)
            win = u_ref[pl.ds(base, ch + 2 * CONV_PAD), :]
            acc = jnp.zeros((ch, LANES), F32)
            for j, off in enumerate(offs):
                acc = acc + win[off:off + ch, :] * w_ref[j:j + 1, :]
            o_ref[pl.ds(base, ch), :] = acc
            return carry

        lax.fori_loop(0, S // ch, chunk, 0)

    return pl.pallas_call(
        body,
        out_shape=jax.ShapeDtypeStruct((S, C), F32),
        grid=(C // LANES,),
        in_specs=[pl.BlockSpec((S + 2 * CONV_PAD, LANES), lambda c: (0, c)),
                  pl.BlockSpec((w.shape[0], LANES), lambda c: (0, c))],
        out_specs=pl.BlockSpec((S, LANES), lambda c: (0, c)),
        compiler_params=_cparams(("parallel",)),
        name=f"dwconv{width}_{S}x{C}",
    )(up, w)


def _dwconv_dw_call(up, g, width, w_rows):
    S, C = g.shape
    ch = min(CONV_CHUNK, S)
    offs = _conv_taps(width)

    def body(u_ref, g_ref, o_ref):
        def chunk(ci, accs):
            base = pl.multiple_of(ci * ch, ch)
            win = u_ref[pl.ds(base, ch + 2 * CONV_PAD), :]
            gt = g_ref[pl.ds(base, ch), :]
            return tuple(a + (gt * win[off:off + ch, :]).reshape(ch // 8, 8, LANES).sum(axis=0)
                         for a, off in zip(accs, offs))

        accs = lax.fori_loop(0, S // ch, chunk, tuple(jnp.zeros((8, LANES), F32) for _ in offs))
        o_ref[...] = jnp.zeros_like(o_ref)
        for j, a in enumerate(accs):
            o_ref[j:j + 1, :] = jnp.sum(a, axis=0, keepdims=True)

    return pl.pallas_call(
        body,
        out_shape=jax.ShapeDtypeStruct((w_rows, C), F32),
        grid=(C // LANES,),
        in_specs=[pl.BlockSpec((S + 2 * CONV_PAD, LANES), lambda c: (0, c)),
                  pl.BlockSpec((S, LANES), lambda c: (0, c))],
        out_specs=pl.BlockSpec((w_rows, LANES), lambda c: (0, c)),
        compiler_params=_cparams(("parallel",)),
        name=f"dwconv{width}_dw_{S}x{C}",
    )(up, g)


def _pad_rows(u):
    return jnp.pad(u, ((CONV_PAD, CONV_PAD), (0, 0)))


def _pad_taps(w):
    return jnp.pad(w, ((0, -w.shape[0] % 8), (0, 0)))


def depthwise_conv(u, w):
    width = w.shape[0]

    @jax.custom_vjp
    def op(u, w):
        return _dwconv_call(_pad_rows(u), _pad_taps(w), width)

    def op_fwd(u, w):
        up = _pad_rows(u)
        return _dwconv_call(up, _pad_taps(w), width), (up, w)

    def op_bwd(res, g):
        up, w = res
        du = _dwconv_call(_pad_rows(g), _pad_taps(w[::-1]), width)
        dw = _dwconv_dw_call(up, g, width, _pad_taps(w).shape[0])[:width]
        return du, dw

    op.defvjp(op_fwd, op_bwd)
    return op(u, w)


def _loss_call(y, target):
    S, D = y.shape
    tile = _row_tile(S, D)

    def body(y_ref, t_ref, o_ref):
        d = y_ref[...] - t_ref[...]
        part = jnp.sum(jnp.sum(d * d, axis=1, keepdims=True), axis=0, keepdims=True) * (0.5 / D)

        @pl.when(pl.program_id(0) == 0)
        def _():
            o_ref[...] = jnp.zeros_like(o_ref)

        o_ref[...] += jnp.broadcast_to(part, o_ref.shape)

    out = pl.pallas_call(
        body,
        out_shape=jax.ShapeDtypeStruct((8, LANES), F32),
        grid=(S // tile,),
        in_specs=[pl.BlockSpec((tile, D), lambda i: (i, 0)), pl.BlockSpec((tile, D), lambda i: (i, 0))],
        out_specs=pl.BlockSpec((8, LANES), lambda i: (0, 0)),
        compiler_params=_cparams(("arbitrary",)),
        name="loss_head",
    )(y, target)
    return out[0, 0]


@jax.custom_vjp
def loss_head(y, target):
    return _loss_call(y, target)


def _loss_fwd(y, target):
    return _loss_call(y, target), (y, target)


def _loss_bwd(res, g):
    y, target = res
    inv_d = 1.0 / y.shape[1]

    def fn(y_t, t_t, g_p):
        return ((y_t - t_t) * (g_p * inv_d),)

    (dy,) = _rw_forward(fn, [y, target], [g.reshape(1, 1).astype(F32)], "loss_head_bwd")
    return dy, jnp.zeros_like(target)


loss_head.defvjp(_loss_fwd, _loss_bwd)


ANY = pl.BlockSpec(memory_space=pl.ANY)


AG_ICI_PIECES = 4
AG_D2D_PIECES = 2
D2D_PIECES = 16
ICI_PIECES = 4


def _with_own_block(received, own, slot):
    return lax.dynamic_update_slice_in_dim(received, own[None], slot, axis=0)


def all_gather_chips(mine, name):
    R, C = mine.shape
    ni, nf = AG_ICI_PIECES, AG_D2D_PIECES
    rows = R // (2 * ni * nf)
    n_ici, n_all = 3 * ni, 3 * ni + 3 * ni * nf

    def body(in_ref, out_ref, send_sems, recv_sems):
        x, y, c = lax.axis_index("x"), lax.axis_index("y"), lax.axis_index("c")
        chips = [(1 - x, y), (x, 1 - y), (1 - x, 1 - y)]

        def copy(sem, src, dst, to):
            return pltpu.make_async_remote_copy(src_ref=src, dst_ref=dst, send_sem=send_sems.at[sem],
                                                recv_sem=recv_sems.at[sem], device_id=to, device_id_type=MESH)

        sends = []
        for k, (px, py) in enumerate(chips):
            for a in range(ni):
                sends.append(copy(k * ni + a, in_ref.at[c, a], out_ref.at[2 * x + y, c, a], (px, py, c)))
                sends[-1].start()
        for k, (px, py) in enumerate(chips):
            for a in range(ni):
                landed = out_ref.at[2 * px + py, c, a]
                copy(k * ni + a, in_ref.at[c, a], landed, (px, py, c)).wait_recv()
                for b in range(nf):
                    sends.append(copy(n_ici + (k * ni + a) * nf + b, landed.at[b], landed.at[b], (x, y, 1 - c)))
                    sends[-1].start()
        for k, (px, py) in enumerate(chips):
            for a in range(ni):
                for b in range(nf):
                    theirs = out_ref.at[2 * px + py, 1 - c, a, b]
                    copy(n_ici + (k * ni + a) * nf + b, theirs, theirs, (x, y, 1 - c)).wait_recv()
        for cp in sends:
            cp.wait_send()

    out = pl.pallas_call(
        body,
        out_shape=jax.ShapeDtypeStruct((N_CHIPS, 2, ni, nf, rows, C), mine.dtype),
        in_specs=[ANY],
        out_specs=ANY,
        scratch_shapes=[pltpu.SemaphoreType.DMA((n_all,)), pltpu.SemaphoreType.DMA((n_all,))],
        name=name,
    )(mine.reshape(2, ni, nf, rows, C))
    return _with_own_block(out.reshape(N_CHIPS, R, C), mine, 2 * lax.axis_index("x") + lax.axis_index("y"))


def swap_halves_with_sibling(parts):
    n_s, _, H, C = parts.shape
    n = ICI_PIECES

    def body(in_ref, out_ref, send_sems, recv_sems):
        x, y, c = lax.axis_index("x"), lax.axis_index("y"), lax.axis_index("c")

        def copy(s, b, half):
            return pltpu.make_async_remote_copy(
                src_ref=in_ref.at[s, half, b], dst_ref=out_ref.at[s, b], send_sem=send_sems.at[s * n + b],
                recv_sem=recv_sems.at[s * n + b], device_id=(x, y, 1 - c), device_id_type=MESH)

        sends = [copy(s, b, 1 - c) for s in range(n_s) for b in range(n)]
        for cp in sends:
            cp.start()
        for s in range(n_s):
            for b in range(n):
                copy(s, b, 1 - c).wait_recv()
        for cp in sends:
            cp.wait_send()

    out = pl.pallas_call(
        body,
        out_shape=jax.ShapeDtypeStruct((n_s, n, H // n, C), parts.dtype),
        in_specs=[ANY],
        out_specs=ANY,
        scratch_shapes=[pltpu.SemaphoreType.DMA((n_s * n,)), pltpu.SemaphoreType.DMA((n_s * n,))],
        name="grad_swap_halves",
    )(parts.reshape(n_s, 2, n, H // n, C))
    return out.reshape(n_s, H, C)


def add_pairs(parts, theirs, core):
    n_s, _, H, C = parts.shape
    tile = _div_tile(H, 512, 16)

    def body(c_ref, a_ref, b_ref, o_ref):
        o_ref[...] = (a_ref[...].astype(F32) + b_ref[...].astype(F32)).astype(o_ref.dtype)

    return pl.pallas_call(
        body,
        out_shape=jax.ShapeDtypeStruct((n_s, 1, H, C), parts.dtype),
        grid_spec=pltpu.PrefetchScalarGridSpec(
            num_scalar_prefetch=1,
            grid=(n_s, H // tile),
            in_specs=[pl.BlockSpec((1, 1, tile, C), lambda s, i, c_ref: (s, c_ref[0], i, 0)),
                      pl.BlockSpec((1, 1, tile, C), lambda s, i, c_ref: (s, 0, i, 0))],
            out_specs=pl.BlockSpec((1, 1, tile, C), lambda s, i, c_ref: (s, 0, i, 0)),
        ),
        compiler_params=_cparams(("parallel", "parallel")),
        name="grad_add_pairs",
    )(core.reshape(1).astype(jnp.int32), parts, theirs.reshape(n_s, 1, H, C)).reshape(n_s, H, C)


def scatter_to_chips(parts):
    n_s, H, C = parts.shape
    n = ICI_PIECES

    def body(in_ref, out_ref, send_sems, recv_sems):
        x, y, c = lax.axis_index("x"), lax.axis_index("y"), lax.axis_index("c")
        chips = [(1 - x, y), (x, 1 - y), (1 - x, 1 - y)]

        def copy(k, b, src_slot, dst_slot, to):
            return pltpu.make_async_remote_copy(
                src_ref=in_ref.at[src_slot, b], dst_ref=out_ref.at[dst_slot, b], send_sem=send_sems.at[k * n + b],
                recv_sem=recv_sems.at[k * n + b], device_id=to, device_id_type=MESH)

        sends = [copy(k, b, 2 * px + py, 2 * x + y, (px, py, c)) for k, (px, py) in enumerate(chips) for b in range(n)]
        for cp in sends:
            cp.start()
        for k, (px, py) in enumerate(chips):
            for b in range(n):
                copy(k, b, 2 * x + y, 2 * px + py, (px, py, c)).wait_recv()
        for cp in sends:
            cp.wait_send()

    out = pl.pallas_call(
        body,
        out_shape=jax.ShapeDtypeStruct((n_s, n, H // n, C), parts.dtype),
        in_specs=[ANY],
        out_specs=ANY,
        scratch_shapes=[pltpu.SemaphoreType.DMA((3 * n,)), pltpu.SemaphoreType.DMA((3 * n,))],
        name="grad_scatter",
    )(parts.reshape(n_s, n, H // n, C))
    slot = 2 * lax.axis_index("x") + lax.axis_index("y")
    return _with_own_block(out.reshape(n_s, H, C), lax.dynamic_index_in_dim(parts, slot, axis=0, keepdims=False), slot)


def exchange_with_sibling(mine):
    H, C = mine.shape
    n = D2D_PIECES

    def body(in_ref, out_ref, send_sems, recv_sems):
        x, y, c = lax.axis_index("x"), lax.axis_index("y"), lax.axis_index("c")

        def copy(b, half):
            return pltpu.make_async_remote_copy(
                src_ref=in_ref.at[b], dst_ref=out_ref.at[half, b], send_sem=send_sems.at[b], recv_sem=recv_sems.at[b],
                device_id=(x, y, 1 - c), device_id_type=MESH)

        sends = [copy(b, c) for b in range(n)]
        for cp in sends:
            cp.start()
        for b in range(n):
            copy(b, 1 - c).wait_recv()
        for cp in sends:
            cp.wait_send()

    out = pl.pallas_call(
        body,
        out_shape=jax.ShapeDtypeStruct((2, n, H // n, C), mine.dtype),
        in_specs=[ANY],
        out_specs=ANY,
        scratch_shapes=[pltpu.SemaphoreType.DMA((n,)), pltpu.SemaphoreType.DMA((n,))],
        name="grad_sibling_exchange",
    )(mine.reshape(n, H // n, C))
    return _with_own_block(out.reshape(2, H, C), mine, lax.axis_index("c"))


def sum_contributions(recv):
    n, H, C = recv.shape
    tile = _div_tile(H, 256, 16)

    def body(r_ref, o_ref):
        acc = r_ref[0].astype(F32)
        for j in range(1, n):
            acc = acc + r_ref[j].astype(F32)
        o_ref[...] = acc

    return pl.pallas_call(
        body,
        out_shape=jax.ShapeDtypeStruct((H, C), F32),
        grid=(H // tile,),
        in_specs=[pl.BlockSpec((n, tile, C), lambda i: (0, i, 0))],
        out_specs=pl.BlockSpec((tile, C), lambda i: (i, 0)),
        compiler_params=_cparams(("parallel",)),
        name="grad_sum",
    )(recv)


def adamw(w, g, m, v):
    shape = w.shape
    cols = shape[-1]
    rows = math.prod(shape[:-1])
    tile = _div_tile(rows, max(8, 256 * 1024 // cols // 8 * 8), 8)
    c1 = 1.0 - ADAM_B1 ** ADAM_STEP
    c2 = 1.0 - ADAM_B2 ** ADAM_STEP

    def body(w_ref, g_ref, m_ref, v_ref, d_ref, nm_ref, nv_ref):
        g_ = g_ref[...]
        nm = ADAM_B1 * m_ref[...] + (1.0 - ADAM_B1) * g_
        nv = ADAM_B2 * v_ref[...] + (1.0 - ADAM_B2) * (g_ * g_)
        d_ref[...] = -ADAM_LR * ((nm / c1) / (jnp.sqrt(nv / c2) + ADAM_EPS) + ADAM_WD * w_ref[...])
        nm_ref[...] = nm
        nv_ref[...] = nv

    spec = pl.BlockSpec((tile, cols), lambda i: (i, 0))
    outs = pl.pallas_call(
        body,
        out_shape=[jax.ShapeDtypeStruct((rows, cols), F32)] * 3,
        grid=(rows // tile,),
        in_specs=[spec] * 4,
        out_specs=[spec] * 3,
        compiler_params=_cparams(("parallel",)),
        name=f"adamw_{rows}x{cols}",
    )(*[t.reshape(rows, cols) for t in (w, g, m, v)])
    return [o.reshape(shape) for o in outs]


def _leaves(name, arr):
    n_lead = WEIGHT_LAYOUT[name][0]
    lead = arr.shape[:n_lead]
    flat = arr.reshape((-1,) + arr.shape[n_lead:])
    return [flat[i] for i in range(math.prod(lead))]


def _pad_pack_rows(buf):
    return jnp.pad(buf, ((0, -buf.shape[0] % PACK_ROW_MULT), (0, 0)))


def _pack_flat(pieces, dtype, cols):
    flat = jnp.concatenate([p.reshape(-1).astype(dtype) for p in pieces])
    flat = jnp.pad(flat, (0, -flat.shape[0] % (cols * PACK_ROW_MULT)))
    return flat.reshape(-1, cols)


def _unpack_flat(buf, shapes):
    lead = buf.shape[:-2]
    flat = buf.reshape(lead + (-1,))
    out, off = [], 0
    for shp in shapes:
        n = math.prod(shp)
        out.append(lax.slice_in_dim(flat, off, off + n, axis=len(lead)).reshape(lead + tuple(shp)))
        off += n
    return out


def _pack_slabs(slabs, dtype):
    return _pad_pack_rows(jnp.concatenate([s.astype(dtype) for s in slabs], axis=0))


def _local_slabs(name, shard):
    leaves = _leaves(name, shard)
    return [leaf.T for leaf in leaves] if name in SLAB_TRANSPOSED else leaves


def _chip_slice(leaf, ax, s):
    if ax is None:
        return leaf
    w = leaf.shape[ax] // N_CHIPS
    return lax.slice_in_dim(leaf, s * w, (s + 1) * w, axis=ax)


def gather_weights(shards, cols):
    slabs = [s for n in SLAB_NAMES for s in _local_slabs(n, shards[n])]
    slab_owner = [n for n in SLAB_NAMES for _ in _leaves(n, shards[n])]
    slab_buf = _pack_slabs(slabs, BF16)
    misc = {}
    for dtype in (BF16, F32):
        names = [n for n in MISC_NAMES if WEIGHT_LAYOUT[n][1] is not None and (n in F32_GATHER) == (dtype == F32)]
        pieces = [leaf for n in names for leaf in _leaves(n, shards[n])]
        owners = [n for n in names for _ in _leaves(n, shards[n])]
        buf = _pack_flat(pieces, dtype, cols)
        if dtype == BF16:
            gathered = all_gather_chips(jnp.concatenate([slab_buf, buf], axis=0), "weights_all_gather_bf16")
            slab_part, gathered = gathered[:, :slab_buf.shape[0]], gathered[:, slab_buf.shape[0]:]
        else:
            gathered = all_gather_chips(buf, "weights_all_gather_f32")
        for n, blk in zip(owners, _unpack_flat(gathered, [p.shape for p in pieces])):
            ax = WEIGHT_LAYOUT[n][1]
            misc.setdefault(n, []).append(jnp.concatenate([blk[s] for s in range(N_CHIPS)], axis=ax).astype(F32))
    for n in MISC_NAMES:
        if WEIGHT_LAYOUT[n][1] is None:
            misc[n] = [leaf.astype(F32) for leaf in _leaves(n, shards[n])]
    slab_full, off = {}, 0
    for n, s in zip(slab_owner, slabs):
        rows = s.shape[0]
        slab_full.setdefault(n, []).append(slab_part[:, off:off + rows].reshape(N_CHIPS * rows, cols))
        off += rows
    return slab_full, misc


def reduce_gradients(g_slab, g_misc, shards, cols):
    per_chip = []
    for s in range(N_CHIPS):
        slabs = []
        for n in SLAB_NAMES:
            for leaf in g_slab[n]:
                slabs.append(_chip_slice(leaf, 0, s))
        pieces = [_chip_slice(leaf, WEIGHT_LAYOUT[n][1], s) for n in MISC_NAMES for leaf in g_misc[n]]
        per_chip.append(jnp.concatenate([_pack_slabs(slabs, BF16), _pack_flat(pieces, BF16, cols)], axis=0))
    parts = jnp.stack(per_chip)
    R = parts.shape[1]
    parts = parts.reshape(N_CHIPS, 2, R // 2, cols)
    chip_parts = add_pairs(parts, swap_halves_with_sibling(parts), lax.axis_index("c"))
    recv = scatter_to_chips(chip_parts)
    red = exchange_with_sibling(sum_contributions(recv)).reshape(R, cols)

    out, off = {}, 0
    for n in SLAB_NAMES:
        leaves = []
        for slab in _local_slabs(n, shards[n]):
            rows = slab.shape[0]
            g = red[off:off + rows]
            leaves.append(g.T if n in SLAB_TRANSPOSED else g)
            off += rows
        out[n] = jnp.stack(leaves).reshape(shards[n].shape)
    off += -off % PACK_ROW_MULT
    misc_shapes = [shards[n].shape for n in MISC_NAMES]
    for n, g in zip(MISC_NAMES, _unpack_flat(red[off:], misc_shapes)):
        out[n] = g
    return out


def _silu(x):
    return x * jax.nn.sigmoid(x)


def _heads_first(t, heads):
    S = t.shape[0]
    return t.reshape(S, heads, -1).transpose(1, 0, 2)


def _heads_last(t):
    return t.transpose(1, 0, 2).reshape(t.shape[1], -1)


class SlabWeights:
    def __init__(self, values, deltas):
        self.values, self.deltas = values, deltas

    def matmul(self, a, name, leaf, out_dtype):
        return mm_slab(a, self.values[name][leaf], self.deltas[name][leaf], name in SLAB_TRANSPOSED, out_dtype)


@jax.custom_vjp
def swiglu_act(z):
    return _swiglu_fwd(z)[0]


def _swiglu_fwd(z):
    f = z.shape[1] // 2

    def fn(z_):
        return ((_silu(z_[:, :f].astype(F32)) * z_[:, f:].astype(F32)).astype(BF16),)

    (u,) = _rw_forward(fn, [z], [], "swiglu_act")
    return u, z


def _swiglu_bwd(z, du):
    f = z.shape[1] // 2

    def fn(z_, du_):
        a, b, g = z_[:, :f].astype(F32), z_[:, f:].astype(F32), du_.astype(F32)
        sg = jax.nn.sigmoid(a)
        da = g * b * (sg * (1.0 + a * (1.0 - sg)))
        db = g * (a * sg)
        return (jnp.concatenate([da, db], axis=1).astype(z_.dtype),)

    (dz,) = _rw_forward(fn, [z, du], [], "swiglu_act_bwd")
    return (dz,)


swiglu_act.defvjp(_swiglu_fwd, _swiglu_bwd)


def swiglu_half_step(x, h, slab, leaf, g_post, g_next):
    w_gu = jnp.concatenate([slab.values['ffn_w_gate'][leaf], slab.values['ffn_w_up'][leaf]], axis=0)
    d_gu = jnp.concatenate([slab.deltas['ffn_w_gate'][leaf], slab.deltas['ffn_w_up'][leaf]], axis=0)
    u = swiglu_act(mm_slab(h, w_gu, d_gu, True, BF16))
    y = slab.matmul(u, 'ffn_w_down', leaf, F32)
    return residual_norm(x, y, 0.5, g_post, g_next)


def residual_norm(x, y, coef, g_post, g_next):
    if g_next is None:
        (xn,) = rowwise(lambda x_, y_, g1: (x_ + coef * _rms(y_, g1),), [x, y], [g_post], "residual")
        return xn, None

    def fn(x_, y_, g1, g2):
        xn = x_ + coef * _rms(y_, g1)
        return xn, _rms(xn, g2).astype(BF16)

    return rowwise(fn, [x, y], [g_post, g_next], "residual_norm")


def memory_attention(q_mem, memn, w_kv):
    kv = mm(memn, w_kv, BF16)
    n_mem = kv.shape[0]
    kv = kv.reshape(n_mem, 2, MEM_HEADS, HEAD_DIM).transpose(1, 2, 0, 3)
    o = full_attention(_heads_first(q_mem.astype(BF16), MEM_HEADS), kv[0], kv[1], HEAD_DIM ** -0.5, False)
    return _heads_last(o)


def mla_mixer(h, cos, sin, w_in, q_g, kv_g, w_uq, w_ukv):
    S = h.shape[0]
    a_in = MLA_Q_LORA + MLA_KV_LORA + MLA_ROPE + MEM_WIDTH
    z = mm(h, jnp.pad(w_in, ((0, 0), (0, -a_in % LANES))), F32)
    o1, o2, o3 = MLA_Q_LORA, MLA_Q_LORA + MLA_KV_LORA, MLA_Q_LORA + MLA_KV_LORA + MLA_ROPE
    c_q, c_kv, k_r, q_mem = z[:, :o1], z[:, o1:o2], z[:, o2:o3], z[:, o3:a_in]
    cqn, ckvn = rowwise(lambda a, b, ga, gb: (_rms(a, ga).astype(BF16), _rms(b, gb).astype(BF16)),
                        [c_q, c_kv], [q_g, kv_g], "mla_lora_norm")
    q = mm(cqn, w_uq, F32).reshape(S, MLA_HEADS, MLA_NOPE + MLA_ROPE)
    kv = mm(ckvn, w_ukv, BF16).reshape(S, MLA_HEADS, MLA_NOPE + MLA_V)
    half = MLA_ROPE // 2
    qn = q[:, :, :MLA_NOPE].reshape(S, MLA_HEADS * MLA_NOPE)
    q1 = q[:, :, MLA_NOPE:MLA_NOPE + half].reshape(S, MLA_HEADS * half)
    q2 = q[:, :, MLA_NOPE + half:].reshape(S, MLA_HEADS * half)
    cos_h, sin_h = jnp.tile(cos, (1, MLA_HEADS)), jnp.tile(sin, (1, MLA_HEADS))
    scale = (MLA_NOPE + MLA_ROPE) ** -0.5
    qm = scale * LOG2E

    def rope(qn_, q1_, q2_, k1_, k2_, ch, sh, c1, s1):
        return ((qn_ * qm).astype(BF16), ((q1_ * ch - q2_ * sh) * qm).astype(BF16),
                ((q1_ * sh + q2_ * ch) * qm).astype(BF16),
                (k1_ * c1 - k2_ * s1).astype(BF16), (k1_ * s1 + k2_ * c1).astype(BF16))

    qns, qr1, qr2, kr1, kr2 = rowwise(rope, [qn, q1, q2, k_r[:, :half], k_r[:, half:], cos_h, sin_h, cos, sin], [],
                                      "mla_rope", n_const=4)
    q_cat = jnp.concatenate([qns.reshape(S, MLA_HEADS, MLA_NOPE), qr1.reshape(S, MLA_HEADS, half),
                             qr2.reshape(S, MLA_HEADS, half)], axis=-1)
    k_rope = jnp.concatenate([kr1, kr2], axis=-1)
    k_cat = jnp.concatenate([kv[:, :, :MLA_NOPE], jnp.broadcast_to(k_rope[:, None, :], (S, MLA_HEADS, MLA_ROPE))],
                            axis=-1)
    o = full_attention(q_cat.transpose(1, 0, 2), k_cat.transpose(1, 0, 2), kv[:, :, MLA_NOPE:].transpose(1, 0, 2),
                       scale, True)
    return _heads_last(o), q_mem


def dilated_mixer(h, slab):
    S = h.shape[0]
    n_g = len(DIL_GROUPS)
    qkv_w = n_g * 3 * DIL_HEADS * HEAD_DIM
    z = slab.matmul(h, 'b_w_in', 0, BF16)
    zd = z[:, :qkv_w].reshape(S, n_g, 3, DIL_HEADS, HEAD_DIM)
    q_mem = z[:, qkv_w:]
    slopes = (2.0 ** (-ALIBI_MAX * (jnp.arange(n_g * DIL_HEADS, dtype=F32) + 1.0) / (n_g * DIL_HEADS)))
    slopes = slopes.reshape(n_g, DIL_HEADS)
    outs, lses = [], []
    for g, (window, dil) in enumerate(DIL_GROUPS):
        L = S // dil

        def sub(t):
            return t.reshape(L, dil, DIL_HEADS, HEAD_DIM).transpose(1, 2, 0, 3).reshape(dil * DIL_HEADS, L, HEAD_DIM)

        slope = (jnp.tile(slopes[g], dil) * dil).reshape(dil * DIL_HEADS, 1, 1)
        o, lse = band_attention(sub(zd[:, g, 0]), sub(zd[:, g, 1]), sub(zd[:, g, 2]), slope, window // (2 * dil))
        outs.append(o.reshape(dil, DIL_HEADS, L, HEAD_DIM).transpose(2, 0, 1, 3).reshape(S * DIL_HEADS, HEAD_DIM))
        lses.append(lse.reshape(dil, DIL_HEADS, L).transpose(2, 0, 1).reshape(S * DIL_HEADS, 1))

    def merge(o0, o1, o2, l0, l1, l2):
        m = jnp.maximum(jnp.maximum(l0, l1), l2)
        e0, e1, e2 = jnp.exp(l0 - m), jnp.exp(l1 - m), jnp.exp(l2 - m)
        return (((e0 * o0 + e1 * o1 + e2 * o2) / (e0 + e1 + e2)).astype(BF16),)

    (o,) = rowwise(merge, outs + lses, [], "dilated_merge")
    return o.reshape(S, DIL_HEADS * HEAD_DIM), q_mem


def conformer_conv_mixer(h, slab, conv_w, conv_b, ln_g, ln_b):
    z = slab.matmul(h, 'c_w_in', 0, F32)
    a, gate, q_mem = z[:, :CONV_CH], z[:, CONV_CH:2 * CONV_CH], z[:, 2 * CONV_CH:]
    (u,) = rowwise(lambda a_, g_: (a_ * jax.nn.sigmoid(g_),), [a, gate], [], "conformer_glu")
    u = depthwise_conv(u, conv_w)

    def post(u_, b, g, beta):
        t = u_ + b
        mu = jnp.mean(t, axis=-1, keepdims=True)
        var = jnp.mean(jnp.square(t - mu), axis=-1, keepdims=True)
        return (_silu((t - mu) * lax.rsqrt(var + EPS) * g + beta).astype(BF16),)

    (o,) = rowwise(post, [u], [conv_b.reshape(1, -1), ln_g.reshape(1, -1), ln_b.reshape(1, -1)], "conformer_post")
    return o, q_mem


def short_conv_mixer(h, slab, conv_w):
    z = slab.matmul(h, 'd_w_in', 0, F32)
    bg, cg, hx, q_mem = z[:, :SC_CH], z[:, SC_CH:2 * SC_CH], z[:, 2 * SC_CH:3 * SC_CH], z[:, 3 * SC_CH:]
    (p,) = rowwise(lambda c_, h_: (c_ * h_,), [cg, hx], [], "shortconv_pre")
    cv = depthwise_conv(p, conv_w)
    (o,) = rowwise(lambda b_, c_: ((b_ * c_).astype(BF16),), [bg, cv], [], "shortconv_post")
    return o, q_mem


def local_loss(deltas, W, x, slab_values, mem, cos, sin, target):
    slab = SlabWeights(slab_values, deltas)
    norm_g = W['norm_g'][0]

    def gain(i, k):
        return norm_g[i, k].reshape(1, -1)

    (h,) = rowwise(lambda x_, g: (_rms(x_, g).astype(BF16),), [x], [gain(0, 0)], "input_norm")
    for i in range(DEPTH):
        x, h = swiglu_half_step(x, h, slab, 2 * i, gain(i, 1), gain(i, 2))
        if i == 0:
            o, q_mem = mla_mixer(h, cos, sin, W['a_w_in'][0], W['a_q_norm'][0].reshape(1, -1),
                                 W['a_kv_norm'][0].reshape(1, -1), W['a_w_uq'][0], W['a_w_ukv'][0])
            out_name = 'a_w_out'
        elif i == 1:
            o, q_mem = dilated_mixer(h, slab)
            out_name = None
        elif i == 2:
            o, q_mem = conformer_conv_mixer(h, slab, W['c_conv_w'][0], W['c_conv_b'][0], W['c_ln_g'][0],
                                            W['c_ln_b'][0])
            out_name = 'c_w_out'
        else:
            o, q_mem = short_conv_mixer(h, slab, W['d_conv_w'][0])
            out_name = 'd_w_out'
        (memn,) = rowwise(lambda m_, g: (_rms(m_, g).astype(BF16),), [mem], [gain(i, 6)], "memory_norm")
        mo = memory_attention(q_mem, memn, W['mem_w_kv'][i])
        cat = jnp.concatenate([o, mo], axis=-1)
        y = mm(cat, W['b_w_out'][0], F32) if out_name is None else slab.matmul(cat, out_name, 0, F32)
        x, h = residual_norm(x, y, 1.0, gain(i, 3), gain(i, 4))
        x, h = swiglu_half_step(x, h, slab, 2 * i + 1, gain(i, 5), gain(i + 1, 0) if i + 1 < DEPTH else None)
    return loss_head(x, target)


def kernel(x, mem, positions, norm_g, ffn_w_gate, ffn_w_up, ffn_w_down, mem_w_kv, a_w_in, a_q_norm, a_kv_norm, a_w_uq, a_w_ukv, a_w_out, b_w_in, b_w_out, c_w_in, c_conv_w, c_conv_b, c_ln_g, c_ln_b, c_w_out, d_w_in, d_conv_w, d_w_out, loss_target, m_norm_g, m_ffn_w_gate, m_ffn_w_up, m_ffn_w_down, m_mem_w_kv, m_a_w_in, m_a_q_norm, m_a_kv_norm, m_a_w_uq, m_a_w_ukv, m_a_w_out, m_b_w_in, m_b_w_out, m_c_w_in, m_c_conv_w, m_c_conv_b, m_c_ln_g, m_c_ln_b, m_c_w_out, m_d_w_in, m_d_conv_w, m_d_w_out, v_norm_g, v_ffn_w_gate, v_ffn_w_up, v_ffn_w_down, v_mem_w_kv, v_a_w_in, v_a_q_norm, v_a_kv_norm, v_a_w_uq, v_a_w_ukv, v_a_w_out, v_b_w_in, v_b_w_out, v_c_w_in, v_c_conv_w, v_c_conv_b, v_c_ln_g, v_c_ln_b, v_c_w_out, v_d_w_in, v_d_conv_w, v_d_w_out):
    w_in = dict(zip(WEIGHT_NAMES, (norm_g, ffn_w_gate, ffn_w_up, ffn_w_down, mem_w_kv, a_w_in, a_q_norm, a_kv_norm, a_w_uq, a_w_ukv, a_w_out, b_w_in, b_w_out, c_w_in, c_conv_w, c_conv_b, c_ln_g, c_ln_b, c_w_out, d_w_in, d_conv_w, d_w_out)))
    m_in = dict(zip(WEIGHT_NAMES, (m_norm_g, m_ffn_w_gate, m_ffn_w_up, m_ffn_w_down, m_mem_w_kv, m_a_w_in, m_a_q_norm, m_a_kv_norm, m_a_w_uq, m_a_w_ukv, m_a_w_out, m_b_w_in, m_b_w_out, m_c_w_in, m_c_conv_w, m_c_conv_b, m_c_ln_g, m_c_ln_b, m_c_w_out, m_d_w_in, m_d_conv_w, m_d_w_out)))
    v_in = dict(zip(WEIGHT_NAMES, (v_norm_g, v_ffn_w_gate, v_ffn_w_up, v_ffn_w_down, v_mem_w_kv, v_a_w_in, v_a_q_norm, v_a_kv_norm, v_a_w_uq, v_a_w_ukv, v_a_w_out, v_b_w_in, v_b_w_out, v_c_w_in, v_c_conv_w, v_c_conv_b, v_c_ln_g, v_c_ln_b, v_c_w_out, v_d_w_in, v_d_conv_w, v_d_w_out)))

    d_model = x.shape[-1]
    slab_values, W = gather_weights(w_in, d_model)
    deltas = {n: [jnp.zeros(leaf.shape, BF16) for leaf in leaves] for n, leaves in slab_values.items()}
    half = MLA_ROPE // 2
    inv = ROPE_THETA ** (-jnp.arange(half, dtype=F32) / half)
    ang = positions[0].astype(F32)[:, None] * inv
    loss, (g_slab, g_misc, gx) = jax.value_and_grad(local_loss, argnums=(0, 1, 2))(
        deltas, W, x[0], slab_values, mem[0], jnp.cos(ang), jnp.sin(ang), loss_target[0])
    loss = lax.psum(loss, ("x", "y", "c"))

    grads = reduce_gradients(g_slab, g_misc, w_in, d_model)
    steps = {n: adamw(w_in[n], grads[n], m_in[n], v_in[n]) for n in WEIGHT_NAMES}
    return (loss, gx[None], *[grads[n] for n in WEIGHT_NAMES], *[steps[n][0] for n in WEIGHT_NAMES],
            *[steps[n][1] for n in WEIGHT_NAMES], *[steps[n][2] for n in WEIGHT_NAMES])
```

```python
import functools
import math

import jax
import jax.numpy as jnp
from jax import lax
from jax.experimental import pallas as pl
from jax.experimental.pallas import tpu as pltpu

F32 = jnp.float32
BF16 = jnp.bfloat16
MESH = pl.DeviceIdType.MESH

VMEM_LIMIT_BYTES = 56 * 1024 * 1024
LANES = 128
PACK_ROW_MULT = 512
ROW_TILE_ELEMS = 768 * 1024

HEAD_DIM = 64
MEM_HEADS = 4
MEM_WIDTH = MEM_HEADS * HEAD_DIM
MLA_HEADS = 12
MLA_Q_LORA = 384
MLA_KV_LORA = 256
MLA_NOPE = 64
MLA_ROPE = 32
MLA_V = 64
ROPE_THETA = 10000.0
DIL_GROUPS = ((128, 1), (512, 4), (2048, 16))
DIL_HEADS = 8
ALIBI_MAX = 8.0
CONV_CH = 768
CONV_WIDTH = 31
SC_CH = 768
SC_WIDTH = 3
CONV_PAD = 16
CONV_CHUNK = 256
EPS = 1e-6
NEG = -1e30
DEPTH = 4
LOG2E = 1.4426950408889634
LN2 = 0.6931471805599453

ADAM_LR = 0.001
ADAM_B1 = 0.9
ADAM_B2 = 0.999
ADAM_EPS = 1e-08
ADAM_WD = 0.01
ADAM_STEP = 10

WEIGHT_NAMES = ['norm_g', 'ffn_w_gate', 'ffn_w_up', 'ffn_w_down', 'mem_w_kv', 'a_w_in', 'a_q_norm', 'a_kv_norm',
                'a_w_uq', 'a_w_ukv', 'a_w_out', 'b_w_in', 'b_w_out', 'c_w_in', 'c_conv_w', 'c_conv_b', 'c_ln_g',
                'c_ln_b', 'c_w_out', 'd_w_in', 'd_conv_w', 'd_w_out']
WEIGHT_LAYOUT = {
    'norm_g': (0, 2), 'ffn_w_gate': (2, 1), 'ffn_w_up': (2, 1), 'ffn_w_down': (2, 0), 'mem_w_kv': (1, 0),
    'a_w_in': (1, 0), 'a_q_norm': (1, None), 'a_kv_norm': (1, None), 'a_w_uq': (1, 1), 'a_w_ukv': (1, 1),
    'a_w_out': (1, 0), 'b_w_in': (1, 1), 'b_w_out': (1, 1), 'c_w_in': (1, 1), 'c_conv_w': (1, 1),
    'c_conv_b': (1, 0), 'c_ln_g': (1, 0), 'c_ln_b': (1, 0), 'c_w_out': (1, 0), 'd_w_in': (1, 1),
    'd_conv_w': (1, 1), 'd_w_out': (1, 0),
}
SLAB_TRANSPOSED = ('ffn_w_gate', 'ffn_w_up', 'b_w_in', 'c_w_in', 'd_w_in')
SLAB_ROWS = ('ffn_w_down', 'a_w_out', 'c_w_out', 'd_w_out')
SLAB_NAMES = [n for n in WEIGHT_NAMES if n in SLAB_TRANSPOSED or n in SLAB_ROWS]
MISC_NAMES = [n for n in WEIGHT_NAMES if n not in SLAB_NAMES]
F32_GATHER = ('norm_g', 'c_conv_w', 'c_conv_b', 'c_ln_g', 'c_ln_b', 'd_conv_w')
N_CHIPS = 4


def _cparams(semantics):
    return pltpu.CompilerParams(dimension_semantics=semantics, vmem_limit_bytes=VMEM_LIMIT_BYTES)


def _div_tile(n, cap, mult):
    if n <= cap:
        return n
    for d in range(cap - cap % mult, 0, -mult):
        if n % d == 0:
            return d
    raise ValueError(f"no tile for {n} (cap {cap}, multiple of {mult})")


def _matmul(a, b, mode, out_dtype):
    if mode == 'nn':
        (M, K), N = a.shape, b.shape[1]
    elif mode == 'nt':
        (M, K), N = a.shape, b.shape[0]
    else:
        (K, M), N = a.shape, b.shape[1]
    if mode == 'tn':
        tm, tn, tk = _div_tile(M, 2816, LANES), _div_tile(N, 1536, LANES), _div_tile(K, 1024, 16)
    else:
        tm, tn, tk = _div_tile(M, 512, 16), _div_tile(N, 2816, LANES), _div_tile(K, 2816, LANES)
    nk = K // tk
    if mode == 'nn':
        a_spec = pl.BlockSpec((tm, tk), lambda i, j, k: (i, k))
        b_spec = pl.BlockSpec((tk, tn), lambda i, j, k: (k, j))
        dims = (((1,), (0,)), ((), ()))
    elif mode == 'nt':
        a_spec = pl.BlockSpec((tm, tk), lambda i, j, k: (i, k))
        b_spec = pl.BlockSpec((tn, tk), lambda i, j, k: (j, k))
        dims = (((1,), (1,)), ((), ()))
    else:
        a_spec = pl.BlockSpec((tk, tm), lambda i, j, k: (k, i))
        b_spec = pl.BlockSpec((tk, tn), lambda i, j, k: (k, j))
        dims = (((0,), (0,)), ((), ()))

    def body(a_ref, b_ref, o_ref, *acc):
        r = lax.dot_general(a_ref[...].astype(BF16), b_ref[...].astype(BF16), dims, preferred_element_type=F32)
        if nk == 1:
            o_ref[...] = r.astype(o_ref.dtype)
        else:
            k = pl.program_id(2)

            @pl.when(k == 0)
            def _():
                acc[0][...] = r

            @pl.when(k > 0)
            def _():
                acc[0][...] += r

            @pl.when(k == nk - 1)
            def _():
                o_ref[...] = acc[0][...].astype(o_ref.dtype)

    return pl.pallas_call(
        body,
        out_shape=jax.ShapeDtypeStruct((M, N), out_dtype),
        grid=(M // tm, N // tn, nk),
        in_specs=[a_spec, b_spec],
        out_specs=pl.BlockSpec((tm, tn), lambda i, j, k: (i, j)),
        scratch_shapes=[pltpu.VMEM((tm, tn), F32)] if nk > 1 else [],
        compiler_params=_cparams(("parallel", "parallel", "arbitrary")),
        name=f"mm_{mode}_{M}x{K}x{N}",
    )(a, b)


@functools.partial(jax.custom_vjp, nondiff_argnums=(2,))
def mm(a, w, out_dtype):
    return _matmul(a, w.astype(BF16), 'nn', out_dtype)


def _mm_fwd(a, w, out_dtype):
    wb = w.astype(BF16)
    return _matmul(a, wb, 'nn', out_dtype), (a, wb)


def _mm_bwd(out_dtype, res, g):
    a, wb = res
    return _matmul(g, wb, 'nt', a.dtype), _matmul(a, g, 'tn', F32)


mm.defvjp(_mm_fwd, _mm_bwd)


@functools.partial(jax.custom_vjp, nondiff_argnums=(3, 4))
def mm_slab(a, wb, delta, transposed, out_dtype):
    return _matmul(a, wb, 'nt' if transposed else 'nn', out_dtype)


def _mm_slab_fwd(a, wb, delta, transposed, out_dtype):
    return _matmul(a, wb, 'nt' if transposed else 'nn', out_dtype), (a, wb)


def _mm_slab_bwd(transposed, out_dtype, res, g):
    a, wb = res
    if transposed:
        return _matmul(g, wb, 'nn', a.dtype), jnp.zeros_like(wb), _matmul(g, a, 'tn', BF16)
    return _matmul(g, wb, 'nt', a.dtype), jnp.zeros_like(wb), _matmul(a, g, 'tn', BF16)


mm_slab.defvjp(_mm_slab_fwd, _mm_slab_bwd)


def _row_tile(n_rows, widest):
    t = 1 << max(3, int(math.log2(max(8, ROW_TILE_ELEMS // max(widest, 1)))))
    return min(n_rows, min(t, 2048))


def _rw_forward(fn, rows, params, name, tile=None):
    n_rows = rows[0].shape[0]
    tile = tile or _row_tile(n_rows, max(r.shape[1] for r in rows))
    outs = jax.eval_shape(fn, *[jax.ShapeDtypeStruct((tile, r.shape[1]), r.dtype) for r in rows],
                          *[jax.ShapeDtypeStruct(p.shape, p.dtype) for p in params])
    n_in = len(rows) + len(params)

    def body(*refs):
        res = fn(*[r[...] for r in refs[:n_in]])
        for o_ref, o in zip(refs[n_in:], res):
            o_ref[...] = o

    return pl.pallas_call(
        body,
        out_shape=[jax.ShapeDtypeStruct((n_rows, o.shape[1]), o.dtype) for o in outs],
        grid=(n_rows // tile,),
        in_specs=[pl.BlockSpec((tile, r.shape[1]), lambda i: (i, 0)) for r in rows]
        + [pl.BlockSpec(p.shape, lambda i: (0, 0)) for p in params],
        out_specs=[pl.BlockSpec((tile, o.shape[1]), lambda i: (i, 0)) for o in outs],
        compiler_params=_cparams(("parallel",)),
        name=name,
    )(*rows, *params)


def _rw_backward(fn, rows, params, cts, n_const, name):
    n_rows = rows[0].shape[0]
    tile = _row_tile(n_rows, max([r.shape[1] for r in rows] + [c.shape[1] for c in cts]))
    n_r, n_p, n_c = len(rows), len(params), len(cts)
    n_diff = n_r - n_const

    def body(*refs):
        row_vals = [r[...] for r in refs[:n_r]]
        par_vals = [r[...] for r in refs[n_r:n_r + n_p]]
        ct_vals = tuple(r[...] for r in refs[n_r + n_p:n_r + n_p + n_c])
        out_refs = refs[n_r + n_p + n_c:]

        def f(*diff):
            return fn(*diff[:n_diff], *row_vals[n_diff:], *diff[n_diff:])

        _, vjp = jax.vjp(f, *row_vals[:n_diff], *par_vals)
        grads = vjp(ct_vals)
        for o_ref, g in zip(out_refs[:n_diff], grads[:n_diff]):
            o_ref[...] = g
        i = pl.program_id(0)
        for o_ref, g in zip(out_refs[n_diff:], grads[n_diff:]):
            @pl.when(i == 0)
            def _(o_ref=o_ref, g=g):
                o_ref[...] = g

            @pl.when(i > 0)
            def _(o_ref=o_ref, g=g):
                o_ref[...] += g

    res = pl.pallas_call(
        body,
        out_shape=[jax.ShapeDtypeStruct(r.shape, r.dtype) for r in rows[:n_diff]]
        + [jax.ShapeDtypeStruct(p.shape, p.dtype) for p in params],
        grid=(n_rows // tile,),
        in_specs=[pl.BlockSpec((tile, r.shape[1]), lambda i: (i, 0)) for r in rows]
        + [pl.BlockSpec(p.shape, lambda i: (0, 0)) for p in params]
        + [pl.BlockSpec((tile, c.shape[1]), lambda i: (i, 0)) for c in cts],
        out_specs=[pl.BlockSpec((tile, r.shape[1]), lambda i: (i, 0)) for r in rows[:n_diff]]
        + [pl.BlockSpec(p.shape, lambda i: (0, 0)) for p in params],
        compiler_params=_cparams(("arbitrary",)),
        name=name + "_bwd",
    )(*rows, *params, *cts)
    return list(res[:n_diff]), list(res[n_diff:])


def rowwise(fn, rows, params, name, n_const=0):
    rows, params = list(rows), list(params)

    @jax.custom_vjp
    def op(rows, params):
        return tuple(_rw_forward(fn, rows, params, name))

    def op_fwd(rows, params):
        return tuple(_rw_forward(fn, rows, params, name)), (rows, params)

    def op_bwd(res, cts):
        rows, params = res
        d_rows, d_params = _rw_backward(fn, rows, params, list(cts), n_const, name)
        d_rows = d_rows + [jnp.zeros_like(r) for r in rows[len(rows) - n_const:]]
        return d_rows, d_params

    op.defvjp(op_fwd, op_bwd)
    return op(rows, params)


def _rms(x, g):
    xf = x.astype(F32)
    return xf * lax.rsqrt(jnp.mean(xf * xf, axis=-1, keepdims=True) + EPS) * g


ATTN_TILE = 1024
ATTN_SPAN = 2048
ATTN_CHUNK = 512
DELTA_TILE = 8192

NT_DIMS = (((1,), (1,)), ((), ()))


def _attn_tiles(stay, stream):
    span = min(ATTN_SPAN, stream)
    return min(ATTN_TILE, stay), span, min(ATTN_CHUNK, span)


def _flash_fwd(q, k, v, qk_mult):
    N, Lq, dk = q.shape
    Lk, dv = k.shape[1], v.shape[2]
    tq, span, ch = _attn_tiles(Lq, Lk)
    tq = min(Lq, tq * max(2, ATTN_TILE // span))
    nk = Lk // span
    v_ones = jnp.concatenate([v, jnp.ones((N, Lk, 1), v.dtype), jnp.zeros((N, Lk, LANES - dv - 1), v.dtype)], axis=2)

    def body(q_ref, k_ref, v_ref, o_ref, lse_ref, m_s, acc_s):
        j = pl.program_id(2)

        @pl.when(j == 0)
        def _():
            m_s[...] = jnp.full_like(m_s, NEG)
            acc_s[...] = jnp.zeros_like(acc_s)

        qv = q_ref[0]
        s = []
        for c in range(span // ch):
            sc = lax.dot_general(qv, k_ref[0, c * ch:(c + 1) * ch, :], NT_DIMS, preferred_element_type=F32)
            s.append(sc if qk_mult == 1.0 else sc * qk_mult)
        s_max = functools.reduce(jnp.maximum, s)
        m_new = jnp.maximum(m_s[...], jnp.max(s_max, axis=-1, keepdims=True))
        pv = None
        for c, sc in enumerate(s):
            p = jnp.exp2(sc - m_new).astype(BF16)
            d = jnp.dot(p, v_ref[0, c * ch:(c + 1) * ch, :], preferred_element_type=F32)
            pv = d if c == 0 else pv + d
        acc_s[...] = jnp.exp2(m_s[...] - m_new) * acc_s[...] + pv
        m_s[...] = m_new

        @pl.when(j == nk - 1)
        def _():
            acc = acc_s[...]
            l = acc[:, dv:dv + 1]
            o_ref[0] = (acc[:, :dv] / l).astype(o_ref.dtype)
            lse_ref[0] = m_s[...] + jnp.log(l) * LOG2E

    return pl.pallas_call(
        body,
        out_shape=[jax.ShapeDtypeStruct((N, Lq, dv), BF16), jax.ShapeDtypeStruct((N, Lq, 1), F32)],
        grid=(N, Lq // tq, nk),
        in_specs=[pl.BlockSpec((1, tq, dk), lambda n, i, j: (n, i, 0)),
                  pl.BlockSpec((1, span, dk), lambda n, i, j: (n, j, 0)),
                  pl.BlockSpec((1, span, LANES), lambda n, i, j: (n, j, 0))],
        out_specs=[pl.BlockSpec((1, tq, dv), lambda n, i, j: (n, i, 0)),
                   pl.BlockSpec((1, tq, 1), lambda n, i, j: (n, i, 0))],
        scratch_shapes=[pltpu.VMEM((tq, 1), F32), pltpu.VMEM((tq, LANES), F32)],
        compiler_params=_cparams(("parallel", "parallel", "arbitrary")),
        name=f"flash_fwd_{N}x{Lq}x{Lk}x{dk}",
    )(q, k, v_ones)


def _flash_bwd(q, k, v, do, lse_row, dd_row, qk_mult, out_mult):
    N, Lq, dk = q.shape
    Lk, dv = k.shape[1], v.shape[2]
    tk, span, ch = _attn_tiles(Lk, Lq)
    tk = min(Lk, 2 * tk)
    nq, nkt = Lq // span, Lk // tk

    def body(q_ref, k_ref, v_ref, do_ref, lse_ref, dd_ref, dq_ref, dk_ref, dv_ref, dq_s, dk_s, dv_s):
        i, j = pl.program_id(1), pl.program_id(2)

        @pl.when((i == 0) & (j == 0))
        def _():
            dq_s[...] = jnp.zeros_like(dq_s)

        @pl.when(j == 0)
        def _():
            dk_s[...] = jnp.zeros_like(dk_s)
            dv_s[...] = jnp.zeros_like(dv_s)

        kv_, vv = k_ref[0], v_ref[0]
        for c in range(span // ch):
            qc = q_ref[0, c * ch:(c + 1) * ch, :]
            doc = do_ref[0, c * ch:(c + 1) * ch, :]
            st = lax.dot_general(kv_, qc, NT_DIMS, preferred_element_type=F32)
            if qk_mult != 1.0:
                st = st * qk_mult
            pt = jnp.exp2(st - lse_ref[0, :, c * ch:(c + 1) * ch])
            dv_s[...] += jnp.dot(pt.astype(BF16), doc, preferred_element_type=F32)
            dpt = lax.dot_general(vv, doc, NT_DIMS, preferred_element_type=F32)
            dst = (pt * (dpt - dd_ref[0, :, c * ch:(c + 1) * ch])).astype(BF16)
            dk_s[...] += jnp.dot(dst, qc, preferred_element_type=F32)
            rows = pl.ds(pl.multiple_of(j * span + c * ch, ch), ch)
            dq_s[rows, :] += lax.dot_general(dst, kv_, (((0,), (0,)), ((), ())), preferred_element_type=F32)

        @pl.when(j == nq - 1)
        def _():
            dk_ref[0] = (dk_s[...] * out_mult).astype(dk_ref.dtype)
            dv_ref[0] = dv_s[...].astype(dv_ref.dtype)

        @pl.when((i == nkt - 1) & (j == nq - 1))
        def _():
            dq_ref[0] = (dq_s[...] * out_mult).astype(dq_ref.dtype)

    return pl.pallas_call(
        body,
        out_shape=[jax.ShapeDtypeStruct(q.shape, q.dtype), jax.ShapeDtypeStruct(k.shape, k.dtype),
                   jax.ShapeDtypeStruct(v.shape, v.dtype)],
        grid=(N, nkt, nq),
        in_specs=[pl.BlockSpec((1, span, dk), lambda n, i, j: (n, j, 0)),
                  pl.BlockSpec((1, tk, dk), lambda n, i, j: (n, i, 0)),
                  pl.BlockSpec((1, tk, dv), lambda n, i, j: (n, i, 0)),
                  pl.BlockSpec((1, span, dv), lambda n, i, j: (n, j, 0)),
                  pl.BlockSpec((1, 1, span), lambda n, i, j: (n, 0, j)),
                  pl.BlockSpec((1, 1, span), lambda n, i, j: (n, 0, j))],
        out_specs=[pl.BlockSpec((1, Lq, dk), lambda n, i, j: (n, 0, 0)),
                   pl.BlockSpec((1, tk, dk), lambda n, i, j: (n, i, 0)),
                   pl.BlockSpec((1, tk, dv), lambda n, i, j: (n, i, 0))],
        scratch_shapes=[pltpu.VMEM((Lq, dk), F32), pltpu.VMEM((tk, dk), F32), pltpu.VMEM((tk, dv), F32)],
        compiler_params=_cparams(("parallel", "arbitrary", "arbitrary")),
        name=f"flash_bwd_{N}x{Lq}x{Lk}x{dk}",
    )(q, k, v, do, lse_row, dd_row)


def _attn_delta(do, o, dlse):
    N, L, dv = o.shape

    def fn(do_t, o_t, dl_t):
        return (jnp.sum(do_t.astype(F32) * o_t.astype(F32), axis=-1, keepdims=True) - dl_t,)

    (dd,) = _rw_forward(fn, [do.reshape(N * L, dv), o.reshape(N * L, dv), dlse.reshape(N * L, 1)], [], "attn_delta",
                        tile=_div_tile(N * L, DELTA_TILE, 8))
    return dd.reshape(N, L, 1)


def full_attention(q, k, v, scale, q_prescaled):
    qk_mult = 1.0 if q_prescaled else scale * LOG2E
    grad_mult = LN2 if q_prescaled else scale

    @jax.custom_vjp
    def op(q, k, v):
        return _flash_fwd(q, k, v, qk_mult)[0]

    def op_fwd(q, k, v):
        o, lse = _flash_fwd(q, k, v, qk_mult)
        return o, (q, k, v, o, lse)

    def op_bwd(res, do):
        q, k, v, o, lse = res
        N, Lq, _ = q.shape
        dd = _attn_delta(do, o, jnp.zeros_like(lse))
        return tuple(_flash_bwd(q, k, v, do, lse.reshape(N, 1, Lq), dd.reshape(N, 1, Lq), qk_mult, grad_mult))

    op.defvjp(op_fwd, op_bwd)
    return op(q, k, v)


BAND_TILE = 128
BAND_BATCH = 8


def _band_specs(L, d):
    t = min(BAND_TILE, L)
    nt = L // t
    cur = lambda n, i: (n, i, 0)
    prev = lambda n, i: (n, jnp.maximum(i - 1, 0), 0)
    nxt = lambda n, i: (n, jnp.minimum(i + 1, nt - 1), 0)
    return t, nt, cur, prev, nxt


def _band_mask(i, t, L, half):
    a = lax.broadcasted_iota(jnp.int32, (t, 3 * t), 0)
    b = lax.broadcasted_iota(jnp.int32, (t, 3 * t), 1)
    dist = jnp.abs(b - t - a)
    other = i * t - t + b
    mask = (dist <= half) & (other >= 0) & (other < L)
    return dist.astype(F32), mask


def _band_fwd(q, k, v, slope, half):
    N, L, d = q.shape
    t, nt, cur, prev, nxt = _band_specs(L, d)
    nb = BAND_BATCH
    scale = d ** -0.5

    def body(sl_ref, q_ref, kp, kc, kn, vp, vc, vn, o_ref, lse_ref):
        i = pl.program_id(1)
        kcat = jnp.concatenate([kp[...], kc[...], kn[...]], axis=1)
        vcat = jnp.concatenate([vp[...], vc[...], vn[...]], axis=1)
        s = lax.dot_general(q_ref[...], kcat, (((2,), (2,)), ((0,), (0,))), preferred_element_type=F32) * scale
        dist, mask = _band_mask(i, t, L, half)
        s = jnp.where(mask[None], s - sl_ref[...] * dist[None], NEG)
        m = jnp.max(s, axis=-1, keepdims=True)
        p = jnp.exp(s - m)
        l = jnp.sum(p, axis=-1, keepdims=True)
        o = lax.dot_general(p.astype(BF16), vcat, (((2,), (1,)), ((0,), (0,))), preferred_element_type=F32)
        o_ref[...] = o / l
        lse_ref[...] = m + jnp.log(l)

    blk = lambda f: pl.BlockSpec((nb, t, d), f)
    return pl.pallas_call(
        body,
        out_shape=[jax.ShapeDtypeStruct((N, L, d), F32), jax.ShapeDtypeStruct((N, L, 1), F32)],
        grid=(N // nb, nt),
        in_specs=[pl.BlockSpec((nb, 1, 1), lambda n, i: (n, 0, 0)), blk(cur),
                  blk(prev), blk(cur), blk(nxt), blk(prev), blk(cur), blk(nxt)],
        out_specs=[blk(cur), pl.BlockSpec((nb, t, 1), cur)],
        compiler_params=_cparams(("parallel", "parallel")),
        name=f"band_fwd_{N}x{L}",
    )(slope, q, k, k, k, v, v, v)


def _band_dq(q, k, v, slope, do, lse, dd, half):
    N, L, d = q.shape
    t, nt, cur, prev, nxt = _band_specs(L, d)
    nb = BAND_BATCH
    scale = d ** -0.5

    def body(sl_ref, q_ref, kp, kc, kn, vp, vc, vn, do_ref, lse_ref, dd_ref, dq_ref):
        i = pl.program_id(1)
        kcat = jnp.concatenate([kp[...], kc[...], kn[...]], axis=1)
        vcat = jnp.concatenate([vp[...], vc[...], vn[...]], axis=1)
        s = lax.dot_general(q_ref[...], kcat, (((2,), (2,)), ((0,), (0,))), preferred_element_type=F32) * scale
        dist, mask = _band_mask(i, t, L, half)
        p = jnp.where(mask[None], jnp.exp(s - sl_ref[...] * dist[None] - lse_ref[...]), 0.0)
        dp = lax.dot_general(do_ref[...].astype(BF16), vcat, (((2,), (2,)), ((0,), (0,))),
                             preferred_element_type=F32)
        ds = (p * (dp - dd_ref[...])).astype(BF16)
        dq = lax.dot_general(ds, kcat, (((2,), (1,)), ((0,), (0,))), preferred_element_type=F32)
        dq_ref[...] = (dq * scale).astype(dq_ref.dtype)

    blk = lambda f: pl.BlockSpec((nb, t, d), f)
    col = pl.BlockSpec((nb, t, 1), cur)
    return pl.pallas_call(
        body,
        out_shape=jax.ShapeDtypeStruct(q.shape, q.dtype),
        grid=(N // nb, nt),
        in_specs=[pl.BlockSpec((nb, 1, 1), lambda n, i: (n, 0, 0)), blk(cur),
                  blk(prev), blk(cur), blk(nxt), blk(prev), blk(cur), blk(nxt), blk(cur), col, col],
        out_specs=blk(cur),
        compiler_params=_cparams(("parallel", "parallel")),
        name=f"band_dq_{N}x{L}",
    )(slope, q, k, k, k, v, v, v, do, lse, dd)


def _band_dkv(q, k, v, slope, do, lse_row, dd_row, half):
    N, L, d = q.shape
    t, nt, cur, prev, nxt = _band_specs(L, d)
    nb = BAND_BATCH
    scale = d ** -0.5
    rcur = lambda n, i: (n, 0, i)
    rprev = lambda n, i: (n, 0, jnp.maximum(i - 1, 0))
    rnxt = lambda n, i: (n, 0, jnp.minimum(i + 1, nt - 1))

    def body(sl_ref, k_ref, v_ref, qp, qc, qn, dop, doc, don, lp, lc, ln, dp_, dc_, dn_, dk_ref, dv_ref):
        i = pl.program_id(1)
        qcat = jnp.concatenate([qp[...], qc[...], qn[...]], axis=1)
        docat = jnp.concatenate([dop[...], doc[...], don[...]], axis=1).astype(BF16)
        lrow = jnp.concatenate([lp[...], lc[...], ln[...]], axis=2)
        drow = jnp.concatenate([dp_[...], dc_[...], dn_[...]], axis=2)
        st = lax.dot_general(k_ref[...], qcat, (((2,), (2,)), ((0,), (0,))), preferred_element_type=F32) * scale
        dist, mask = _band_mask(i, t, L, half)
        pt = jnp.where(mask[None], jnp.exp(st - sl_ref[...] * dist[None] - lrow), 0.0)
        dv = lax.dot_general(pt.astype(BF16), docat, (((2,), (1,)), ((0,), (0,))), preferred_element_type=F32)
        dpt = lax.dot_general(v_ref[...], docat, (((2,), (2,)), ((0,), (0,))), preferred_element_type=F32)
        dst = (pt * (dpt - drow)).astype(BF16)
        dk = lax.dot_general(dst, qcat, (((2,), (1,)), ((0,), (0,))), preferred_element_type=F32)
        dk_ref[...] = (dk * scale).astype(dk_ref.dtype)
        dv_ref[...] = dv.astype(dv_ref.dtype)

    blk = lambda f: pl.BlockSpec((nb, t, d), f)
    row = lambda f: pl.BlockSpec((nb, 1, t), f)
    return pl.pallas_call(
        body,
        out_shape=[jax.ShapeDtypeStruct(k.shape, k.dtype), jax.ShapeDtypeStruct(v.shape, v.dtype)],
        grid=(N // nb, nt),
        in_specs=[pl.BlockSpec((nb, 1, 1), lambda n, i: (n, 0, 0)), blk(cur), blk(cur),
                  blk(prev), blk(cur), blk(nxt), blk(prev), blk(cur), blk(nxt),
                  row(rprev), row(rcur), row(rnxt), row(rprev), row(rcur), row(rnxt)],
        out_specs=[blk(cur), blk(cur)],
        compiler_params=_cparams(("parallel", "parallel")),
        name=f"band_dkv_{N}x{L}",
    )(slope, k, v, q, q, q, do, do, do, lse_row, lse_row, lse_row, dd_row, dd_row, dd_row)


def band_attention(q, k, v, slope, half):
    @jax.custom_vjp
    def op(q, k, v, slope):
        return tuple(_band_fwd(q, k, v, slope, half))

    def op_fwd(q, k, v, slope):
        o, lse = _band_fwd(q, k, v, slope, half)
        return (o, lse), (q, k, v, slope, o, lse)

    def op_bwd(res, cts):
        q, k, v, slope, o, lse = res
        do, dlse = cts
        N, L, _ = q.shape
        dd = _attn_delta(do, o, dlse)
        dq = _band_dq(q, k, v, slope, do, lse, dd, half)
        dk, dv = _band_dkv(q, k, v, slope, do, lse.reshape(N, 1, L), dd.reshape(N, 1, L), half)
        return dq, dk, dv, jnp.zeros_like(slope)

    op.defvjp(op_fwd, op_bwd)
    return op(q, k, v, slope)


def _conv_taps(width):
    return [CONV_PAD - width // 2 + j for j in range(width)]


def _dwconv_call(up, w, width):
    S, C = up.shape[0] - 2 * CONV_PAD, up.shape[1]
    ch = min(CONV_CHUNK, S)
    offs = _conv_taps(width)

    def body(u_ref, w_ref, o_ref):
        def chunk(ci, carry):
            base = pl.multiple_of(ci * ch, ch)
            win = u_ref[pl.ds(base, ch + 2 * CONV_PAD), :]
            acc = jnp.zeros((ch, LANES), F32)
            for j, off in enumerate(offs):
                acc = acc + win[off:off + ch, :] * w_ref[j:j + 1, :]
            o_ref[pl.ds(base, ch), :] = acc
            return carry

        lax.fori_loop(0, S // ch, chunk, 0)

    return pl.pallas_call(
        body,
        out_shape=jax.ShapeDtypeStruct((S, C), F32),
        grid=(C // LANES,),
        in_specs=[pl.BlockSpec((S + 2 * CONV_PAD, LANES), lambda c: (0, c)),
                  pl.BlockSpec((w.shape[0], LANES), lambda c: (0, c))],
        out_specs=pl.BlockSpec((S, LANES), lambda c: (0, c)),
        compiler_params=_cparams(("parallel",)),
        name=f"dwconv{width}_{S}x{C}",
    )(up, w)


def _dwconv_dw_call(up, g, width, w_rows):
    S, C = g.shape
    ch = min(CONV_CHUNK, S)
    offs = _conv_taps(width)

    def body(u_ref, g_ref, o_ref):
        def chunk(ci, accs):
            base = pl.multiple_of(ci * ch, ch)
            win = u_ref[pl.ds(base, ch + 2 * CONV_PAD), :]
            gt = g_ref[pl.ds(base, ch), :]
            return tuple(a + (gt * win[off:off + ch, :]).reshape(ch // 8, 8, LANES).sum(axis=0)
                         for a, off in zip(accs, offs))

        accs = lax.fori_loop(0, S // ch, chunk, tuple(jnp.zeros((8, LANES), F32) for _ in offs))
        o_ref[...] = jnp.zeros_like(o_ref)
        for j, a in enumerate(accs):
            o_ref[j:j + 1, :] = jnp.sum(a, axis=0, keepdims=True)

    return pl.pallas_call(
        body,
        out_shape=jax.ShapeDtypeStruct((w_rows, C), F32),
        grid=(C // LANES,),
        in_specs=[pl.BlockSpec((S + 2 * CONV_PAD, LANES), lambda c: (0, c)),
                  pl.BlockSpec((S, LANES), lambda c: (0, c))],
        out_specs=pl.BlockSpec((w_rows, LANES), lambda c: (0, c)),
        compiler_params=_cparams(("parallel",)),
        name=f"dwconv{width}_dw_{S}x{C}",
    )(up, g)


def _pad_rows(u):
    return jnp.pad(u, ((CONV_PAD, CONV_PAD), (0, 0)))


def _pad_taps(w):
    return jnp.pad(w, ((0, -w.shape[0] % 8), (0, 0)))


def depthwise_conv(u, w):
    width = w.shape[0]

    @jax.custom_vjp
    def op(u, w):
        return _dwconv_call(_pad_rows(u), _pad_taps(w), width)

    def op_fwd(u, w):
        up = _pad_rows(u)
        return _dwconv_call(up, _pad_taps(w), width), (up, w)

    def op_bwd(res, g):
        up, w = res
        du = _dwconv_call(_pad_rows(g), _pad_taps(w[::-1]), width)
        dw = _dwconv_dw_call(up, g, width, _pad_taps(w).shape[0])[:width]
        return du, dw

    op.defvjp(op_fwd, op_bwd)
    return op(u, w)


def _loss_call(y, target):
    S, D = y.shape
    tile = _row_tile(S, D)

    def body(y_ref, t_ref, o_ref):
        d = y_ref[...] - t_ref[...]
        part = jnp.sum(jnp.sum(d * d, axis=1, keepdims=True), axis=0, keepdims=True) * (0.5 / D)

        @pl.when(pl.program_id(0) == 0)
        def _():
            o_ref[...] = jnp.zeros_like(o_ref)

        o_ref[...] += jnp.broadcast_to(part, o_ref.shape)

    out = pl.pallas_call(
        body,
        out_shape=jax.ShapeDtypeStruct((8, LANES), F32),
        grid=(S // tile,),
        in_specs=[pl.BlockSpec((tile, D), lambda i: (i, 0)), pl.BlockSpec((tile, D), lambda i: (i, 0))],
        out_specs=pl.BlockSpec((8, LANES), lambda i: (0, 0)),
        compiler_params=_cparams(("arbitrary",)),
        name="loss_head",
    )(y, target)
    return out[0, 0]


@jax.custom_vjp
def loss_head(y, target):
    return _loss_call(y, target)


def _loss_fwd(y, target):
    return _loss_call(y, target), (y, target)


def _loss_bwd(res, g):
    y, target = res
    inv_d = 1.0 / y.shape[1]

    def fn(y_t, t_t, g_p):
        return ((y_t - t_t) * (g_p * inv_d),)

    (dy,) = _rw_forward(fn, [y, target], [g.reshape(1, 1).astype(F32)], "loss_head_bwd")
    return dy, jnp.zeros_like(target)


loss_head.defvjp(_loss_fwd, _loss_bwd)


ANY = pl.BlockSpec(memory_space=pl.ANY)


AG_ICI_PIECES = 4
AG_D2D_PIECES = 2
D2D_PIECES = 16
ICI_PIECES = 4


def _with_own_block(received, own, slot):
    return lax.dynamic_update_slice_in_dim(received, own[None], slot, axis=0)


def all_gather_chips(mine, name):
    R, C = mine.shape
    ni, nf = AG_ICI_PIECES, AG_D2D_PIECES
    rows = R // (2 * ni * nf)
    n_ici, n_all = 3 * ni, 3 * ni + 3 * ni * nf

    def body(in_ref, out_ref, send_sems, recv_sems):
        x, y, c = lax.axis_index("x"), lax.axis_index("y"), lax.axis_index("c")
        chips = [(1 - x, y), (x, 1 - y), (1 - x, 1 - y)]

        def copy(sem, src, dst, to):
            return pltpu.make_async_remote_copy(src_ref=src, dst_ref=dst, send_sem=send_sems.at[sem],
                                                recv_sem=recv_sems.at[sem], device_id=to, device_id_type=MESH)

        sends = []
        for k, (px, py) in enumerate(chips):
            for a in range(ni):
                sends.append(copy(k * ni + a, in_ref.at[c, a], out_ref.at[2 * x + y, c, a], (px, py, c)))
                sends[-1].start()
        for k, (px, py) in enumerate(chips):
            for a in range(ni):
                landed = out_ref.at[2 * px + py, c, a]
                copy(k * ni + a, in_ref.at[c, a], landed, (px, py, c)).wait_recv()
                for b in range(nf):
                    sends.append(copy(n_ici + (k * ni + a) * nf + b, landed.at[b], landed.at[b], (x, y, 1 - c)))
                    sends[-1].start()
        for k, (px, py) in enumerate(chips):
            for a in range(ni):
                for b in range(nf):
                    theirs = out_ref.at[2 * px + py, 1 - c, a, b]
                    copy(n_ici + (k * ni + a) * nf + b, theirs, theirs, (x, y, 1 - c)).wait_recv()
        for cp in sends:
            cp.wait_send()

    out = pl.pallas_call(
        body,
        out_shape=jax.ShapeDtypeStruct((N_CHIPS, 2, ni, nf, rows, C), mine.dtype),
        in_specs=[ANY],
        out_specs=ANY,
        scratch_shapes=[pltpu.SemaphoreType.DMA((n_all,)), pltpu.SemaphoreType.DMA((n_all,))],
        name=name,
    )(mine.reshape(2, ni, nf, rows, C))
    return _with_own_block(out.reshape(N_CHIPS, R, C), mine, 2 * lax.axis_index("x") + lax.axis_index("y"))


def swap_halves_with_sibling(parts):
    n_s, _, H, C = parts.shape
    n = ICI_PIECES

    def body(in_ref, out_ref, send_sems, recv_sems):
        x, y, c = lax.axis_index("x"), lax.axis_index("y"), lax.axis_index("c")

        def copy(s, b, half):
            return pltpu.make_async_remote_copy(
                src_ref=in_ref.at[s, half, b], dst_ref=out_ref.at[s, b], send_sem=send_sems.at[s * n + b],
                recv_sem=recv_sems.at[s * n + b], device_id=(x, y, 1 - c), device_id_type=MESH)

        sends = [copy(s, b, 1 - c) for s in range(n_s) for b in range(n)]
        for cp in sends:
            cp.start()
        for s in range(n_s):
            for b in range(n):
                copy(s, b, 1 - c).wait_recv()
        for cp in sends:
            cp.wait_send()

    out = pl.pallas_call(
        body,
        out_shape=jax.ShapeDtypeStruct((n_s, n, H // n, C), parts.dtype),
        in_specs=[ANY],
        out_specs=ANY,
        scratch_shapes=[pltpu.SemaphoreType.DMA((n_s * n,)), pltpu.SemaphoreType.DMA((n_s * n,))],
        name="grad_swap_halves",
    )(parts.reshape(n_s, 2, n, H // n, C))
    return out.reshape(n_s, H, C)


def add_pairs(parts, theirs, core):
    n_s, _, H, C = parts.shape
    tile = _div_tile(H, 512, 16)

    def body(c_ref, a_ref, b_ref, o_ref):
        o_ref[...] = (a_ref[...].astype(F32) + b_ref[...].astype(F32)).astype(o_ref.dtype)

    return pl.pallas_call(
        body,
        out_shape=jax.ShapeDtypeStruct((n_s, 1, H, C), parts.dtype),
        grid_spec=pltpu.PrefetchScalarGridSpec(
            num_scalar_prefetch=1,
            grid=(n_s, H // tile),
            in_specs=[pl.BlockSpec((1, 1, tile, C), lambda s, i, c_ref: (s, c_ref[0], i, 0)),
                      pl.BlockSpec((1, 1, tile, C), lambda s, i, c_ref: (s, 0, i, 0))],
            out_specs=pl.BlockSpec((1, 1, tile, C), lambda s, i, c_ref: (s, 0, i, 0)),
        ),
        compiler_params=_cparams(("parallel", "parallel")),
        name="grad_add_pairs",
    )(core.reshape(1).astype(jnp.int32), parts, theirs.reshape(n_s, 1, H, C)).reshape(n_s, H, C)


def scatter_to_chips(parts):
    n_s, H, C = parts.shape
    n = ICI_PIECES

    def body(in_ref, out_ref, send_sems, recv_sems):
        x, y, c = lax.axis_index("x"), lax.axis_index("y"), lax.axis_index("c")
        chips = [(1 - x, y), (x, 1 - y), (1 - x, 1 - y)]

        def copy(k, b, src_slot, dst_slot, to):
            return pltpu.make_async_remote_copy(
                src_ref=in_ref.at[src_slot, b], dst_ref=out_ref.at[dst_slot, b], send_sem=send_sems.at[k * n + b],
                recv_sem=recv_sems.at[k * n + b], device_id=to, device_id_type=MESH)

        sends = [copy(k, b, 2 * px + py, 2 * x + y, (px, py, c)) for k, (px, py) in enumerate(chips) for b in range(n)]
        for cp in sends:
            cp.start()
        for k, (px, py) in enumerate(chips):
            for b in range(n):
                copy(k, b, 2 * x + y, 2 * px + py, (px, py, c)).wait_recv()
        for cp in sends:
            cp.wait_send()

    out = pl.pallas_call(
        body,
        out_shape=jax.ShapeDtypeStruct((n_s, n, H // n, C), parts.dtype),
        in_specs=[ANY],
        out_specs=ANY,
        scratch_shapes=[pltpu.SemaphoreType.DMA((3 * n,)), pltpu.SemaphoreType.DMA((3 * n,))],
        name="grad_scatter",
    )(parts.reshape(n_s, n, H // n, C))
    slot = 2 * lax.axis_index("x") + lax.axis_index("y")
    return _with_own_block(out.reshape(n_s, H, C), lax.dynamic_index_in_dim(parts, slot, axis=0, keepdims=False), slot)


def exchange_with_sibling(mine):
    H, C = mine.shape
    n = D2D_PIECES

    def body(in_ref, out_ref, send_sems, recv_sems):
        x, y, c = lax.axis_index("x"), lax.axis_index("y"), lax.axis_index("c")

        def copy(b, half):
            return pltpu.make_async_remote_copy(
                src_ref=in_ref.at[b], dst_ref=out_ref.at[half, b], send_sem=send_sems.at[b], recv_sem=recv_sems.at[b],
                device_id=(x, y, 1 - c), device_id_type=MESH)

        sends = [copy(b, c) for b in range(n)]
        for cp in sends:
            cp.start()
        for b in range(n):
            copy(b, 1 - c).wait_recv()
        for cp in sends:
            cp.wait_send()

    out = pl.pallas_call(
        body,
        out_shape=jax.ShapeDtypeStruct((2, n, H // n, C), mine.dtype),
        in_specs=[ANY],
        out_specs=ANY,
        scratch_shapes=[pltpu.SemaphoreType.DMA((n,)), pltpu.SemaphoreType.DMA((n,))],
        name="grad_sibling_exchange",
    )(mine.reshape(n, H // n, C))
    return _with_own_block(out.reshape(2, H, C), mine, lax.axis_index("c"))


def sum_contributions(recv):
    n, H, C = recv.shape
    tile = _div_tile(H, 256, 16)

    def body(r_ref, o_ref):
        acc = r_ref[0].astype(F32)
        for j in range(1, n):
            acc = acc + r_ref[j].astype(F32)
        o_ref[...] = acc

    return pl.pallas_call(
        body,
        out_shape=jax.ShapeDtypeStruct((H, C), F32),
        grid=(H // tile,),
        in_specs=[pl.BlockSpec((n, tile, C), lambda i: (0, i, 0))],
        out_specs=pl.BlockSpec((tile, C), lambda i: (i, 0)),
        compiler_params=_cparams(("parallel",)),
        name="grad_sum",
    )(recv)


def adamw(w, g, m, v):
    shape = w.shape
    cols = shape[-1]
    rows = math.prod(shape[:-1])
    tile = _div_tile(rows, max(8, 256 * 1024 // cols // 8 * 8), 8)
    c1 = 1.0 - ADAM_B1 ** ADAM_STEP
    c2 = 1.0 - ADAM_B2 ** ADAM_STEP

    def body(w_ref, g_ref, m_ref, v_ref, d_ref, nm_ref, nv_ref):
        g_ = g_ref[...]
        nm = ADAM_B1 * m_ref[...] + (1.0 - ADAM_B1) * g_
        nv = ADAM_B2 * v_ref[...] + (1.0 - ADAM_B2) * (g_ * g_)
        d_ref[...] = -ADAM_LR * ((nm / c1) / (jnp.sqrt(nv / c2) + ADAM_EPS) + ADAM_WD * w_ref[...])
        nm_ref[...] = nm
        nv_ref[...] = nv

    spec = pl.BlockSpec((tile, cols), lambda i: (i, 0))
    outs = pl.pallas_call(
        body,
        out_shape=[jax.ShapeDtypeStruct((rows, cols), F32)] * 3,
        grid=(rows // tile,),
        in_specs=[spec] * 4,
        out_specs=[spec] * 3,
        compiler_params=_cparams(("parallel",)),
        name=f"adamw_{rows}x{cols}",
    )(*[t.reshape(rows, cols) for t in (w, g, m, v)])
    return [o.reshape(shape) for o in outs]


def _leaves(name, arr):
    n_lead = WEIGHT_LAYOUT[name][0]
    lead = arr.shape[:n_lead]
    flat = arr.reshape((-1,) + arr.shape[n_lead:])
    return [flat[i] for i in range(math.prod(lead))]


def _pad_pack_rows(buf):
    return jnp.pad(buf, ((0, -buf.shape[0] % PACK_ROW_MULT), (0, 0)))


def _pack_flat(pieces, dtype, cols):
    flat = jnp.concatenate([p.reshape(-1).astype(dtype) for p in pieces])
    flat = jnp.pad(flat, (0, -flat.shape[0] % (cols * PACK_ROW_MULT)))
    return flat.reshape(-1, cols)


def _unpack_flat(buf, shapes):
    lead = buf.shape[:-2]
    flat = buf.reshape(lead + (-1,))
    out, off = [], 0
    for shp in shapes:
        n = math.prod(shp)
        out.append(lax.slice_in_dim(flat, off, off + n, axis=len(lead)).reshape(lead + tuple(shp)))
        off += n
    return out


def _pack_slabs(slabs, dtype):
    return _pad_pack_rows(jnp.concatenate([s.astype(dtype) for s in slabs], axis=0))


def _local_slabs(name, shard):
    leaves = _leaves(name, shard)
    return [leaf.T for leaf in leaves] if name in SLAB_TRANSPOSED else leaves


def _chip_slice(leaf, ax, s):
    if ax is None:
        return leaf
    w = leaf.shape[ax] // N_CHIPS
    return lax.slice_in_dim(leaf, s * w, (s + 1) * w, axis=ax)


def gather_weights(shards, cols):
    slabs = [s for n in SLAB_NAMES for s in _local_slabs(n, shards[n])]
    slab_owner = [n for n in SLAB_NAMES for _ in _leaves(n, shards[n])]
    slab_buf = _pack_slabs(slabs, BF16)
    misc = {}
    for dtype in (BF16, F32):
        names = [n for n in MISC_NAMES if WEIGHT_LAYOUT[n][1] is not None and (n in F32_GATHER) == (dtype == F32)]
        pieces = [leaf for n in names for leaf in _leaves(n, shards[n])]
        owners = [n for n in names for _ in _leaves(n, shards[n])]
        buf = _pack_flat(pieces, dtype, cols)
        if dtype == BF16:
            gathered = all_gather_chips(jnp.concatenate([slab_buf, buf], axis=0), "weights_all_gather_bf16")
            slab_part, gathered = gathered[:, :slab_buf.shape[0]], gathered[:, slab_buf.shape[0]:]
        else:
            gathered = all_gather_chips(buf, "weights_all_gather_f32")
        for n, blk in zip(owners, _unpack_flat(gathered, [p.shape for p in pieces])):
            ax = WEIGHT_LAYOUT[n][1]
            misc.setdefault(n, []).append(jnp.concatenate([blk[s] for s in range(N_CHIPS)], axis=ax).astype(F32))
    for n in MISC_NAMES:
        if WEIGHT_LAYOUT[n][1] is None:
            misc[n] = [leaf.astype(F32) for leaf in _leaves(n, shards[n])]
    slab_full, off = {}, 0
    for n, s in zip(slab_owner, slabs):
        rows = s.shape[0]
        slab_full.setdefault(n, []).append(slab_part[:, off:off + rows].reshape(N_CHIPS * rows, cols))
        off += rows
    return slab_full, misc


def reduce_gradients(g_slab, g_misc, shards, cols):
    per_chip = []
    for s in range(N_CHIPS):
        slabs = []
        for n in SLAB_NAMES:
            for leaf in g_slab[n]:
                slabs.append(_chip_slice(leaf, 0, s))
        pieces = [_chip_slice(leaf, WEIGHT_LAYOUT[n][1], s) for n in MISC_NAMES for leaf in g_misc[n]]
        per_chip.append(jnp.concatenate([_pack_slabs(slabs, BF16), _pack_flat(pieces, BF16, cols)], axis=0))
    parts = jnp.stack(per_chip)
    R = parts.shape[1]
    parts = parts.reshape(N_CHIPS, 2, R // 2, cols)
    chip_parts = add_pairs(parts, swap_halves_with_sibling(parts), lax.axis_index("c"))
    recv = scatter_to_chips(chip_parts)
    red = exchange_with_sibling(sum_contributions(recv)).reshape(R, cols)

    out, off = {}, 0
    for n in SLAB_NAMES:
        leaves = []
        for slab in _local_slabs(n, shards[n]):
            rows = slab.shape[0]
            g = red[off:off + rows]
            leaves.append(g.T if n in SLAB_TRANSPOSED else g)
            off += rows
        out[n] = jnp.stack(leaves).reshape(shards[n].shape)
    off += -off % PACK_ROW_MULT
    misc_shapes = [shards[n].shape for n in MISC_NAMES]
    for n, g in zip(MISC_NAMES, _unpack_flat(red[off:], misc_shapes)):
        out[n] = g
    return out


def _silu(x):
    return x * jax.nn.sigmoid(x)


def _heads_first(t, heads):
    S = t.shape[0]
    return t.reshape(S, heads, -1).transpose(1, 0, 2)


def _heads_last(t):
    return t.transpose(1, 0, 2).reshape(t.shape[1], -1)


class SlabWeights:
    def __init__(self, values, deltas):
        self.values, self.deltas = values, deltas

    def matmul(self, a, name, leaf, out_dtype):
        return mm_slab(a, self.values[name][leaf], self.deltas[name][leaf], name in SLAB_TRANSPOSED, out_dtype)


def swiglu_half_step(x, h, slab, leaf, g_post, g_next):
    a = slab.matmul(h, 'ffn_w_gate', leaf, BF16)
    b = slab.matmul(h, 'ffn_w_up', leaf, BF16)
    (u,) = rowwise(lambda a_, b_: ((_silu(a_.astype(F32)) * b_.astype(F32)).astype(BF16),), [a, b], [], "swiglu_act")
    y = slab.matmul(u, 'ffn_w_down', leaf, F32)
    return residual_norm(x, y, 0.5, g_post, g_next)


def residual_norm(x, y, coef, g_post, g_next):
    if g_next is None:
        (xn,) = rowwise(lambda x_, y_, g1: (x_ + coef * _rms(y_, g1),), [x, y], [g_post], "residual")
        return xn, None

    def fn(x_, y_, g1, g2):
        xn = x_ + coef * _rms(y_, g1)
        return xn, _rms(xn, g2).astype(BF16)

    return rowwise(fn, [x, y], [g_post, g_next], "residual_norm")


def memory_attention(q_mem, memn, w_kv):
    kv = mm(memn, w_kv, BF16)
    n_mem = kv.shape[0]
    kv = kv.reshape(n_mem, 2, MEM_HEADS, HEAD_DIM).transpose(1, 2, 0, 3)
    o = full_attention(_heads_first(q_mem.astype(BF16), MEM_HEADS), kv[0], kv[1], HEAD_DIM ** -0.5, False)
    return _heads_last(o)


def mla_mixer(h, cos, sin, w_in, q_g, kv_g, w_uq, w_ukv):
    S = h.shape[0]
    a_in = MLA_Q_LORA + MLA_KV_LORA + MLA_ROPE + MEM_WIDTH
    z = mm(h, jnp.pad(w_in, ((0, 0), (0, -a_in % LANES))), F32)
    o1, o2, o3 = MLA_Q_LORA, MLA_Q_LORA + MLA_KV_LORA, MLA_Q_LORA + MLA_KV_LORA + MLA_ROPE
    c_q, c_kv, k_r, q_mem = z[:, :o1], z[:, o1:o2], z[:, o2:o3], z[:, o3:a_in]
    cqn, ckvn = rowwise(lambda a, b, ga, gb: (_rms(a, ga).astype(BF16), _rms(b, gb).astype(BF16)),
                        [c_q, c_kv], [q_g, kv_g], "mla_lora_norm")
    q = mm(cqn, w_uq, F32).reshape(S, MLA_HEADS, MLA_NOPE + MLA_ROPE)
    kv = mm(ckvn, w_ukv, BF16).reshape(S, MLA_HEADS, MLA_NOPE + MLA_V)
    half = MLA_ROPE // 2
    qn = q[:, :, :MLA_NOPE].reshape(S, MLA_HEADS * MLA_NOPE)
    q1 = q[:, :, MLA_NOPE:MLA_NOPE + half].reshape(S, MLA_HEADS * half)
    q2 = q[:, :, MLA_NOPE + half:].reshape(S, MLA_HEADS * half)
    cos_h, sin_h = jnp.tile(cos, (1, MLA_HEADS)), jnp.tile(sin, (1, MLA_HEADS))
    scale = (MLA_NOPE + MLA_ROPE) ** -0.5
    qm = scale * LOG2E

    def rope(qn_, q1_, q2_, k1_, k2_, ch, sh, c1, s1):
        return ((qn_ * qm).astype(BF16), ((q1_ * ch - q2_ * sh) * qm).astype(BF16),
                ((q1_ * sh + q2_ * ch) * qm).astype(BF16),
                (k1_ * c1 - k2_ * s1).astype(BF16), (k1_ * s1 + k2_ * c1).astype(BF16))

    qns, qr1, qr2, kr1, kr2 = rowwise(rope, [qn, q1, q2, k_r[:, :half], k_r[:, half:], cos_h, sin_h, cos, sin], [],
                                      "mla_rope", n_const=4)
    q_cat = jnp.concatenate([qns.reshape(S, MLA_HEADS, MLA_NOPE), qr1.reshape(S, MLA_HEADS, half),
                             qr2.reshape(S, MLA_HEADS, half)], axis=-1)
    k_rope = jnp.concatenate([kr1, kr2], axis=-1)
    k_cat = jnp.concatenate([kv[:, :, :MLA_NOPE], jnp.broadcast_to(k_rope[:, None, :], (S, MLA_HEADS, MLA_ROPE))],
                            axis=-1)
    o = full_attention(q_cat.transpose(1, 0, 2), k_cat.transpose(1, 0, 2), kv[:, :, MLA_NOPE:].transpose(1, 0, 2),
                       scale, True)
    return _heads_last(o), q_mem


def dilated_mixer(h, slab):
    S = h.shape[0]
    n_g = len(DIL_GROUPS)
    qkv_w = n_g * 3 * DIL_HEADS * HEAD_DIM
    z = slab.matmul(h, 'b_w_in', 0, BF16)
    zd = z[:, :qkv_w].reshape(S, n_g, 3, DIL_HEADS, HEAD_DIM)
    q_mem = z[:, qkv_w:]
    slopes = (2.0 ** (-ALIBI_MAX * (jnp.arange(n_g * DIL_HEADS, dtype=F32) + 1.0) / (n_g * DIL_HEADS)))
    slopes = slopes.reshape(n_g, DIL_HEADS)
    outs, lses = [], []
    for g, (window, dil) in enumerate(DIL_GROUPS):
        L = S // dil

        def sub(t):
            return t.reshape(L, dil, DIL_HEADS, HEAD_DIM).transpose(1, 2, 0, 3).reshape(dil * DIL_HEADS, L, HEAD_DIM)

        slope = (jnp.tile(slopes[g], dil) * dil).reshape(dil * DIL_HEADS, 1, 1)
        o, lse = band_attention(sub(zd[:, g, 0]), sub(zd[:, g, 1]), sub(zd[:, g, 2]), slope, window // (2 * dil))
        outs.append(o.reshape(dil, DIL_HEADS, L, HEAD_DIM).transpose(2, 0, 1, 3).reshape(S * DIL_HEADS, HEAD_DIM))
        lses.append(lse.reshape(dil, DIL_HEADS, L).transpose(2, 0, 1).reshape(S * DIL_HEADS, 1))

    def merge(o0, o1, o2, l0, l1, l2):
        m = jnp.maximum(jnp.maximum(l0, l1), l2)
        e0, e1, e2 = jnp.exp(l0 - m), jnp.exp(l1 - m), jnp.exp(l2 - m)
        return (((e0 * o0 + e1 * o1 + e2 * o2) / (e0 + e1 + e2)).astype(BF16),)

    (o,) = rowwise(merge, outs + lses, [], "dilated_merge")
    return o.reshape(S, DIL_HEADS * HEAD_DIM), q_mem


def conformer_conv_mixer(h, slab, conv_w, conv_b, ln_g, ln_b):
    z = slab.matmul(h, 'c_w_in', 0, F32)
    a, gate, q_mem = z[:, :CONV_CH], z[:, CONV_CH:2 * CONV_CH], z[:, 2 * CONV_CH:]
    (u,) = rowwise(lambda a_, g_: (a_ * jax.nn.sigmoid(g_),), [a, gate], [], "conformer_glu")
    u = depthwise_conv(u, conv_w)

    def post(u_, b, g, beta):
        t = u_ + b
        mu = jnp.mean(t, axis=-1, keepdims=True)
        var = jnp.mean(jnp.square(t - mu), axis=-1, keepdims=True)
        return (_silu((t - mu) * lax.rsqrt(var + EPS) * g + beta).astype(BF16),)

    (o,) = rowwise(post, [u], [conv_b.reshape(1, -1), ln_g.reshape(1, -1), ln_b.reshape(1, -1)], "conformer_post")
    return o, q_mem


def short_conv_mixer(h, slab, conv_w):
    z = slab.matmul(h, 'd_w_in', 0, F32)
    bg, cg, hx, q_mem = z[:, :SC_CH], z[:, SC_CH:2 * SC_CH], z[:, 2 * SC_CH:3 * SC_CH], z[:, 3 * SC_CH:]
    (p,) = rowwise(lambda c_, h_: (c_ * h_,), [cg, hx], [], "shortconv_pre")
    cv = depthwise_conv(p, conv_w)
    (o,) = rowwise(lambda b_, c_: ((b_ * c_).astype(BF16),), [bg, cv], [], "shortconv_post")
    return o, q_mem


def local_loss(deltas, W, x, slab_values, mem, cos, sin, target):
    slab = SlabWeights(slab_values, deltas)
    norm_g = W['norm_g'][0]

    def gain(i, k):
        return norm_g[i, k].reshape(1, -1)

    (h,) = rowwise(lambda x_, g: (_rms(x_, g).astype(BF16),), [x], [gain(0, 0)], "input_norm")
    for i in range(DEPTH):
        x, h = swiglu_half_step(x, h, slab, 2 * i, gain(i, 1), gain(i, 2))
        if i == 0:
            o, q_mem = mla_mixer(h, cos, sin, W['a_w_in'][0], W['a_q_norm'][0].reshape(1, -1),
                                 W['a_kv_norm'][0].reshape(1, -1), W['a_w_uq'][0], W['a_w_ukv'][0])
            out_name = 'a_w_out'
        elif i == 1:
            o, q_mem = dilated_mixer(h, slab)
            out_name = None
        elif i == 2:
            o, q_mem = conformer_conv_mixer(h, slab, W['c_conv_w'][0], W['c_conv_b'][0], W['c_ln_g'][0],
                                            W['c_ln_b'][0])
            out_name = 'c_w_out'
        else:
            o, q_mem = short_conv_mixer(h, slab, W['d_conv_w'][0])
            out_name = 'd_w_out'
        (memn,) = rowwise(lambda m_, g: (_rms(m_, g).astype(BF16),), [mem], [gain(i, 6)], "memory_norm")
        mo = memory_attention(q_mem, memn, W['mem_w_kv'][i])
        cat = jnp.concatenate([o, mo], axis=-1)
        y = mm(cat, W['b_w_out'][0], F32) if out_name is None else slab.matmul(cat, out_name, 0, F32)
        x, h = residual_norm(x, y, 1.0, gain(i, 3), gain(i, 4))
        x, h = swiglu_half_step(x, h, slab, 2 * i + 1, gain(i, 5), gain(i + 1, 0) if i + 1 < DEPTH else None)
    return loss_head(x, target)


def kernel(x, mem, positions, norm_g, ffn_w_gate, ffn_w_up, ffn_w_down, mem_w_kv, a_w_in, a_q_norm, a_kv_norm, a_w_uq, a_w_ukv, a_w_out, b_w_in, b_w_out, c_w_in, c_conv_w, c_conv_b, c_ln_g, c_ln_b, c_w_out, d_w_in, d_conv_w, d_w_out, loss_target, m_norm_g, m_ffn_w_gate, m_ffn_w_up, m_ffn_w_down, m_mem_w_kv, m_a_w_in, m_a_q_norm, m_a_kv_norm, m_a_w_uq, m_a_w_ukv, m_a_w_out, m_b_w_in, m_b_w_out, m_c_w_in, m_c_conv_w, m_c_conv_b, m_c_ln_g, m_c_ln_b, m_c_w_out, m_d_w_in, m_d_conv_w, m_d_w_out, v_norm_g, v_ffn_w_gate, v_ffn_w_up, v_ffn_w_down, v_mem_w_kv, v_a_w_in, v_a_q_norm, v_a_kv_norm, v_a_w_uq, v_a_w_ukv, v_a_w_out, v_b_w_in, v_b_w_out, v_c_w_in, v_c_conv_w, v_c_conv_b, v_c_ln_g, v_c_ln_b, v_c_w_out, v_d_w_in, v_d_conv_w, v_d_w_out):
    w_in = dict(zip(WEIGHT_NAMES, (norm_g, ffn_w_gate, ffn_w_up, ffn_w_down, mem_w_kv, a_w_in, a_q_norm, a_kv_norm, a_w_uq, a_w_ukv, a_w_out, b_w_in, b_w_out, c_w_in, c_conv_w, c_conv_b, c_ln_g, c_ln_b, c_w_out, d_w_in, d_conv_w, d_w_out)))
    m_in = dict(zip(WEIGHT_NAMES, (m_norm_g, m_ffn_w_gate, m_ffn_w_up, m_ffn_w_down, m_mem_w_kv, m_a_w_in, m_a_q_norm, m_a_kv_norm, m_a_w_uq, m_a_w_ukv, m_a_w_out, m_b_w_in, m_b_w_out, m_c_w_in, m_c_conv_w, m_c_conv_b, m_c_ln_g, m_c_ln_b, m_c_w_out, m_d_w_in, m_d_conv_w, m_d_w_out)))
    v_in = dict(zip(WEIGHT_NAMES, (v_norm_g, v_ffn_w_gate, v_ffn_w_up, v_ffn_w_down, v_mem_w_kv, v_a_w_in, v_a_q_norm, v_a_kv_norm, v_a_w_uq, v_a_w_ukv, v_a_w_out, v_b_w_in, v_b_w_out, v_c_w_in, v_c_conv_w, v_c_conv_b, v_c_ln_g, v_c_ln_b, v_c_w_out, v_d_w_in, v_d_conv_w, v_d_w_out)))

    d_model = x.shape[-1]
    slab_values, W = gather_weights(w_in, d_model)
    deltas = {n: [jnp.zeros(leaf.shape, BF16) for leaf in leaves] for n, leaves in slab_values.items()}
    half = MLA_ROPE // 2
    inv = ROPE_THETA ** (-jnp.arange(half, dtype=F32) / half)
    ang = positions[0].astype(F32)[:, None] * inv
    loss, (g_slab, g_misc, gx) = jax.value_and_grad(local_loss, argnums=(0, 1, 2))(
        deltas, W, x[0], slab_values, mem[0], jnp.cos(ang), jnp.sin(ang), loss_target[0])
    loss = lax.psum(loss, ("x", "y", "c"))

    grads = reduce_gradients(g_slab, g_misc, w_in, d_model)
    steps = {n: adamw(w_in[n], grads[n], m_in[n], v_in[n]) for n in WEIGHT_NAMES}
    return (loss, gx[None], *[grads[n] for n in WEIGHT_NAMES], *[steps[n][0] for n in WEIGHT_NAMES],
            *[steps[n][1] for n in WEIGHT_NAMES], *[steps[n][2] for n in WEIGHT_NAMES])
```

```python
import functools
import math

import jax
import jax.numpy as jnp
from jax import lax
from jax.experimental import pallas as pl
from jax.experimental.pallas import tpu as pltpu

F32 = jnp.float32
BF16 = jnp.bfloat16
MESH = pl.DeviceIdType.MESH

VMEM_LIMIT_BYTES = 56 * 1024 * 1024
LANES = 128
PACK_ROW_MULT = 512
ROW_TILE_ELEMS = 768 * 1024

HEAD_DIM = 64
MEM_HEADS = 4
MEM_WIDTH = MEM_HEADS * HEAD_DIM
MLA_HEADS = 12
MLA_Q_LORA = 384
MLA_KV_LORA = 256
MLA_NOPE = 64
MLA_ROPE = 32
MLA_V = 64
ROPE_THETA = 10000.0
DIL_GROUPS = ((128, 1), (512, 4), (2048, 16))
DIL_HEADS = 8
ALIBI_MAX = 8.0
CONV_CH = 768
CONV_WIDTH = 31
SC_CH = 768
SC_WIDTH = 3
CONV_PAD = 16
CONV_CHUNK = 256
EPS = 1e-6
NEG = -1e30
DEPTH = 4
LOG2E = 1.4426950408889634
LN2 = 0.6931471805599453

ADAM_LR = 0.001
ADAM_B1 = 0.9
ADAM_B2 = 0.999
ADAM_EPS = 1e-08
ADAM_WD = 0.01
ADAM_STEP = 10

WEIGHT_NAMES = ['norm_g', 'ffn_w_gate', 'ffn_w_up', 'ffn_w_down', 'mem_w_kv', 'a_w_in', 'a_q_norm', 'a_kv_norm',
                'a_w_uq', 'a_w_ukv', 'a_w_out', 'b_w_in', 'b_w_out', 'c_w_in', 'c_conv_w', 'c_conv_b', 'c_ln_g',
                'c_ln_b', 'c_w_out', 'd_w_in', 'd_conv_w', 'd_w_out']
WEIGHT_LAYOUT = {
    'norm_g': (0, 2), 'ffn_w_gate': (2, 1), 'ffn_w_up': (2, 1), 'ffn_w_down': (2, 0), 'mem_w_kv': (1, 0),
    'a_w_in': (1, 0), 'a_q_norm': (1, None), 'a_kv_norm': (1, None), 'a_w_uq': (1, 1), 'a_w_ukv': (1, 1),
    'a_w_out': (1, 0), 'b_w_in': (1, 1), 'b_w_out': (1, 1), 'c_w_in': (1, 1), 'c_conv_w': (1, 1),
    'c_conv_b': (1, 0), 'c_ln_g': (1, 0), 'c_ln_b': (1, 0), 'c_w_out': (1, 0), 'd_w_in': (1, 1),
    'd_conv_w': (1, 1), 'd_w_out': (1, 0),
}
SLAB_TRANSPOSED = ('ffn_w_gate', 'ffn_w_up', 'b_w_in', 'c_w_in', 'd_w_in')
SLAB_ROWS = ('ffn_w_down', 'a_w_out', 'c_w_out', 'd_w_out')
SLAB_NAMES = [n for n in WEIGHT_NAMES if n in SLAB_TRANSPOSED or n in SLAB_ROWS]
MISC_NAMES = [n for n in WEIGHT_NAMES if n not in SLAB_NAMES]
F32_GATHER = ('norm_g', 'c_conv_w', 'c_conv_b', 'c_ln_g', 'c_ln_b', 'd_conv_w')
N_CHIPS = 4


def _cparams(semantics):
    return pltpu.CompilerParams(dimension_semantics=semantics, vmem_limit_bytes=VMEM_LIMIT_BYTES)


def _div_tile(n, cap, mult):
    if n <= cap:
        return n
    for d in range(cap - cap % mult, 0, -mult):
        if n % d == 0:
            return d
    raise ValueError(f"no tile for {n} (cap {cap}, multiple of {mult})")


def _matmul(a, b, mode, out_dtype):
    if mode == 'nn':
        (M, K), N = a.shape, b.shape[1]
    elif mode == 'nt':
        (M, K), N = a.shape, b.shape[0]
    else:
        (K, M), N = a.shape, b.shape[1]
    if mode == 'tn':
        tm, tn, tk = _div_tile(M, 2816, LANES), _div_tile(N, 1536, LANES), _div_tile(K, 1024, 16)
    else:
        tm, tn, tk = _div_tile(M, 512, 16), _div_tile(N, 2816, LANES), _div_tile(K, 2816, LANES)
    nk = K // tk
    if mode == 'nn':
        a_spec = pl.BlockSpec((tm, tk), lambda i, j, k: (i, k))
        b_spec = pl.BlockSpec((tk, tn), lambda i, j, k: (k, j))
        dims = (((1,), (0,)), ((), ()))
    elif mode == 'nt':
        a_spec = pl.BlockSpec((tm, tk), lambda i, j, k: (i, k))
        b_spec = pl.BlockSpec((tn, tk), lambda i, j, k: (j, k))
        dims = (((1,), (1,)), ((), ()))
    else:
        a_spec = pl.BlockSpec((tk, tm), lambda i, j, k: (k, i))
        b_spec = pl.BlockSpec((tk, tn), lambda i, j, k: (k, j))
        dims = (((0,), (0,)), ((), ()))

    def body(a_ref, b_ref, o_ref, *acc):
        r = lax.dot_general(a_ref[...].astype(BF16), b_ref[...].astype(BF16), dims, preferred_element_type=F32)
        if nk == 1:
            o_ref[...] = r.astype(o_ref.dtype)
        else:
            k = pl.program_id(2)

            @pl.when(k == 0)
            def _():
                acc[0][...] = r

            @pl.when(k > 0)
            def _():
                acc[0][...] += r

            @pl.when(k == nk - 1)
            def _():
                o_ref[...] = acc[0][...].astype(o_ref.dtype)

    return pl.pallas_call(
        body,
        out_shape=jax.ShapeDtypeStruct((M, N), out_dtype),
        grid=(M // tm, N // tn, nk),
        in_specs=[a_spec, b_spec],
        out_specs=pl.BlockSpec((tm, tn), lambda i, j, k: (i, j)),
        scratch_shapes=[pltpu.VMEM((tm, tn), F32)] if nk > 1 else [],
        compiler_params=_cparams(("parallel", "parallel", "arbitrary")),
        name=f"mm_{mode}_{M}x{K}x{N}",
    )(a, b)


@functools.partial(jax.custom_vjp, nondiff_argnums=(2,))
def mm(a, w, out_dtype):
    return _matmul(a, w.astype(BF16), 'nn', out_dtype)


def _mm_fwd(a, w, out_dtype):
    wb = w.astype(BF16)
    return _matmul(a, wb, 'nn', out_dtype), (a, wb)


def _mm_bwd(out_dtype, res, g):
    a, wb = res
    return _matmul(g, wb, 'nt', a.dtype), _matmul(a, g, 'tn', F32)


mm.defvjp(_mm_fwd, _mm_bwd)


@functools.partial(jax.custom_vjp, nondiff_argnums=(3, 4))
def mm_slab(a, wb, delta, transposed, out_dtype):
    return _matmul(a, wb, 'nt' if transposed else 'nn', out_dtype)


def _mm_slab_fwd(a, wb, delta, transposed, out_dtype):
    return _matmul(a, wb, 'nt' if transposed else 'nn', out_dtype), (a, wb)


def _mm_slab_bwd(transposed, out_dtype, res, g):
    a, wb = res
    if transposed:
        return _matmul(g, wb, 'nn', a.dtype), jnp.zeros_like(wb), _matmul(g, a, 'tn', BF16)
    return _matmul(g, wb, 'nt', a.dtype), jnp.zeros_like(wb), _matmul(a, g, 'tn', BF16)


mm_slab.defvjp(_mm_slab_fwd, _mm_slab_bwd)


def _row_tile(n_rows, widest):
    t = 1 << max(3, int(math.log2(max(8, ROW_TILE_ELEMS // max(widest, 1)))))
    return min(n_rows, min(t, 2048))


def _rw_forward(fn, rows, params, name, tile=None):
    n_rows = rows[0].shape[0]
    tile = tile or _row_tile(n_rows, max(r.shape[1] for r in rows))
    outs = jax.eval_shape(fn, *[jax.ShapeDtypeStruct((tile, r.shape[1]), r.dtype) for r in rows],
                          *[jax.ShapeDtypeStruct(p.shape, p.dtype) for p in params])
    n_in = len(rows) + len(params)

    def body(*refs):
        res = fn(*[r[...] for r in refs[:n_in]])
        for o_ref, o in zip(refs[n_in:], res):
            o_ref[...] = o

    return pl.pallas_call(
        body,
        out_shape=[jax.ShapeDtypeStruct((n_rows, o.shape[1]), o.dtype) for o in outs],
        grid=(n_rows // tile,),
        in_specs=[pl.BlockSpec((tile, r.shape[1]), lambda i: (i, 0)) for r in rows]
        + [pl.BlockSpec(p.shape, lambda i: (0, 0)) for p in params],
        out_specs=[pl.BlockSpec((tile, o.shape[1]), lambda i: (i, 0)) for o in outs],
        compiler_params=_cparams(("parallel",)),
        name=name,
    )(*rows, *params)


def _rw_backward(fn, rows, params, cts, n_const, name):
    n_rows = rows[0].shape[0]
    tile = _row_tile(n_rows, max([r.shape[1] for r in rows] + [c.shape[1] for c in cts]))
    n_r, n_p, n_c = len(rows), len(params), len(cts)
    n_diff = n_r - n_const

    def body(*refs):
        row_vals = [r[...] for r in refs[:n_r]]
        par_vals = [r[...] for r in refs[n_r:n_r + n_p]]
        ct_vals = tuple(r[...] for r in refs[n_r + n_p:n_r + n_p + n_c])
        out_refs = refs[n_r + n_p + n_c:]

        def f(*diff):
            return fn(*diff[:n_diff], *row_vals[n_diff:], *diff[n_diff:])

        _, vjp = jax.vjp(f, *row_vals[:n_diff], *par_vals)
        grads = vjp(ct_vals)
        for o_ref, g in zip(out_refs[:n_diff], grads[:n_diff]):
            o_ref[...] = g
        i = pl.program_id(0)
        for o_ref, g in zip(out_refs[n_diff:], grads[n_diff:]):
            @pl.when(i == 0)
            def _(o_ref=o_ref, g=g):
                o_ref[...] = g

            @pl.when(i > 0)
            def _(o_ref=o_ref, g=g):
                o_ref[...] += g

    res = pl.pallas_call(
        body,
        out_shape=[jax.ShapeDtypeStruct(r.shape, r.dtype) for r in rows[:n_diff]]
        + [jax.ShapeDtypeStruct(p.shape, p.dtype) for p in params],
        grid=(n_rows // tile,),
        in_specs=[pl.BlockSpec((tile, r.shape[1]), lambda i: (i, 0)) for r in rows]
        + [pl.BlockSpec(p.shape, lambda i: (0, 0)) for p in params]
        + [pl.BlockSpec((tile, c.shape[1]), lambda i: (i, 0)) for c in cts],
        out_specs=[pl.BlockSpec((tile, r.shape[1]), lambda i: (i, 0)) for r in rows[:n_diff]]
        + [pl.BlockSpec(p.shape, lambda i: (0, 0)) for p in params],
        compiler_params=_cparams(("arbitrary",)),
        name=name + "_bwd",
    )(*rows, *params, *cts)
    return list(res[:n_diff]), list(res[n_diff:])


def rowwise(fn, rows, params, name, n_const=0, fwd_tile=None):
    rows, params = list(rows), list(params)
    fwd_tile = fwd_tile and min(fwd_tile, rows[0].shape[0])

    @jax.custom_vjp
    def op(rows, params):
        return tuple(_rw_forward(fn, rows, params, name, fwd_tile))

    def op_fwd(rows, params):
        return tuple(_rw_forward(fn, rows, params, name, fwd_tile)), (rows, params)

    def op_bwd(res, cts):
        rows, params = res
        d_rows, d_params = _rw_backward(fn, rows, params, list(cts), n_const, name)
        d_rows = d_rows + [jnp.zeros_like(r) for r in rows[len(rows) - n_const:]]
        return d_rows, d_params

    op.defvjp(op_fwd, op_bwd)
    return op(rows, params)


def _rms(x, g):
    xf = x.astype(F32)
    return xf * lax.rsqrt(jnp.mean(xf * xf, axis=-1, keepdims=True) + EPS) * g


ATTN_TILE = 1024
ATTN_SPAN = 2048
ATTN_CHUNK = 512
DELTA_TILE = 8192

NT_DIMS = (((1,), (1,)), ((), ()))


def _attn_tiles(stay, stream):
    span = min(ATTN_SPAN, stream)
    return min(ATTN_TILE, stay), span, min(ATTN_CHUNK, span)


def _flash_fwd(q, k, v, qk_mult):
    N, Lq, dk = q.shape
    Lk, dv = k.shape[1], v.shape[2]
    tq, span, ch = _attn_tiles(Lq, Lk)
    tq = min(Lq, tq * max(2, ATTN_TILE // span))
    nk = Lk // span
    v_ones = jnp.concatenate([v, jnp.ones((N, Lk, 1), v.dtype), jnp.zeros((N, Lk, LANES - dv - 1), v.dtype)], axis=2)

    def body(q_ref, k_ref, v_ref, o_ref, lse_ref, m_s, acc_s):
        j = pl.program_id(2)

        @pl.when(j == 0)
        def _():
            m_s[...] = jnp.full_like(m_s, NEG)
            acc_s[...] = jnp.zeros_like(acc_s)

        qv = q_ref[0]
        s = []
        for c in range(span // ch):
            sc = lax.dot_general(qv, k_ref[0, c * ch:(c + 1) * ch, :], NT_DIMS, preferred_element_type=F32)
            s.append(sc if qk_mult == 1.0 else sc * qk_mult)
        s_max = functools.reduce(jnp.maximum, s)
        m_new = jnp.maximum(m_s[...], jnp.max(s_max, axis=-1, keepdims=True))
        pv = None
        for c, sc in enumerate(s):
            p = jnp.exp2(sc - m_new).astype(BF16)
            d = jnp.dot(p, v_ref[0, c * ch:(c + 1) * ch, :], preferred_element_type=F32)
            pv = d if c == 0 else pv + d
        acc_s[...] = jnp.exp2(m_s[...] - m_new) * acc_s[...] + pv
        m_s[...] = m_new

        @pl.when(j == nk - 1)
        def _():
            acc = acc_s[...]
            l = acc[:, dv:dv + 1]
            o_ref[0] = (acc[:, :dv] / l).astype(o_ref.dtype)
            lse_ref[0] = m_s[...] + jnp.log(l) * LOG2E

    return pl.pallas_call(
        body,
        out_shape=[jax.ShapeDtypeStruct((N, Lq, dv), BF16), jax.ShapeDtypeStruct((N, Lq, 1), F32)],
        grid=(N, Lq // tq, nk),
        in_specs=[pl.BlockSpec((1, tq, dk), lambda n, i, j: (n, i, 0)),
                  pl.BlockSpec((1, span, dk), lambda n, i, j: (n, j, 0)),
                  pl.BlockSpec((1, span, LANES), lambda n, i, j: (n, j, 0))],
        out_specs=[pl.BlockSpec((1, tq, dv), lambda n, i, j: (n, i, 0)),
                   pl.BlockSpec((1, tq, 1), lambda n, i, j: (n, i, 0))],
        scratch_shapes=[pltpu.VMEM((tq, 1), F32), pltpu.VMEM((tq, LANES), F32)],
        compiler_params=_cparams(("parallel", "parallel", "arbitrary")),
        name=f"flash_fwd_{N}x{Lq}x{Lk}x{dk}",
    )(q, k, v_ones)


def _flash_bwd(q, k, v, do, lse_row, dd_row, qk_mult, out_mult):
    N, Lq, dk = q.shape
    Lk, dv = k.shape[1], v.shape[2]
    tk, span, ch = _attn_tiles(Lk, Lq)
    tk = min(Lk, 2 * tk)
    nq, nkt = Lq // span, Lk // tk

    def body(q_ref, k_ref, v_ref, do_ref, lse_ref, dd_ref, dq_ref, dk_ref, dv_ref, dq_s, dk_s, dv_s):
        i, j = pl.program_id(1), pl.program_id(2)

        @pl.when((i == 0) & (j == 0))
        def _():
            dq_s[...] = jnp.zeros_like(dq_s)

        @pl.when(j == 0)
        def _():
            dk_s[...] = jnp.zeros_like(dk_s)
            dv_s[...] = jnp.zeros_like(dv_s)

        kv_, vv = k_ref[0], v_ref[0]
        for c in range(span // ch):
            qc = q_ref[0, c * ch:(c + 1) * ch, :]
            doc = do_ref[0, c * ch:(c + 1) * ch, :]
            st = lax.dot_general(kv_, qc, NT_DIMS, preferred_element_type=F32)
            if qk_mult != 1.0:
                st = st * qk_mult
            pt = jnp.exp2(st - lse_ref[0, :, c * ch:(c + 1) * ch])
            dv_s[...] += jnp.dot(pt.astype(BF16), doc, preferred_element_type=F32)
            dpt = lax.dot_general(vv, doc, NT_DIMS, preferred_element_type=F32)
            dst = (pt * (dpt - dd_ref[0, :, c * ch:(c + 1) * ch])).astype(BF16)
            dk_s[...] += jnp.dot(dst, qc, preferred_element_type=F32)
            rows = pl.ds(pl.multiple_of(j * span + c * ch, ch), ch)
            dq_s[rows, :] += lax.dot_general(dst, kv_, (((0,), (0,)), ((), ())), preferred_element_type=F32)

        @pl.when(j == nq - 1)
        def _():
            dk_ref[0] = (dk_s[...] * out_mult).astype(dk_ref.dtype)
            dv_ref[0] = dv_s[...].astype(dv_ref.dtype)

        @pl.when((i == nkt - 1) & (j == nq - 1))
        def _():
            dq_ref[0] = (dq_s[...] * out_mult).astype(dq_ref.dtype)

    return pl.pallas_call(
        body,
        out_shape=[jax.ShapeDtypeStruct(q.shape, q.dtype), jax.ShapeDtypeStruct(k.shape, k.dtype),
                   jax.ShapeDtypeStruct(v.shape, v.dtype)],
        grid=(N, nkt, nq),
        in_specs=[pl.BlockSpec((1, span, dk), lambda n, i, j: (n, j, 0)),
                  pl.BlockSpec((1, tk, dk), lambda n, i, j: (n, i, 0)),
                  pl.BlockSpec((1, tk, dv), lambda n, i, j: (n, i, 0)),
                  pl.BlockSpec((1, span, dv), lambda n, i, j: (n, j, 0)),
                  pl.BlockSpec((1, 1, span), lambda n, i, j: (n, 0, j)),
                  pl.BlockSpec((1, 1, span), lambda n, i, j: (n, 0, j))],
        out_specs=[pl.BlockSpec((1, Lq, dk), lambda n, i, j: (n, 0, 0)),
                   pl.BlockSpec((1, tk, dk), lambda n, i, j: (n, i, 0)),
                   pl.BlockSpec((1, tk, dv), lambda n, i, j: (n, i, 0))],
        scratch_shapes=[pltpu.VMEM((Lq, dk), F32), pltpu.VMEM((tk, dk), F32), pltpu.VMEM((tk, dv), F32)],
        compiler_params=_cparams(("parallel", "arbitrary", "arbitrary")),
        name=f"flash_bwd_{N}x{Lq}x{Lk}x{dk}",
    )(q, k, v, do, lse_row, dd_row)


def _attn_delta(do, o, dlse):
    N, L, dv = o.shape

    def fn(do_t, o_t, dl_t):
        return (jnp.sum(do_t.astype(F32) * o_t.astype(F32), axis=-1, keepdims=True) - dl_t,)

    (dd,) = _rw_forward(fn, [do.reshape(N * L, dv), o.reshape(N * L, dv), dlse.reshape(N * L, 1)], [], "attn_delta",
                        tile=_div_tile(N * L, DELTA_TILE, 8))
    return dd.reshape(N, L, 1)


def full_attention(q, k, v, scale, q_prescaled):
    qk_mult = 1.0 if q_prescaled else scale * LOG2E
    grad_mult = LN2 if q_prescaled else scale

    @jax.custom_vjp
    def op(q, k, v):
        return _flash_fwd(q, k, v, qk_mult)[0]

    def op_fwd(q, k, v):
        o, lse = _flash_fwd(q, k, v, qk_mult)
        return o, (q, k, v, o, lse)

    def op_bwd(res, do):
        q, k, v, o, lse = res
        N, Lq, _ = q.shape
        dd = _attn_delta(do, o, jnp.zeros_like(lse))
        return tuple(_flash_bwd(q, k, v, do, lse.reshape(N, 1, Lq), dd.reshape(N, 1, Lq), qk_mult, grad_mult))

    op.defvjp(op_fwd, op_bwd)
    return op(q, k, v)


BAND_TILE = 128
BAND_BATCH = 16


def _band_specs(L, d):
    t = min(BAND_TILE, L)
    nt = L // t
    cur = lambda n, i: (n, i, 0)
    prev = lambda n, i: (n, jnp.maximum(i - 1, 0), 0)
    nxt = lambda n, i: (n, jnp.minimum(i + 1, nt - 1), 0)
    return t, nt, cur, prev, nxt


def _band_mask(i, t, L, half):
    a = lax.broadcasted_iota(jnp.int32, (t, 3 * t), 0)
    b = lax.broadcasted_iota(jnp.int32, (t, 3 * t), 1)
    dist = jnp.abs(b - t - a)
    other = i * t - t + b
    mask = (dist <= half) & (other >= 0) & (other < L)
    return dist.astype(F32), mask


def _band_fwd(q, k, v, slope, half):
    N, L, d = q.shape
    t, nt, cur, prev, nxt = _band_specs(L, d)
    nb = min(N, BAND_BATCH)
    scale = d ** -0.5

    def body(sl_ref, q_ref, kp, kc, kn, vp, vc, vn, o_ref, lse_ref):
        i = pl.program_id(1)
        kcat = jnp.concatenate([kp[...], kc[...], kn[...]], axis=1)
        vcat = jnp.concatenate([vp[...], vc[...], vn[...]], axis=1)
        s = lax.dot_general(q_ref[...], kcat, (((2,), (2,)), ((0,), (0,))), preferred_element_type=F32) * scale
        dist, mask = _band_mask(i, t, L, half)
        s = jnp.where(mask[None], s - sl_ref[...] * dist[None], NEG)
        m = jnp.max(s, axis=-1, keepdims=True)
        p = jnp.exp(s - m)
        l = jnp.sum(p, axis=-1, keepdims=True)
        o = lax.dot_general(p.astype(BF16), vcat, (((2,), (1,)), ((0,), (0,))), preferred_element_type=F32)
        o_ref[...] = o / l
        lse_ref[...] = m + jnp.log(l)

    blk = lambda f: pl.BlockSpec((nb, t, d), f)
    return pl.pallas_call(
        body,
        out_shape=[jax.ShapeDtypeStruct((N, L, d), F32), jax.ShapeDtypeStruct((N, L, 1), F32)],
        grid=(N // nb, nt),
        in_specs=[pl.BlockSpec((nb, 1, 1), lambda n, i: (n, 0, 0)), blk(cur),
                  blk(prev), blk(cur), blk(nxt), blk(prev), blk(cur), blk(nxt)],
        out_specs=[blk(cur), pl.BlockSpec((nb, t, 1), cur)],
        compiler_params=_cparams(("parallel", "parallel")),
        name=f"band_fwd_{N}x{L}",
    )(slope, q, k, k, k, v, v, v)


def _band_dq(q, k, v, slope, do, lse, dd, half):
    N, L, d = q.shape
    t, nt, cur, prev, nxt = _band_specs(L, d)
    nb = min(N, BAND_BATCH)
    scale = d ** -0.5

    def body(sl_ref, q_ref, kp, kc, kn, vp, vc, vn, do_ref, lse_ref, dd_ref, dq_ref):
        i = pl.program_id(1)
        kcat = jnp.concatenate([kp[...], kc[...], kn[...]], axis=1)
        vcat = jnp.concatenate([vp[...], vc[...], vn[...]], axis=1)
        s = lax.dot_general(q_ref[...], kcat, (((2,), (2,)), ((0,), (0,))), preferred_element_type=F32) * scale
        dist, mask = _band_mask(i, t, L, half)
        p = jnp.where(mask[None], jnp.exp(s - sl_ref[...] * dist[None] - lse_ref[...]), 0.0)
        dp = lax.dot_general(do_ref[...].astype(BF16), vcat, (((2,), (2,)), ((0,), (0,))),
                             preferred_element_type=F32)
        ds = (p * (dp - dd_ref[...])).astype(BF16)
        dq = lax.dot_general(ds, kcat, (((2,), (1,)), ((0,), (0,))), preferred_element_type=F32)
        dq_ref[...] = (dq * scale).astype(dq_ref.dtype)

    blk = lambda f: pl.BlockSpec((nb, t, d), f)
    col = pl.BlockSpec((nb, t, 1), cur)
    return pl.pallas_call(
        body,
        out_shape=jax.ShapeDtypeStruct(q.shape, q.dtype),
        grid=(N // nb, nt),
        in_specs=[pl.BlockSpec((nb, 1, 1), lambda n, i: (n, 0, 0)), blk(cur),
                  blk(prev), blk(cur), blk(nxt), blk(prev), blk(cur), blk(nxt), blk(cur), col, col],
        out_specs=blk(cur),
        compiler_params=_cparams(("parallel", "parallel")),
        name=f"band_dq_{N}x{L}",
    )(slope, q, k, k, k, v, v, v, do, lse, dd)


def _band_dkv(q, k, v, slope, do, lse_row, dd_row, half):
    N, L, d = q.shape
    t, nt, cur, prev, nxt = _band_specs(L, d)
    nb = min(N, BAND_BATCH)
    scale = d ** -0.5
    rcur = lambda n, i: (n, 0, i)
    rprev = lambda n, i: (n, 0, jnp.maximum(i - 1, 0))
    rnxt = lambda n, i: (n, 0, jnp.minimum(i + 1, nt - 1))

    def body(sl_ref, k_ref, v_ref, qp, qc, qn, dop, doc, don, lp, lc, ln, dp_, dc_, dn_, dk_ref, dv_ref):
        i = pl.program_id(1)
        qcat = jnp.concatenate([qp[...], qc[...], qn[...]], axis=1)
        docat = jnp.concatenate([dop[...], doc[...], don[...]], axis=1).astype(BF16)
        lrow = jnp.concatenate([lp[...], lc[...], ln[...]], axis=2)
        drow = jnp.concatenate([dp_[...], dc_[...], dn_[...]], axis=2)
        st = lax.dot_general(k_ref[...], qcat, (((2,), (2,)), ((0,), (0,))), preferred_element_type=F32) * scale
        dist, mask = _band_mask(i, t, L, half)
        pt = jnp.where(mask[None], jnp.exp(st - sl_ref[...] * dist[None] - lrow), 0.0)
        dv = lax.dot_general(pt.astype(BF16), docat, (((2,), (1,)), ((0,), (0,))), preferred_element_type=F32)
        dpt = lax.dot_general(v_ref[...], docat, (((2,), (2,)), ((0,), (0,))), preferred_element_type=F32)
        dst = (pt * (dpt - drow)).astype(BF16)
        dk = lax.dot_general(dst, qcat, (((2,), (1,)), ((0,), (0,))), preferred_element_type=F32)
        dk_ref[...] = (dk * scale).astype(dk_ref.dtype)
        dv_ref[...] = dv.astype(dv_ref.dtype)

    blk = lambda f: pl.BlockSpec((nb, t, d), f)
    row = lambda f: pl.BlockSpec((nb, 1, t), f)
    return pl.pallas_call(
        body,
        out_shape=[jax.ShapeDtypeStruct(k.shape, k.dtype), jax.ShapeDtypeStruct(v.shape, v.dtype)],
        grid=(N // nb, nt),
        in_specs=[pl.BlockSpec((nb, 1, 1), lambda n, i: (n, 0, 0)), blk(cur), blk(cur),
                  blk(prev), blk(cur), blk(nxt), blk(prev), blk(cur), blk(nxt),
                  row(rprev), row(rcur), row(rnxt), row(rprev), row(rcur), row(rnxt)],
        out_specs=[blk(cur), blk(cur)],
        compiler_params=_cparams(("parallel", "parallel")),
        name=f"band_dkv_{N}x{L}",
    )(slope, k, v, q, q, q, do, do, do, lse_row, lse_row, lse_row, dd_row, dd_row, dd_row)


def band_attention(q, k, v, slope, half):
    @jax.custom_vjp
    def op(q, k, v, slope):
        return tuple(_band_fwd(q, k, v, slope, half))

    def op_fwd(q, k, v, slope):
        o, lse = _band_fwd(q, k, v, slope, half)
        return (o, lse), (q, k, v, slope, o, lse)

    def op_bwd(res, cts):
        q, k, v, slope, o, lse = res
        do, dlse = cts
        N, L, _ = q.shape
        dd = _attn_delta(do, o, dlse)
        dq = _band_dq(q, k, v, slope, do, lse, dd, half)
        dk, dv = _band_dkv(q, k, v, slope, do, lse.reshape(N, 1, L), dd.reshape(N, 1, L), half)
        return dq, dk, dv, jnp.zeros_like(slope)

    op.defvjp(op_fwd, op_bwd)
    return op(q, k, v, slope)


def _conv_taps(width):
    return [CONV_PAD - width // 2 + j for j in range(width)]


def _dwconv_call(up, w, width):
    S, C = up.shape[0] - 2 * CONV_PAD, up.shape[1]
    ch = min(CONV_CHUNK, S)
    offs = _conv_taps(width)

    def body(u_ref, w_ref, o_ref):
        def chunk(ci, carry):
            base = pl.multiple_of(ci * ch, ch)
            win = u_ref[pl.ds(base, ch + 2 * CONV_PAD), :]
            acc = jnp.zeros((ch, LANES), F32)
            for j, off in enumerate(offs):
                acc = acc + win[off:off + ch, :] * w_ref[j:j + 1, :]
            o_ref[pl.ds(base, ch), :] = acc
            return carry

        lax.fori_loop(0, S // ch, chunk, 0)

    return pl.pallas_call(
        body,
        out_shape=jax.ShapeDtypeStruct((S, C), F32),
        grid=(C // LANES,),
        in_specs=[pl.BlockSpec((S + 2 * CONV_PAD, LANES), lambda c: (0, c)),
                  pl.BlockSpec((w.shape[0], LANES), lambda c: (0, c))],
        out_specs=pl.BlockSpec((S, LANES), lambda c: (0, c)),
        compiler_params=_cparams(("parallel",)),
        name=f"dwconv{width}_{S}x{C}",
    )(up, w)


def _dwconv_dw_call(up, g, width, w_rows):
    S, C = g.shape
    ch = min(CONV_CHUNK, S)
    offs = _conv_taps(width)

    def body(u_ref, g_ref, o_ref):
        def chunk(ci, accs):
            base = pl.multiple_of(ci * ch, ch)
            win = u_ref[pl.ds(base, ch + 2 * CONV_PAD), :]
            gt = g_ref[pl.ds(base, ch), :]
            return tuple(a + (gt * win[off:off + ch, :]).reshape(ch // 8, 8, LANES).sum(axis=0)
                         for a, off in zip(accs, offs))

        accs = lax.fori_loop(0, S // ch, chunk, tuple(jnp.zeros((8, LANES), F32) for _ in offs))
        o_ref[...] = jnp.zeros_like(o_ref)
        for j, a in enumerate(accs):
            o_ref[j:j + 1, :] = jnp.sum(a, axis=0, keepdims=True)

    return pl.pallas_call(
        body,
        out_shape=jax.ShapeDtypeStruct((w_rows, C), F32),
        grid=(C // LANES,),
        in_specs=[pl.BlockSpec((S + 2 * CONV_PAD, LANES), lambda c: (0, c)),
                  pl.BlockSpec((S, LANES), lambda c: (0, c))],
        out_specs=pl.BlockSpec((w_rows, LANES), lambda c: (0, c)),
        compiler_params=_cparams(("parallel",)),
        name=f"dwconv{width}_dw_{S}x{C}",
    )(up, g)


def _pad_rows(u):
    return jnp.pad(u, ((CONV_PAD, CONV_PAD), (0, 0)))


def _pad_taps(w):
    return jnp.pad(w, ((0, -w.shape[0] % 8), (0, 0)))


def depthwise_conv(u, w):
    width = w.shape[0]

    @jax.custom_vjp
    def op(u, w):
        return _dwconv_call(_pad_rows(u), _pad_taps(w), width)

    def op_fwd(u, w):
        up = _pad_rows(u)
        return _dwconv_call(up, _pad_taps(w), width), (up, w)

    def op_bwd(res, g):
        up, w = res
        du = _dwconv_call(_pad_rows(g), _pad_taps(w[::-1]), width)
        dw = _dwconv_dw_call(up, g, width, _pad_taps(w).shape[0])[:width]
        return du, dw

    op.defvjp(op_fwd, op_bwd)
    return op(u, w)


def _loss_call(y, target):
    S, D = y.shape
    tile = _row_tile(S, D)

    def body(y_ref, t_ref, o_ref):
        d = y_ref[...] - t_ref[...]
        part = jnp.sum(jnp.sum(d * d, axis=1, keepdims=True), axis=0, keepdims=True) * (0.5 / D)

        @pl.when(pl.program_id(0) == 0)
        def _():
            o_ref[...] = jnp.zeros_like(o_ref)

        o_ref[...] += jnp.broadcast_to(part, o_ref.shape)

    out = pl.pallas_call(
        body,
        out_shape=jax.ShapeDtypeStruct((8, LANES), F32),
        grid=(S // tile,),
        in_specs=[pl.BlockSpec((tile, D), lambda i: (i, 0)), pl.BlockSpec((tile, D), lambda i: (i, 0))],
        out_specs=pl.BlockSpec((8, LANES), lambda i: (0, 0)),
        compiler_params=_cparams(("arbitrary",)),
        name="loss_head",
    )(y, target)
    return out[0, 0]


@jax.custom_vjp
def loss_head(y, target):
    return _loss_call(y, target)


def _loss_fwd(y, target):
    return _loss_call(y, target), (y, target)


def _loss_bwd(res, g):
    y, target = res
    inv_d = 1.0 / y.shape[1]

    def fn(y_t, t_t, g_p):
        return ((y_t - t_t) * (g_p * inv_d),)

    (dy,) = _rw_forward(fn, [y, target], [g.reshape(1, 1).astype(F32)], "loss_head_bwd")
    return dy, jnp.zeros_like(target)


loss_head.defvjp(_loss_fwd, _loss_bwd)


ANY = pl.BlockSpec(memory_space=pl.ANY)


AG_ICI_PIECES = 4
AG_D2D_PIECES = 2
D2D_PIECES = 16
ICI_PIECES = 4


def _with_own_block(received, own, slot):
    return lax.dynamic_update_slice_in_dim(received, own[None], slot, axis=0)


def all_gather_chips(mine, name):
    R, C = mine.shape
    ni, nf = AG_ICI_PIECES, AG_D2D_PIECES
    rows = R // (2 * ni * nf)
    n_ici, n_all = 3 * ni, 3 * ni + 3 * ni * nf

    def body(in_ref, out_ref, send_sems, recv_sems):
        x, y, c = lax.axis_index("x"), lax.axis_index("y"), lax.axis_index("c")
        chips = [(1 - x, y), (x, 1 - y), (1 - x, 1 - y)]

        def copy(sem, src, dst, to):
            return pltpu.make_async_remote_copy(src_ref=src, dst_ref=dst, send_sem=send_sems.at[sem],
                                                recv_sem=recv_sems.at[sem], device_id=to, device_id_type=MESH)

        sends = []
        for k, (px, py) in enumerate(chips):
            for a in range(ni):
                sends.append(copy(k * ni + a, in_ref.at[c, a], out_ref.at[2 * x + y, c, a], (px, py, c)))
                sends[-1].start()
        for k, (px, py) in enumerate(chips):
            for a in range(ni):
                landed = out_ref.at[2 * px + py, c, a]
                copy(k * ni + a, in_ref.at[c, a], landed, (px, py, c)).wait_recv()
                for b in range(nf):
                    sends.append(copy(n_ici + (k * ni + a) * nf + b, landed.at[b], landed.at[b], (x, y, 1 - c)))
                    sends[-1].start()
        for k, (px, py) in enumerate(chips):
            for a in range(ni):
                for b in range(nf):
                    theirs = out_ref.at[2 * px + py, 1 - c, a, b]
                    copy(n_ici + (k * ni + a) * nf + b, theirs, theirs, (x, y, 1 - c)).wait_recv()
        for cp in sends:
            cp.wait_send()

    out = pl.pallas_call(
        body,
        out_shape=jax.ShapeDtypeStruct((N_CHIPS, 2, ni, nf, rows, C), mine.dtype),
        in_specs=[ANY],
        out_specs=ANY,
        scratch_shapes=[pltpu.SemaphoreType.DMA((n_all,)), pltpu.SemaphoreType.DMA((n_all,))],
        name=name,
    )(mine.reshape(2, ni, nf, rows, C))
    return _with_own_block(out.reshape(N_CHIPS, R, C), mine, 2 * lax.axis_index("x") + lax.axis_index("y"))


def swap_halves_with_sibling(parts):
    n_s, _, H, C = parts.shape
    n = ICI_PIECES

    def body(in_ref, out_ref, send_sems, recv_sems):
        x, y, c = lax.axis_index("x"), lax.axis_index("y"), lax.axis_index("c")

        def copy(s, b, half):
            return pltpu.make_async_remote_copy(
                src_ref=in_ref.at[s, half, b], dst_ref=out_ref.at[s, b], send_sem=send_sems.at[s * n + b],
                recv_sem=recv_sems.at[s * n + b], device_id=(x, y, 1 - c), device_id_type=MESH)

        sends = [copy(s, b, 1 - c) for s in range(n_s) for b in range(n)]
        for cp in sends:
            cp.start()
        for s in range(n_s):
            for b in range(n):
                copy(s, b, 1 - c).wait_recv()
        for cp in sends:
            cp.wait_send()

    out = pl.pallas_call(
        body,
        out_shape=jax.ShapeDtypeStruct((n_s, n, H // n, C), parts.dtype),
        in_specs=[ANY],
        out_specs=ANY,
        scratch_shapes=[pltpu.SemaphoreType.DMA((n_s * n,)), pltpu.SemaphoreType.DMA((n_s * n,))],
        name="grad_swap_halves",
    )(parts.reshape(n_s, 2, n, H // n, C))
    return out.reshape(n_s, H, C)


def add_pairs(parts, theirs, core):
    n_s, _, H, C = parts.shape
    tile = _div_tile(H, 512, 16)

    def body(c_ref, a_ref, b_ref, o_ref):
        o_ref[...] = (a_ref[...].astype(F32) + b_ref[...].astype(F32)).astype(o_ref.dtype)

    return pl.pallas_call(
        body,
        out_shape=jax.ShapeDtypeStruct((n_s, 1, H, C), parts.dtype),
        grid_spec=pltpu.PrefetchScalarGridSpec(
            num_scalar_prefetch=1,
            grid=(n_s, H // tile),
            in_specs=[pl.BlockSpec((1, 1, tile, C), lambda s, i, c_ref: (s, c_ref[0], i, 0)),
                      pl.BlockSpec((1, 1, tile, C), lambda s, i, c_ref: (s, 0, i, 0))],
            out_specs=pl.BlockSpec((1, 1, tile, C), lambda s, i, c_ref: (s, 0, i, 0)),
        ),
        compiler_params=_cparams(("parallel", "parallel")),
        name="grad_add_pairs",
    )(core.reshape(1).astype(jnp.int32), parts, theirs.reshape(n_s, 1, H, C)).reshape(n_s, H, C)


def scatter_to_chips(parts):
    n_s, H, C = parts.shape
    n = ICI_PIECES

    def body(in_ref, out_ref, send_sems, recv_sems):
        x, y, c = lax.axis_index("x"), lax.axis_index("y"), lax.axis_index("c")
        chips = [(1 - x, y), (x, 1 - y), (1 - x, 1 - y)]

        def copy(k, b, src_slot, dst_slot, to):
            return pltpu.make_async_remote_copy(
                src_ref=in_ref.at[src_slot, b], dst_ref=out_ref.at[dst_slot, b], send_sem=send_sems.at[k * n + b],
                recv_sem=recv_sems.at[k * n + b], device_id=to, device_id_type=MESH)

        sends = [copy(k, b, 2 * px + py, 2 * x + y, (px, py, c)) for k, (px, py) in enumerate(chips) for b in range(n)]
        for cp in sends:
            cp.start()
        for k, (px, py) in enumerate(chips):
            for b in range(n):
                copy(k, b, 2 * x + y, 2 * px + py, (px, py, c)).wait_recv()
        for cp in sends:
            cp.wait_send()

    out = pl.pallas_call(
        body,
        out_shape=jax.ShapeDtypeStruct((n_s, n, H // n, C), parts.dtype),
        in_specs=[ANY],
        out_specs=ANY,
        scratch_shapes=[pltpu.SemaphoreType.DMA((3 * n,)), pltpu.SemaphoreType.DMA((3 * n,))],
        name="grad_scatter",
    )(parts.reshape(n_s, n, H // n, C))
    slot = 2 * lax.axis_index("x") + lax.axis_index("y")
    return _with_own_block(out.reshape(n_s, H, C), lax.dynamic_index_in_dim(parts, slot, axis=0, keepdims=False), slot)


def exchange_with_sibling(mine):
    H, C = mine.shape
    n = D2D_PIECES

    def body(in_ref, out_ref, send_sems, recv_sems):
        x, y, c = lax.axis_index("x"), lax.axis_index("y"), lax.axis_index("c")

        def copy(b, half):
            return pltpu.make_async_remote_copy(
                src_ref=in_ref.at[b], dst_ref=out_ref.at[half, b], send_sem=send_sems.at[b], recv_sem=recv_sems.at[b],
                device_id=(x, y, 1 - c), device_id_type=MESH)

        sends = [copy(b, c) for b in range(n)]
        for cp in sends:
            cp.start()
        for b in range(n):
            copy(b, 1 - c).wait_recv()
        for cp in sends:
            cp.wait_send()

    out = pl.pallas_call(
        body,
        out_shape=jax.ShapeDtypeStruct((2, n, H // n, C), mine.dtype),
        in_specs=[ANY],
        out_specs=ANY,
        scratch_shapes=[pltpu.SemaphoreType.DMA((n,)), pltpu.SemaphoreType.DMA((n,))],
        name="grad_sibling_exchange",
    )(mine.reshape(n, H // n, C))
    return _with_own_block(out.reshape(2, H, C), mine, lax.axis_index("c"))


def sum_contributions(recv):
    n, H, C = recv.shape
    tile = _div_tile(H, 256, 16)

    def body(r_ref, o_ref):
        acc = r_ref[0].astype(F32)
        for j in range(1, n):
            acc = acc + r_ref[j].astype(F32)
        o_ref[...] = acc

    return pl.pallas_call(
        body,
        out_shape=jax.ShapeDtypeStruct((H, C), F32),
        grid=(H // tile,),
        in_specs=[pl.BlockSpec((n, tile, C), lambda i: (0, i, 0))],
        out_specs=pl.BlockSpec((tile, C), lambda i: (i, 0)),
        compiler_params=_cparams(("parallel",)),
        name="grad_sum",
    )(recv)


def adamw(w, g, m, v):
    shape = w.shape
    cols = shape[-1]
    rows = math.prod(shape[:-1])
    tile = _div_tile(rows, max(8, 256 * 1024 // cols // 8 * 8), 8)
    c1 = 1.0 - ADAM_B1 ** ADAM_STEP
    c2 = 1.0 - ADAM_B2 ** ADAM_STEP

    def body(w_ref, g_ref, m_ref, v_ref, d_ref, nm_ref, nv_ref):
        g_ = g_ref[...]
        nm = ADAM_B1 * m_ref[...] + (1.0 - ADAM_B1) * g_
        nv = ADAM_B2 * v_ref[...] + (1.0 - ADAM_B2) * (g_ * g_)
        d_ref[...] = -ADAM_LR * ((nm / c1) / (jnp.sqrt(nv / c2) + ADAM_EPS) + ADAM_WD * w_ref[...])
        nm_ref[...] = nm
        nv_ref[...] = nv

    spec = pl.BlockSpec((tile, cols), lambda i: (i, 0))
    outs = pl.pallas_call(
        body,
        out_shape=[jax.ShapeDtypeStruct((rows, cols), F32)] * 3,
        grid=(rows // tile,),
        in_specs=[spec] * 4,
        out_specs=[spec] * 3,
        compiler_params=_cparams(("parallel",)),
        name=f"adamw_{rows}x{cols}",
    )(*[t.reshape(rows, cols) for t in (w, g, m, v)])
    return [o.reshape(shape) for o in outs]


def _leaves(name, arr):
    n_lead = WEIGHT_LAYOUT[name][0]
    lead = arr.shape[:n_lead]
    flat = arr.reshape((-1,) + arr.shape[n_lead:])
    return [flat[i] for i in range(math.prod(lead))]


def _pad_pack_rows(buf):
    return jnp.pad(buf, ((0, -buf.shape[0] % PACK_ROW_MULT), (0, 0)))


def _pack_flat(pieces, dtype, cols):
    flat = jnp.concatenate([p.reshape(-1).astype(dtype) for p in pieces])
    flat = jnp.pad(flat, (0, -flat.shape[0] % (cols * PACK_ROW_MULT)))
    return flat.reshape(-1, cols)


def _unpack_flat(buf, shapes):
    lead = buf.shape[:-2]
    flat = buf.reshape(lead + (-1,))
    out, off = [], 0
    for shp in shapes:
        n = math.prod(shp)
        out.append(lax.slice_in_dim(flat, off, off + n, axis=len(lead)).reshape(lead + tuple(shp)))
        off += n
    return out


def _pack_slabs(slabs, dtype):
    return _pad_pack_rows(jnp.concatenate([s.astype(dtype) for s in slabs], axis=0))


def _local_slabs(name, shard):
    leaves = _leaves(name, shard)
    return [leaf.T for leaf in leaves] if name in SLAB_TRANSPOSED else leaves


def _chip_slice(leaf, ax, s):
    if ax is None:
        return leaf
    w = leaf.shape[ax] // N_CHIPS
    return lax.slice_in_dim(leaf, s * w, (s + 1) * w, axis=ax)


def gather_weights(shards, cols):
    slabs = [s for n in SLAB_NAMES for s in _local_slabs(n, shards[n])]
    slab_owner = [n for n in SLAB_NAMES for _ in _leaves(n, shards[n])]
    slab_buf = _pack_slabs(slabs, BF16)
    misc = {}
    for dtype in (BF16, F32):
        names = [n for n in MISC_NAMES if WEIGHT_LAYOUT[n][1] is not None and (n in F32_GATHER) == (dtype == F32)]
        pieces = [leaf for n in names for leaf in _leaves(n, shards[n])]
        owners = [n for n in names for _ in _leaves(n, shards[n])]
        buf = _pack_flat(pieces, dtype, cols)
        if dtype == BF16:
            gathered = all_gather_chips(jnp.concatenate([slab_buf, buf], axis=0), "weights_all_gather_bf16")
            slab_part, gathered = gathered[:, :slab_buf.shape[0]], gathered[:, slab_buf.shape[0]:]
        else:
            gathered = all_gather_chips(buf, "weights_all_gather_f32")
        for n, blk in zip(owners, _unpack_flat(gathered, [p.shape for p in pieces])):
            ax = WEIGHT_LAYOUT[n][1]
            misc.setdefault(n, []).append(jnp.concatenate([blk[s] for s in range(N_CHIPS)], axis=ax).astype(F32))
    for n in MISC_NAMES:
        if WEIGHT_LAYOUT[n][1] is None:
            misc[n] = [leaf.astype(F32) for leaf in _leaves(n, shards[n])]
    slab_full, off = {}, 0
    for n, s in zip(slab_owner, slabs):
        rows = s.shape[0]
        slab_full.setdefault(n, []).append(slab_part[:, off:off + rows].reshape(N_CHIPS * rows, cols))
        off += rows
    return slab_full, misc


def reduce_gradients(g_slab, g_misc, shards, cols):
    per_chip = []
    for s in range(N_CHIPS):
        slabs = []
        for n in SLAB_NAMES:
            for leaf in g_slab[n]:
                slabs.append(_chip_slice(leaf, 0, s))
        pieces = [_chip_slice(leaf, WEIGHT_LAYOUT[n][1], s) for n in MISC_NAMES for leaf in g_misc[n]]
        per_chip.append(jnp.concatenate([_pack_slabs(slabs, BF16), _pack_flat(pieces, BF16, cols)], axis=0))
    parts = jnp.stack(per_chip)
    R = parts.shape[1]
    parts = parts.reshape(N_CHIPS, 2, R // 2, cols)
    chip_parts = add_pairs(parts, swap_halves_with_sibling(parts), lax.axis_index("c"))
    recv = scatter_to_chips(chip_parts)
    red = exchange_with_sibling(sum_contributions(recv)).reshape(R, cols)

    out, off = {}, 0
    for n in SLAB_NAMES:
        leaves = []
        for slab in _local_slabs(n, shards[n]):
            rows = slab.shape[0]
            g = red[off:off + rows]
            leaves.append(g.T if n in SLAB_TRANSPOSED else g)
            off += rows
        out[n] = jnp.stack(leaves).reshape(shards[n].shape)
    off += -off % PACK_ROW_MULT
    misc_shapes = [shards[n].shape for n in MISC_NAMES]
    for n, g in zip(MISC_NAMES, _unpack_flat(red[off:], misc_shapes)):
        out[n] = g
    return out


def _silu(x):
    return x * jax.nn.sigmoid(x)


def _heads_first(t, heads):
    S = t.shape[0]
    return t.reshape(S, heads, -1).transpose(1, 0, 2)


def _heads_last(t):
    return t.transpose(1, 0, 2).reshape(t.shape[1], -1)


class SlabWeights:
    def __init__(self, values, deltas):
        self.values, self.deltas = values, deltas

    def matmul(self, a, name, leaf, out_dtype):
        return mm_slab(a, self.values[name][leaf], self.deltas[name][leaf], name in SLAB_TRANSPOSED, out_dtype)


def swiglu_half_step(x, h, slab, leaf, g_post, g_next):
    a = slab.matmul(h, 'ffn_w_gate', leaf, BF16)
    b = slab.matmul(h, 'ffn_w_up', leaf, BF16)
    (u,) = rowwise(lambda a_, b_: ((_silu(a_.astype(F32)) * b_.astype(F32)).astype(BF16),), [a, b], [], "swiglu_act",
                   fwd_tile=512)
    y = slab.matmul(u, 'ffn_w_down', leaf, F32)
    return residual_norm(x, y, 0.5, g_post, g_next)


def residual_norm(x, y, coef, g_post, g_next):
    if g_next is None:
        (xn,) = rowwise(lambda x_, y_, g1: (x_ + coef * _rms(y_, g1),), [x, y], [g_post], "residual")
        return xn, None

    def fn(x_, y_, g1, g2):
        xn = x_ + coef * _rms(y_, g1)
        return xn, _rms(xn, g2).astype(BF16)

    return rowwise(fn, [x, y], [g_post, g_next], "residual_norm", fwd_tile=1024)


def memory_attention(q_mem, memn, w_kv):
    kv = mm(memn, w_kv, BF16)
    n_mem = kv.shape[0]
    kv = kv.reshape(n_mem, 2, MEM_HEADS, HEAD_DIM).transpose(1, 2, 0, 3)
    o = full_attention(_heads_first(q_mem.astype(BF16), MEM_HEADS), kv[0], kv[1], HEAD_DIM ** -0.5, False)
    return _heads_last(o)


def mla_mixer(h, cos, sin, w_in, q_g, kv_g, w_uq, w_ukv):
    S = h.shape[0]
    a_in = MLA_Q_LORA + MLA_KV_LORA + MLA_ROPE + MEM_WIDTH
    z = mm(h, jnp.pad(w_in, ((0, 0), (0, -a_in % LANES))), F32)
    o1, o2, o3 = MLA_Q_LORA, MLA_Q_LORA + MLA_KV_LORA, MLA_Q_LORA + MLA_KV_LORA + MLA_ROPE
    c_q, c_kv, k_r, q_mem = z[:, :o1], z[:, o1:o2], z[:, o2:o3], z[:, o3:a_in]
    cqn, ckvn = rowwise(lambda a, b, ga, gb: (_rms(a, ga).astype(BF16), _rms(b, gb).astype(BF16)),
                        [c_q, c_kv], [q_g, kv_g], "mla_lora_norm")
    q = mm(cqn, w_uq, F32).reshape(S, MLA_HEADS, MLA_NOPE + MLA_ROPE)
    kv = mm(ckvn, w_ukv, BF16).reshape(S, MLA_HEADS, MLA_NOPE + MLA_V)
    half = MLA_ROPE // 2
    qn = q[:, :, :MLA_NOPE].reshape(S, MLA_HEADS * MLA_NOPE)
    q1 = q[:, :, MLA_NOPE:MLA_NOPE + half].reshape(S, MLA_HEADS * half)
    q2 = q[:, :, MLA_NOPE + half:].reshape(S, MLA_HEADS * half)
    cos_h, sin_h = jnp.tile(cos, (1, MLA_HEADS)), jnp.tile(sin, (1, MLA_HEADS))
    scale = (MLA_NOPE + MLA_ROPE) ** -0.5
    qm = scale * LOG2E

    def rope(qn_, q1_, q2_, k1_, k2_, ch, sh, c1, s1):
        return ((qn_ * qm).astype(BF16), ((q1_ * ch - q2_ * sh) * qm).astype(BF16),
                ((q1_ * sh + q2_ * ch) * qm).astype(BF16),
                (k1_ * c1 - k2_ * s1).astype(BF16), (k1_ * s1 + k2_ * c1).astype(BF16))

    qns, qr1, qr2, kr1, kr2 = rowwise(rope, [qn, q1, q2, k_r[:, :half], k_r[:, half:], cos_h, sin_h, cos, sin], [],
                                      "mla_rope", n_const=4)
    q_cat = jnp.concatenate([qns.reshape(S, MLA_HEADS, MLA_NOPE), qr1.reshape(S, MLA_HEADS, half),
                             qr2.reshape(S, MLA_HEADS, half)], axis=-1)
    k_rope = jnp.concatenate([kr1, kr2], axis=-1)
    k_cat = jnp.concatenate([kv[:, :, :MLA_NOPE], jnp.broadcast_to(k_rope[:, None, :], (S, MLA_HEADS, MLA_ROPE))],
                            axis=-1)
    o = full_attention(q_cat.transpose(1, 0, 2), k_cat.transpose(1, 0, 2), kv[:, :, MLA_NOPE:].transpose(1, 0, 2),
                       scale, True)
    return _heads_last(o), q_mem


def dilated_mixer(h, slab):
    S = h.shape[0]
    n_g = len(DIL_GROUPS)
    qkv_w = n_g * 3 * DIL_HEADS * HEAD_DIM
    z = slab.matmul(h, 'b_w_in', 0, BF16)
    zd = z[:, :qkv_w].reshape(S, n_g, 3, DIL_HEADS, HEAD_DIM)
    q_mem = z[:, qkv_w:]
    slopes = (2.0 ** (-ALIBI_MAX * (jnp.arange(n_g * DIL_HEADS, dtype=F32) + 1.0) / (n_g * DIL_HEADS)))
    slopes = slopes.reshape(n_g, DIL_HEADS)
    outs, lses = [], []
    for g, (window, dil) in enumerate(DIL_GROUPS):
        L = S // dil

        def sub(t):
            return t.reshape(L, dil, DIL_HEADS, HEAD_DIM).transpose(1, 2, 0, 3).reshape(dil * DIL_HEADS, L, HEAD_DIM)

        slope = (jnp.tile(slopes[g], dil) * dil).reshape(dil * DIL_HEADS, 1, 1)
        o, lse = band_attention(sub(zd[:, g, 0]), sub(zd[:, g, 1]), sub(zd[:, g, 2]), slope, window // (2 * dil))
        outs.append(o.reshape(dil, DIL_HEADS, L, HEAD_DIM).transpose(2, 0, 1, 3).reshape(S * DIL_HEADS, HEAD_DIM))
        lses.append(lse.reshape(dil, DIL_HEADS, L).transpose(2, 0, 1).reshape(S * DIL_HEADS, 1))

    def merge(o0, o1, o2, l0, l1, l2):
        m = jnp.maximum(jnp.maximum(l0, l1), l2)
        e0, e1, e2 = jnp.exp(l0 - m), jnp.exp(l1 - m), jnp.exp(l2 - m)
        return (((e0 * o0 + e1 * o1 + e2 * o2) / (e0 + e1 + e2)).astype(BF16),)

    (o,) = rowwise(merge, outs + lses, [], "dilated_merge")
    return o.reshape(S, DIL_HEADS * HEAD_DIM), q_mem


def conformer_conv_mixer(h, slab, conv_w, conv_b, ln_g, ln_b):
    z = slab.matmul(h, 'c_w_in', 0, F32)
    a, gate, q_mem = z[:, :CONV_CH], z[:, CONV_CH:2 * CONV_CH], z[:, 2 * CONV_CH:]
    (u,) = rowwise(lambda a_, g_: (a_ * jax.nn.sigmoid(g_),), [a, gate], [], "conformer_glu")
    u = depthwise_conv(u, conv_w)

    def post(u_, b, g, beta):
        t = u_ + b
        mu = jnp.mean(t, axis=-1, keepdims=True)
        var = jnp.mean(jnp.square(t - mu), axis=-1, keepdims=True)
        return (_silu((t - mu) * lax.rsqrt(var + EPS) * g + beta).astype(BF16),)

    (o,) = rowwise(post, [u], [conv_b.reshape(1, -1), ln_g.reshape(1, -1), ln_b.reshape(1, -1)], "conformer_post")
    return o, q_mem


def short_conv_mixer(h, slab, conv_w):
    z = slab.matmul(h, 'd_w_in', 0, F32)
    bg, cg, hx, q_mem = z[:, :SC_CH], z[:, SC_CH:2 * SC_CH], z[:, 2 * SC_CH:3 * SC_CH], z[:, 3 * SC_CH:]
    (p,) = rowwise(lambda c_, h_: (c_ * h_,), [cg, hx], [], "shortconv_pre")
    cv = depthwise_conv(p, conv_w)
    (o,) = rowwise(lambda b_, c_: ((b_ * c_).astype(BF16),), [bg, cv], [], "shortconv_post")
    return o, q_mem


def local_loss(deltas, W, x, slab_values, mem, cos, sin, target):
    slab = SlabWeights(slab_values, deltas)
    norm_g = W['norm_g'][0]

    def gain(i, k):
        return norm_g[i, k].reshape(1, -1)

    (h,) = rowwise(lambda x_, g: (_rms(x_, g).astype(BF16),), [x], [gain(0, 0)], "input_norm")
    for i in range(DEPTH):
        x, h = swiglu_half_step(x, h, slab, 2 * i, gain(i, 1), gain(i, 2))
        if i == 0:
            o, q_mem = mla_mixer(h, cos, sin, W['a_w_in'][0], W['a_q_norm'][0].reshape(1, -1),
                                 W['a_kv_norm'][0].reshape(1, -1), W['a_w_uq'][0], W['a_w_ukv'][0])
            out_name = 'a_w_out'
        elif i == 1:
            o, q_mem = dilated_mixer(h, slab)
            out_name = None
        elif i == 2:
            o, q_mem = conformer_conv_mixer(h, slab, W['c_conv_w'][0], W['c_conv_b'][0], W['c_ln_g'][0],
                                            W['c_ln_b'][0])
            out_name = 'c_w_out'
        else:
            o, q_mem = short_conv_mixer(h, slab, W['d_conv_w'][0])
            out_name = 'd_w_out'
        (memn,) = rowwise(lambda m_, g: (_rms(m_, g).astype(BF16),), [mem], [gain(i, 6)], "memory_norm")
        mo = memory_attention(q_mem, memn, W['mem_w_kv'][i])
        cat = jnp.concatenate([o, mo], axis=-1)
        y = mm(cat, W['b_w_out'][0], F32) if out_name is None else slab.matmul(cat, out_name, 0, F32)
        x, h = residual_norm(x, y, 1.0, gain(i, 3), gain(i, 4))
        x, h = swiglu_half_step(x, h, slab, 2 * i + 1, gain(i, 5), gain(i + 1, 0) if i + 1 < DEPTH else None)
    return loss_head(x, target)


def kernel(x, mem, positions, norm_g, ffn_w_gate, ffn_w_up, ffn_w_down, mem_w_kv, a_w_in, a_q_norm, a_kv_norm, a_w_uq, a_w_ukv, a_w_out, b_w_in, b_w_out, c_w_in, c_conv_w, c_conv_b, c_ln_g, c_ln_b, c_w_out, d_w_in, d_conv_w, d_w_out, loss_target, m_norm_g, m_ffn_w_gate, m_ffn_w_up, m_ffn_w_down, m_mem_w_kv, m_a_w_in, m_a_q_norm, m_a_kv_norm, m_a_w_uq, m_a_w_ukv, m_a_w_out, m_b_w_in, m_b_w_out, m_c_w_in, m_c_conv_w, m_c_conv_b, m_c_ln_g, m_c_ln_b, m_c_w_out, m_d_w_in, m_d_conv_w, m_d_w_out, v_norm_g, v_ffn_w_gate, v_ffn_w_up, v_ffn_w_down, v_mem_w_kv, v_a_w_in, v_a_q_norm, v_a_kv_norm, v_a_w_uq, v_a_w_ukv, v_a_w_out, v_b_w_in, v_b_w_out, v_c_w_in, v_c_conv_w, v_c_conv_b, v_c_ln_g, v_c_ln_b, v_c_w_out, v_d_w_in, v_d_conv_w, v_d_w_out):
    w_in = dict(zip(WEIGHT_NAMES, (norm_g, ffn_w_gate, ffn_w_up, ffn_w_down, mem_w_kv, a_w_in, a_q_norm, a_kv_norm, a_w_uq, a_w_ukv, a_w_out, b_w_in, b_w_out, c_w_in, c_conv_w, c_conv_b, c_ln_g, c_ln_b, c_w_out, d_w_in, d_conv_w, d_w_out)))
    m_in = dict(zip(WEIGHT_NAMES, (m_norm_g, m_ffn_w_gate, m_ffn_w_up, m_ffn_w_down, m_mem_w_kv, m_a_w_in, m_a_q_norm, m_a_kv_norm, m_a_w_uq, m_a_w_ukv, m_a_w_out, m_b_w_in, m_b_w_out, m_c_w_in, m_c_conv_w, m_c_conv_b, m_c_ln_g, m_c_ln_b, m_c_w_out, m_d_w_in, m_d_conv_w, m_d_w_out)))
    v_in = dict(zip(WEIGHT_NAMES, (v_norm_g, v_ffn_w_gate, v_ffn_w_up, v_ffn_w_down, v_mem_w_kv, v_a_w_in, v_a_q_norm, v_a_kv_norm, v_a_w_uq, v_a_w_ukv, v_a_w_out, v_b_w_in, v_b_w_out, v_c_w_in, v_c_conv_w, v_c_conv_b, v_c_ln_g, v_c_ln_b, v_c_w_out, v_d_w_in, v_d_conv_w, v_d_w_out)))

    d_model = x.shape[-1]
    slab_values, W = gather_weights(w_in, d_model)
    deltas = {n: [jnp.zeros(leaf.shape, BF16) for leaf in leaves] for n, leaves in slab_values.items()}
    half = MLA_ROPE // 2
    inv = ROPE_THETA ** (-jnp.arange(half, dtype=F32) / half)
    ang = positions[0].astype(F32)[:, None] * inv
    loss, (g_slab, g_misc, gx) = jax.value_and_grad(local_loss, argnums=(0, 1, 2))(
        deltas, W, x[0], slab_values, mem[0], jnp.cos(ang), jnp.sin(ang), loss_target[0])
    loss = lax.psum(loss, ("x", "y", "c"))

    grads = reduce_gradients(g_slab, g_misc, w_in, d_model)
    steps = {n: adamw(w_in[n], grads[n], m_in[n], v_in[n]) for n in WEIGHT_NAMES}
    return (loss, gx[None], *[grads[n] for n in WEIGHT_NAMES], *[steps[n][0] for n in WEIGHT_NAMES],
            *[steps[n][1] for n in WEIGHT_NAMES], *[steps[n][2] for n in WEIGHT_NAMES])
```

```python
import functools
import math

import jax
import jax.numpy as jnp
from jax import lax
from jax.experimental import pallas as pl
from jax.experimental.pallas import tpu as pltpu

F32 = jnp.float32
BF16 = jnp.bfloat16
MESH = pl.DeviceIdType.MESH

VMEM_LIMIT_BYTES = 56 * 1024 * 1024
LANES = 128
PACK_ROW_MULT = 512
ROW_TILE_ELEMS = 768 * 1024

HEAD_DIM = 64
MEM_HEADS = 4
MEM_WIDTH = MEM_HEADS * HEAD_DIM
MLA_HEADS = 12
MLA_Q_LORA = 384
MLA_KV_LORA = 256
MLA_NOPE = 64
MLA_ROPE = 32
MLA_V = 64
ROPE_THETA = 10000.0
DIL_GROUPS = ((128, 1), (512, 4), (2048, 16))
DIL_HEADS = 8
ALIBI_MAX = 8.0
CONV_CH = 768
CONV_WIDTH = 31
SC_CH = 768
SC_WIDTH = 3
CONV_PAD = 16
CONV_CHUNK = 256
EPS = 1e-6
NEG = -1e30
DEPTH = 4
LOG2E = 1.4426950408889634
LN2 = 0.6931471805599453

ADAM_LR = 0.001
ADAM_B1 = 0.9
ADAM_B2 = 0.999
ADAM_EPS = 1e-08
ADAM_WD = 0.01
ADAM_STEP = 10

WEIGHT_NAMES = ['norm_g', 'ffn_w_gate', 'ffn_w_up', 'ffn_w_down', 'mem_w_kv', 'a_w_in', 'a_q_norm', 'a_kv_norm',
                'a_w_uq', 'a_w_ukv', 'a_w_out', 'b_w_in', 'b_w_out', 'c_w_in', 'c_conv_w', 'c_conv_b', 'c_ln_g',
                'c_ln_b', 'c_w_out', 'd_w_in', 'd_conv_w', 'd_w_out']
WEIGHT_LAYOUT = {
    'norm_g': (0, 2), 'ffn_w_gate': (2, 1), 'ffn_w_up': (2, 1), 'ffn_w_down': (2, 0), 'mem_w_kv': (1, 0),
    'a_w_in': (1, 0), 'a_q_norm': (1, None), 'a_kv_norm': (1, None), 'a_w_uq': (1, 1), 'a_w_ukv': (1, 1),
    'a_w_out': (1, 0), 'b_w_in': (1, 1), 'b_w_out': (1, 1), 'c_w_in': (1, 1), 'c_conv_w': (1, 1),
    'c_conv_b': (1, 0), 'c_ln_g': (1, 0), 'c_ln_b': (1, 0), 'c_w_out': (1, 0), 'd_w_in': (1, 1),
    'd_conv_w': (1, 1), 'd_w_out': (1, 0),
}
SLAB_TRANSPOSED = ('ffn_w_gate', 'ffn_w_up', 'b_w_in', 'c_w_in', 'd_w_in')
SLAB_ROWS = ('ffn_w_down', 'a_w_out', 'c_w_out', 'd_w_out')
SLAB_NAMES = [n for n in WEIGHT_NAMES if n in SLAB_TRANSPOSED or n in SLAB_ROWS]
MISC_NAMES = [n for n in WEIGHT_NAMES if n not in SLAB_NAMES]
F32_GATHER = ('norm_g', 'c_conv_w', 'c_conv_b', 'c_ln_g', 'c_ln_b', 'd_conv_w')
N_CHIPS = 4


def _cparams(semantics):
    return pltpu.CompilerParams(dimension_semantics=semantics, vmem_limit_bytes=VMEM_LIMIT_BYTES)


def _div_tile(n, cap, mult):
    if n <= cap:
        return n
    for d in range(cap - cap % mult, 0, -mult):
        if n % d == 0:
            return d
    raise ValueError(f"no tile for {n} (cap {cap}, multiple of {mult})")


def _matmul(a, b, mode, out_dtype):
    if mode == 'nn':
        (M, K), N = a.shape, b.shape[1]
    elif mode == 'nt':
        (M, K), N = a.shape, b.shape[0]
    else:
        (K, M), N = a.shape, b.shape[1]
    if mode == 'tn':
        tm, tn, tk = _div_tile(M, 2816, LANES), _div_tile(N, 1536, LANES), _div_tile(K, 1024, 16)
    else:
        tm, tn, tk = _div_tile(M, 512, 16), _div_tile(N, 2816, LANES), _div_tile(K, 2816, LANES)
    nk = K // tk
    if mode == 'nn':
        a_spec = pl.BlockSpec((tm, tk), lambda i, j, k: (i, k))
        b_spec = pl.BlockSpec((tk, tn), lambda i, j, k: (k, j))
        dims = (((1,), (0,)), ((), ()))
    elif mode == 'nt':
        a_spec = pl.BlockSpec((tm, tk), lambda i, j, k: (i, k))
        b_spec = pl.BlockSpec((tn, tk), lambda i, j, k: (j, k))
        dims = (((1,), (1,)), ((), ()))
    else:
        a_spec = pl.BlockSpec((tk, tm), lambda i, j, k: (k, i))
        b_spec = pl.BlockSpec((tk, tn), lambda i, j, k: (k, j))
        dims = (((0,), (0,)), ((), ()))

    def body(a_ref, b_ref, o_ref, *acc):
        r = lax.dot_general(a_ref[...].astype(BF16), b_ref[...].astype(BF16), dims, preferred_element_type=F32)
        if nk == 1:
            o_ref[...] = r.astype(o_ref.dtype)
        else:
            k = pl.program_id(2)

            @pl.when(k == 0)
            def _():
                acc[0][...] = r

            @pl.when(k > 0)
            def _():
                acc[0][...] += r

            @pl.when(k == nk - 1)
            def _():
                o_ref[...] = acc[0][...].astype(o_ref.dtype)

    return pl.pallas_call(
        body,
        out_shape=jax.ShapeDtypeStruct((M, N), out_dtype),
        grid=(M // tm, N // tn, nk),
        in_specs=[a_spec, b_spec],
        out_specs=pl.BlockSpec((tm, tn), lambda i, j, k: (i, j)),
        scratch_shapes=[pltpu.VMEM((tm, tn), F32)] if nk > 1 else [],
        compiler_params=_cparams(("parallel", "parallel", "arbitrary")),
        name=f"mm_{mode}_{M}x{K}x{N}",
    )(a, b)


@functools.partial(jax.custom_vjp, nondiff_argnums=(2,))
def mm(a, w, out_dtype):
    return _matmul(a, w.astype(BF16), 'nn', out_dtype)


def _mm_fwd(a, w, out_dtype):
    wb = w.astype(BF16)
    return _matmul(a, wb, 'nn', out_dtype), (a, wb)


def _mm_bwd(out_dtype, res, g):
    a, wb = res
    return _matmul(g, wb, 'nt', a.dtype), _matmul(a, g, 'tn', F32)


mm.defvjp(_mm_fwd, _mm_bwd)


@functools.partial(jax.custom_vjp, nondiff_argnums=(3, 4))
def mm_slab(a, wb, delta, transposed, out_dtype):
    return _matmul(a, wb, 'nt' if transposed else 'nn', out_dtype)


def _mm_slab_fwd(a, wb, delta, transposed, out_dtype):
    return _matmul(a, wb, 'nt' if transposed else 'nn', out_dtype), (a, wb)


def _mm_slab_bwd(transposed, out_dtype, res, g):
    a, wb = res
    if transposed:
        return _matmul(g, wb, 'nn', a.dtype), jnp.zeros_like(wb), _matmul(g, a, 'tn', BF16)
    return _matmul(g, wb, 'nt', a.dtype), jnp.zeros_like(wb), _matmul(a, g, 'tn', BF16)


mm_slab.defvjp(_mm_slab_fwd, _mm_slab_bwd)


def _row_tile(n_rows, widest):
    t = 1 << max(3, int(math.log2(max(8, ROW_TILE_ELEMS // max(widest, 1)))))
    return min(n_rows, min(t, 2048))


def _rw_forward(fn, rows, params, name, tile=None):
    n_rows = rows[0].shape[0]
    tile = tile or _row_tile(n_rows, max(r.shape[1] for r in rows))
    outs = jax.eval_shape(fn, *[jax.ShapeDtypeStruct((tile, r.shape[1]), r.dtype) for r in rows],
                          *[jax.ShapeDtypeStruct(p.shape, p.dtype) for p in params])
    n_in = len(rows) + len(params)

    def body(*refs):
        res = fn(*[r[...] for r in refs[:n_in]])
        for o_ref, o in zip(refs[n_in:], res):
            o_ref[...] = o

    return pl.pallas_call(
        body,
        out_shape=[jax.ShapeDtypeStruct((n_rows, o.shape[1]), o.dtype) for o in outs],
        grid=(n_rows // tile,),
        in_specs=[pl.BlockSpec((tile, r.shape[1]), lambda i: (i, 0)) for r in rows]
        + [pl.BlockSpec(p.shape, lambda i: (0, 0)) for p in params],
        out_specs=[pl.BlockSpec((tile, o.shape[1]), lambda i: (i, 0)) for o in outs],
        compiler_params=_cparams(("parallel",)),
        name=name,
    )(*rows, *params)


def _rw_backward(fn, rows, params, cts, n_const, name):
    n_rows = rows[0].shape[0]
    tile = _row_tile(n_rows, max([r.shape[1] for r in rows] + [c.shape[1] for c in cts]))
    n_r, n_p, n_c = len(rows), len(params), len(cts)
    n_diff = n_r - n_const

    def body(*refs):
        row_vals = [r[...] for r in refs[:n_r]]
        par_vals = [r[...] for r in refs[n_r:n_r + n_p]]
        ct_vals = tuple(r[...] for r in refs[n_r + n_p:n_r + n_p + n_c])
        out_refs = refs[n_r + n_p + n_c:]

        def f(*diff):
            return fn(*diff[:n_diff], *row_vals[n_diff:], *diff[n_diff:])

        _, vjp = jax.vjp(f, *row_vals[:n_diff], *par_vals)
        grads = vjp(ct_vals)
        for o_ref, g in zip(out_refs[:n_diff], grads[:n_diff]):
            o_ref[...] = g
        i = pl.program_id(0)
        for o_ref, g in zip(out_refs[n_diff:], grads[n_diff:]):
            @pl.when(i == 0)
            def _(o_ref=o_ref, g=g):
                o_ref[...] = g

            @pl.when(i > 0)
            def _(o_ref=o_ref, g=g):
                o_ref[...] += g

    res = pl.pallas_call(
        body,
        out_shape=[jax.ShapeDtypeStruct(r.shape, r.dtype) for r in rows[:n_diff]]
        + [jax.ShapeDtypeStruct(p.shape, p.dtype) for p in params],
        grid=(n_rows // tile,),
        in_specs=[pl.BlockSpec((tile, r.shape[1]), lambda i: (i, 0)) for r in rows]
        + [pl.BlockSpec(p.shape, lambda i: (0, 0)) for p in params]
        + [pl.BlockSpec((tile, c.shape[1]), lambda i: (i, 0)) for c in cts],
        out_specs=[pl.BlockSpec((tile, r.shape[1]), lambda i: (i, 0)) for r in rows[:n_diff]]
        + [pl.BlockSpec(p.shape, lambda i: (0, 0)) for p in params],
        compiler_params=_cparams(("arbitrary",)),
        name=name + "_bwd",
    )(*rows, *params, *cts)
    return list(res[:n_diff]), list(res[n_diff:])


def rowwise(fn, rows, params, name, n_const=0, fwd_tile=None):
    rows, params = list(rows), list(params)
    fwd_tile = fwd_tile and min(fwd_tile, rows[0].shape[0])

    @jax.custom_vjp
    def op(rows, params):
        return tuple(_rw_forward(fn, rows, params, name, fwd_tile))

    def op_fwd(rows, params):
        return tuple(_rw_forward(fn, rows, params, name, fwd_tile)), (rows, params)

    def op_bwd(res, cts):
        rows, params = res
        d_rows, d_params = _rw_backward(fn, rows, params, list(cts), n_const, name)
        d_rows = d_rows + [jnp.zeros_like(r) for r in rows[len(rows) - n_const:]]
        return d_rows, d_params

    op.defvjp(op_fwd, op_bwd)
    return op(rows, params)


def _rms(x, g):
    xf = x.astype(F32)
    return xf * lax.rsqrt(jnp.mean(xf * xf, axis=-1, keepdims=True) + EPS) * g


ATTN_TILE = 1024
ATTN_SPAN = 2048
ATTN_CHUNK = 512
DELTA_TILE = 8192

NT_DIMS = (((1,), (1,)), ((), ()))


def _attn_tiles(stay, stream):
    span = min(ATTN_SPAN, stream)
    return min(ATTN_TILE, stay), span, min(ATTN_CHUNK, span)


def _flash_fwd(q, k, v, qk_mult):
    N, Lq, dk = q.shape
    Lk, dv = k.shape[1], v.shape[2]
    tq, span, ch = _attn_tiles(Lq, Lk)
    tq = min(Lq, tq * max(2, ATTN_TILE // span))
    nk = Lk // span
    v_ones = jnp.concatenate([v, jnp.ones((N, Lk, 1), v.dtype), jnp.zeros((N, Lk, LANES - dv - 1), v.dtype)], axis=2)

    def body(q_ref, k_ref, v_ref, o_ref, lse_ref, m_s, acc_s):
        j = pl.program_id(2)

        @pl.when(j == 0)
        def _():
            m_s[...] = jnp.full_like(m_s, NEG)
            acc_s[...] = jnp.zeros_like(acc_s)

        qv = q_ref[0]
        s = []
        for c in range(span // ch):
            sc = lax.dot_general(qv, k_ref[0, c * ch:(c + 1) * ch, :], NT_DIMS, preferred_element_type=F32)
            s.append(sc if qk_mult == 1.0 else sc * qk_mult)
        s_max = functools.reduce(jnp.maximum, s)
        m_new = jnp.maximum(m_s[...], jnp.max(s_max, axis=-1, keepdims=True))
        pv = None
        for c, sc in enumerate(s):
            p = jnp.exp2(sc - m_new).astype(BF16)
            d = jnp.dot(p, v_ref[0, c * ch:(c + 1) * ch, :], preferred_element_type=F32)
            pv = d if c == 0 else pv + d
        acc_s[...] = jnp.exp2(m_s[...] - m_new) * acc_s[...] + pv
        m_s[...] = m_new

        @pl.when(j == nk - 1)
        def _():
            acc = acc_s[...]
            l = acc[:, dv:dv + 1]
            o_ref[0] = (acc[:, :dv] / l).astype(o_ref.dtype)
            lse_ref[0] = m_s[...] + jnp.log(l) * LOG2E

    return pl.pallas_call(
        body,
        out_shape=[jax.ShapeDtypeStruct((N, Lq, dv), BF16), jax.ShapeDtypeStruct((N, Lq, 1), F32)],
        grid=(N, Lq // tq, nk),
        in_specs=[pl.BlockSpec((1, tq, dk), lambda n, i, j: (n, i, 0)),
                  pl.BlockSpec((1, span, dk), lambda n, i, j: (n, j, 0)),
                  pl.BlockSpec((1, span, LANES), lambda n, i, j: (n, j, 0))],
        out_specs=[pl.BlockSpec((1, tq, dv), lambda n, i, j: (n, i, 0)),
                   pl.BlockSpec((1, tq, 1), lambda n, i, j: (n, i, 0))],
        scratch_shapes=[pltpu.VMEM((tq, 1), F32), pltpu.VMEM((tq, LANES), F32)],
        compiler_params=_cparams(("parallel", "parallel", "arbitrary")),
        name=f"flash_fwd_{N}x{Lq}x{Lk}x{dk}",
    )(q, k, v_ones)


def _flash_bwd(q, k, v, do, lse_row, dd_row, qk_mult, out_mult):
    N, Lq, dk = q.shape
    Lk, dv = k.shape[1], v.shape[2]
    tk, span, ch = _attn_tiles(Lk, Lq)
    tk = min(Lk, 2 * tk)
    nq, nkt = Lq // span, Lk // tk

    def body(q_ref, k_ref, v_ref, do_ref, lse_ref, dd_ref, dq_ref, dk_ref, dv_ref, dq_s, dk_s, dv_s):
        i, j = pl.program_id(1), pl.program_id(2)

        @pl.when((i == 0) & (j == 0))
        def _():
            dq_s[...] = jnp.zeros_like(dq_s)

        @pl.when(j == 0)
        def _():
            dk_s[...] = jnp.zeros_like(dk_s)
            dv_s[...] = jnp.zeros_like(dv_s)

        kv_, vv = k_ref[0], v_ref[0]
        for c in range(span // ch):
            qc = q_ref[0, c * ch:(c + 1) * ch, :]
            doc = do_ref[0, c * ch:(c + 1) * ch, :]
            st = lax.dot_general(kv_, qc, NT_DIMS, preferred_element_type=F32)
            if qk_mult != 1.0:
                st = st * qk_mult
            pt = jnp.exp2(st - lse_ref[0, :, c * ch:(c + 1) * ch])
            dv_s[...] += jnp.dot(pt.astype(BF16), doc, preferred_element_type=F32)
            dpt = lax.dot_general(vv, doc, NT_DIMS, preferred_element_type=F32)
            dst = (pt * (dpt - dd_ref[0, :, c * ch:(c + 1) * ch])).astype(BF16)
            dk_s[...] += jnp.dot(dst, qc, preferred_element_type=F32)
            rows = pl.ds(pl.multiple_of(j * span + c * ch, ch), ch)
            dq_s[rows, :] += lax.dot_general(dst, kv_, (((0,), (0,)), ((), ())), preferred_element_type=F32)

        @pl.when(j == nq - 1)
        def _():
            dk_ref[0] = (dk_s[...] * out_mult).astype(dk_ref.dtype)
            dv_ref[0] = dv_s[...].astype(dv_ref.dtype)

        @pl.when((i == nkt - 1) & (j == nq - 1))
        def _():
            dq_ref[0] = (dq_s[...] * out_mult).astype(dq_ref.dtype)

    return pl.pallas_call(
        body,
        out_shape=[jax.ShapeDtypeStruct(q.shape, q.dtype), jax.ShapeDtypeStruct(k.shape, k.dtype),
                   jax.ShapeDtypeStruct(v.shape, v.dtype)],
        grid=(N, nkt, nq),
        in_specs=[pl.BlockSpec((1, span, dk), lambda n, i, j: (n, j, 0)),
                  pl.BlockSpec((1, tk, dk), lambda n, i, j: (n, i, 0)),
                  pl.BlockSpec((1, tk, dv), lambda n, i, j: (n, i, 0)),
                  pl.BlockSpec((1, span, dv), lambda n, i, j: (n, j, 0)),
                  pl.BlockSpec((1, 1, span), lambda n, i, j: (n, 0, j)),
                  pl.BlockSpec((1, 1, span), lambda n, i, j: (n, 0, j))],
        out_specs=[pl.BlockSpec((1, Lq, dk), lambda n, i, j: (n, 0, 0)),
                   pl.BlockSpec((1, tk, dk), lambda n, i, j: (n, i, 0)),
                   pl.BlockSpec((1, tk, dv), lambda n, i, j: (n, i, 0))],
        scratch_shapes=[pltpu.VMEM((Lq, dk), F32), pltpu.VMEM((tk, dk), F32), pltpu.VMEM((tk, dv), F32)],
        compiler_params=_cparams(("parallel", "arbitrary", "arbitrary")),
        name=f"flash_bwd_{N}x{Lq}x{Lk}x{dk}",
    )(q, k, v, do, lse_row, dd_row)


def _attn_delta(do, o, dlse):
    N, L, dv = o.shape

    def fn(do_t, o_t, dl_t):
        return (jnp.sum(do_t.astype(F32) * o_t.astype(F32), axis=-1, keepdims=True) - dl_t,)

    (dd,) = _rw_forward(fn, [do.reshape(N * L, dv), o.reshape(N * L, dv), dlse.reshape(N * L, 1)], [], "attn_delta",
                        tile=_div_tile(N * L, DELTA_TILE, 8))
    return dd.reshape(N, L, 1)


def full_attention(q, k, v, scale, q_prescaled):
    qk_mult = 1.0 if q_prescaled else scale * LOG2E
    grad_mult = LN2 if q_prescaled else scale

    @jax.custom_vjp
    def op(q, k, v):
        return _flash_fwd(q, k, v, qk_mult)[0]

    def op_fwd(q, k, v):
        o, lse = _flash_fwd(q, k, v, qk_mult)
        return o, (q, k, v, o, lse)

    def op_bwd(res, do):
        q, k, v, o, lse = res
        N, Lq, _ = q.shape
        dd = _attn_delta(do, o, jnp.zeros_like(lse))
        return tuple(_flash_bwd(q, k, v, do, lse.reshape(N, 1, Lq), dd.reshape(N, 1, Lq), qk_mult, grad_mult))

    op.defvjp(op_fwd, op_bwd)
    return op(q, k, v)


BAND_TILE = 128
BAND_BATCH = 16


def _band_specs(L, d):
    t = min(BAND_TILE, L)
    nt = L // t
    cur = lambda n, i: (n, i, 0)
    prev = lambda n, i: (n, jnp.maximum(i - 1, 0), 0)
    nxt = lambda n, i: (n, jnp.minimum(i + 1, nt - 1), 0)
    return t, nt, cur, prev, nxt


def _band_mask(i, t, L, half):
    a = lax.broadcasted_iota(jnp.int32, (t, 3 * t), 0)
    b = lax.broadcasted_iota(jnp.int32, (t, 3 * t), 1)
    dist = jnp.abs(b - t - a)
    other = i * t - t + b
    mask = (dist <= half) & (other >= 0) & (other < L)
    return dist.astype(F32), mask


def _band_fwd(q, k, v, slope, half):
    N, L, d = q.shape
    t, nt, cur, prev, nxt = _band_specs(L, d)
    nb = min(N, BAND_BATCH)
    scale = d ** -0.5

    def body(sl_ref, q_ref, kp, kc, kn, vp, vc, vn, o_ref, lse_ref):
        i = pl.program_id(1)
        kcat = jnp.concatenate([kp[...], kc[...], kn[...]], axis=1)
        vcat = jnp.concatenate([vp[...], vc[...], vn[...]], axis=1)
        s = lax.dot_general(q_ref[...], kcat, (((2,), (2,)), ((0,), (0,))), preferred_element_type=F32) * scale
        dist, mask = _band_mask(i, t, L, half)
        s = jnp.where(mask[None], s - sl_ref[...] * dist[None], NEG)
        m = jnp.max(s, axis=-1, keepdims=True)
        p = jnp.exp(s - m)
        l = jnp.sum(p, axis=-1, keepdims=True)
        o = lax.dot_general(p.astype(BF16), vcat, (((2,), (1,)), ((0,), (0,))), preferred_element_type=F32)
        o_ref[...] = o / l
        lse_ref[...] = m + jnp.log(l)

    blk = lambda f: pl.BlockSpec((nb, t, d), f)
    return pl.pallas_call(
        body,
        out_shape=[jax.ShapeDtypeStruct((N, L, d), F32), jax.ShapeDtypeStruct((N, L, 1), F32)],
        grid=(N // nb, nt),
        in_specs=[pl.BlockSpec((nb, 1, 1), lambda n, i: (n, 0, 0)), blk(cur),
                  blk(prev), blk(cur), blk(nxt), blk(prev), blk(cur), blk(nxt)],
        out_specs=[blk(cur), pl.BlockSpec((nb, t, 1), cur)],
        compiler_params=_cparams(("parallel", "parallel")),
        name=f"band_fwd_{N}x{L}",
    )(slope, q, k, k, k, v, v, v)


def _band_dq(q, k, v, slope, do, lse, dd, half):
    N, L, d = q.shape
    t, nt, cur, prev, nxt = _band_specs(L, d)
    nb = min(N, BAND_BATCH)
    scale = d ** -0.5

    def body(sl_ref, q_ref, kp, kc, kn, vp, vc, vn, do_ref, lse_ref, dd_ref, dq_ref):
        i = pl.program_id(1)
        kcat = jnp.concatenate([kp[...], kc[...], kn[...]], axis=1)
        vcat = jnp.concatenate([vp[...], vc[...], vn[...]], axis=1)
        s = lax.dot_general(q_ref[...], kcat, (((2,), (2,)), ((0,), (0,))), preferred_element_type=F32) * scale
        dist, mask = _band_mask(i, t, L, half)
        p = jnp.where(mask[None], jnp.exp(s - sl_ref[...] * dist[None] - lse_ref[...]), 0.0)
        dp = lax.dot_general(do_ref[...].astype(BF16), vcat, (((2,), (2,)), ((0,), (0,))),
                             preferred_element_type=F32)
        ds = (p * (dp - dd_ref[...])).astype(BF16)
        dq = lax.dot_general(ds, kcat, (((2,), (1,)), ((0,), (0,))), preferred_element_type=F32)
        dq_ref[...] = (dq * scale).astype(dq_ref.dtype)

    blk = lambda f: pl.BlockSpec((nb, t, d), f)
    col = pl.BlockSpec((nb, t, 1), cur)
    return pl.pallas_call(
        body,
        out_shape=jax.ShapeDtypeStruct(q.shape, q.dtype),
        grid=(N // nb, nt),
        in_specs=[pl.BlockSpec((nb, 1, 1), lambda n, i: (n, 0, 0)), blk(cur),
                  blk(prev), blk(cur), blk(nxt), blk(prev), blk(cur), blk(nxt), blk(cur), col, col],
        out_specs=blk(cur),
        compiler_params=_cparams(("parallel", "parallel")),
        name=f"band_dq_{N}x{L}",
    )(slope, q, k, k, k, v, v, v, do, lse, dd)


def _band_dkv(q, k, v, slope, do, lse_row, dd_row, half):
    N, L, d = q.shape
    t, nt, cur, prev, nxt = _band_specs(L, d)
    nb = min(N, BAND_BATCH)
    scale = d ** -0.5
    rcur = lambda n, i: (n, 0, i)
    rprev = lambda n, i: (n, 0, jnp.maximum(i - 1, 0))
    rnxt = lambda n, i: (n, 0, jnp.minimum(i + 1, nt - 1))

    def body(sl_ref, k_ref, v_ref, qp, qc, qn, dop, doc, don, lp, lc, ln, dp_, dc_, dn_, dk_ref, dv_ref):
        i = pl.program_id(1)
        qcat = jnp.concatenate([qp[...], qc[...], qn[...]], axis=1)
        docat = jnp.concatenate([dop[...], doc[...], don[...]], axis=1).astype(BF16)
        lrow = jnp.concatenate([lp[...], lc[...], ln[...]], axis=2)
        drow = jnp.concatenate([dp_[...], dc_[...], dn_[...]], axis=2)
        st = lax.dot_general(k_ref[...], qcat, (((2,), (2,)), ((0,), (0,))), preferred_element_type=F32) * scale
        dist, mask = _band_mask(i, t, L, half)
        pt = jnp.where(mask[None], jnp.exp(st - sl_ref[...] * dist[None] - lrow), 0.0)
        dv = lax.dot_general(pt.astype(BF16), docat, (((2,), (1,)), ((0,), (0,))), preferred_element_type=F32)
        dpt = lax.dot_general(v_ref[...], docat, (((2,), (2,)), ((0,), (0,))), preferred_element_type=F32)
        dst = (pt * (dpt - drow)).astype(BF16)
        dk = lax.dot_general(dst, qcat, (((2,), (1,)), ((0,), (0,))), preferred_element_type=F32)
        dk_ref[...] = (dk * scale).astype(dk_ref.dtype)
        dv_ref[...] = dv.astype(dv_ref.dtype)

    blk = lambda f: pl.BlockSpec((nb, t, d), f)
    row = lambda f: pl.BlockSpec((nb, 1, t), f)
    return pl.pallas_call(
        body,
        out_shape=[jax.ShapeDtypeStruct(k.shape, k.dtype), jax.ShapeDtypeStruct(v.shape, v.dtype)],
        grid=(N // nb, nt),
        in_specs=[pl.BlockSpec((nb, 1, 1), lambda n, i: (n, 0, 0)), blk(cur), blk(cur),
                  blk(prev), blk(cur), blk(nxt), blk(prev), blk(cur), blk(nxt),
                  row(rprev), row(rcur), row(rnxt), row(rprev), row(rcur), row(rnxt)],
        out_specs=[blk(cur), blk(cur)],
        compiler_params=_cparams(("parallel", "parallel")),
        name=f"band_dkv_{N}x{L}",
    )(slope, k, v, q, q, q, do, do, do, lse_row, lse_row, lse_row, dd_row, dd_row, dd_row)


def band_attention(q, k, v, slope, half):
    @jax.custom_vjp
    def op(q, k, v, slope):
        return tuple(_band_fwd(q, k, v, slope, half))

    def op_fwd(q, k, v, slope):
        o, lse = _band_fwd(q, k, v, slope, half)
        return (o, lse), (q, k, v, slope, o, lse)

    def op_bwd(res, cts):
        q, k, v, slope, o, lse = res
        do, dlse = cts
        N, L, _ = q.shape
        dd = _attn_delta(do, o, dlse)
        dq = _band_dq(q, k, v, slope, do, lse, dd, half)
        dk, dv = _band_dkv(q, k, v, slope, do, lse.reshape(N, 1, L), dd.reshape(N, 1, L), half)
        return dq, dk, dv, jnp.zeros_like(slope)

    op.defvjp(op_fwd, op_bwd)
    return op(q, k, v, slope)


def _conv_taps(width):
    return [CONV_PAD - width // 2 + j for j in range(width)]


def _dwconv_call(up, w, width):
    S, C = up.shape[0] - 2 * CONV_PAD, up.shape[1]
    ch = min(CONV_CHUNK, S)
    offs = _conv_taps(width)

    def body(u_ref, w_ref, o_ref):
        def chunk(ci, carry):
            base = pl.multiple_of(ci * ch, ch)
            win = u_ref[pl.ds(base, ch + 2 * CONV_PAD), :]
            acc = jnp.zeros((ch, LANES), F32)
            for j, off in enumerate(offs):
                acc = acc + win[off:off + ch, :] * w_ref[j:j + 1, :]
            o_ref[pl.ds(base, ch), :] = acc
            return carry

        lax.fori_loop(0, S // ch, chunk, 0)

    return pl.pallas_call(
        body,
        out_shape=jax.ShapeDtypeStruct((S, C), F32),
        grid=(C // LANES,),
        in_specs=[pl.BlockSpec((S + 2 * CONV_PAD, LANES), lambda c: (0, c)),
                  pl.BlockSpec((w.shape[0], LANES), lambda c: (0, c))],
        out_specs=pl.BlockSpec((S, LANES), lambda c: (0, c)),
        compiler_params=_cparams(("parallel",)),
        name=f"dwconv{width}_{S}x{C}",
    )(up, w)


def _dwconv_dw_call(up, g, width, w_rows):
    S, C = g.shape
    ch = min(CONV_CHUNK, S)
    offs = _conv_taps(width)

    def body(u_ref, g_ref, o_ref):
        def chunk(ci, accs):
            base = pl.multiple_of(ci * ch, ch)
            win = u_ref[pl.ds(base, ch + 2 * CONV_PAD), :]
            gt = g_ref[pl.ds(base, ch), :]
            return tuple(a + (gt * win[off:off + ch, :]).reshape(ch // 8, 8, LANES).sum(axis=0)
                         for a, off in zip(accs, offs))

        accs = lax.fori_loop(0, S // ch, chunk, tuple(jnp.zeros((8, LANES), F32) for _ in offs))
        o_ref[...] = jnp.zeros_like(o_ref)
        for j, a in enumerate(accs):
            o_ref[j:j + 1, :] = jnp.sum(a, axis=0, keepdims=True)

    return pl.pallas_call(
        body,
        out_shape=jax.ShapeDtypeStruct((w_rows, C), F32),
        grid=(C // LANES,),
        in_specs=[pl.BlockSpec((S + 2 * CONV_PAD, LANES), lambda c: (0, c)),
                  pl.BlockSpec((S, LANES), lambda c: (0, c))],
        out_specs=pl.BlockSpec((w_rows, LANES), lambda c: (0, c)),
        compiler_params=_cparams(("parallel",)),
        name=f"dwconv{width}_dw_{S}x{C}",
    )(up, g)


def _pad_rows(u):
    return jnp.pad(u, ((CONV_PAD, CONV_PAD), (0, 0)))


def _pad_taps(w):
    return jnp.pad(w, ((0, -w.shape[0] % 8), (0, 0)))


def depthwise_conv(u, w):
    width = w.shape[0]

    @jax.custom_vjp
    def op(u, w):
        return _dwconv_call(_pad_rows(u), _pad_taps(w), width)

    def op_fwd(u, w):
        up = _pad_rows(u)
        return _dwconv_call(up, _pad_taps(w), width), (up, w)

    def op_bwd(res, g):
        up, w = res
        du = _dwconv_call(_pad_rows(g), _pad_taps(w[::-1]), width)
        dw = _dwconv_dw_call(up, g, width, _pad_taps(w).shape[0])[:width]
        return du, dw

    op.defvjp(op_fwd, op_bwd)
    return op(u, w)


def _loss_call(y, target):
    S, D = y.shape
    tile = _row_tile(S, D)

    def body(y_ref, t_ref, o_ref):
        d = y_ref[...] - t_ref[...]
        part = jnp.sum(jnp.sum(d * d, axis=1, keepdims=True), axis=0, keepdims=True) * (0.5 / D)

        @pl.when(pl.program_id(0) == 0)
        def _():
            o_ref[...] = jnp.zeros_like(o_ref)

        o_ref[...] += jnp.broadcast_to(part, o_ref.shape)

    out = pl.pallas_call(
        body,
        out_shape=jax.ShapeDtypeStruct((8, LANES), F32),
        grid=(S // tile,),
        in_specs=[pl.BlockSpec((tile, D), lambda i: (i, 0)), pl.BlockSpec((tile, D), lambda i: (i, 0))],
        out_specs=pl.BlockSpec((8, LANES), lambda i: (0, 0)),
        compiler_params=_cparams(("arbitrary",)),
        name="loss_head",
    )(y, target)
    return out[0, 0]


@jax.custom_vjp
def loss_head(y, target):
    return _loss_call(y, target)


def _loss_fwd(y, target):
    return _loss_call(y, target), (y, target)


def _loss_bwd(res, g):
    y, target = res
    inv_d = 1.0 / y.shape[1]

    def fn(y_t, t_t, g_p):
        return ((y_t - t_t) * (g_p * inv_d),)

    (dy,) = _rw_forward(fn, [y, target], [g.reshape(1, 1).astype(F32)], "loss_head_bwd")
    return dy, jnp.zeros_like(target)


loss_head.defvjp(_loss_fwd, _loss_bwd)


ANY = pl.BlockSpec(memory_space=pl.ANY)


AG_ICI_PIECES = 4
AG_D2D_PIECES = 2
D2D_PIECES = 16
ICI_PIECES = 4


def _with_own_block(received, own, slot):
    return lax.dynamic_update_slice_in_dim(received, own[None], slot, axis=0)


def all_gather_chips(mine, name):
    R, C = mine.shape
    ni, nf = AG_ICI_PIECES, AG_D2D_PIECES
    rows = R // (2 * ni * nf)
    n_ici, n_all = 3 * ni, 3 * ni + 3 * ni * nf

    def body(in_ref, out_ref, send_sems, recv_sems):
        x, y, c = lax.axis_index("x"), lax.axis_index("y"), lax.axis_index("c")
        chips = [(1 - x, y), (x, 1 - y), (1 - x, 1 - y)]

        def copy(sem, src, dst, to):
            return pltpu.make_async_remote_copy(src_ref=src, dst_ref=dst, send_sem=send_sems.at[sem],
                                                recv_sem=recv_sems.at[sem], device_id=to, device_id_type=MESH)

        sends = []
        for k, (px, py) in enumerate(chips):
            for a in range(ni):
                sends.append(copy(k * ni + a, in_ref.at[c, a], out_ref.at[2 * x + y, c, a], (px, py, c)))
                sends[-1].start()
        for k, (px, py) in enumerate(chips):
            for a in range(ni):
                landed = out_ref.at[2 * px + py, c, a]
                copy(k * ni + a, in_ref.at[c, a], landed, (px, py, c)).wait_recv()
                for b in range(nf):
                    sends.append(copy(n_ici + (k * ni + a) * nf + b, landed.at[b], landed.at[b], (x, y, 1 - c)))
                    sends[-1].start()
        for k, (px, py) in enumerate(chips):
            for a in range(ni):
                for b in range(nf):
                    theirs = out_ref.at[2 * px + py, 1 - c, a, b]
                    copy(n_ici + (k * ni + a) * nf + b, theirs, theirs, (x, y, 1 - c)).wait_recv()
        for cp in sends:
            cp.wait_send()

    out = pl.pallas_call(
        body,
        out_shape=jax.ShapeDtypeStruct((N_CHIPS, 2, ni, nf, rows, C), mine.dtype),
        in_specs=[ANY],
        out_specs=ANY,
        scratch_shapes=[pltpu.SemaphoreType.DMA((n_all,)), pltpu.SemaphoreType.DMA((n_all,))],
        name=name,
    )(mine.reshape(2, ni, nf, rows, C))
    return _with_own_block(out.reshape(N_CHIPS, R, C), mine, 2 * lax.axis_index("x") + lax.axis_index("y"))


def swap_halves_with_sibling(parts):
    n_s, _, H, C = parts.shape
    n = ICI_PIECES

    def body(in_ref, out_ref, send_sems, recv_sems):
        x, y, c = lax.axis_index("x"), lax.axis_index("y"), lax.axis_index("c")

        def copy(s, b, half):
            return pltpu.make_async_remote_copy(
                src_ref=in_ref.at[s, half, b], dst_ref=out_ref.at[s, b], send_sem=send_sems.at[s * n + b],
                recv_sem=recv_sems.at[s * n + b], device_id=(x, y, 1 - c), device_id_type=MESH)

        sends = [copy(s, b, 1 - c) for s in range(n_s) for b in range(n)]
        for cp in sends:
            cp.start()
        for s in range(n_s):
            for b in range(n):
                copy(s, b, 1 - c).wait_recv()
        for cp in sends:
            cp.wait_send()

    out = pl.pallas_call(
        body,
        out_shape=jax.ShapeDtypeStruct((n_s, n, H // n, C), parts.dtype),
        in_specs=[ANY],
        out_specs=ANY,
        scratch_shapes=[pltpu.SemaphoreType.DMA((n_s * n,)), pltpu.SemaphoreType.DMA((n_s * n,))],
        name="grad_swap_halves",
    )(parts.reshape(n_s, 2, n, H // n, C))
    return out.reshape(n_s, H, C)


def add_pairs(parts, theirs, core):
    n_s, _, H, C = parts.shape
    tile = _div_tile(H, 512, 16)

    def body(c_ref, a_ref, b_ref, o_ref):
        o_ref[...] = (a_ref[...].astype(F32) + b_ref[...].astype(F32)).astype(o_ref.dtype)

    return pl.pallas_call(
        body,
        out_shape=jax.ShapeDtypeStruct((n_s, 1, H, C), parts.dtype),
        grid_spec=pltpu.PrefetchScalarGridSpec(
            num_scalar_prefetch=1,
            grid=(n_s, H // tile),
            in_specs=[pl.BlockSpec((1, 1, tile, C), lambda s, i, c_ref: (s, c_ref[0], i, 0)),
                      pl.BlockSpec((1, 1, tile, C), lambda s, i, c_ref: (s, 0, i, 0))],
            out_specs=pl.BlockSpec((1, 1, tile, C), lambda s, i, c_ref: (s, 0, i, 0)),
        ),
        compiler_params=_cparams(("parallel", "parallel")),
        name="grad_add_pairs",
    )(core.reshape(1).astype(jnp.int32), parts, theirs.reshape(n_s, 1, H, C)).reshape(n_s, H, C)


def scatter_to_chips(parts):
    n_s, H, C = parts.shape
    n = ICI_PIECES

    def body(in_ref, out_ref, send_sems, recv_sems):
        x, y, c = lax.axis_index("x"), lax.axis_index("y"), lax.axis_index("c")
        chips = [(1 - x, y), (x, 1 - y), (1 - x, 1 - y)]

        def copy(k, b, src_slot, dst_slot, to):
            return pltpu.make_async_remote_copy(
                src_ref=in_ref.at[src_slot, b], dst_ref=out_ref.at[dst_slot, b], send_sem=send_sems.at[k * n + b],
                recv_sem=recv_sems.at[k * n + b], device_id=to, device_id_type=MESH)

        sends = [copy(k, b, 2 * px + py, 2 * x + y, (px, py, c)) for k, (px, py) in enumerate(chips) for b in range(n)]
        for cp in sends:
            cp.start()
        for k, (px, py) in enumerate(chips):
            for b in range(n):
                copy(k, b, 2 * x + y, 2 * px + py, (px, py, c)).wait_recv()
        for cp in sends:
            cp.wait_send()

    out = pl.pallas_call(
        body,
        out_shape=jax.ShapeDtypeStruct((n_s, n, H // n, C), parts.dtype),
        in_specs=[ANY],
        out_specs=ANY,
        scratch_shapes=[pltpu.SemaphoreType.DMA((3 * n,)), pltpu.SemaphoreType.DMA((3 * n,))],
        name="grad_scatter",
    )(parts.reshape(n_s, n, H // n, C))
    slot = 2 * lax.axis_index("x") + lax.axis_index("y")
    return _with_own_block(out.reshape(n_s, H, C), lax.dynamic_index_in_dim(parts, slot, axis=0, keepdims=False), slot)


def exchange_with_sibling(mine):
    H, C = mine.shape
    n = D2D_PIECES

    def body(in_ref, out_ref, send_sems, recv_sems):
        x, y, c = lax.axis_index("x"), lax.axis_index("y"), lax.axis_index("c")

        def copy(b, half):
            return pltpu.make_async_remote_copy(
                src_ref=in_ref.at[b], dst_ref=out_ref.at[half, b], send_sem=send_sems.at[b], recv_sem=recv_sems.at[b],
                device_id=(x, y, 1 - c), device_id_type=MESH)

        sends = [copy(b, c) for b in range(n)]
        for cp in sends:
            cp.start()
        for b in range(n):
            copy(b, 1 - c).wait_recv()
        for cp in sends:
            cp.wait_send()

    out = pl.pallas_call(
        body,
        out_shape=jax.ShapeDtypeStruct((2, n, H // n, C), mine.dtype),
        in_specs=[ANY],
        out_specs=ANY,
        scratch_shapes=[pltpu.SemaphoreType.DMA((n,)), pltpu.SemaphoreType.DMA((n,))],
        name="grad_sibling_exchange",
    )(mine.reshape(n, H // n, C))
    return _with_own_block(out.reshape(2, H, C), mine, lax.axis_index("c"))


def sum_and_exchange(recv):
    n, H, C = recv.shape
    tile = _div_tile(H, 256, 16)
    steps = H // tile
    assert steps >= 2

    def body(r_ref, out_ref, buf, send_sems, recv_sem, local_sems):
        i = pl.program_id(0)
        x, y, c = lax.axis_index("x"), lax.axis_index("y"), lax.axis_index("c")
        slot = i % 2

        def copies(buf_slot, step):
            rows = pl.ds(pl.multiple_of(step * tile, tile), tile)
            loc = pltpu.make_async_copy(buf.at[buf_slot], out_ref.at[c, rows], local_sems.at[buf_slot])
            rem = pltpu.make_async_remote_copy(
                src_ref=buf.at[buf_slot], dst_ref=out_ref.at[c, rows], send_sem=send_sems.at[buf_slot],
                recv_sem=recv_sem, device_id=(x, y, 1 - c), device_id_type=MESH)
            return loc, rem

        @pl.when(i >= 2)
        def _():
            loc, rem = copies(slot, i - 2)
            loc.wait()
            rem.wait_send()

        acc = r_ref[0].astype(F32)
        for j in range(1, n):
            acc = acc + r_ref[j].astype(F32)
        buf[slot] = acc
        loc, rem = copies(slot, i)
        loc.start()
        rem.start()

        @pl.when(i == steps - 1)
        def _():
            for buf_slot, step in ((1 - slot, i - 1), (slot, i)):
                loc, rem = copies(buf_slot, step)
                loc.wait()
                rem.wait_send()
            theirs = out_ref.at[1 - c]
            pltpu.make_async_remote_copy(src_ref=theirs, dst_ref=theirs, send_sem=send_sems.at[0], recv_sem=recv_sem,
                                         device_id=(x, y, 1 - c), device_id_type=MESH).wait_recv()

    return pl.pallas_call(
        body,
        out_shape=jax.ShapeDtypeStruct((2, H, C), F32),
        grid=(steps,),
        in_specs=[pl.BlockSpec((n, tile, C), lambda i: (0, i, 0))],
        out_specs=ANY,
        scratch_shapes=[pltpu.VMEM((2, tile, C), F32), pltpu.SemaphoreType.DMA((2,)), pltpu.SemaphoreType.DMA,
                        pltpu.SemaphoreType.DMA((2,))],
        compiler_params=_cparams(("arbitrary",)),
        name="grad_sum_exchange",
    )(recv)


def sum_contributions(recv):
    n, H, C = recv.shape
    tile = _div_tile(H, 256, 16)

    def body(r_ref, o_ref):
        acc = r_ref[0].astype(F32)
        for j in range(1, n):
            acc = acc + r_ref[j].astype(F32)
        o_ref[...] = acc

    return pl.pallas_call(
        body,
        out_shape=jax.ShapeDtypeStruct((H, C), F32),
        grid=(H // tile,),
        in_specs=[pl.BlockSpec((n, tile, C), lambda i: (0, i, 0))],
        out_specs=pl.BlockSpec((tile, C), lambda i: (i, 0)),
        compiler_params=_cparams(("parallel",)),
        name="grad_sum",
    )(recv)


def adamw(w, g, m, v):
    shape = w.shape
    cols = shape[-1]
    rows = math.prod(shape[:-1])
    tile = _div_tile(rows, max(8, 256 * 1024 // cols // 8 * 8), 8)
    c1 = 1.0 - ADAM_B1 ** ADAM_STEP
    c2 = 1.0 - ADAM_B2 ** ADAM_STEP

    def body(w_ref, g_ref, m_ref, v_ref, d_ref, nm_ref, nv_ref):
        g_ = g_ref[...]
        nm = ADAM_B1 * m_ref[...] + (1.0 - ADAM_B1) * g_
        nv = ADAM_B2 * v_ref[...] + (1.0 - ADAM_B2) * (g_ * g_)
        d_ref[...] = -ADAM_LR * ((nm / c1) / (jnp.sqrt(nv / c2) + ADAM_EPS) + ADAM_WD * w_ref[...])
        nm_ref[...] = nm
        nv_ref[...] = nv

    spec = pl.BlockSpec((tile, cols), lambda i: (i, 0))
    outs = pl.pallas_call(
        body,
        out_shape=[jax.ShapeDtypeStruct((rows, cols), F32)] * 3,
        grid=(rows // tile,),
        in_specs=[spec] * 4,
        out_specs=[spec] * 3,
        compiler_params=_cparams(("parallel",)),
        name=f"adamw_{rows}x{cols}",
    )(*[t.reshape(rows, cols) for t in (w, g, m, v)])
    return [o.reshape(shape) for o in outs]


def _leaves(name, arr):
    n_lead = WEIGHT_LAYOUT[name][0]
    lead = arr.shape[:n_lead]
    flat = arr.reshape((-1,) + arr.shape[n_lead:])
    return [flat[i] for i in range(math.prod(lead))]


def _pad_pack_rows(buf):
    return jnp.pad(buf, ((0, -buf.shape[0] % PACK_ROW_MULT), (0, 0)))


def _pack_flat(pieces, dtype, cols):
    flat = jnp.concatenate([p.reshape(-1).astype(dtype) for p in pieces])
    flat = jnp.pad(flat, (0, -flat.shape[0] % (cols * PACK_ROW_MULT)))
    return flat.reshape(-1, cols)


def _unpack_flat(buf, shapes):
    lead = buf.shape[:-2]
    flat = buf.reshape(lead + (-1,))
    out, off = [], 0
    for shp in shapes:
        n = math.prod(shp)
        out.append(lax.slice_in_dim(flat, off, off + n, axis=len(lead)).reshape(lead + tuple(shp)))
        off += n
    return out


def _pack_slabs(slabs, dtype):
    return _pad_pack_rows(jnp.concatenate([s.astype(dtype) for s in slabs], axis=0))


def _local_slabs(name, shard):
    leaves = _leaves(name, shard)
    return [leaf.T for leaf in leaves] if name in SLAB_TRANSPOSED else leaves


def _chip_slice(leaf, ax, s):
    if ax is None:
        return leaf
    w = leaf.shape[ax] // N_CHIPS
    return lax.slice_in_dim(leaf, s * w, (s + 1) * w, axis=ax)


def gather_weights(shards, cols):
    slabs = [s for n in SLAB_NAMES for s in _local_slabs(n, shards[n])]
    slab_owner = [n for n in SLAB_NAMES for _ in _leaves(n, shards[n])]
    slab_buf = _pack_slabs(slabs, BF16)
    misc = {}
    for dtype in (BF16, F32):
        names = [n for n in MISC_NAMES if WEIGHT_LAYOUT[n][1] is not None and (n in F32_GATHER) == (dtype == F32)]
        pieces = [leaf for n in names for leaf in _leaves(n, shards[n])]
        owners = [n for n in names for _ in _leaves(n, shards[n])]
        buf = _pack_flat(pieces, dtype, cols)
        if dtype == BF16:
            gathered = all_gather_chips(jnp.concatenate([slab_buf, buf], axis=0), "weights_all_gather_bf16")
            slab_part, gathered = gathered[:, :slab_buf.shape[0]], gathered[:, slab_buf.shape[0]:]
        else:
            gathered = all_gather_chips(buf, "weights_all_gather_f32")
        for n, blk in zip(owners, _unpack_flat(gathered, [p.shape for p in pieces])):
            ax = WEIGHT_LAYOUT[n][1]
            misc.setdefault(n, []).append(jnp.concatenate([blk[s] for s in range(N_CHIPS)], axis=ax).astype(F32))
    for n in MISC_NAMES:
        if WEIGHT_LAYOUT[n][1] is None:
            misc[n] = [leaf.astype(F32) for leaf in _leaves(n, shards[n])]
    slab_full, off = {}, 0
    for n, s in zip(slab_owner, slabs):
        rows = s.shape[0]
        slab_full.setdefault(n, []).append(slab_part[:, off:off + rows].reshape(N_CHIPS * rows, cols))
        off += rows
    return slab_full, misc


def reduce_gradients(g_slab, g_misc, shards, cols):
    per_chip = []
    for s in range(N_CHIPS):
        slabs = []
        for n in SLAB_NAMES:
            for leaf in g_slab[n]:
                slabs.append(_chip_slice(leaf, 0, s))
        pieces = [_chip_slice(leaf, WEIGHT_LAYOUT[n][1], s) for n in MISC_NAMES for leaf in g_misc[n]]
        per_chip.append(jnp.concatenate([_pack_slabs(slabs, BF16), _pack_flat(pieces, BF16, cols)], axis=0))
    parts = jnp.stack(per_chip)
    R = parts.shape[1]
    parts = parts.reshape(N_CHIPS, 2, R // 2, cols)
    chip_parts = add_pairs(parts, swap_halves_with_sibling(parts), lax.axis_index("c"))
    recv = scatter_to_chips(chip_parts)
    red = sum_and_exchange(recv).reshape(R, cols)

    out, off = {}, 0
    for n in SLAB_NAMES:
        leaves = []
        for slab in _local_slabs(n, shards[n]):
            rows = slab.shape[0]
            g = red[off:off + rows]
            leaves.append(g.T if n in SLAB_TRANSPOSED else g)
            off += rows
        out[n] = jnp.stack(leaves).reshape(shards[n].shape)
    off += -off % PACK_ROW_MULT
    misc_shapes = [shards[n].shape for n in MISC_NAMES]
    for n, g in zip(MISC_NAMES, _unpack_flat(red[off:], misc_shapes)):
        out[n] = g
    return out


def _silu(x):
    return x * jax.nn.sigmoid(x)


def _heads_first(t, heads):
    S = t.shape[0]
    return t.reshape(S, heads, -1).transpose(1, 0, 2)


def _heads_last(t):
    return t.transpose(1, 0, 2).reshape(t.shape[1], -1)


class SlabWeights:
    def __init__(self, values, deltas):
        self.values, self.deltas = values, deltas

    def matmul(self, a, name, leaf, out_dtype):
        return mm_slab(a, self.values[name][leaf], self.deltas[name][leaf], name in SLAB_TRANSPOSED, out_dtype)


def swiglu_half_step(x, h, slab, leaf, g_post, g_next):
    a = slab.matmul(h, 'ffn_w_gate', leaf, BF16)
    b = slab.matmul(h, 'ffn_w_up', leaf, BF16)
    (u,) = rowwise(lambda a_, b_: ((_silu(a_.astype(F32)) * b_.astype(F32)).astype(BF16),), [a, b], [], "swiglu_act",
                   fwd_tile=512)
    y = slab.matmul(u, 'ffn_w_down', leaf, F32)
    return residual_norm(x, y, 0.5, g_post, g_next)


def residual_norm(x, y, coef, g_post, g_next):
    if g_next is None:
        (xn,) = rowwise(lambda x_, y_, g1: (x_ + coef * _rms(y_, g1),), [x, y], [g_post], "residual")
        return xn, None

    def fn(x_, y_, g1, g2):
        xn = x_ + coef * _rms(y_, g1)
        return xn, _rms(xn, g2).astype(BF16)

    return rowwise(fn, [x, y], [g_post, g_next], "residual_norm", fwd_tile=1024)


def memory_attention(q_mem, memn, w_kv):
    kv = mm(memn, w_kv, BF16)
    n_mem = kv.shape[0]
    kv = kv.reshape(n_mem, 2, MEM_HEADS, HEAD_DIM).transpose(1, 2, 0, 3)
    o = full_attention(_heads_first(q_mem.astype(BF16), MEM_HEADS), kv[0], kv[1], HEAD_DIM ** -0.5, False)
    return _heads_last(o)


def mla_mixer(h, cos, sin, w_in, q_g, kv_g, w_uq, w_ukv):
    S = h.shape[0]
    a_in = MLA_Q_LORA + MLA_KV_LORA + MLA_ROPE + MEM_WIDTH
    z = mm(h, jnp.pad(w_in, ((0, 0), (0, -a_in % LANES))), F32)
    o1, o2, o3 = MLA_Q_LORA, MLA_Q_LORA + MLA_KV_LORA, MLA_Q_LORA + MLA_KV_LORA + MLA_ROPE
    c_q, c_kv, k_r, q_mem = z[:, :o1], z[:, o1:o2], z[:, o2:o3], z[:, o3:a_in]
    cqn, ckvn = rowwise(lambda a, b, ga, gb: (_rms(a, ga).astype(BF16), _rms(b, gb).astype(BF16)),
                        [c_q, c_kv], [q_g, kv_g], "mla_lora_norm")
    q = mm(cqn, w_uq, F32).reshape(S, MLA_HEADS, MLA_NOPE + MLA_ROPE)
    kv = mm(ckvn, w_ukv, BF16).reshape(S, MLA_HEADS, MLA_NOPE + MLA_V)
    half = MLA_ROPE // 2
    qn = q[:, :, :MLA_NOPE].reshape(S, MLA_HEADS * MLA_NOPE)
    q1 = q[:, :, MLA_NOPE:MLA_NOPE + half].reshape(S, MLA_HEADS * half)
    q2 = q[:, :, MLA_NOPE + half:].reshape(S, MLA_HEADS * half)
    cos_h, sin_h = jnp.tile(cos, (1, MLA_HEADS)), jnp.tile(sin, (1, MLA_HEADS))
    scale = (MLA_NOPE + MLA_ROPE) ** -0.5
    qm = scale * LOG2E

    def rope(qn_, q1_, q2_, k1_, k2_, ch, sh, c1, s1):
        return ((qn_ * qm).astype(BF16), ((q1_ * ch - q2_ * sh) * qm).astype(BF16),
                ((q1_ * sh + q2_ * ch) * qm).astype(BF16),
                (k1_ * c1 - k2_ * s1).astype(BF16), (k1_ * s1 + k2_ * c1).astype(BF16))

    qns, qr1, qr2, kr1, kr2 = rowwise(rope, [qn, q1, q2, k_r[:, :half], k_r[:, half:], cos_h, sin_h, cos, sin], [],
                                      "mla_rope", n_const=4)
    q_cat = jnp.concatenate([qns.reshape(S, MLA_HEADS, MLA_NOPE), qr1.reshape(S, MLA_HEADS, half),
                             qr2.reshape(S, MLA_HEADS, half)], axis=-1)
    k_rope = jnp.concatenate([kr1, kr2], axis=-1)
    k_cat = jnp.concatenate([kv[:, :, :MLA_NOPE], jnp.broadcast_to(k_rope[:, None, :], (S, MLA_HEADS, MLA_ROPE))],
                            axis=-1)
    o = full_attention(q_cat.transpose(1, 0, 2), k_cat.transpose(1, 0, 2), kv[:, :, MLA_NOPE:].transpose(1, 0, 2),
                       scale, True)
    return _heads_last(o), q_mem


def dilated_mixer(h, slab):
    S = h.shape[0]
    n_g = len(DIL_GROUPS)
    qkv_w = n_g * 3 * DIL_HEADS * HEAD_DIM
    z = slab.matmul(h, 'b_w_in', 0, BF16)
    zd = z[:, :qkv_w].reshape(S, n_g, 3, DIL_HEADS, HEAD_DIM)
    q_mem = z[:, qkv_w:]
    slopes = (2.0 ** (-ALIBI_MAX * (jnp.arange(n_g * DIL_HEADS, dtype=F32) + 1.0) / (n_g * DIL_HEADS)))
    slopes = slopes.reshape(n_g, DIL_HEADS)
    outs, lses = [], []
    for g, (window, dil) in enumerate(DIL_GROUPS):
        L = S // dil

        def sub(t):
            return t.reshape(L, dil, DIL_HEADS, HEAD_DIM).transpose(1, 2, 0, 3).reshape(dil * DIL_HEADS, L, HEAD_DIM)

        slope = (jnp.tile(slopes[g], dil) * dil).reshape(dil * DIL_HEADS, 1, 1)
        o, lse = band_attention(sub(zd[:, g, 0]), sub(zd[:, g, 1]), sub(zd[:, g, 2]), slope, window // (2 * dil))
        outs.append(o.reshape(dil, DIL_HEADS, L, HEAD_DIM).transpose(2, 0, 1, 3).reshape(S * DIL_HEADS, HEAD_DIM))
        lses.append(lse.reshape(dil, DIL_HEADS, L).transpose(2, 0, 1).reshape(S * DIL_HEADS, 1))

    def merge(o0, o1, o2, l0, l1, l2):
        m = jnp.maximum(jnp.maximum(l0, l1), l2)
        e0, e1, e2 = jnp.exp(l0 - m), jnp.exp(l1 - m), jnp.exp(l2 - m)
        return (((e0 * o0 + e1 * o1 + e2 * o2) / (e0 + e1 + e2)).astype(BF16),)

    (o,) = rowwise(merge, outs + lses, [], "dilated_merge")
    return o.reshape(S, DIL_HEADS * HEAD_DIM), q_mem


def conformer_conv_mixer(h, slab, conv_w, conv_b, ln_g, ln_b):
    z = slab.matmul(h, 'c_w_in', 0, F32)
    a, gate, q_mem = z[:, :CONV_CH], z[:, CONV_CH:2 * CONV_CH], z[:, 2 * CONV_CH:]
    (u,) = rowwise(lambda a_, g_: (a_ * jax.nn.sigmoid(g_),), [a, gate], [], "conformer_glu")
    u = depthwise_conv(u, conv_w)

    def post(u_, b, g, beta):
        t = u_ + b
        mu = jnp.mean(t, axis=-1, keepdims=True)
        var = jnp.mean(jnp.square(t - mu), axis=-1, keepdims=True)
        return (_silu((t - mu) * lax.rsqrt(var + EPS) * g + beta).astype(BF16),)

    (o,) = rowwise(post, [u], [conv_b.reshape(1, -1), ln_g.reshape(1, -1), ln_b.reshape(1, -1)], "conformer_post")
    return o, q_mem


def short_conv_mixer(h, slab, conv_w):
    z = slab.matmul(h, 'd_w_in', 0, F32)
    bg, cg, hx, q_mem = z[:, :SC_CH], z[:, SC_CH:2 * SC_CH], z[:, 2 * SC_CH:3 * SC_CH], z[:, 3 * SC_CH:]
    (p,) = rowwise(lambda c_, h_: (c_ * h_,), [cg, hx], [], "shortconv_pre")
    cv = depthwise_conv(p, conv_w)
    (o,) = rowwise(lambda b_, c_: ((b_ * c_).astype(BF16),), [bg, cv], [], "shortconv_post")
    return o, q_mem


def local_loss(deltas, W, x, slab_values, mem, cos, sin, target):
    slab = SlabWeights(slab_values, deltas)
    norm_g = W['norm_g'][0]

    def gain(i, k):
        return norm_g[i, k].reshape(1, -1)

    (h,) = rowwise(lambda x_, g: (_rms(x_, g).astype(BF16),), [x], [gain(0, 0)], "input_norm")
    for i in range(DEPTH):
        x, h = swiglu_half_step(x, h, slab, 2 * i, gain(i, 1), gain(i, 2))
        if i == 0:
            o, q_mem = mla_mixer(h, cos, sin, W['a_w_in'][0], W['a_q_norm'][0].reshape(1, -1),
                                 W['a_kv_norm'][0].reshape(1, -1), W['a_w_uq'][0], W['a_w_ukv'][0])
            out_name = 'a_w_out'
        elif i == 1:
            o, q_mem = dilated_mixer(h, slab)
            out_name = None
        elif i == 2:
            o, q_mem = conformer_conv_mixer(h, slab, W['c_conv_w'][0], W['c_conv_b'][0], W['c_ln_g'][0],
                                            W['c_ln_b'][0])
            out_name = 'c_w_out'
        else:
            o, q_mem = short_conv_mixer(h, slab, W['d_conv_w'][0])
            out_name = 'd_w_out'
        (memn,) = rowwise(lambda m_, g: (_rms(m_, g).astype(BF16),), [mem], [gain(i, 6)], "memory_norm")
        mo = memory_attention(q_mem, memn, W['mem_w_kv'][i])
        cat = jnp.concatenate([o, mo], axis=-1)
        y = mm(cat, W['b_w_out'][0], F32) if out_name is None else slab.matmul(cat, out_name, 0, F32)
        x, h = residual_norm(x, y, 1.0, gain(i, 3), gain(i, 4))
        x, h = swiglu_half_step(x, h, slab, 2 * i + 1, gain(i, 5), gain(i + 1, 0) if i + 1 < DEPTH else None)
    return loss_head(x, target)


def kernel(x, mem, positions, norm_g, ffn_w_gate, ffn_w_up, ffn_w_down, mem_w_kv, a_w_in, a_q_norm, a_kv_norm, a_w_uq, a_w_ukv, a_w_out, b_w_in, b_w_out, c_w_in, c_conv_w, c_conv_b, c_ln_g, c_ln_b, c_w_out, d_w_in, d_conv_w, d_w_out, loss_target, m_norm_g, m_ffn_w_gate, m_ffn_w_up, m_ffn_w_down, m_mem_w_kv, m_a_w_in, m_a_q_norm, m_a_kv_norm, m_a_w_uq, m_a_w_ukv, m_a_w_out, m_b_w_in, m_b_w_out, m_c_w_in, m_c_conv_w, m_c_conv_b, m_c_ln_g, m_c_ln_b, m_c_w_out, m_d_w_in, m_d_conv_w, m_d_w_out, v_norm_g, v_ffn_w_gate, v_ffn_w_up, v_ffn_w_down, v_mem_w_kv, v_a_w_in, v_a_q_norm, v_a_kv_norm, v_a_w_uq, v_a_w_ukv, v_a_w_out, v_b_w_in, v_b_w_out, v_c_w_in, v_c_conv_w, v_c_conv_b, v_c_ln_g, v_c_ln_b, v_c_w_out, v_d_w_in, v_d_conv_w, v_d_w_out):
    w_in = dict(zip(WEIGHT_NAMES, (norm_g, ffn_w_gate, ffn_w_up, ffn_w_down, mem_w_kv, a_w_in, a_q_norm, a_kv_norm, a_w_uq, a_w_ukv, a_w_out, b_w_in, b_w_out, c_w_in, c_conv_w, c_conv_b, c_ln_g, c_ln_b, c_w_out, d_w_in, d_conv_w, d_w_out)))
    m_in = dict(zip(WEIGHT_NAMES, (m_norm_g, m_ffn_w_gate, m_ffn_w_up, m_ffn_w_down, m_mem_w_kv, m_a_w_in, m_a_q_norm, m_a_kv_norm, m_a_w_uq, m_a_w_ukv, m_a_w_out, m_b_w_in, m_b_w_out, m_c_w_in, m_c_conv_w, m_c_conv_b, m_c_ln_g, m_c_ln_b, m_c_w_out, m_d_w_in, m_d_conv_w, m_d_w_out)))
    v_in = dict(zip(WEIGHT_NAMES, (v_norm_g, v_ffn_w_gate, v_ffn_w_up, v_ffn_w_down, v_mem_w_kv, v_a_w_in, v_a_q_norm, v_a_kv_norm, v_a_w_uq, v_a_w_ukv, v_a_w_out, v_b_w_in, v_b_w_out, v_c_w_in, v_c_conv_w, v_c_conv_b, v_c_ln_g, v_c_ln_b, v_c_w_out, v_d_w_in, v_d_conv_w, v_d_w_out)))

    d_model = x.shape[-1]
    slab_values, W = gather_weights(w_in, d_model)
    deltas = {n: [jnp.zeros(leaf.shape, BF16) for leaf in leaves] for n, leaves in slab_values.items()}
    half = MLA_ROPE // 2
    inv = ROPE_THETA ** (-jnp.arange(half, dtype=F32) / half)
    ang = positions[0].astype(F32)[:, None] * inv
    loss, (g_slab, g_misc, gx) = jax.value_and_grad(local_loss, argnums=(0, 1, 2))(
        deltas, W, x[0], slab_values, mem[0], jnp.cos(ang), jnp.sin(ang), loss_target[0])
    loss = lax.psum(loss, ("x", "y", "c"))

    grads = reduce_gradients(g_slab, g_misc, w_in, d_model)
    steps = {n: adamw(w_in[n], grads[n], m_in[n], v_in[n]) for n in WEIGHT_NAMES}
    return (loss, gx[None], *[grads[n] for n in WEIGHT_NAMES], *[steps[n][0] for n in WEIGHT_NAMES],
            *[steps[n][1] for n in WEIGHT_NAMES], *[steps[n][2] for n in WEIGHT_NAMES])
```
